```python
import math
import jax, jax.numpy as jnp
from jax import lax
import numpy as np

D_MODEL = 2048
BATCH = 16
SEQ = 2048
DEPTH = 2

GRID_W = 64
CTX_LEN = 256
ROPE_THETA = 10000.0
NORM_EPS = 1e-6

GLA_HEADS = 4
GLA_DK = D_MODEL // 4 // GLA_HEADS
GLA_DV = D_MODEL // 2 // GLA_HEADS
GLA_GATE_RANK = 16
GLA_GATE_NORM = 16.0
GLA_CHUNK = 64

SWA_HEADS = 8
SWA_KV_HEADS = 2
SWA_HD = D_MODEL // 2 // SWA_HEADS
SWA_WINDOW = 128
SWA_BLOCK = 128

DIFF_HD = 64
DIFF_HEADS = D_MODEL // (2 * DIFF_HD)
Q_BLOCK = 128

N_EXPERTS = 16
EC_FACTOR = 2
D_FF_EXPERT = D_MODEL

EVEN_SIZES = (GLA_HEADS * GLA_DK, GLA_HEADS * GLA_DK, GLA_HEADS * GLA_DV, GLA_HEADS * GLA_DV,
              GLA_GATE_RANK, GLA_GATE_RANK,
              SWA_HEADS * SWA_HD, SWA_KV_HEADS * SWA_HD, SWA_KV_HEADS * SWA_HD)
EVEN_IN = sum(EVEN_SIZES)

kernel_name = 'hybrid_gla_swa_diffattn_ec_moe_dit'


def rms_norm(x, g):
    xf = x.astype(jnp.float32)
    y = xf * lax.rsqrt(jnp.mean(xf * xf, axis=-1, keepdims=True) + NORM_EPS)
    return (y * g.astype(jnp.float32)).astype(x.dtype)


def heads(t, n_heads):
    b, l, _ = t.shape
    return t.reshape(b, l, n_heads, -1).transpose(0, 2, 1, 3)


def merge_heads(t):
    b, n, l, d = t.shape
    return t.transpose(0, 2, 1, 3).reshape(b, l, n * d)


def axial_rope_tables(n_tokens, head_dim):
    rows = n_tokens // GRID_W
    r = jnp.repeat(jnp.arange(rows, dtype=jnp.float32), GRID_W)
    col = jnp.tile(jnp.arange(GRID_W, dtype=jnp.float32), rows)
    axis_dim = head_dim // 2
    inv = ROPE_THETA ** (-jnp.arange(0, axis_dim, 2, dtype=jnp.float32) / axis_dim)
    ang = jnp.concatenate([r[:, None] * inv, col[:, None] * inv], axis=-1)
    return jnp.cos(ang), jnp.sin(ang)


def apply_axial_rope(x, cos, sin):
    a = x.shape[-1] // 2
    f = a // 2
    cos = cos.astype(x.dtype)
    sin = sin.astype(x.dtype)

    def rot(t, cs, sn):
        t1, t2 = t[..., :f], t[..., f:]
        return jnp.concatenate([t1 * cs - t2 * sn, t2 * cs + t1 * sn], axis=-1)

    return jnp.concatenate([rot(x[..., :a], cos[:, :f], sin[:, :f]),
                            rot(x[..., a:], cos[:, f:], sin[:, f:])], axis=-1)


def softmax_with_sink(s, sink):
    s = s.astype(jnp.float32)
    sink = sink.astype(jnp.float32)
    m = jnp.maximum(s.max(axis=-1), sink)
    e = jnp.exp(s - m[..., None])
    return e / (e.sum(axis=-1) + jnp.exp(sink - m))[..., None]


def gla_chunk_scan(q, k, v, log_a, s0):
    b, h, l, _ = q.shape
    dv = v.shape[-1]
    n = l // GLA_CHUNK

    def to_chunks(t):
        return jnp.moveaxis(t.reshape(b, h, n, GLA_CHUNK, t.shape[-1]), 2, 0).astype(jnp.float32)

    qc, kc, vc, gc = to_chunks(q), to_chunks(k), to_chunks(v), to_chunks(log_a)
    lower = jnp.tril(jnp.ones((GLA_CHUNK, GLA_CHUNK), dtype=bool))[:, :, None]

    def step(s, inp):
        qi, ki, vi, gi = inp
        bc = jnp.cumsum(gi, axis=-2)
        o_inter = jnp.einsum('bhcd,bhde->bhce', qi * jnp.exp(bc), s)
        diff = bc[:, :, :, None, :] - bc[:, :, None, :, :]
        decay = jnp.exp(jnp.where(lower, diff, -jnp.inf))
        att = jnp.einsum('bhid,bhjd,bhijd->bhij', qi, ki, decay)
        o = o_inter + jnp.einsum('bhij,bhje->bhie', att, vi)
        b_last = bc[:, :, -1:, :]
        s_new = jnp.exp(b_last[:, :, 0, :])[..., None] * s + jnp.einsum(
            'bhcd,bhce->bhde', ki * jnp.exp(b_last - bc), vi)
        return s_new, o

    s_fin, oc = lax.scan(step, s0, (qc, kc, vc, gc))
    return jnp.moveaxis(oc, 0, 2).reshape(b, h, l, dv), s_fin


def gla_bidirectional(q, k, v, la_f, la_b, qc, kc, vc, lac_f, lac_b):
    b, h, _, dk = q.shape
    s0 = jnp.zeros((b, h, dk, v.shape[-1]), jnp.float32)
    flip = lambda t: jnp.flip(t, axis=2)
    oc_f, sc_f = gla_chunk_scan(qc, kc, vc, lac_f, s0)
    o_f, _ = gla_chunk_scan(q, k, v, la_f, sc_f)
    oc_b, sc_b = gla_chunk_scan(flip(qc), flip(kc), flip(vc), flip(lac_b), s0)
    o_b, _ = gla_chunk_scan(flip(q), flip(k), flip(v), flip(la_b), sc_b)
    return o_f + flip(o_b), oc_f + flip(oc_b)


def window_attention(q, k, v, kc, vc, sink):
    b, hq, s_len, d = q.shape
    g = hq // SWA_KV_HEADS
    nb = s_len // SWA_BLOCK
    lc = kc.shape[2]
    pad = ((0, 0), (0, 0), (SWA_BLOCK, SWA_BLOCK), (0, 0))
    kp, vp = jnp.pad(k, pad), jnp.pad(v, pad)
    qb = jnp.moveaxis(q.reshape(b, SWA_KV_HEADS, g, nb, SWA_BLOCK, d), 3, 0)
    rel = SWA_BLOCK + jnp.arange(SWA_BLOCK)[:, None] - jnp.arange(3 * SWA_BLOCK)[None, :]
    rel_ok = jnp.abs(rel) <= SWA_WINDOW
    ctx_ok = jnp.ones((SWA_BLOCK, lc), dtype=bool)
    sink_g = sink.reshape(SWA_KV_HEADS, g)[:, :, None]
    scale = d ** -0.5

    def block(args):
        qblk, i = args
        start = i * SWA_BLOCK
        kw = lax.dynamic_slice_in_dim(kp, start, 3 * SWA_BLOCK, axis=2)
        vw = lax.dynamic_slice_in_dim(vp, start, 3 * SWA_BLOCK, axis=2)
        kpos = start - SWA_BLOCK + jnp.arange(3 * SWA_BLOCK)
        valid = jnp.concatenate([rel_ok & ((kpos >= 0) & (kpos < s_len))[None, :], ctx_ok], axis=-1)
        keys = jnp.concatenate([kw, kc], axis=2)
        vals = jnp.concatenate([vw, vc], axis=2)
        sc = jnp.einsum('bhgqd,bhkd->bhgqk', qblk, keys).astype(jnp.float32) * scale
        p = softmax_with_sink(jnp.where(valid, sc, -jnp.inf), sink_g)
        return jnp.einsum('bhgqk,bhkd->bhgqd', p.astype(vals.dtype), vals)

    o = lax.map(block, (qb, jnp.arange(nb)))
    return jnp.moveaxis(o, 0, 3).reshape(b, hq, s_len, d)


def context_attention_sink(q, kc, vc, sink):
    b, hq, l, d = q.shape
    g = hq // SWA_KV_HEADS
    qg = q.reshape(b, SWA_KV_HEADS, g, l, d)
    sc = jnp.einsum('bhgqd,bhkd->bhgqk', qg, kc).astype(jnp.float32) * d ** -0.5
    p = softmax_with_sink(sc, sink.reshape(SWA_KV_HEADS, g)[:, :, None])
    return jnp.einsum('bhgqk,bhkd->bhgqd', p.astype(vc.dtype), vc).reshape(b, hq, l, d)


def even_mixer(h, hc, cos, sin, w_in, gate_w, gate_b, norm_g, sink, w_out, need_ctx):
    splits = [int(i) for i in np.cumsum(EVEN_SIZES)[:-1]]
    aq, ak, av, ag, arf, arb, bq, bk, bv = jnp.split(h @ w_in, splits, axis=-1)
    caq, cak, cav, cag, carf, carb, cbq, cbk, cbv = jnp.split(hc @ w_in, splits, axis=-1)

    def gla_inputs(q, k, v, rf, rb):
        la_f = jax.nn.log_sigmoid((rf @ gate_w[0] + gate_b[0]).astype(jnp.float32)) / GLA_GATE_NORM
        la_b = jax.nn.log_sigmoid((rb @ gate_w[1] + gate_b[1]).astype(jnp.float32)) / GLA_GATE_NORM
        return (heads(q * GLA_DK ** -0.5, GLA_HEADS), heads(k, GLA_HEADS), heads(v, GLA_HEADS),
                heads(la_f, GLA_HEADS), heads(la_b, GLA_HEADS))

    q, k, v, lf, lb = gla_inputs(aq, ak, av, arf, arb)
    qc, kc, vc, lfc, lbc = gla_inputs(caq, cak, cav, carf, carb)
    o_gla, oc_gla = gla_bidirectional(q, k, v, lf, lb, qc, kc, vc, lfc, lbc)
    a_out = merge_heads(rms_norm(o_gla.astype(h.dtype), norm_g) * jax.nn.silu(heads(ag, GLA_HEADS)))

    q_b = apply_axial_rope(heads(bq, SWA_HEADS), cos, sin)
    k_b = apply_axial_rope(heads(bk, SWA_KV_HEADS), cos, sin)
    v_b = heads(bv, SWA_KV_HEADS)
    kc_b, vc_b = heads(cbk, SWA_KV_HEADS), heads(cbv, SWA_KV_HEADS)
    b_out = merge_heads(window_attention(q_b, k_b, v_b, kc_b, vc_b, sink))
    y = jnp.concatenate([a_out, b_out], axis=-1) @ w_out

    yc = None
    if need_ctx:
        ac_out = merge_heads(rms_norm(oc_gla.astype(hc.dtype), norm_g) * jax.nn.silu(heads(cag, GLA_HEADS)))
        bc_out = merge_heads(context_attention_sink(heads(cbq, SWA_HEADS), kc_b, vc_b, sink))
        yc = jnp.concatenate([ac_out, bc_out], axis=-1) @ w_out
    return y, yc


def diff_attend(q, k, v, lam_full):
    s = jnp.einsum('bhmqd,bhmkd->bhmqk', q, k).astype(jnp.float32) * DIFF_HD ** -0.5
    p = jax.nn.softmax(s, axis=-1)
    a = p[:, :, 0] - lam_full * p[:, :, 1]
    return jnp.einsum('bhqk,bhkd->bhqd', a.astype(v.dtype), v)


def diff_mixer(h, hc, cos, sin, w_qkv, lam, subln_g, w_out, lambda_init, need_ctx):
    b, s_len, _ = h.shape

    def split_qkv(t):
        bt, l, _ = t.shape
        q, k, v = jnp.split(t, 3, axis=-1)
        q = q.reshape(bt, l, DIFF_HEADS, 2, DIFF_HD).transpose(0, 2, 3, 1, 4)
        k = k.reshape(bt, l, DIFF_HEADS, 2, DIFF_HD).transpose(0, 2, 3, 1, 4)
        v = v.reshape(bt, l, DIFF_HEADS, 2 * DIFF_HD).transpose(0, 2, 1, 3)
        return q, k, v

    q, k, v = split_qkv(h @ w_qkv)
    qc, kc, vc = split_qkv(hc @ w_qkv)
    q = apply_axial_rope(q, cos, sin)
    k = apply_axial_rope(k, cos, sin)
    lf = lam.astype(jnp.float32)
    lam_full = jnp.exp(jnp.sum(lf[0] * lf[1])) - jnp.exp(jnp.sum(lf[2] * lf[3])) + lambda_init

    k_all = jnp.concatenate([kc, k], axis=3)
    v_all = jnp.concatenate([vc, v], axis=2)
    nb = s_len // Q_BLOCK
    qb = jnp.moveaxis(q.reshape(b, DIFF_HEADS, 2, nb, Q_BLOCK, DIFF_HD), 3, 0)
    o = lax.map(lambda qblk: diff_attend(qblk, k_all, v_all, lam_full), qb)
    o = jnp.moveaxis(o, 0, 2).reshape(b, DIFF_HEADS, s_len, 2 * DIFF_HD)

    def finish(t):
        return merge_heads(rms_norm(t, subln_g) * (1.0 - lambda_init)) @ w_out

    y = finish(o)
    yc = finish(diff_attend(qc, kc, vc, lam_full)) if need_ctx else None
    return y, yc


def expert_choice_ffn(h, w_router, w_gu, w_down):
    b, n, d = h.shape
    cap = n * EC_FACTOR // N_EXPERTS
    aff = jax.nn.softmax((h @ w_router).astype(jnp.float32), axis=-1)
    gates, idx = lax.top_k(jnp.swapaxes(aff, 1, 2), cap)
    xs = jax.vmap(lambda hb, ib: hb[ib])(h, idx)
    g, u = jnp.split(jnp.einsum('becd,edf->becf', xs, w_gu), 2, axis=-1)
    y = jnp.einsum('becf,efd->becd', jax.nn.silu(g) * u, w_down)
    y = y * gates[..., None].astype(y.dtype)
    return jax.vmap(lambda yb, ib: jnp.zeros((n, d), yb.dtype).at[ib.reshape(-1)].add(yb.reshape(-1, d)))(y, idx)


def setup_inputs(seed: int = 0) -> dict:
    key = jax.random.key(seed)
    ks = jax.random.split(key, 22)
    n_even = (DEPTH + 1) // 2
    n_odd = DEPTH // 2
    f32 = jnp.float32
    nrm = lambda k, shape, s: jax.random.normal(k, shape, f32) * s
    gain = lambda k, shape: 1.0 + 0.02 * jax.random.normal(k, shape, f32)
    return {
        'x': nrm(ks[0], (BATCH, SEQ, D_MODEL), 1.0),
        'c': nrm(ks[1], (BATCH, D_MODEL), 1.0),
        'ctx': nrm(ks[2], (BATCH, CTX_LEN, D_MODEL), 1.0),
        'c_ctx': nrm(ks[3], (D_MODEL,), 1.0),
        'w_ada': nrm(ks[4], (DEPTH, D_MODEL, 6 * D_MODEL), 0.5 * D_MODEL ** -0.5),
        'b_ada': nrm(ks[5], (DEPTH, 6 * D_MODEL), 0.01),
        'g_norm_mix': gain(ks[6], (DEPTH, D_MODEL)),
        'g_norm_ffn': gain(ks[7], (DEPTH, D_MODEL)),
        'w_in_even': nrm(ks[8], (n_even, D_MODEL, EVEN_IN), D_MODEL ** -0.5),
        'gla_gate_w': nrm(ks[9], (n_even, 2, GLA_GATE_RANK, GLA_HEADS * GLA_DK), GLA_GATE_RANK ** -0.5),
        'gla_gate_b': nrm(ks[10], (n_even, 2, GLA_HEADS * GLA_DK), 0.1),
        'gla_norm_g': gain(ks[11], (n_even, GLA_DV)),
        'swa_sink': nrm(ks[12], (n_even, SWA_HEADS), 0.5),
        'w_out_even': nrm(ks[13], (n_even, D_MODEL, D_MODEL), D_MODEL ** -0.5),
        'w_qkv_odd': nrm(ks[14], (n_odd, D_MODEL, 3 * D_MODEL), D_MODEL ** -0.5),
        'diff_lambda': nrm(ks[15], (n_odd, 4, DIFF_HD), 0.1),
        'diff_subln_g': gain(ks[16], (n_odd, 2 * DIFF_HD)),
        'w_out_odd': nrm(ks[17], (n_odd, D_MODEL, D_MODEL), D_MODEL ** -0.5),
        'w_router': nrm(ks[18], (DEPTH, D_MODEL, N_EXPERTS), D_MODEL ** -0.5),
        'w_gate_up': nrm(ks[19], (DEPTH, N_EXPERTS, D_MODEL, 2 * D_FF_EXPERT), D_MODEL ** -0.5),
        'w_down': nrm(ks[20], (DEPTH, N_EXPERTS, D_FF_EXPERT, D_MODEL), D_FF_EXPERT ** -0.5),
        'g_final': gain(ks[21], (D_MODEL,)),
    }


def reference(x, c, ctx, c_ctx, w_ada, b_ada, g_norm_mix, g_norm_ffn, w_in_even, gla_gate_w, gla_gate_b,
              gla_norm_g, swa_sink, w_out_even, w_qkv_odd, diff_lambda, diff_subln_g, w_out_odd,
              w_router, w_gate_up, w_down, g_final):
    n_tok = x.shape[1]
    cos_b, sin_b = axial_rope_tables(n_tok, SWA_HD)
    cos_c, sin_c = axial_rope_tables(n_tok, DIFF_HD)
    for layer in range(DEPTH):
        need_ctx = layer < DEPTH - 1
        m = jax.nn.silu(c) @ w_ada[layer] + b_ada[layer]
        mc = jax.nn.silu(c_ctx) @ w_ada[layer] + b_ada[layer]
        sh1, sc1, g1, sh2, sc2, g2 = jnp.split(m[:, None, :], 6, axis=-1)
        csh1, csc1, cg1, csh2, csc2, cg2 = jnp.split(mc, 6, axis=-1)

        h = rms_norm(x, g_norm_mix[layer]) * (1.0 + sc1) + sh1
        hc = rms_norm(ctx, g_norm_mix[layer]) * (1.0 + csc1) + csh1
        i = layer // 2
        if layer % 2 == 0:
            y, yc = even_mixer(h, hc, cos_b, sin_b, w_in_even[i], gla_gate_w[i], gla_gate_b[i],
                               gla_norm_g[i], swa_sink[i], w_out_even[i], need_ctx)
        else:
            lambda_init = 0.8 - 0.6 * math.exp(-0.3 * layer)
            y, yc = diff_mixer(h, hc, cos_c, sin_c, w_qkv_odd[i], diff_lambda[i], diff_subln_g[i],
                               w_out_odd[i], lambda_init, need_ctx)
        x = x + g1 * y
        h = rms_norm(x, g_norm_ffn[layer]) * (1.0 + sc2) + sh2
        x = x + g2 * expert_choice_ffn(h, w_router[layer], w_gate_up[layer], w_down[layer])
        if need_ctx:
            ctx = ctx + cg1 * yc
            hc = rms_norm(ctx, g_norm_ffn[layer]) * (1.0 + csc2) + csh2
            ctx = ctx + cg2 * expert_choice_ffn(hc, w_router[layer], w_gate_up[layer], w_down[layer])
    return rms_norm(x, g_final)
```

```python
import functools

import numpy as np
import jax
import jax.numpy as jnp
from jax import lax
from jax.experimental import pallas as pl
from jax.experimental.pallas import tpu as pltpu

F32 = jnp.float32
BF16 = jnp.bfloat16

GRID_W = 64
ROPE_THETA = 10000.0
NORM_EPS = 1e-6
GLA_HEADS = 4
GLA_GATE_RANK = 16
GLA_GATE_NORM = 16.0
SWA_HEADS = 8
SWA_KV_HEADS = 2
SWA_GROUP = SWA_HEADS // SWA_KV_HEADS
SWA_BLOCK = 128
DIFF_HD = 64
N_EXPERTS = 16
EC_FACTOR = 2

LANES = 128
ROW_TILE = 256
COL_CHUNK = 256
GLA_CHUNK = 128
GLA_LEVELS = 7
DIFF_QBLOCK = 256
VMEM_LIMIT = 56 * 1024 * 1024
NEG_BIG = -1e30


def _cparams(*sem):
    return pltpu.CompilerParams(dimension_semantics=sem, vmem_limit_bytes=VMEM_LIMIT)


def _pick(n, cands):
    for c in cands:
        if n % c == 0:
            return c
    raise ValueError(f"no tile for {n} in {cands}")


def _nt(a, b):
    return lax.dot_general(a, b, (((1,), (1,)), ((), ())), preferred_element_type=F32)


def _tn(a, b):
    return lax.dot_general(a, b, (((0,), (0,)), ((), ())), preferred_element_type=F32)


def _dot(a, b):
    return jnp.dot(a, b, preferred_element_type=F32)


def _split3(a):
    hi = a.astype(BF16)
    r1 = a - hi.astype(F32)
    mid = r1.astype(BF16)
    lo = (r1 - mid.astype(F32)).astype(BF16)
    return hi, mid, lo


def _split2(a):
    hi = a.astype(BF16)
    lo = (a - hi.astype(F32)).astype(BF16)
    return hi, lo


def _silu(x):
    return x * (1.0 / (1.0 + jnp.exp(-x)))


def _rms(x, eps=NORM_EPS):
    return x * lax.rsqrt(jnp.mean(x * x, axis=-1, keepdims=True) + eps)


def _ada_kernel(c_ref, w_ref, b_ref, o_ref):
    s = _silu(c_ref[...])
    s_hi, s_lo = _split2(s)
    w_hi, w_lo = _split2(w_ref[0])
    acc = _dot(s_hi, w_hi) + (_dot(s_lo, w_hi) + _dot(s_hi, w_lo))
    o_ref[0] = acc + b_ref[0]


def _ada_call(cc, w_ada, b_ada):
    depth, d, n = w_ada.shape
    r = cc.shape[0]
    tn = _pick(n, (768, 512, 256, 128))
    return pl.pallas_call(
        _ada_kernel,
        grid=(depth, n // tn),
        in_specs=[pl.BlockSpec((r, d), lambda l, j: (0, 0)),
                  pl.BlockSpec((1, d, tn), lambda l, j: (l, 0, j)),
                  pl.BlockSpec((1, 1, tn), lambda l, j: (l, 0, j))],
        out_specs=pl.BlockSpec((1, r, tn), lambda l, j: (l, 0, j)),
        out_shape=jax.ShapeDtypeStruct((depth, r, n), F32),
        compiler_params=_cparams("parallel", "parallel"),
        name="adaln",
    )(cc, w_ada, b_ada.reshape(depth, 1, n))


def _proj_kernel(x_ref, mod_ref, g_ref, w_ref, cos_ref, sin_ref, *rest, chunk_ops, quarter, has_r):
    if has_r:
        wr_ref, o_ref, r_ref = rest
    else:
        (o_ref,) = rest
    x = x_ref[0]
    y = _rms(x) * g_ref[...]
    h = (y * (1.0 + mod_ref[0, 0, 1:2, :]) + mod_ref[0, 0, 0:1, :]).astype(BF16)
    reps = COL_CHUNK // LANES
    cos_t = jnp.concatenate([cos_ref[...]] * reps, axis=1)
    sin_t = jnp.concatenate([sin_ref[...]] * reps, axis=1)
    lane = lax.broadcasted_iota(jnp.int32, (1, COL_CHUNK), 1)
    first = (lane % (2 * quarter)) < quarter
    for c, (rope, scale) in enumerate(chunk_ops):
        acc = _dot(h, w_ref[:, c * COL_CHUNK:(c + 1) * COL_CHUNK])
        if rope:
            partner = jnp.where(first, pltpu.roll(acc, COL_CHUNK - quarter, 1), pltpu.roll(acc, quarter, 1))
            acc = acc * cos_t + partner * sin_t
        if scale != 1.0:
            acc = acc * scale
        o_ref[0, :, c * COL_CHUNK:(c + 1) * COL_CHUNK] = acc.astype(BF16)
    if has_r:
        r_ref[0] = _dot(h, wr_ref[...]).astype(BF16)


def _proj_call(xa, mod, g, w, cos_t, sin_t, chunk_ops, quarter, n_ctx_blocks, wr=None):
    b, t, d = xa.shape
    n = w.shape[1]
    assert n == len(chunk_ops) * COL_CHUNK and t % ROW_TILE == 0
    has_r = wr is not None
    in_specs = [pl.BlockSpec((1, ROW_TILE, d), lambda bi, i: (bi, i, 0)),
                pl.BlockSpec((1, 1, 6, d), lambda bi, i: (bi, jnp.where(i < n_ctx_blocks, 0, 1), 0, 0)),
                pl.BlockSpec((1, d), lambda bi, i: (0, 0)),
                pl.BlockSpec((d, n), lambda bi, i: (0, 0), pipeline_mode=pl.Buffered(1)),
                pl.BlockSpec((ROW_TILE, LANES), lambda bi, i: (i, 0)),
                pl.BlockSpec((ROW_TILE, LANES), lambda bi, i: (i, 0))]
    args = [xa, mod, g.reshape(1, d), w, cos_t, sin_t]
    out_specs = [pl.BlockSpec((1, ROW_TILE, n), lambda bi, i: (bi, i, 0))]
    out_shape = [jax.ShapeDtypeStruct((b, t, n), BF16)]
    if has_r:
        in_specs.append(pl.BlockSpec((d, LANES), lambda bi, i: (0, 0)))
        args.append(wr)
        out_specs.append(pl.BlockSpec((1, ROW_TILE, LANES), lambda bi, i: (bi, i, 0)))
        out_shape.append(jax.ShapeDtypeStruct((b, t, LANES), BF16))
    return pl.pallas_call(
        functools.partial(_proj_kernel, chunk_ops=tuple(chunk_ops), quarter=quarter, has_r=has_r),
        grid=(b, t // ROW_TILE),
        in_specs=in_specs, out_specs=out_specs, out_shape=out_shape,
        compiler_params=_cparams("parallel", "parallel"),
        name="norm_proj",
    )(*args)


def _gla_constants():
    c = GLA_CHUNK
    idx = np.arange(c)
    cm = np.zeros((2, GLA_LEVELS + 2, c, c), np.float32)
    lmask = np.zeros((2, GLA_LEVELS + 1, c, c), np.float32)
    vq = np.zeros((2, GLA_LEVELS, c, 1), np.float32)
    vk = np.zeros((2, GLA_LEVELS, c, 1), np.float32)
    cm[0, 0] = (idx[None, :] <= idx[:, None])
    cm[1, 0] = (idx[None, :] >= idx[:, None])
    for l in range(GLA_LEVELS):
        s = c >> (l + 1)
        blk = idx // (2 * s)
        second = (idx % (2 * s)) >= s
        same = blk[:, None] == blk[None, :]
        sep_f = blk * 2 * s + s - 1
        cm[0, 1 + l] = (idx[None, :] <= sep_f[:, None])
        lmask[0, l] = same & second[:, None] & (~second)[None, :]
        vq[0, l, :, 0] = second
        vk[0, l, :, 0] = ~second
        sep_b = blk * 2 * s + s
        cm[1, 1 + l] = (idx[None, :] >= sep_b[:, None])
        lmask[1, l] = same & (~second)[:, None] & second[None, :]
        vq[1, l, :, 0] = ~second
        vk[1, l, :, 0] = second
    cm[:, GLA_LEVELS + 1] = 1.0
    lmask[:, GLA_LEVELS] = np.eye(c)
    vq = np.broadcast_to(vq, (2, GLA_LEVELS, c, LANES)).copy()
    vk = np.broadcast_to(vk, (2, GLA_LEVELS, c, LANES)).copy()
    return (jnp.asarray(cm.reshape(2, (GLA_LEVELS + 2) * c, c), BF16), jnp.asarray(lmask, F32),
            jnp.asarray(vq, F32), jnp.asarray(vk, F32))


def _gla_kernel(q_ref, k_ref, v_ref, ag_ref, r_ref, gw_ref, gb_ref, ng_ref, cm_ref, lm_ref, vq_ref, vk_ref,
                o_ref, la_ref, of_ref, ob_ref, st_ref, *, n_ctx_chunks, n_chunks, q_scale):
    c = GLA_CHUNK
    r = r_ref[0]
    for d in range(2):
        z = _dot(r, gw_ref[d]) + gb_ref[d]
        la_ref[d] = (jnp.minimum(z, 0.0) - jnp.log(1.0 + jnp.exp(-jnp.abs(z)))) * (1.0 / GLA_GATE_NORM)
    st_ref[...] = jnp.zeros_like(st_ref)

    def chunk_step(d, ci, out_ref):
        rows = pl.ds(pl.multiple_of(ci * c, c), c)
        g = la_ref[d, rows, :]
        g_hi, g_mid, g_lo = _split3(g)
        cm = cm_ref[d]
        cum = _dot(cm, g_hi) + (_dot(cm, g_mid) + _dot(cm, g_lo))
        bc = cum[0:c]
        tot = cum[(GLA_LEVELS + 1) * c:(GLA_LEVELS + 2) * c]
        q = q_ref[0, rows, :].astype(F32) * q_scale
        k = k_ref[0, rows, :].astype(F32)
        v = v_ref[0, rows, :]
        att = lm_ref[d, GLA_LEVELS] * _nt(q.astype(BF16), k.astype(BF16))
        for l in range(GLA_LEVELS):
            ref_l = cum[(1 + l) * c:(2 + l) * c]
            eq = jnp.exp(jnp.where(vq_ref[d, l] > 0.0, bc - ref_l, NEG_BIG))
            ek = jnp.exp(jnp.where(vk_ref[d, l] > 0.0, ref_l - bc, NEG_BIG))
            att = att + lm_ref[d, l] * _nt((q * eq).astype(BF16), (k * ek).astype(BF16))
        st = st_ref[d]
        o = _dot(att.astype(BF16), v) + _nt((q * jnp.exp(bc)).astype(BF16), st.astype(BF16))
        out_ref[rows, :] = o
        kdec = (k * jnp.exp(tot - bc)).astype(BF16)
        st_ref[d] = st * jnp.exp(tot[0:1, :]) + _tn(v, kdec)

    def body(t, carry):
        chunk_step(0, t, of_ref)
        cb = jnp.where(t < n_ctx_chunks, n_ctx_chunks - 1 - t, n_chunks - 1 - (t - n_ctx_chunks))
        chunk_step(1, cb, ob_ref)
        return carry

    lax.fori_loop(0, n_chunks, body, 0)
    o = of_ref[...] + ob_ref[...]
    ag = ag_ref[0].astype(F32)
    o_ref[0] = (_rms(o) * ng_ref[...] * _silu(ag)).astype(BF16)


def _gla_call(p0, r, gw_pad, gb, norm_g, consts, n_ctx_rows, col_q, col_k, col_v, col_g, dk, dv):
    b, t, _ = p0.shape
    n_chunks = t // GLA_CHUNK
    cm, lm, vq, vk = consts
    kern = functools.partial(_gla_kernel, n_ctx_chunks=n_ctx_rows // GLA_CHUNK, n_chunks=n_chunks,
                             q_scale=float(dk) ** -0.5)
    full = lambda shape: pl.BlockSpec(shape, lambda bi, h: (0,) * len(shape))
    return pl.pallas_call(
        kern,
        grid=(b, GLA_HEADS),
        in_specs=[pl.BlockSpec((1, t, dk), lambda bi, h: (bi, 0, col_q // dk + h)),
                  pl.BlockSpec((1, t, dk), lambda bi, h: (bi, 0, col_k // dk + h)),
                  pl.BlockSpec((1, t, dv), lambda bi, h: (bi, 0, col_v // dv + h)),
                  pl.BlockSpec((1, t, dv), lambda bi, h: (bi, 0, col_g // dv + h)),
                  pl.BlockSpec((1, t, LANES), lambda bi, h: (bi, 0, 0)),
                  pl.BlockSpec((2, LANES, dk), lambda bi, h: (0, 0, h)),
                  pl.BlockSpec((2, 1, dk), lambda bi, h: (0, 0, h)),
                  full((1, dv)), full(cm.shape), full(lm.shape), full(vq.shape), full(vk.shape)],
        out_specs=pl.BlockSpec((1, t, dv), lambda bi, h: (bi, 0, h)),
        out_shape=jax.ShapeDtypeStruct((b, t, GLA_HEADS * dv), BF16),
        scratch_shapes=[pltpu.VMEM((2, t, dk), F32), pltpu.VMEM((t, dv), F32), pltpu.VMEM((t, dv), F32),
                        pltpu.VMEM((2, dv, dk), F32)],
        compiler_params=_cparams("parallel", "parallel"),
        name="gla",
    )(p0, p0, p0, p0, r, gw_pad, gb, norm_g.reshape(1, dv), cm, lm, vq, vk)


def _swa_kernel(q_ref, kp_ref, kc_ref, kn_ref, kx_ref, vp_ref, vc_ref, vn_ref, vx_ref, sink_ref, o_ref,
                *, n_ctx_blocks, n_blocks, scale):
    blk = SWA_BLOCK
    i = pl.program_id(2)
    q = q_ref[0]
    q4 = jnp.concatenate([q[:, g * blk:(g + 1) * blk] for g in range(SWA_GROUP)], axis=0)
    keys = jnp.concatenate([kp_ref[0], kc_ref[0], kn_ref[0], kx_ref[0]], axis=0)
    vals = jnp.concatenate([vp_ref[0], vc_ref[0], vn_ref[0], vx_ref[0]], axis=0)
    s = _nt(q4, keys) * scale
    nq, nk = s.shape
    r = lax.broadcasted_iota(jnp.int32, (nq, 1), 0) % blk
    col = lax.broadcasted_iota(jnp.int32, (1, nk), 1)
    is_ctx = i < n_ctx_blocks
    lo_s = jnp.where(is_ctx, nk, jnp.where(i == n_ctx_blocks, blk, 0))
    hi_s = jnp.where(is_ctx, -1, jnp.where(i == n_blocks - 1, 2 * blk - 1, 3 * blk - 1))
    lo = jnp.maximum(r, lo_s)
    hi = jnp.minimum(r + 2 * blk, hi_s)
    valid = ((col >= lo) & (col <= hi)) | (col >= 3 * blk)
    s = jnp.where(valid, s, NEG_BIG)
    sink = sink_ref[0, :, 0:1]
    m = jnp.maximum(jnp.max(s, axis=1, keepdims=True), sink)
    e = jnp.exp(s - m)
    den = jnp.sum(e, axis=1, keepdims=True) + jnp.exp(sink - m)
    o = _dot(e.astype(BF16), vals) / den
    o_ref[0] = jnp.concatenate([o[g * blk:(g + 1) * blk] for g in range(SWA_GROUP)], axis=1).astype(BF16)


def _swa_call(p0, sink_col, n_ctx_rows, col_q, col_k, col_v, hd):
    b, t, _ = p0.shape
    blk = SWA_BLOCK
    nb = t // blk
    ncb = n_ctx_rows // blk
    gq = SWA_GROUP * hd
    kblk = lambda colb, f: pl.BlockSpec((1, blk, hd), lambda bi, h, i: (bi, f(i), colb + h))
    prev = lambda i: jnp.clip(i - 1, ncb, nb - 1)
    cur = lambda i: i
    nxt = lambda i: jnp.clip(i + 1, ncb, nb - 1)
    ctx_spec = lambda colb: pl.BlockSpec((1, n_ctx_rows, hd), lambda bi, h, i: (bi, 0, colb + h))
    ck, cv = col_k // hd, col_v // hd
    return pl.pallas_call(
        functools.partial(_swa_kernel, n_ctx_blocks=ncb, n_blocks=nb, scale=float(hd) ** -0.5),
        grid=(b, SWA_KV_HEADS, nb),
        in_specs=[pl.BlockSpec((1, blk, gq), lambda bi, h, i: (bi, i, col_q // gq + h)),
                  kblk(ck, prev), kblk(ck, cur), kblk(ck, nxt), ctx_spec(ck),
                  kblk(cv, prev), kblk(cv, cur), kblk(cv, nxt), ctx_spec(cv),
                  pl.BlockSpec((1, SWA_GROUP * blk, LANES), lambda bi, h, i: (h, 0, 0))],
        out_specs=pl.BlockSpec((1, blk, gq), lambda bi, h, i: (bi, i, h)),
        out_shape=jax.ShapeDtypeStruct((b, t, SWA_HEADS * hd), BF16),
        compiler_params=_cparams("parallel", "parallel", "parallel"),
        name="swa",
    )(p0, p0, p0, p0, p0, p0, p0, p0, p0, sink_col)


def _diff_kernel(lam_ref, q_ref, k_ref, v_ref, g_ref, o_ref, *, row_off, n_qblocks, lambda_init):
    tq = DIFF_QBLOCK
    lam = lam_ref[...]
    s01 = jnp.sum(lam[0:1] * lam[1:2], axis=1, keepdims=True)
    s23 = jnp.sum(lam[2:3] * lam[3:4], axis=1, keepdims=True)
    lam_full = jnp.exp(s01) - jnp.exp(s23) + lambda_init
    k = k_ref[0]
    v = v_ref[0]
    first = lax.broadcasted_iota(jnp.int32, (1, 2 * DIFF_HD), 1) < DIFF_HD
    zero = jnp.zeros((), BF16)

    def body(j, carry):
        q = q_ref[0, pl.ds(pl.multiple_of(row_off + j * tq, tq), tq), :]
        qq = jnp.concatenate([jnp.where(first, q, zero), jnp.where(first, zero, q)], axis=0)
        s = _nt(qq, k)
        e = jnp.exp(s - jnp.max(s, axis=1, keepdims=True))
        inv = 1.0 / jnp.sum(e, axis=1, keepdims=True)
        a = e[0:tq] * inv[0:tq] - e[tq:2 * tq] * (lam_full * inv[tq:2 * tq])
        o = _dot(a.astype(BF16), v)
        y = _rms(o) * g_ref[...] * (1.0 - lambda_init)
        o_ref[0, pl.ds(pl.multiple_of(j * tq, tq), tq), :] = y.astype(BF16)
        return carry

    lax.fori_loop(0, n_qblocks, body, 0)


def _diff_call(p1, lam, subln_g, n_ctx_rows, n_heads, lambda_init):
    b, t, _ = p1.shape
    s_len = t - n_ctx_rows
    hd2 = 2 * DIFF_HD
    return pl.pallas_call(
        functools.partial(_diff_kernel, row_off=n_ctx_rows, n_qblocks=s_len // DIFF_QBLOCK,
                          lambda_init=lambda_init),
        grid=(b, n_heads),
        in_specs=[pl.BlockSpec((4, DIFF_HD), lambda bi, h: (0, 0)),
                  pl.BlockSpec((1, t, hd2), lambda bi, h: (bi, 0, h)),
                  pl.BlockSpec((1, t, hd2), lambda bi, h: (bi, 0, n_heads + h)),
                  pl.BlockSpec((1, t, hd2), lambda bi, h: (bi, 0, 2 * n_heads + h)),
                  pl.BlockSpec((1, hd2), lambda bi, h: (0, 0))],
        out_specs=pl.BlockSpec((1, s_len, hd2), lambda bi, h: (bi, 0, h)),
        out_shape=jax.ShapeDtypeStruct((b, s_len, n_heads * hd2), BF16),
        compiler_params=_cparams("parallel", "parallel"),
        name="diff_attn",
    )(lam, p1, p1, p1, subln_g.reshape(1, hd2))


def _outproj_kernel(*refs, n_y):
    y_refs = refs[:n_y]
    w_refs = refs[n_y:2 * n_y]
    x_ref, mod_ref, g_ref, wr_ref, xn_ref, h_ref, aff_ref = refs[2 * n_y:]
    acc = _dot(y_refs[0][0], w_refs[0][...])
    for yr, wr in zip(y_refs[1:], w_refs[1:]):
        acc = acc + _dot(yr[0], wr[...])
    xn = x_ref[0] + mod_ref[0, 0, 2:3, :] * acc
    xn_ref[0] = xn
    h = _rms(xn) * g_ref[...] * (1.0 + mod_ref[0, 0, 4:5, :]) + mod_ref[0, 0, 3:4, :]
    h_ref[0] = h.astype(BF16)
    h_hi, h_lo = _split2(h)
    w_hi, w_lo = _split2(wr_ref[...])
    logits = _nt(w_hi, h_hi) + (_nt(w_lo, h_hi) + _nt(w_hi, h_lo))
    ex = jnp.exp(logits - jnp.max(logits, axis=0, keepdims=True))
    aff_ref[0] = ex / jnp.sum(ex, axis=0, keepdims=True)


def _outproj_call(ys, ws, xa, mod, g, w_router_t, row_block_off, n_rows, n_ctx_blocks):
    b, _, d = xa.shape
    n_y = len(ys)
    e = w_router_t.shape[0]
    in_specs = ([pl.BlockSpec((1, ROW_TILE, y.shape[2]), lambda bi, i: (bi, i, 0)) for y in ys]
                + [pl.BlockSpec(w.shape, lambda bi, i: (0, 0), pipeline_mode=pl.Buffered(1)) for w in ws]
                + [pl.BlockSpec((1, ROW_TILE, d), lambda bi, i: (bi, i + row_block_off, 0)),
                   pl.BlockSpec((1, 1, 6, d),
                                lambda bi, i: (bi, jnp.where(i + row_block_off < n_ctx_blocks, 0, 1), 0, 0)),
                   pl.BlockSpec((1, d), lambda bi, i: (0, 0)),
                   pl.BlockSpec((e, d), lambda bi, i: (0, 0))])
    return pl.pallas_call(
        functools.partial(_outproj_kernel, n_y=n_y),
        grid=(b, n_rows // ROW_TILE),
        in_specs=in_specs,
        out_specs=[pl.BlockSpec((1, ROW_TILE, d), lambda bi, i: (bi, i, 0)),
                   pl.BlockSpec((1, ROW_TILE, d), lambda bi, i: (bi, i, 0)),
                   pl.BlockSpec((1, e, ROW_TILE), lambda bi, i: (bi, 0, i))],
        out_shape=[jax.ShapeDtypeStruct((b, n_rows, d), F32),
                   jax.ShapeDtypeStruct((b, n_rows, d), BF16),
                   jax.ShapeDtypeStruct((b, e, n_rows), F32)],
        compiler_params=_cparams("parallel", "parallel"),
        name="out_proj",
    )(*ys, *ws, xa, mod, g.reshape(1, d), w_router_t)


def _prefix_lanes(m, upper):
    e, n = m.shape
    carry = jnp.zeros((e, 1), F32)
    outs = []
    for blk in range(n // LANES):
        mb = m[:, blk * LANES:(blk + 1) * LANES]
        inc = _dot(mb.astype(BF16), upper)
        outs.append(inc - mb + carry)
        carry = carry + jnp.sum(mb, axis=1, keepdims=True)
    return jnp.concatenate(outs, axis=1)


def _transpose_exact(x, eye):
    hi, mid, lo = _split3(x)
    return _nt(eye, hi) + (_nt(eye, mid) + _nt(eye, lo))


def _route_kernel(aff_ref, pos_ref, posn_ref, gaten_ref, *, segs):
    n_exp = aff_ref.shape[1]
    li = lax.broadcasted_iota(jnp.int32, (LANES, LANES), 0)
    lj = lax.broadcasted_iota(jnp.int32, (LANES, LANES), 1)
    upper = jnp.where(li <= lj, 1.0, 0.0).astype(BF16)
    eye = jnp.where(li == lj, 1.0, 0.0).astype(BF16)
    for (off, n, cap, slot_off) in segs:
        a = aff_ref[0, :, off:off + n]
        bits = pltpu.bitcast(a, jnp.int32)

        def search(it, thr):
            cand = thr | jnp.left_shift(jnp.int32(1), 30 - it)
            cnt = jnp.sum(jnp.where(bits >= cand, 1.0, 0.0), axis=1, keepdims=True)
            return jnp.where(cnt >= cap, cand, thr)

        thr = lax.fori_loop(0, 31, search, jnp.zeros((n_exp, 1), jnp.int32))
        gt = jnp.where(bits > thr, 1.0, 0.0)
        eq = jnp.where(bits == thr, 1.0, 0.0)
        need = cap - jnp.sum(gt, axis=1, keepdims=True)
        sel = gt + eq * jnp.where(_prefix_lanes(eq, upper) < need, 1.0, 0.0)
        slot = _prefix_lanes(sel, upper)
        pos = jnp.where(sel > 0.0, slot + slot_off, -1.0)
        gate = sel * a
        pos_ref[0, :, off:off + n] = pos.astype(jnp.int32)
        for blk in range(n // LANES):
            cols = slice(blk * LANES, (blk + 1) * LANES)
            rows = slice(off + blk * LANES, off + (blk + 1) * LANES)
            posn_ref[0, rows, :] = _transpose_exact(pos[:, cols], eye).astype(jnp.int32)
            gaten_ref[0, rows, :] = _transpose_exact(gate[:, cols], eye)


def _route_call(aff_t, segs):
    b, e, t = aff_t.shape
    return pl.pallas_call(
        functools.partial(_route_kernel, segs=tuple(segs)),
        grid=(b,),
        in_specs=[pl.BlockSpec((1, e, t), lambda bi: (bi, 0, 0))],
        out_specs=[pl.BlockSpec((1, e, t), lambda bi: (bi, 0, 0)),
                   pl.BlockSpec((1, t, e), lambda bi: (bi, 0, 0)),
                   pl.BlockSpec((1, t, e), lambda bi: (bi, 0, 0))],
        out_shape=[jax.ShapeDtypeStruct((b, e, t), jnp.int32),
                   jax.ShapeDtypeStruct((b, t, e), jnp.int32),
                   jax.ShapeDtypeStruct((b, t, e), F32)],
        compiler_params=_cparams("parallel"),
        name="route",
    )(aff_t)


def _gather_kernel(pos_ref, h_ref, o_ref, *, segs):
    e = pl.program_id(1)
    prow = pos_ref[0, pl.ds(e, 1), :]
    for (off, n, cap, slot_off) in segs:
        slots = lax.broadcasted_iota(jnp.int32, (cap, 1), 0) + slot_off
        onehot = jnp.where(prow[:, off:off + n] == slots, 1.0, 0.0).astype(BF16)
        o_ref[0, 0, slot_off:slot_off + cap, :] = _dot(onehot, h_ref[0, off:off + n, :]).astype(BF16)


def _gather_call(pos, h, segs, n_slots):
    b, e, t = pos.shape
    d = h.shape[2]
    return pl.pallas_call(
        functools.partial(_gather_kernel, segs=tuple(segs)),
        grid=(b, e),
        in_specs=[pl.BlockSpec((1, e, t), lambda bi, ei: (bi, 0, 0)),
                  pl.BlockSpec((1, t, d), lambda bi, ei: (bi, 0, 0))],
        out_specs=pl.BlockSpec((1, 1, n_slots, d), lambda bi, ei: (ei, bi, 0, 0)),
        out_shape=jax.ShapeDtypeStruct((e, b, n_slots, d), BF16),
        compiler_params=_cparams("parallel", "arbitrary"),
        name="gather",
    )(pos, h)


def _ffn_up_kernel(x_ref, wg_ref, wu_ref, o_ref, wgb_ref, wub_ref):
    @pl.when(pl.program_id(2) == 0)
    def _():
        wgb_ref[...] = wg_ref[0].astype(BF16)
        wub_ref[...] = wu_ref[0].astype(BF16)

    x = x_ref[0]
    g = _dot(x, wgb_ref[...])
    u = _dot(x, wub_ref[...])
    o_ref[0] = (_silu(g) * u).astype(BF16)


def _ffn_down_kernel(a_ref, w_ref, o_ref, wb_ref):
    @pl.when(pl.program_id(2) == 0)
    def _():
        wb_ref[...] = w_ref[0].astype(BF16)

    o_ref[0] = _dot(a_ref[0], wb_ref[...]).astype(BF16)


def _ffn_call(xs, w_gu, w_dn):
    e, m, d = xs.shape
    f = w_dn.shape[1]
    tm = _pick(m, (1152, 1024, 768, 512, 256, 128, 64, 32, 16, 8))
    tf = _pick(f, (512, 256, 128))
    nf = f // tf
    act = pl.pallas_call(
        _ffn_up_kernel,
        grid=(e, nf, m // tm),
        in_specs=[pl.BlockSpec((1, tm, d), lambda ei, j, i: (ei, i, 0)),
                  pl.BlockSpec((1, d, tf), lambda ei, j, i: (ei, 0, j)),
                  pl.BlockSpec((1, d, tf), lambda ei, j, i: (ei, 0, j + nf))],
        out_specs=pl.BlockSpec((1, tm, tf), lambda ei, j, i: (ei, i, j)),
        out_shape=jax.ShapeDtypeStruct((e, m, f), BF16),
        scratch_shapes=[pltpu.VMEM((d, tf), BF16), pltpu.VMEM((d, tf), BF16)],
        compiler_params=_cparams("parallel", "arbitrary", "arbitrary"),
        name="ffn_up",
    )(xs, w_gu, w_gu)
    tn = _pick(d, (512, 256, 128))
    return pl.pallas_call(
        _ffn_down_kernel,
        grid=(e, d // tn, m // tm),
        in_specs=[pl.BlockSpec((1, tm, f), lambda ei, j, i: (ei, i, 0)),
                  pl.BlockSpec((1, f, tn), lambda ei, j, i: (ei, 0, j))],
        out_specs=pl.BlockSpec((1, tm, tn), lambda ei, j, i: (ei, i, j)),
        out_shape=jax.ShapeDtypeStruct((e, m, d), BF16),
        scratch_shapes=[pltpu.VMEM((f, tn), BF16)],
        compiler_params=_cparams("parallel", "arbitrary", "arbitrary"),
        name="ffn_down",
    )(act, w_dn)


def _combine_kernel(posn_ref, gaten_ref, y_ref, x_ref, mod_ref, gf_ref, o_ref, acc_ref,
                    *, cap_lat, cap_ctx, n_ctx_rows, tt, final_norm):
    i = pl.program_id(1)
    e = pl.program_id(2)
    n_exp = posn_ref.shape[2]

    @pl.when(e == 0)
    def _():
        acc_ref[...] = jnp.zeros_like(acc_ref)

    lane = lax.broadcasted_iota(jnp.int32, (1, n_exp), 1)
    pcol = jnp.sum(jnp.where(lane == e, posn_ref[0].astype(F32), 0.0), axis=1, keepdims=True)
    gcol = jnp.sum(jnp.where(lane == e, gaten_ref[0], 0.0), axis=1, keepdims=True)
    slots = lax.broadcasted_iota(jnp.int32, (1, cap_lat), 1).astype(F32)
    pt = jnp.where(pcol == slots, gcol, 0.0).astype(BF16)
    acc_ref[...] += _dot(pt, y_ref[0, 0, 0:cap_lat, :])
    if cap_ctx:
        @pl.when(i == 0)
        def _():
            cslots = lax.broadcasted_iota(jnp.int32, (1, cap_ctx), 1).astype(F32) + cap_lat
            ptc = jnp.where(pcol[0:n_ctx_rows] == cslots, gcol[0:n_ctx_rows], 0.0).astype(BF16)
            acc_ref[0:n_ctx_rows, :] += _dot(ptc, y_ref[0, 0, cap_lat:cap_lat + cap_ctx, :])

    @pl.when(e == n_exp - 1)
    def _():
        row = lax.broadcasted_iota(jnp.int32, (tt, 1), 0) + i * tt
        g2 = jnp.where(row < n_ctx_rows, mod_ref[0, 0, 5:6, :], mod_ref[0, 1, 5:6, :])
        out = x_ref[0] + g2 * acc_ref[...]
        if final_norm:
            out = _rms(out) * gf_ref[...]
        o_ref[0] = out


def _combine_call(posn, gaten, y, xa, mod, g_final, cap_lat, cap_ctx, n_ctx_rows, final_norm):
    b, t, d = xa.shape
    e = posn.shape[2]
    n_slots = y.shape[2]
    tt = _pick(t, (768, 1024, 512, 256))
    assert n_ctx_rows <= tt
    return pl.pallas_call(
        functools.partial(_combine_kernel, cap_lat=cap_lat, cap_ctx=cap_ctx, n_ctx_rows=n_ctx_rows, tt=tt,
                          final_norm=final_norm),
        grid=(b, t // tt, e),
        in_specs=[pl.BlockSpec((1, tt, e), lambda bi, i, ei: (bi, i, 0)),
                  pl.BlockSpec((1, tt, e), lambda bi, i, ei: (bi, i, 0)),
                  pl.BlockSpec((1, 1, n_slots, d), lambda bi, i, ei: (ei, bi, 0, 0)),
                  pl.BlockSpec((1, tt, d), lambda bi, i, ei: (bi, i, 0)),
                  pl.BlockSpec((1, 2, 6, d), lambda bi, i, ei: (bi, 0, 0, 0)),
                  pl.BlockSpec((1, d), lambda bi, i, ei: (0, 0))],
        out_specs=pl.BlockSpec((1, tt, d), lambda bi, i, ei: (bi, i, 0)),
        out_shape=jax.ShapeDtypeStruct((b, t, d), F32),
        scratch_shapes=[pltpu.VMEM((tt, d), F32)],
        compiler_params=_cparams("parallel", "parallel", "arbitrary"),
        name="combine",
    )(posn, gaten, y, xa, mod, g_final.reshape(1, d))


def _rope_tables(n_tokens, n_ctx_rows, head_dim):
    rows = n_tokens // GRID_W
    r = jnp.repeat(jnp.arange(rows, dtype=F32), GRID_W)
    col = jnp.tile(jnp.arange(GRID_W, dtype=F32), rows)
    axis_dim = head_dim // 2
    inv = ROPE_THETA ** (-jnp.arange(0, axis_dim, 2, dtype=F32) / axis_dim)
    ar, ac = r[:, None] * inv, col[:, None] * inv
    cos_p = jnp.concatenate([jnp.cos(ar), jnp.cos(ar), jnp.cos(ac), jnp.cos(ac)], axis=-1)
    sin_p = jnp.concatenate([-jnp.sin(ar), jnp.sin(ar), -jnp.sin(ac), jnp.sin(ac)], axis=-1)
    reps = LANES // head_dim
    cos_p, sin_p = jnp.tile(cos_p, (1, reps)), jnp.tile(sin_p, (1, reps))
    cos_t = jnp.concatenate([jnp.ones((n_ctx_rows, LANES), F32), cos_p], axis=0)
    sin_t = jnp.concatenate([jnp.zeros((n_ctx_rows, LANES), F32), sin_p], axis=0)
    return cos_t, sin_t


def _moe(h2, aff_t, posn_x, mod, g_final, w_gu, w_dn, segs, cap_lat, cap_ctx, n_ctx_rows, final_norm):
    b, t, d = h2.shape
    e = aff_t.shape[1]
    n_slots = cap_lat + cap_ctx
    pos, posn, gaten = _route_call(aff_t, segs)
    xs = _gather_call(pos, h2, segs, n_slots)
    y = _ffn_call(xs.reshape(e, b * n_slots, d), w_gu, w_dn).reshape(e, b, n_slots, d)
    return _combine_call(posn, gaten, y, posn_x, mod, g_final, cap_lat, cap_ctx, n_ctx_rows, final_norm)


def kernel(x, c, ctx, c_ctx, w_ada, b_ada, g_norm_mix, g_norm_ffn, w_in_even, gla_gate_w, gla_gate_b, gla_norm_g, swa_sink, w_out_even, w_qkv_odd, diff_lambda, diff_subln_g, w_out_odd, w_router, w_gate_up, w_down, g_final):
    b, s_len, d = x.shape
    lc = ctx.shape[1]
    depth = w_ada.shape[0]
    assert depth == 2 and lc % ROW_TILE == 0 and s_len % ROW_TILE == 0 and s_len % GRID_W == 0
    t = lc + s_len
    ncb = lc // ROW_TILE
    gla_dk, gla_dv = d // 4 // GLA_HEADS, d // 2 // GLA_HEADS
    swa_hd = d // 2 // SWA_HEADS
    diff_heads = d // (2 * DIFF_HD)
    n_exp = w_router.shape[2]
    assert gla_dk == LANES and swa_hd == LANES and 2 * DIFF_HD == LANES

    n_rows = -(-(b + 1) // 8) * 8
    cc = jnp.concatenate([c, c_ctx[None, :], jnp.zeros((n_rows - b - 1, d), F32)], axis=0)
    m_all = _ada_call(cc, w_ada, b_ada).reshape(depth, n_rows, 6, d)

    def mod_for(layer):
        lat = m_all[layer, :b]
        ctxm = jnp.broadcast_to(m_all[layer, b][None], (b, 6, d))
        return jnp.stack([ctxm, lat], axis=1)

    xa = jnp.concatenate([ctx, x], axis=1)

    mod0 = mod_for(0)
    w_in = w_in_even[0]
    sizes = (GLA_HEADS * gla_dk, GLA_HEADS * gla_dk, GLA_HEADS * gla_dv, GLA_HEADS * gla_dv,
             GLA_GATE_RANK, GLA_GATE_RANK, SWA_HEADS * swa_hd, SWA_KV_HEADS * swa_hd, SWA_KV_HEADS * swa_hd)
    offs = np.concatenate([[0], np.cumsum(sizes)])
    seg = lambda i: w_in[:, offs[i]:offs[i + 1]]
    w_main = jnp.concatenate([seg(0), seg(1), seg(2), seg(3), seg(6), seg(7), seg(8)], axis=1).astype(BF16)
    w_rank = jnp.concatenate([seg(4), seg(5), jnp.zeros((d, LANES - 2 * GLA_GATE_RANK), F32)],
                             axis=1).astype(BF16)
    col_aq, col_ak = 0, sizes[0]
    col_av = col_ak + sizes[1]
    col_ag = col_av + sizes[2]
    col_bq = col_ag + sizes[3]
    col_bk = col_bq + sizes[6]
    col_bv = col_bk + sizes[7]
    n_main = col_bv + sizes[8]
    rope0 = [(col_bq <= ch * COL_CHUNK < col_bv, 1.0) for ch in range(n_main // COL_CHUNK)]
    cos_b, sin_b = _rope_tables(s_len, lc, swa_hd)
    p0, r0 = _proj_call(xa, mod0, g_norm_mix[0], w_main, cos_b, sin_b, rope0, swa_hd // 4, ncb, wr=w_rank)

    gw = gla_gate_w[0]
    gw_pad = jnp.zeros((2, LANES, GLA_HEADS * gla_dk), F32)
    gw_pad = gw_pad.at[0, 0:GLA_GATE_RANK].set(gw[0]).at[1, GLA_GATE_RANK:2 * GLA_GATE_RANK].set(gw[1])
    a_out = _gla_call(p0, r0, gw_pad.astype(BF16), gla_gate_b[0].reshape(2, 1, -1), gla_norm_g[0],
                      _gla_constants(), lc, col_aq, col_ak, col_av, col_ag, gla_dk, gla_dv)
    sink_col = jnp.broadcast_to(
        jnp.repeat(swa_sink[0].reshape(SWA_KV_HEADS, SWA_GROUP), SWA_BLOCK, axis=1)[:, :, None],
        (SWA_KV_HEADS, SWA_GROUP * SWA_BLOCK, LANES))
    b_out = _swa_call(p0, sink_col, lc, col_bq, col_bk, col_bv, swa_hd)

    w_o = w_out_even[0].astype(BF16)
    n_a = GLA_HEADS * gla_dv
    w_router_t0 = jnp.transpose(w_router[0])
    xa1, h2, aff_t = _outproj_call([a_out, b_out], [w_o[:n_a], w_o[n_a:]], xa, mod0, g_norm_ffn[0],
                                   w_router_t0, 0, t, ncb)
    cap_lat = s_len * EC_FACTOR // n_exp
    cap_ctx = lc * EC_FACTOR // n_exp
    segs0 = [(lc, s_len, cap_lat, 0), (0, lc, cap_ctx, cap_lat)]
    xa2 = _moe(h2, aff_t, xa1, mod0, g_final, w_gate_up[0], w_down[0], segs0, cap_lat, cap_ctx, lc, False)

    mod1 = mod_for(1)
    lambda_init = 0.8 - 0.6 * float(np.exp(-0.3 * 1))
    w_qkv = w_qkv_odd[0].astype(BF16)
    n_qkv = w_qkv.shape[1]
    rope1 = [(ch * COL_CHUNK < 2 * d, DIFF_HD ** -0.5 if ch * COL_CHUNK < d else 1.0)
             for ch in range(n_qkv // COL_CHUNK)]
    cos_c, sin_c = _rope_tables(s_len, lc, DIFF_HD)
    (p1,) = _proj_call(xa2, mod1, g_norm_mix[1], w_qkv, cos_c, sin_c, rope1, DIFF_HD // 4, ncb)
    y1 = _diff_call(p1, diff_lambda[0], diff_subln_g[0], lc, diff_heads, lambda_init)
    x3, h2b, aff_tb = _outproj_call([y1], [w_out_odd[0].astype(BF16)], xa2, mod1, g_norm_ffn[1],
                                    jnp.transpose(w_router[1]), ncb, s_len, ncb)
    segs1 = [(0, s_len, cap_lat, 0)]
    return _moe(h2b, aff_tb, x3, mod1, g_final, w_gate_up[1], w_down[1], segs1, cap_lat, 0, 0, True)
```

```python
import functools

import numpy as np
import jax
import jax.numpy as jnp
from jax import lax
from jax.experimental import pallas as pl
from jax.experimental.pallas import tpu as pltpu

F32 = jnp.float32
BF16 = jnp.bfloat16

GRID_W = 64
ROPE_THETA = 10000.0
NORM_EPS = 1e-6
GLA_HEADS = 4
GLA_GATE_RANK = 16
GLA_GATE_NORM = 16.0
SWA_HEADS = 8
SWA_KV_HEADS = 2
SWA_GROUP = SWA_HEADS // SWA_KV_HEADS
SWA_BLOCK = 128
DIFF_HD = 64
N_EXPERTS = 16
EC_FACTOR = 2

LANES = 128
ROW_TILE = 256
COL_CHUNK = 256
GLA_CHUNK = 128
GLA_LEVELS = 7
GLA_SAFE_RANGE = 60.0
DIFF_QBLOCK = 256
COMBINE_GROUP = 4
VMEM_LIMIT = 56 * 1024 * 1024
NEG_BIG = -1e30


def _cparams(*sem):
    return pltpu.CompilerParams(dimension_semantics=sem, vmem_limit_bytes=VMEM_LIMIT)


def _pick(n, cands):
    for c in cands:
        if n % c == 0:
            return c
    raise ValueError(f"no tile for {n} in {cands}")


def _nt(a, b):
    return lax.dot_general(a, b, (((1,), (1,)), ((), ())), preferred_element_type=F32)


def _tn(a, b):
    return lax.dot_general(a, b, (((0,), (0,)), ((), ())), preferred_element_type=F32)


def _dot(a, b):
    return jnp.dot(a, b, preferred_element_type=F32)


def _split3(a):
    hi = a.astype(BF16)
    r1 = a - hi.astype(F32)
    mid = r1.astype(BF16)
    lo = (r1 - mid.astype(F32)).astype(BF16)
    return hi, mid, lo


def _split2(a):
    hi = a.astype(BF16)
    lo = (a - hi.astype(F32)).astype(BF16)
    return hi, lo


def _silu(x):
    return x * (1.0 / (1.0 + jnp.exp(-x)))


def _rms(x, eps=NORM_EPS):
    return x * lax.rsqrt(jnp.mean(x * x, axis=-1, keepdims=True) + eps)


def _ada_kernel(c_ref, w_ref, b_ref, o_ref):
    s = _silu(c_ref[...])
    s_hi, s_lo = _split2(s)
    w_hi, w_lo = _split2(w_ref[0])
    acc = _dot(s_hi, w_hi) + (_dot(s_lo, w_hi) + _dot(s_hi, w_lo))
    o_ref[0] = acc + b_ref[0]


def _ada_call(cc, w_ada, b_ada):
    depth, d, n = w_ada.shape
    r = cc.shape[0]
    tn = _pick(n, (768, 512, 256, 128))
    return pl.pallas_call(
        _ada_kernel,
        grid=(depth, n // tn),
        in_specs=[pl.BlockSpec((r, d), lambda l, j: (0, 0)),
                  pl.BlockSpec((1, d, tn), lambda l, j: (l, 0, j)),
                  pl.BlockSpec((1, 1, tn), lambda l, j: (l, 0, j))],
        out_specs=pl.BlockSpec((1, r, tn), lambda l, j: (l, 0, j)),
        out_shape=jax.ShapeDtypeStruct((depth, r, n), F32),
        compiler_params=_cparams("parallel", "parallel"),
        name="adaln",
    )(cc, w_ada, b_ada.reshape(depth, 1, n))


def _proj_kernel(x_ref, mod_ref, g_ref, w_ref, cos_ref, sin_ref, *rest, chunk_ops, quarter, has_r):
    if has_r:
        wr_ref, o_ref, r_ref = rest
    else:
        (o_ref,) = rest
    x = x_ref[0]
    y = _rms(x) * g_ref[...]
    h = (y * (1.0 + mod_ref[0, 0, 1:2, :]) + mod_ref[0, 0, 0:1, :]).astype(BF16)
    reps = COL_CHUNK // LANES
    cos_t = jnp.concatenate([cos_ref[...]] * reps, axis=1)
    sin_t = jnp.concatenate([sin_ref[...]] * reps, axis=1)
    lane = lax.broadcasted_iota(jnp.int32, (1, COL_CHUNK), 1)
    first = (lane % (2 * quarter)) < quarter
    for c, (rope, scale) in enumerate(chunk_ops):
        acc = _dot(h, w_ref[:, c * COL_CHUNK:(c + 1) * COL_CHUNK])
        if rope:
            partner = jnp.where(first, pltpu.roll(acc, COL_CHUNK - quarter, 1), pltpu.roll(acc, quarter, 1))
            acc = acc * cos_t + partner * sin_t
        if scale != 1.0:
            acc = acc * scale
        o_ref[0, :, c * COL_CHUNK:(c + 1) * COL_CHUNK] = acc.astype(BF16)
    if has_r:
        r_ref[0] = _dot(h, wr_ref[...]).astype(BF16)


def _proj_call(xa, mod, g, w, cos_t, sin_t, chunk_ops, quarter, n_ctx_blocks, wr=None):
    b, t, d = xa.shape
    n = w.shape[1]
    assert n == len(chunk_ops) * COL_CHUNK and t % ROW_TILE == 0
    has_r = wr is not None
    in_specs = [pl.BlockSpec((1, ROW_TILE, d), lambda bi, i: (bi, i, 0)),
                pl.BlockSpec((1, 1, 6, d), lambda bi, i: (bi, jnp.where(i < n_ctx_blocks, 0, 1), 0, 0)),
                pl.BlockSpec((1, d), lambda bi, i: (0, 0)),
                pl.BlockSpec((d, n), lambda bi, i: (0, 0), pipeline_mode=pl.Buffered(1)),
                pl.BlockSpec((ROW_TILE, LANES), lambda bi, i: (i, 0)),
                pl.BlockSpec((ROW_TILE, LANES), lambda bi, i: (i, 0))]
    args = [xa, mod, g.reshape(1, d), w, cos_t, sin_t]
    out_specs = [pl.BlockSpec((1, ROW_TILE, n), lambda bi, i: (bi, i, 0))]
    out_shape = [jax.ShapeDtypeStruct((b, t, n), BF16)]
    if has_r:
        in_specs.append(pl.BlockSpec((d, LANES), lambda bi, i: (0, 0)))
        args.append(wr)
        out_specs.append(pl.BlockSpec((1, ROW_TILE, LANES), lambda bi, i: (bi, i, 0)))
        out_shape.append(jax.ShapeDtypeStruct((b, t, LANES), BF16))
    return pl.pallas_call(
        functools.partial(_proj_kernel, chunk_ops=tuple(chunk_ops), quarter=quarter, has_r=has_r),
        grid=(b, t // ROW_TILE),
        in_specs=in_specs, out_specs=out_specs, out_shape=out_shape,
        compiler_params=_cparams("parallel", "parallel"),
        name="norm_proj",
    )(*args)


def _gla_constants():
    c = GLA_CHUNK
    idx = np.arange(c)
    cm = np.zeros((2, GLA_LEVELS + 2, c, c), np.float32)
    lmask = np.zeros((2, GLA_LEVELS + 2, c, c), np.float32)
    cm[0, 0] = (idx[None, :] <= idx[:, None])
    cm[1, 0] = (idx[None, :] >= idx[:, None])
    for l in range(GLA_LEVELS):
        s = c >> (l + 1)
        blk = idx // (2 * s)
        second = (idx % (2 * s)) >= s
        same = blk[:, None] == blk[None, :]
        sep_f = blk * 2 * s + s - 1
        cm[0, 1 + l] = (idx[None, :] <= sep_f[:, None])
        lmask[0, l] = same & second[:, None] & (~second)[None, :]
        sep_b = blk * 2 * s + s
        cm[1, 1 + l] = (idx[None, :] >= sep_b[:, None])
        lmask[1, l] = same & (~second)[:, None] & second[None, :]
    cm[:, GLA_LEVELS + 1] = 1.0
    lmask[:, GLA_LEVELS] = np.eye(c)
    lmask[:, GLA_LEVELS + 1] = cm[:, 0]
    return jnp.asarray(cm.reshape(2, (GLA_LEVELS + 2) * c, c), BF16), jnp.asarray(lmask, F32)


def _gla_kernel(q_ref, k_ref, v_ref, ag_ref, r_ref, gw_ref, gb_ref, ng_ref, cm_ref, lm_ref,
                o_ref, la_ref, of_ref, ob_ref, qe_ref, kd_ref, dec_ref, st_ref, *, n_ctx_chunks, n_chunks, q_scale):
    c = GLA_CHUNK
    r = r_ref[0]
    for d in range(2):
        z = _dot(r, gw_ref[d]) + gb_ref[d]
        la_ref[d] = (jnp.minimum(z, 0.0) - jnp.log(1.0 + jnp.exp(-jnp.abs(z)))) * (1.0 / GLA_GATE_NORM)
    st_ref[...] = jnp.zeros_like(st_ref)
    out_refs = (of_ref, ob_ref)

    def chunk_rows(ci):
        return pl.ds(pl.multiple_of(ci * c, c), c)

    min_tot = None
    for d in range(2):
        m = jnp.min(jnp.sum(la_ref[d].reshape(n_chunks, c, la_ref.shape[2]), axis=1))
        min_tot = m if min_tot is None else jnp.minimum(min_tot, m)
    is_safe = min_tot >= -GLA_SAFE_RANGE

    def prep_fast(ci, carry):
        rows = chunk_rows(ci)
        q = q_ref[0, rows, :].astype(F32) * q_scale
        k = k_ref[0, rows, :].astype(F32)
        v = v_ref[0, rows, :]
        for d in range(2):
            g_hi, g_lo = _split2(la_ref[d, rows, :])
            cm = cm_ref[d, 0:c, :]
            bc = _dot(cm, g_hi) + _dot(cm, g_lo)
            tot = bc[c - 1:c] if d == 0 else bc[0:1]
            qe = (q * jnp.exp(bc)).astype(BF16)
            kin = k * jnp.exp(-bc)
            att = lm_ref[d, GLA_LEVELS + 1] * _nt(qe, kin.astype(BF16))
            out_refs[d][rows, :] = _dot(att.astype(BF16), v)
            qe_ref[d, rows, :] = qe
            dec = jnp.exp(tot)
            kd_ref[d, rows, :] = (kin * dec).astype(BF16)
            dec_ref[d, pl.ds(pl.multiple_of(ci * 8, 8), 8), :] = jnp.broadcast_to(dec, (8, dec.shape[1]))
        return carry

    def prep(ci, carry):
        rows = chunk_rows(ci)
        q = q_ref[0, rows, :].astype(F32) * q_scale
        k = k_ref[0, rows, :].astype(F32)
        v = v_ref[0, rows, :]
        qk = _nt(q.astype(BF16), k.astype(BF16))
        for d in range(2):
            g_hi, g_lo = _split2(la_ref[d, rows, :])
            cm = cm_ref[d]
            cum = _dot(cm, g_hi) + _dot(cm, g_lo)
            bc = cum[0:c]
            tot = cum[(GLA_LEVELS + 1) * c:(GLA_LEVELS + 2) * c]
            att = lm_ref[d, GLA_LEVELS] * qk
            for l in range(GLA_LEVELS):
                ref_l = cum[(1 + l) * c:(2 + l) * c]
                eq = jnp.exp(jnp.minimum(bc - ref_l, 0.0))
                ek = jnp.exp(jnp.minimum(ref_l - bc, 0.0))
                att = att + lm_ref[d, l] * _nt((q * eq).astype(BF16), (k * ek).astype(BF16))
            out_refs[d][rows, :] = _dot(att.astype(BF16), v)
            qe_ref[d, rows, :] = (q * jnp.exp(bc)).astype(BF16)
            kd_ref[d, rows, :] = (k * jnp.exp(tot - bc)).astype(BF16)
            dec_ref[d, pl.ds(pl.multiple_of(ci * 8, 8), 8), :] = jnp.exp(tot[0:8])
        return carry

    @pl.when(is_safe)
    def _():
        lax.fori_loop(0, n_chunks, prep_fast, 0, unroll=2)

    @pl.when(jnp.logical_not(is_safe))
    def _():
        lax.fori_loop(0, n_chunks, prep, 0)

    def scan(t, carry):
        cb = jnp.where(t < n_ctx_chunks, n_ctx_chunks - 1 - t, n_chunks - 1 - (t - n_ctx_chunks))
        for d, ci in ((0, t), (1, cb)):
            rows = chunk_rows(ci)
            st = st_ref[d]
            out_refs[d][rows, :] += _nt(qe_ref[d, rows, :], st.astype(BF16))
            dec = dec_ref[d, pl.ds(pl.multiple_of(ci * 8, 8), 1), :]
            st_ref[d] = st * dec + _tn(v_ref[0, rows, :], kd_ref[d, rows, :])
        return carry

    lax.fori_loop(0, n_chunks, scan, 0)
    o = of_ref[...] + ob_ref[...]
    ag = ag_ref[0].astype(F32)
    o_ref[0] = (_rms(o) * ng_ref[...] * _silu(ag)).astype(BF16)


def _gla_call(p0, r, gw_pad, gb, norm_g, consts, n_ctx_rows, col_q, col_k, col_v, col_g, dk, dv):
    b, t, _ = p0.shape
    n_chunks = t // GLA_CHUNK
    cm, lm = consts
    kern = functools.partial(_gla_kernel, n_ctx_chunks=n_ctx_rows // GLA_CHUNK, n_chunks=n_chunks,
                             q_scale=float(dk) ** -0.5)
    full = lambda shape: pl.BlockSpec(shape, lambda bi, h: (0,) * len(shape))
    return pl.pallas_call(
        kern,
        grid=(b, GLA_HEADS),
        in_specs=[pl.BlockSpec((1, t, dk), lambda bi, h: (bi, 0, col_q // dk + h)),
                  pl.BlockSpec((1, t, dk), lambda bi, h: (bi, 0, col_k // dk + h)),
                  pl.BlockSpec((1, t, dv), lambda bi, h: (bi, 0, col_v // dv + h)),
                  pl.BlockSpec((1, t, dv), lambda bi, h: (bi, 0, col_g // dv + h)),
                  pl.BlockSpec((1, t, LANES), lambda bi, h: (bi, 0, 0)),
                  pl.BlockSpec((2, LANES, dk), lambda bi, h: (0, 0, h)),
                  pl.BlockSpec((2, 1, dk), lambda bi, h: (0, 0, h)),
                  full((1, dv)), full(cm.shape), full(lm.shape)],
        out_specs=pl.BlockSpec((1, t, dv), lambda bi, h: (bi, 0, h)),
        out_shape=jax.ShapeDtypeStruct((b, t, GLA_HEADS * dv), BF16),
        scratch_shapes=[pltpu.VMEM((2, t, dk), F32), pltpu.VMEM((t, dv), F32), pltpu.VMEM((t, dv), F32),
                        pltpu.VMEM((2, t, dk), BF16), pltpu.VMEM((2, t, dk), BF16),
                        pltpu.VMEM((2, n_chunks * 8, dk), F32), pltpu.VMEM((2, dv, dk), F32)],
        compiler_params=_cparams("parallel", "parallel"),
        name="gla",
    )(p0, p0, p0, p0, r, gw_pad, gb, norm_g.reshape(1, dv), cm, lm)


def _swa_kernel(q_ref, kp_ref, kc_ref, kn_ref, kx_ref, vp_ref, vc_ref, vn_ref, vx_ref, sink_ref, o_ref,
                *, n_ctx_blocks, n_blocks, scale):
    blk = SWA_BLOCK
    i = pl.program_id(2)
    q = q_ref[0]
    q4 = jnp.concatenate([q[:, g * blk:(g + 1) * blk] for g in range(SWA_GROUP)], axis=0)
    keys = jnp.concatenate([kp_ref[0], kc_ref[0], kn_ref[0], kx_ref[0]], axis=0)
    vals = jnp.concatenate([vp_ref[0], vc_ref[0], vn_ref[0], vx_ref[0]], axis=0)
    s = _nt(q4, keys) * scale
    nq, nk = s.shape
    r = lax.broadcasted_iota(jnp.int32, (nq, 1), 0) % blk
    col = lax.broadcasted_iota(jnp.int32, (1, nk), 1)
    is_ctx = i < n_ctx_blocks
    lo_s = jnp.where(is_ctx, nk, jnp.where(i == n_ctx_blocks, blk, 0))
    hi_s = jnp.where(is_ctx, -1, jnp.where(i == n_blocks - 1, 2 * blk - 1, 3 * blk - 1))
    lo = jnp.maximum(r, lo_s)
    hi = jnp.minimum(r + 2 * blk, hi_s)
    valid = ((col >= lo) & (col <= hi)) | (col >= 3 * blk)
    s = jnp.where(valid, s, NEG_BIG)
    sink = sink_ref[0, :, 0:1]
    m = jnp.maximum(jnp.max(s, axis=1, keepdims=True), sink)
    e = jnp.exp(s - m)
    den = jnp.sum(e, axis=1, keepdims=True) + jnp.exp(sink - m)
    o = _dot(e.astype(BF16), vals) / den
    o_ref[0] = jnp.concatenate([o[g * blk:(g + 1) * blk] for g in range(SWA_GROUP)], axis=1).astype(BF16)


def _swa_call(p0, sink_col, n_ctx_rows, col_q, col_k, col_v, hd):
    b, t, _ = p0.shape
    blk = SWA_BLOCK
    nb = t // blk
    ncb = n_ctx_rows // blk
    gq = SWA_GROUP * hd
    kblk = lambda colb, f: pl.BlockSpec((1, blk, hd), lambda bi, h, i: (bi, f(i), colb + h))
    prev = lambda i: jnp.clip(i - 1, ncb, nb - 1)
    cur = lambda i: i
    nxt = lambda i: jnp.clip(i + 1, ncb, nb - 1)
    ctx_spec = lambda colb: pl.BlockSpec((1, n_ctx_rows, hd), lambda bi, h, i: (bi, 0, colb + h))
    ck, cv = col_k // hd, col_v // hd
    return pl.pallas_call(
        functools.partial(_swa_kernel, n_ctx_blocks=ncb, n_blocks=nb, scale=float(hd) ** -0.5),
        grid=(b, SWA_KV_HEADS, nb),
        in_specs=[pl.BlockSpec((1, blk, gq), lambda bi, h, i: (bi, i, col_q // gq + h)),
                  kblk(ck, prev), kblk(ck, cur), kblk(ck, nxt), ctx_spec(ck),
                  kblk(cv, prev), kblk(cv, cur), kblk(cv, nxt), ctx_spec(cv),
                  pl.BlockSpec((1, SWA_GROUP * blk, LANES), lambda bi, h, i: (h, 0, 0))],
        out_specs=pl.BlockSpec((1, blk, gq), lambda bi, h, i: (bi, i, h)),
        out_shape=jax.ShapeDtypeStruct((b, t, SWA_HEADS * hd), BF16),
        compiler_params=_cparams("parallel", "parallel", "parallel"),
        name="swa",
    )(p0, p0, p0, p0, p0, p0, p0, p0, p0, sink_col)


def _diff_kernel(lam_ref, q_ref, k_ref, v_ref, g_ref, o_ref, s_ref, *, row_off, n_qblocks, lambda_init):
    tq = DIFF_QBLOCK
    lam = lam_ref[...]
    s01 = jnp.sum(lam[0:1] * lam[1:2], axis=1, keepdims=True)
    s23 = jnp.sum(lam[2:3] * lam[3:4], axis=1, keepdims=True)
    lam_full = jnp.exp(s01) - jnp.exp(s23) + lambda_init
    first = lax.broadcasted_iota(jnp.int32, (1, 2 * DIFF_HD), 1) < DIFF_HD
    zero = jnp.zeros((), BF16)

    def scores(j, slot):
        q = q_ref[0, pl.ds(pl.multiple_of(row_off + j * tq, tq), tq), :]
        qq = jnp.concatenate([jnp.where(first, q, zero), jnp.where(first, zero, q)], axis=0)
        s_ref[slot] = _nt(qq, k_ref[0])

    def finish(j, slot):
        s = s_ref[slot]
        e = jnp.exp2(s - jnp.max(s, axis=1, keepdims=True))
        z = jnp.sum(e, axis=1, keepdims=True)
        coef = lam_full * z[0:tq] / z[tq:2 * tq]
        a = e[0:tq] - coef * e[tq:2 * tq]
        o = _dot(a.astype(BF16), v_ref[0]) * (1.0 / z[0:tq])
        y = _rms(o) * g_ref[...] * (1.0 - lambda_init)
        o_ref[0, pl.ds(pl.multiple_of(j * tq, tq), tq), :] = y.astype(BF16)

    scores(0, 0)

    def body(jj, carry):
        j0 = 2 * jj
        scores(j0 + 1, 1)
        finish(j0, 0)
        scores(j0 + 2, 0)
        finish(j0 + 1, 1)
        return carry

    lax.fori_loop(0, n_qblocks // 2 - 1, body, 0)
    scores(n_qblocks - 1, 1)
    finish(n_qblocks - 2, 0)
    finish(n_qblocks - 1, 1)


def _diff_call(p1, lam, subln_g, n_ctx_rows, n_heads, lambda_init):
    b, t, _ = p1.shape
    s_len = t - n_ctx_rows
    hd2 = 2 * DIFF_HD
    n_qblocks = s_len // DIFF_QBLOCK
    assert n_qblocks >= 2 and n_qblocks % 2 == 0
    return pl.pallas_call(
        functools.partial(_diff_kernel, row_off=n_ctx_rows, n_qblocks=n_qblocks, lambda_init=lambda_init),
        grid=(b, n_heads),
        in_specs=[pl.BlockSpec((4, DIFF_HD), lambda bi, h: (0, 0)),
                  pl.BlockSpec((1, t, hd2), lambda bi, h: (bi, 0, h)),
                  pl.BlockSpec((1, t, hd2), lambda bi, h: (bi, 0, n_heads + h)),
                  pl.BlockSpec((1, t, hd2), lambda bi, h: (bi, 0, 2 * n_heads + h)),
                  pl.BlockSpec((1, hd2), lambda bi, h: (0, 0))],
        out_specs=pl.BlockSpec((1, s_len, hd2), lambda bi, h: (bi, 0, h)),
        out_shape=jax.ShapeDtypeStruct((b, s_len, n_heads * hd2), BF16),
        scratch_shapes=[pltpu.VMEM((2, 2 * DIFF_QBLOCK, t), F32)],
        compiler_params=_cparams("parallel", "parallel"),
        name="diff_attn",
    )(lam, p1, p1, p1, subln_g.reshape(1, hd2))


def _outproj_kernel(*refs, n_y):
    y_refs = refs[:n_y]
    w_refs = refs[n_y:2 * n_y]
    x_ref, mod_ref, g_ref, wr_ref, xn_ref, h_ref, aff_ref = refs[2 * n_y:]
    acc = _dot(y_refs[0][0], w_refs[0][...])
    for yr, wr in zip(y_refs[1:], w_refs[1:]):
        acc = acc + _dot(yr[0], wr[...])
    xn = x_ref[0] + mod_ref[0, 0, 2:3, :] * acc
    xn_ref[0] = xn
    h = _rms(xn) * g_ref[...] * (1.0 + mod_ref[0, 0, 4:5, :]) + mod_ref[0, 0, 3:4, :]
    h_ref[0] = h.astype(BF16)
    h_hi, h_lo = _split2(h)
    w_hi, w_lo = _split2(wr_ref[...])
    logits = _nt(w_hi, h_hi) + (_nt(w_lo, h_hi) + _nt(w_hi, h_lo))
    ex = jnp.exp(logits - jnp.max(logits, axis=0, keepdims=True))
    aff_ref[0] = ex / jnp.sum(ex, axis=0, keepdims=True)


def _outproj_call(ys, ws, xa, mod, g, w_router_t, row_block_off, n_rows, n_ctx_blocks):
    b, _, d = xa.shape
    n_y = len(ys)
    e = w_router_t.shape[0]
    in_specs = ([pl.BlockSpec((1, ROW_TILE, y.shape[2]), lambda bi, i: (bi, i, 0)) for y in ys]
                + [pl.BlockSpec(w.shape, lambda bi, i: (0, 0), pipeline_mode=pl.Buffered(1)) for w in ws]
                + [pl.BlockSpec((1, ROW_TILE, d), lambda bi, i: (bi, i + row_block_off, 0)),
                   pl.BlockSpec((1, 1, 6, d),
                                lambda bi, i: (bi, jnp.where(i + row_block_off < n_ctx_blocks, 0, 1), 0, 0)),
                   pl.BlockSpec((1, d), lambda bi, i: (0, 0)),
                   pl.BlockSpec((e, d), lambda bi, i: (0, 0))])
    return pl.pallas_call(
        functools.partial(_outproj_kernel, n_y=n_y),
        grid=(b, n_rows // ROW_TILE),
        in_specs=in_specs,
        out_specs=[pl.BlockSpec((1, ROW_TILE, d), lambda bi, i: (bi, i, 0)),
                   pl.BlockSpec((1, ROW_TILE, d), lambda bi, i: (bi, i, 0)),
                   pl.BlockSpec((1, e, ROW_TILE), lambda bi, i: (bi, 0, i))],
        out_shape=[jax.ShapeDtypeStruct((b, n_rows, d), F32),
                   jax.ShapeDtypeStruct((b, n_rows, d), BF16),
                   jax.ShapeDtypeStruct((b, e, n_rows), F32)],
        compiler_params=_cparams("parallel", "parallel"),
        name="out_proj",
    )(*ys, *ws, xa, mod, g.reshape(1, d), w_router_t)


def _prefix_lanes(m, upper):
    e, n = m.shape
    carry = jnp.zeros((e, 1), F32)
    outs = []
    for blk in range(n // LANES):
        mb = m[:, blk * LANES:(blk + 1) * LANES]
        inc = _dot(mb.astype(BF16), upper)
        outs.append(inc - mb + carry)
        carry = carry + jnp.sum(mb, axis=1, keepdims=True)
    return jnp.concatenate(outs, axis=1)


def _transpose_exact(x, eye):
    hi, mid, lo = _split3(x)
    return _nt(eye, hi) + (_nt(eye, mid) + _nt(eye, lo))


def _route_kernel(aff_ref, pos_ref, posn_ref, gaten_ref, *, segs):
    n_exp = aff_ref.shape[1]
    li = lax.broadcasted_iota(jnp.int32, (LANES, LANES), 0)
    lj = lax.broadcasted_iota(jnp.int32, (LANES, LANES), 1)
    upper = jnp.where(li <= lj, 1.0, 0.0).astype(BF16)
    eye = jnp.where(li == lj, 1.0, 0.0).astype(BF16)
    for (off, n, cap, slot_off) in segs:
        a = aff_ref[0, :, off:off + n]
        bits = pltpu.bitcast(a, jnp.int32)

        def search(it, thr):
            cand = thr | jnp.left_shift(jnp.int32(1), 30 - it)
            cnt = jnp.sum(jnp.where(bits >= cand, 1.0, 0.0), axis=1, keepdims=True)
            return jnp.where(cnt >= cap, cand, thr)

        thr = lax.fori_loop(0, 31, search, jnp.zeros((n_exp, 1), jnp.int32))
        gt = jnp.where(bits > thr, 1.0, 0.0)
        eq = jnp.where(bits == thr, 1.0, 0.0)
        need = cap - jnp.sum(gt, axis=1, keepdims=True)
        sel = gt + eq * jnp.where(_prefix_lanes(eq, upper) < need, 1.0, 0.0)
        slot = _prefix_lanes(sel, upper)
        pos = jnp.where(sel > 0.0, slot + slot_off, -1.0)
        gate = sel * a
        pos_ref[0, :, off:off + n] = pos.astype(jnp.int32)
        for blk in range(n // LANES):
            cols = slice(blk * LANES, (blk + 1) * LANES)
            rows = slice(off + blk * LANES, off + (blk + 1) * LANES)
            posn_ref[0, rows, :] = _transpose_exact(pos[:, cols], eye).astype(jnp.int32)
            gaten_ref[0, rows, :] = _transpose_exact(gate[:, cols], eye)


def _route_call(aff_t, segs):
    b, e, t = aff_t.shape
    return pl.pallas_call(
        functools.partial(_route_kernel, segs=tuple(segs)),
        grid=(b,),
        in_specs=[pl.BlockSpec((1, e, t), lambda bi: (bi, 0, 0))],
        out_specs=[pl.BlockSpec((1, e, t), lambda bi: (bi, 0, 0)),
                   pl.BlockSpec((1, t, e), lambda bi: (bi, 0, 0)),
                   pl.BlockSpec((1, t, e), lambda bi: (bi, 0, 0))],
        out_shape=[jax.ShapeDtypeStruct((b, e, t), jnp.int32),
                   jax.ShapeDtypeStruct((b, t, e), jnp.int32),
                   jax.ShapeDtypeStruct((b, t, e), F32)],
        compiler_params=_cparams("parallel"),
        name="route",
    )(aff_t)


def _gather_kernel(pos_ref, h_ref, o_ref, *, segs):
    e = pl.program_id(1)
    prow = pos_ref[0, pl.ds(e, 1), :]
    for (off, n, cap, slot_off) in segs:
        slots = lax.broadcasted_iota(jnp.int32, (cap, 1), 0) + slot_off
        onehot = jnp.where(prow[:, off:off + n] == slots, 1.0, 0.0).astype(BF16)
        o_ref[0, 0, slot_off:slot_off + cap, :] = _dot(onehot, h_ref[0, off:off + n, :]).astype(BF16)


def _gather_call(pos, h, segs, n_slots):
    b, e, t = pos.shape
    d = h.shape[2]
    return pl.pallas_call(
        functools.partial(_gather_kernel, segs=tuple(segs)),
        grid=(b, e),
        in_specs=[pl.BlockSpec((1, e, t), lambda bi, ei: (bi, 0, 0)),
                  pl.BlockSpec((1, t, d), lambda bi, ei: (bi, 0, 0))],
        out_specs=pl.BlockSpec((1, 1, n_slots, d), lambda bi, ei: (ei, bi, 0, 0)),
        out_shape=jax.ShapeDtypeStruct((e, b, n_slots, d), BF16),
        compiler_params=_cparams("parallel", "arbitrary"),
        name="gather",
    )(pos, h)


def _ffn_up_kernel(x_ref, wg_ref, wu_ref, o_ref, wgb_ref, wub_ref):
    @pl.when(pl.program_id(2) == 0)
    def _():
        wgb_ref[...] = wg_ref[0, 0].astype(BF16)
        wub_ref[...] = wu_ref[0, 0].astype(BF16)

    x = x_ref[0]
    g = _dot(x, wgb_ref[...])
    u = _dot(x, wub_ref[...])
    o_ref[0] = (_silu(g) * u).astype(BF16)


def _ffn_down_kernel(a_ref, w_ref, o_ref, wb_ref):
    @pl.when(pl.program_id(2) == 0)
    def _():
        wb_ref[...] = w_ref[0, 0].astype(BF16)

    o_ref[0] = _dot(a_ref[0], wb_ref[...]).astype(BF16)


def _ffn_call(xs, w_gu, w_dn, layer):
    e, m, d = xs.shape
    f = w_dn.shape[2]
    tm = _pick(m, (1152, 1024, 768, 512, 256, 128, 64, 32, 16, 8))
    tf = _pick(f, (512, 256, 128))
    nf = f // tf
    act = pl.pallas_call(
        _ffn_up_kernel,
        grid=(e, nf, m // tm),
        in_specs=[pl.BlockSpec((1, tm, d), lambda ei, j, i: (ei, i, 0)),
                  pl.BlockSpec((1, 1, d, tf), lambda ei, j, i: (layer, ei, 0, j)),
                  pl.BlockSpec((1, 1, d, tf), lambda ei, j, i: (layer, ei, 0, j + nf))],
        out_specs=pl.BlockSpec((1, tm, tf), lambda ei, j, i: (ei, i, j)),
        out_shape=jax.ShapeDtypeStruct((e, m, f), BF16),
        scratch_shapes=[pltpu.VMEM((d, tf), BF16), pltpu.VMEM((d, tf), BF16)],
        compiler_params=_cparams("parallel", "arbitrary", "arbitrary"),
        name="ffn_up",
    )(xs, w_gu, w_gu)
    tn = _pick(d, (512, 256, 128))
    return pl.pallas_call(
        _ffn_down_kernel,
        grid=(e, d // tn, m // tm),
        in_specs=[pl.BlockSpec((1, tm, f), lambda ei, j, i: (ei, i, 0)),
                  pl.BlockSpec((1, 1, f, tn), lambda ei, j, i: (layer, ei, 0, j))],
        out_specs=pl.BlockSpec((1, tm, tn), lambda ei, j, i: (ei, i, j)),
        out_shape=jax.ShapeDtypeStruct((e, m, d), BF16),
        scratch_shapes=[pltpu.VMEM((f, tn), BF16)],
        compiler_params=_cparams("parallel", "arbitrary", "arbitrary"),
        name="ffn_down",
    )(act, w_dn)


def _combine_kernel(posn_ref, gaten_ref, y_ref, x_ref, mod_ref, gf_ref, o_ref, acc_ref,
                    *, cap_lat, cap_ctx, n_ctx_rows, tt, n_group, final_norm):
    i = pl.program_id(1)
    eg = pl.program_id(2)
    n_exp = posn_ref.shape[2]
    lane = lax.broadcasted_iota(jnp.int32, (1, n_exp), 1)
    posn = posn_ref[0].astype(F32)
    gaten = gaten_ref[0]
    slots = lax.broadcasted_iota(jnp.int32, (1, cap_lat), 1).astype(F32)
    cols = []
    total = None
    for le in range(n_group):
        e = eg * n_group + le
        pcol = jnp.sum(jnp.where(lane == e, posn, 0.0), axis=1, keepdims=True)
        gcol = jnp.sum(jnp.where(lane == e, gaten, 0.0), axis=1, keepdims=True)
        cols.append((pcol, gcol))
        pt = jnp.where(pcol == slots, gcol, 0.0).astype(BF16)
        part = _dot(pt, y_ref[le, 0, 0:cap_lat, :])
        total = part if total is None else total + part

    @pl.when(eg == 0)
    def _():
        acc_ref[...] = total

    @pl.when(eg > 0)
    def _():
        acc_ref[...] += total

    if cap_ctx:
        @pl.when(i == 0)
        def _():
            cslots = lax.broadcasted_iota(jnp.int32, (1, cap_ctx), 1).astype(F32) + cap_lat
            ctot = None
            for le, (pcol, gcol) in enumerate(cols):
                ptc = jnp.where(pcol[0:n_ctx_rows] == cslots, gcol[0:n_ctx_rows], 0.0).astype(BF16)
                part = _dot(ptc, y_ref[le, 0, cap_lat:cap_lat + cap_ctx, :])
                ctot = part if ctot is None else ctot + part
            acc_ref[0:n_ctx_rows, :] += ctot

    @pl.when(eg == n_exp // n_group - 1)
    def _():
        row = lax.broadcasted_iota(jnp.int32, (tt, 1), 0) + i * tt
        g2 = jnp.where(row < n_ctx_rows, mod_ref[0, 0, 5:6, :], mod_ref[0, 1, 5:6, :])
        out = x_ref[0] + g2 * acc_ref[...]
        if final_norm:
            out = _rms(out) * gf_ref[...]
        o_ref[0] = out


def _combine_call(posn, gaten, y, xa, mod, g_final, cap_lat, cap_ctx, n_ctx_rows, final_norm):
    b, t, d = xa.shape
    e = posn.shape[2]
    n_slots = y.shape[2]
    tt = _pick(t, (768, 512, 256))
    n_group = COMBINE_GROUP
    assert n_ctx_rows <= tt and e % n_group == 0
    return pl.pallas_call(
        functools.partial(_combine_kernel, cap_lat=cap_lat, cap_ctx=cap_ctx, n_ctx_rows=n_ctx_rows, tt=tt,
                          n_group=n_group, final_norm=final_norm),
        grid=(b, t // tt, e // n_group),
        in_specs=[pl.BlockSpec((1, tt, e), lambda bi, i, ei: (bi, i, 0)),
                  pl.BlockSpec((1, tt, e), lambda bi, i, ei: (bi, i, 0)),
                  pl.BlockSpec((n_group, 1, n_slots, d), lambda bi, i, ei: (ei, bi, 0, 0)),
                  pl.BlockSpec((1, tt, d), lambda bi, i, ei: (bi, i, 0)),
                  pl.BlockSpec((1, 2, 6, d), lambda bi, i, ei: (bi, 0, 0, 0)),
                  pl.BlockSpec((1, d), lambda bi, i, ei: (0, 0))],
        out_specs=pl.BlockSpec((1, tt, d), lambda bi, i, ei: (bi, i, 0)),
        out_shape=jax.ShapeDtypeStruct((b, t, d), F32),
        scratch_shapes=[pltpu.VMEM((tt, d), F32)],
        compiler_params=_cparams("parallel", "parallel", "arbitrary"),
        name="combine",
    )(posn, gaten, y, xa, mod, g_final.reshape(1, d))


def _rope_tables(n_tokens, n_ctx_rows, head_dim):
    rows = n_tokens // GRID_W
    r = jnp.repeat(jnp.arange(rows, dtype=F32), GRID_W)
    col = jnp.tile(jnp.arange(GRID_W, dtype=F32), rows)
    axis_dim = head_dim // 2
    inv = ROPE_THETA ** (-jnp.arange(0, axis_dim, 2, dtype=F32) / axis_dim)
    ar, ac = r[:, None] * inv, col[:, None] * inv
    cos_p = jnp.concatenate([jnp.cos(ar), jnp.cos(ar), jnp.cos(ac), jnp.cos(ac)], axis=-1)
    sin_p = jnp.concatenate([-jnp.sin(ar), jnp.sin(ar), -jnp.sin(ac), jnp.sin(ac)], axis=-1)
    reps = LANES // head_dim
    cos_p, sin_p = jnp.tile(cos_p, (1, reps)), jnp.tile(sin_p, (1, reps))
    cos_t = jnp.concatenate([jnp.ones((n_ctx_rows, LANES), F32), cos_p], axis=0)
    sin_t = jnp.concatenate([jnp.zeros((n_ctx_rows, LANES), F32), sin_p], axis=0)
    return cos_t, sin_t


def _moe(h2, aff_t, x_res, mod, g_final, w_gu, w_dn, layer, segs, cap_lat, cap_ctx, n_ctx_rows, final_norm):
    b, t, d = h2.shape
    e = aff_t.shape[1]
    n_slots = cap_lat + cap_ctx
    pos, posn, gaten = _route_call(aff_t, segs)
    xs = _gather_call(pos, h2, segs, n_slots)
    y = _ffn_call(xs.reshape(e, b * n_slots, d), w_gu, w_dn, layer).reshape(e, b, n_slots, d)
    return _combine_call(posn, gaten, y, x_res, mod, g_final, cap_lat, cap_ctx, n_ctx_rows, final_norm)


def kernel(x, c, ctx, c_ctx, w_ada, b_ada, g_norm_mix, g_norm_ffn, w_in_even, gla_gate_w, gla_gate_b, gla_norm_g, swa_sink, w_out_even, w_qkv_odd, diff_lambda, diff_subln_g, w_out_odd, w_router, w_gate_up, w_down, g_final):
    b, s_len, d = x.shape
    lc = ctx.shape[1]
    depth = w_ada.shape[0]
    assert depth == 2 and lc % ROW_TILE == 0 and s_len % ROW_TILE == 0 and s_len % GRID_W == 0
    t = lc + s_len
    ncb = lc // ROW_TILE
    gla_dk, gla_dv = d // 4 // GLA_HEADS, d // 2 // GLA_HEADS
    swa_hd = d // 2 // SWA_HEADS
    diff_heads = d // (2 * DIFF_HD)
    n_exp = w_router.shape[2]
    assert gla_dk == LANES and swa_hd == LANES and 2 * DIFF_HD == LANES

    n_rows = -(-(b + 1) // 8) * 8
    cc = jnp.concatenate([c, c_ctx[None, :], jnp.zeros((n_rows - b - 1, d), F32)], axis=0)
    m_all = _ada_call(cc, w_ada, b_ada).reshape(depth, n_rows, 6, d)

    def mod_for(layer):
        lat = m_all[layer, :b]
        ctxm = jnp.broadcast_to(m_all[layer, b][None], (b, 6, d))
        return jnp.stack([ctxm, lat], axis=1)

    xa = jnp.concatenate([ctx, x], axis=1)

    mod0 = mod_for(0)
    w_in = w_in_even[0]
    sizes = (GLA_HEADS * gla_dk, GLA_HEADS * gla_dk, GLA_HEADS * gla_dv, GLA_HEADS * gla_dv,
             GLA_GATE_RANK, GLA_GATE_RANK, SWA_HEADS * swa_hd, SWA_KV_HEADS * swa_hd, SWA_KV_HEADS * swa_hd)
    offs = np.concatenate([[0], np.cumsum(sizes)])
    seg = lambda i: w_in[:, offs[i]:offs[i + 1]]
    w_main = jnp.concatenate([seg(0), seg(1), seg(2), seg(3), seg(6), seg(7), seg(8)], axis=1).astype(BF16)
    w_rank = jnp.concatenate([seg(4), seg(5), jnp.zeros((d, LANES - 2 * GLA_GATE_RANK), F32)],
                             axis=1).astype(BF16)
    col_aq, col_ak = 0, sizes[0]
    col_av = col_ak + sizes[1]
    col_ag = col_av + sizes[2]
    col_bq = col_ag + sizes[3]
    col_bk = col_bq + sizes[6]
    col_bv = col_bk + sizes[7]
    n_main = col_bv + sizes[8]
    rope0 = [(col_bq <= ch * COL_CHUNK < col_bv, 1.0) for ch in range(n_main // COL_CHUNK)]
    cos_b, sin_b = _rope_tables(s_len, lc, swa_hd)
    p0, r0 = _proj_call(xa, mod0, g_norm_mix[0], w_main, cos_b, sin_b, rope0, swa_hd // 4, ncb, wr=w_rank)

    gw = gla_gate_w[0]
    gw_pad = jnp.zeros((2, LANES, GLA_HEADS * gla_dk), F32)
    gw_pad = gw_pad.at[0, 0:GLA_GATE_RANK].set(gw[0]).at[1, GLA_GATE_RANK:2 * GLA_GATE_RANK].set(gw[1])
    a_out = _gla_call(p0, r0, gw_pad.astype(BF16), gla_gate_b[0].reshape(2, 1, -1), gla_norm_g[0],
                      _gla_constants(), lc, col_aq, col_ak, col_av, col_ag, gla_dk, gla_dv)
    sink_col = jnp.broadcast_to(
        jnp.repeat(swa_sink[0].reshape(SWA_KV_HEADS, SWA_GROUP), SWA_BLOCK, axis=1)[:, :, None],
        (SWA_KV_HEADS, SWA_GROUP * SWA_BLOCK, LANES))
    b_out = _swa_call(p0, sink_col, lc, col_bq, col_bk, col_bv, swa_hd)

    w_o = w_out_even[0].astype(BF16)
    n_a = GLA_HEADS * gla_dv
    w_router_t0 = jnp.transpose(w_router[0])
    xa1, h2, aff_t = _outproj_call([a_out, b_out], [w_o[:n_a], w_o[n_a:]], xa, mod0, g_norm_ffn[0],
                                   w_router_t0, 0, t, ncb)
    cap_lat = s_len * EC_FACTOR // n_exp
    cap_ctx = lc * EC_FACTOR // n_exp
    segs0 = [(lc, s_len, cap_lat, 0), (0, lc, cap_ctx, cap_lat)]
    xa2 = _moe(h2, aff_t, xa1, mod0, g_final, w_gate_up, w_down, 0, segs0, cap_lat, cap_ctx, lc, False)

    mod1 = mod_for(1)
    lambda_init = 0.8 - 0.6 * float(np.exp(-0.3 * 1))
    w_qkv = w_qkv_odd[0].astype(BF16)
    n_qkv = w_qkv.shape[1]
    q_scale = DIFF_HD ** -0.5 * float(np.log2(np.e))
    rope1 = [(ch * COL_CHUNK < 2 * d, q_scale if ch * COL_CHUNK < d else 1.0)
             for ch in range(n_qkv // COL_CHUNK)]
    cos_c, sin_c = _rope_tables(s_len, lc, DIFF_HD)
    (p1,) = _proj_call(xa2, mod1, g_norm_mix[1], w_qkv, cos_c, sin_c, rope1, DIFF_HD // 4, ncb)
    y1 = _diff_call(p1, diff_lambda[0], diff_subln_g[0], lc, diff_heads, lambda_init)
    x3, h2b, aff_tb = _outproj_call([y1], [w_out_odd[0].astype(BF16)], xa2, mod1, g_norm_ffn[1],
                                    jnp.transpose(w_router[1]), ncb, s_len, ncb)
    segs1 = [(0, s_len, cap_lat, 0)]
    return _moe(h2b, aff_tb, x3, mod1, g_final, w_gate_up, w_down, 1, segs1, cap_lat, 0, 0, True)
```

```python
import functools

import numpy as np
import jax
import jax.numpy as jnp
from jax import lax
from jax.experimental import pallas as pl
from jax.experimental.pallas import tpu as pltpu

F32 = jnp.float32
BF16 = jnp.bfloat16

GRID_W = 64
ROPE_THETA = 10000.0
NORM_EPS = 1e-6
GLA_HEADS = 4
GLA_GATE_RANK = 16
GLA_GATE_NORM = 16.0
SWA_HEADS = 8
SWA_KV_HEADS = 2
SWA_GROUP = SWA_HEADS // SWA_KV_HEADS
SWA_BLOCK = 128
DIFF_HD = 64
N_EXPERTS = 16
EC_FACTOR = 2

LANES = 128
ROW_TILE = 256
COL_CHUNK = 256
GLA_CHUNK = 128
GLA_LEVELS = 7
GLA_SAFE_RANGE = 60.0
DIFF_QBLOCK = 256
DIFF_SAFE_BOUND = 60.0
VMEM_LIMIT = 56 * 1024 * 1024
NEG_BIG = -1e30


def _cparams(*sem):
    return pltpu.CompilerParams(dimension_semantics=sem, vmem_limit_bytes=VMEM_LIMIT)


def _pick(n, cands):
    for c in cands:
        if n % c == 0:
            return c
    raise ValueError(f"no tile for {n} in {cands}")


def _nt(a, b):
    return lax.dot_general(a, b, (((1,), (1,)), ((), ())), preferred_element_type=F32)


def _tn(a, b):
    return lax.dot_general(a, b, (((0,), (0,)), ((), ())), preferred_element_type=F32)


def _dot(a, b):
    return jnp.dot(a, b, preferred_element_type=F32)


def _split3(a):
    hi = a.astype(BF16)
    r1 = a - hi.astype(F32)
    mid = r1.astype(BF16)
    lo = (r1 - mid.astype(F32)).astype(BF16)
    return hi, mid, lo


def _split2(a):
    hi = a.astype(BF16)
    lo = (a - hi.astype(F32)).astype(BF16)
    return hi, lo


def _silu(x):
    return x * (1.0 / (1.0 + jnp.exp(-x)))


def _rms(x, eps=NORM_EPS):
    return x * lax.rsqrt(jnp.mean(x * x, axis=-1, keepdims=True) + eps)


def _row_specs(stream, n_ctx_blocks, off=0):
    if isinstance(stream, tuple):
        ctx, x = stream
        d = x.shape[2]
        return ([pl.BlockSpec((1, ROW_TILE, d), lambda bi, i: (bi, jnp.minimum(i + off, n_ctx_blocks - 1), 0)),
                 pl.BlockSpec((1, ROW_TILE, d), lambda bi, i: (bi, jnp.maximum(i + off - n_ctx_blocks, 0), 0))],
                [ctx, x])
    d = stream.shape[2]
    return [pl.BlockSpec((1, ROW_TILE, d), lambda bi, i: (bi, i + off, 0))], [stream]


def _load_rows(row_refs, n_ctx_blocks, off=0):
    if len(row_refs) == 2:
        return jnp.where(pl.program_id(1) + off < n_ctx_blocks, row_refs[0][0], row_refs[1][0])
    return row_refs[0][0]


def _ada_kernel(c_ref, w_ref, b_ref, o_ref):
    s = _silu(c_ref[...])
    s_hi, s_lo = _split2(s)
    w_hi, w_lo = _split2(w_ref[0])
    acc = _dot(s_hi, w_hi) + (_dot(s_lo, w_hi) + _dot(s_hi, w_lo))
    o_ref[0] = acc + b_ref[0]


def _ada_call(cc, w_ada, b_ada):
    depth, d, n = w_ada.shape
    r = cc.shape[0]
    tn = _pick(n, (768, 512, 256, 128))
    return pl.pallas_call(
        _ada_kernel,
        grid=(depth, n // tn),
        in_specs=[pl.BlockSpec((r, d), lambda l, j: (0, 0)),
                  pl.BlockSpec((1, d, tn), lambda l, j: (l, 0, j)),
                  pl.BlockSpec((1, 1, tn), lambda l, j: (l, 0, j))],
        out_specs=pl.BlockSpec((1, r, tn), lambda l, j: (l, 0, j)),
        out_shape=jax.ShapeDtypeStruct((depth, r, n), F32),
        compiler_params=_cparams("parallel", "parallel"),
        name="adaln",
    )(cc, w_ada, b_ada.reshape(depth, 1, n))


def _proj_kernel(*refs, n_row_refs, n_ctx_blocks, chunk_ops, quarter, has_r):
    row_refs = refs[:n_row_refs]
    mod_ref, g_ref, w_ref, cos_ref, sin_ref = refs[n_row_refs:n_row_refs + 5]
    rest = refs[n_row_refs + 5:]
    if has_r:
        wr_ref, o_ref, r_ref = rest
    else:
        (o_ref,) = rest
    x = _load_rows(row_refs, n_ctx_blocks)
    y = _rms(x) * g_ref[...]
    h = (y * (1.0 + mod_ref[0, 0, 1:2, :]) + mod_ref[0, 0, 0:1, :]).astype(BF16)
    reps = COL_CHUNK // LANES
    cos_t = jnp.concatenate([cos_ref[...]] * reps, axis=1)
    sin_t = jnp.concatenate([sin_ref[...]] * reps, axis=1)
    lane = lax.broadcasted_iota(jnp.int32, (1, COL_CHUNK), 1)
    first = (lane % (2 * quarter)) < quarter
    for c, (rope, scale) in enumerate(chunk_ops):
        acc = _dot(h, w_ref[:, c * COL_CHUNK:(c + 1) * COL_CHUNK])
        if rope:
            partner = jnp.where(first, pltpu.roll(acc, COL_CHUNK - quarter, 1), pltpu.roll(acc, quarter, 1))
            acc = acc * cos_t + partner * sin_t
        if scale != 1.0:
            acc = acc * scale
        o_ref[0, :, c * COL_CHUNK:(c + 1) * COL_CHUNK] = acc.astype(BF16)
    if has_r:
        r_ref[0] = _dot(h, wr_ref[...]).astype(BF16)


def _proj_call(stream, mod, g, w, cos_t, sin_t, chunk_ops, quarter, n_ctx_blocks, wr=None):
    d, n = w.shape
    t = cos_t.shape[0]
    b = mod.shape[0]
    assert n == len(chunk_ops) * COL_CHUNK and t % ROW_TILE == 0
    has_r = wr is not None
    row_specs, row_args = _row_specs(stream, n_ctx_blocks)
    in_specs = row_specs + [
        pl.BlockSpec((1, 1, 6, d), lambda bi, i: (bi, jnp.where(i < n_ctx_blocks, 0, 1), 0, 0)),
        pl.BlockSpec((1, d), lambda bi, i: (0, 0)),
        pl.BlockSpec((d, n), lambda bi, i: (0, 0), pipeline_mode=pl.Buffered(1)),
        pl.BlockSpec((ROW_TILE, LANES), lambda bi, i: (i, 0)),
        pl.BlockSpec((ROW_TILE, LANES), lambda bi, i: (i, 0))]
    args = row_args + [mod, g.reshape(1, d), w, cos_t, sin_t]
    out_specs = [pl.BlockSpec((1, ROW_TILE, n), lambda bi, i: (bi, i, 0))]
    out_shape = [jax.ShapeDtypeStruct((b, t, n), BF16)]
    if has_r:
        in_specs.append(pl.BlockSpec((d, LANES), lambda bi, i: (0, 0)))
        args.append(wr)
        out_specs.append(pl.BlockSpec((1, ROW_TILE, LANES), lambda bi, i: (bi, i, 0)))
        out_shape.append(jax.ShapeDtypeStruct((b, t, LANES), BF16))
    return pl.pallas_call(
        functools.partial(_proj_kernel, n_row_refs=len(row_args), n_ctx_blocks=n_ctx_blocks,
                          chunk_ops=tuple(chunk_ops), quarter=quarter, has_r=has_r),
        grid=(b, t // ROW_TILE),
        in_specs=in_specs, out_specs=out_specs, out_shape=out_shape,
        compiler_params=_cparams("parallel", "parallel"),
        name="norm_proj",
    )(*args)


def _gla_constants():
    c = GLA_CHUNK
    idx = np.arange(c)
    cm = np.zeros((2, GLA_LEVELS + 2, c, c), np.float32)
    lmask = np.zeros((2, GLA_LEVELS + 2, c, c), np.float32)
    cm[0, 0] = (idx[None, :] <= idx[:, None])
    cm[1, 0] = (idx[None, :] >= idx[:, None])
    for l in range(GLA_LEVELS):
        s = c >> (l + 1)
        blk = idx // (2 * s)
        second = (idx % (2 * s)) >= s
        same = blk[:, None] == blk[None, :]
        sep_f = blk * 2 * s + s - 1
        cm[0, 1 + l] = (idx[None, :] <= sep_f[:, None])
        lmask[0, l] = same & second[:, None] & (~second)[None, :]
        sep_b = blk * 2 * s + s
        cm[1, 1 + l] = (idx[None, :] >= sep_b[:, None])
        lmask[1, l] = same & (~second)[:, None] & second[None, :]
    cm[:, GLA_LEVELS + 1] = 1.0
    lmask[:, GLA_LEVELS] = np.eye(c)
    lmask[:, GLA_LEVELS + 1] = cm[:, 0]
    return jnp.asarray(cm.reshape(2, (GLA_LEVELS + 2) * c, c), BF16), jnp.asarray(lmask, F32)


def _gla_kernel(q_ref, k_ref, v_ref, ag_ref, r_ref, gw_ref, gb_ref, ng_ref, cm_ref, lm_ref,
                o_ref, la_ref, of_ref, ob_ref, qe_ref, kd_ref, dec_ref, st_ref, *, n_ctx_chunks, n_chunks, q_scale):
    c = GLA_CHUNK
    r = r_ref[0]
    for d in range(2):
        z = _dot(r, gw_ref[d]) + gb_ref[d]
        la_ref[d] = (jnp.minimum(z, 0.0) - jnp.log(1.0 + jnp.exp(-jnp.abs(z)))) * (1.0 / GLA_GATE_NORM)
    st_ref[...] = jnp.zeros_like(st_ref)
    out_refs = (of_ref, ob_ref)

    def chunk_rows(ci):
        return pl.ds(pl.multiple_of(ci * c, c), c)

    min_tot = None
    for d in range(2):
        m = jnp.min(jnp.sum(la_ref[d].reshape(n_chunks, c, la_ref.shape[2]), axis=1))
        min_tot = m if min_tot is None else jnp.minimum(min_tot, m)
    is_safe = min_tot >= -GLA_SAFE_RANGE

    def prep_fast(ci, carry):
        rows = chunk_rows(ci)
        q = q_ref[0, rows, :].astype(F32) * q_scale
        k = k_ref[0, rows, :].astype(F32)
        v = v_ref[0, rows, :]
        for d in range(2):
            g_hi, g_lo = _split2(la_ref[d, rows, :])
            cm = cm_ref[d, 0:c, :]
            bc = _dot(cm, g_hi) + _dot(cm, g_lo)
            tot = bc[c - 1:c] if d == 0 else bc[0:1]
            qe = (q * jnp.exp(bc)).astype(BF16)
            kin = k * jnp.exp(-bc)
            att = lm_ref[d, GLA_LEVELS + 1] * _nt(qe, kin.astype(BF16))
            out_refs[d][rows, :] = _dot(att.astype(BF16), v)
            qe_ref[d, rows, :] = qe
            dec = jnp.exp(tot)
            kd_ref[d, rows, :] = (kin * dec).astype(BF16)
            dec_ref[d, pl.ds(pl.multiple_of(ci * 8, 8), 8), :] = jnp.broadcast_to(dec, (8, dec.shape[1]))
        return carry

    def prep(ci, carry):
        rows = chunk_rows(ci)
        q = q_ref[0, rows, :].astype(F32) * q_scale
        k = k_ref[0, rows, :].astype(F32)
        v = v_ref[0, rows, :]
        qk = _nt(q.astype(BF16), k.astype(BF16))
        for d in range(2):
            g_hi, g_lo = _split2(la_ref[d, rows, :])
            cm = cm_ref[d]
            cum = _dot(cm, g_hi) + _dot(cm, g_lo)
            bc = cum[0:c]
            tot = cum[(GLA_LEVELS + 1) * c:(GLA_LEVELS + 2) * c]
            att = lm_ref[d, GLA_LEVELS] * qk
            for l in range(GLA_LEVELS):
                ref_l = cum[(1 + l) * c:(2 + l) * c]
                eq = jnp.exp(jnp.minimum(bc - ref_l, 0.0))
                ek = jnp.exp(jnp.minimum(ref_l - bc, 0.0))
                att = att + lm_ref[d, l] * _nt((q * eq).astype(BF16), (k * ek).astype(BF16))
            out_refs[d][rows, :] = _dot(att.astype(BF16), v)
            qe_ref[d, rows, :] = (q * jnp.exp(bc)).astype(BF16)
            kd_ref[d, rows, :] = (k * jnp.exp(tot - bc)).astype(BF16)
            dec_ref[d, pl.ds(pl.multiple_of(ci * 8, 8), 8), :] = jnp.exp(tot[0:8])
        return carry

    @pl.when(is_safe)
    def _():
        lax.fori_loop(0, n_chunks, prep_fast, 0, unroll=2)

    @pl.when(jnp.logical_not(is_safe))
    def _():
        lax.fori_loop(0, n_chunks, prep, 0)

    def scan(t, carry):
        cb = jnp.where(t < n_ctx_chunks, n_ctx_chunks - 1 - t, n_chunks - 1 - (t - n_ctx_chunks))
        for d, ci in ((0, t), (1, cb)):
            rows = chunk_rows(ci)
            st = st_ref[d]
            out_refs[d][rows, :] += _nt(qe_ref[d, rows, :], st.astype(BF16))
            dec = dec_ref[d, pl.ds(pl.multiple_of(ci * 8, 8), 1), :]
            st_ref[d] = st * dec + _tn(v_ref[0, rows, :], kd_ref[d, rows, :])
        return carry

    lax.fori_loop(0, n_chunks, scan, 0)
    o = of_ref[...] + ob_ref[...]
    ag = ag_ref[0].astype(F32)
    o_ref[0] = (_rms(o) * ng_ref[...] * _silu(ag)).astype(BF16)


def _gla_call(p0, r, gw_pad, gb, norm_g, consts, n_ctx_rows, col_q, col_k, col_v, col_g, dk, dv):
    b, t, _ = p0.shape
    n_chunks = t // GLA_CHUNK
    cm, lm = consts
    kern = functools.partial(_gla_kernel, n_ctx_chunks=n_ctx_rows // GLA_CHUNK, n_chunks=n_chunks,
                             q_scale=float(dk) ** -0.5)
    full = lambda shape: pl.BlockSpec(shape, lambda bi, h: (0,) * len(shape))
    return pl.pallas_call(
        kern,
        grid=(b, GLA_HEADS),
        in_specs=[pl.BlockSpec((1, t, dk), lambda bi, h: (bi, 0, col_q // dk + h)),
                  pl.BlockSpec((1, t, dk), lambda bi, h: (bi, 0, col_k // dk + h)),
                  pl.BlockSpec((1, t, dv), lambda bi, h: (bi, 0, col_v // dv + h)),
                  pl.BlockSpec((1, t, dv), lambda bi, h: (bi, 0, col_g // dv + h)),
                  pl.BlockSpec((1, t, LANES), lambda bi, h: (bi, 0, 0)),
                  pl.BlockSpec((2, LANES, dk), lambda bi, h: (0, 0, h)),
                  pl.BlockSpec((2, 1, dk), lambda bi, h: (0, 0, h)),
                  full((1, dv)), full(cm.shape), full(lm.shape)],
        out_specs=pl.BlockSpec((1, t, dv), lambda bi, h: (bi, 0, h)),
        out_shape=jax.ShapeDtypeStruct((b, t, GLA_HEADS * dv), BF16),
        scratch_shapes=[pltpu.VMEM((2, t, dk), F32), pltpu.VMEM((t, dv), F32), pltpu.VMEM((t, dv), F32),
                        pltpu.VMEM((2, t, dk), BF16), pltpu.VMEM((2, t, dk), BF16),
                        pltpu.VMEM((2, n_chunks * 8, dk), F32), pltpu.VMEM((2, dv, dk), F32)],
        compiler_params=_cparams("parallel", "parallel"),
        name="gla",
    )(p0, p0, p0, p0, r, gw_pad, gb, norm_g.reshape(1, dv), cm, lm)


def _swa_kernel(q_ref, kp_ref, kc_ref, kn_ref, kx_ref, vp_ref, vc_ref, vn_ref, vx_ref, sink_ref, o_ref,
                *, n_ctx_blocks, n_blocks, scale):
    blk = SWA_BLOCK
    i = pl.program_id(2)
    q = q_ref[0]
    q4 = jnp.concatenate([q[:, g * blk:(g + 1) * blk] for g in range(SWA_GROUP)], axis=0)
    keys = jnp.concatenate([kp_ref[0], kc_ref[0], kn_ref[0], kx_ref[0]], axis=0)
    vals = jnp.concatenate([vp_ref[0], vc_ref[0], vn_ref[0], vx_ref[0]], axis=0)
    s = _nt(q4, keys) * scale
    nq, nk = s.shape
    r = lax.broadcasted_iota(jnp.int32, (nq, 1), 0) % blk
    col = lax.broadcasted_iota(jnp.int32, (1, nk), 1)
    is_ctx = i < n_ctx_blocks
    lo_s = jnp.where(is_ctx, nk, jnp.where(i == n_ctx_blocks, blk, 0))
    hi_s = jnp.where(is_ctx, -1, jnp.where(i == n_blocks - 1, 2 * blk - 1, 3 * blk - 1))
    lo = jnp.maximum(r, lo_s)
    hi = jnp.minimum(r + 2 * blk, hi_s)
    valid = ((col >= lo) & (col <= hi)) | (col >= 3 * blk)
    s = jnp.where(valid, s, NEG_BIG)
    sink = sink_ref[0, :, 0:1]
    m = jnp.maximum(jnp.max(s, axis=1, keepdims=True), sink)
    e = jnp.exp(s - m)
    den = jnp.sum(e, axis=1, keepdims=True) + jnp.exp(sink - m)
    o = _dot(e.astype(BF16), vals) / den
    o_ref[0] = jnp.concatenate([o[g * blk:(g + 1) * blk] for g in range(SWA_GROUP)], axis=1).astype(BF16)


def _swa_call(p0, sink_col, n_ctx_rows, col_q, col_k, col_v, hd):
    b, t, _ = p0.shape
    blk = SWA_BLOCK
    nb = t // blk
    ncb = n_ctx_rows // blk
    gq = SWA_GROUP * hd
    kblk = lambda colb, f: pl.BlockSpec((1, blk, hd), lambda bi, h, i: (bi, f(i), colb + h))
    prev = lambda i: jnp.clip(i - 1, ncb, nb - 1)
    cur = lambda i: i
    nxt = lambda i: jnp.clip(i + 1, ncb, nb - 1)
    ctx_spec = lambda colb: pl.BlockSpec((1, n_ctx_rows, hd), lambda bi, h, i: (bi, 0, colb + h))
    ck, cv = col_k // hd, col_v // hd
    return pl.pallas_call(
        functools.partial(_swa_kernel, n_ctx_blocks=ncb, n_blocks=nb, scale=float(hd) ** -0.5),
        grid=(b, SWA_KV_HEADS, nb),
        in_specs=[pl.BlockSpec((1, blk, gq), lambda bi, h, i: (bi, i, col_q // gq + h)),
                  kblk(ck, prev), kblk(ck, cur), kblk(ck, nxt), ctx_spec(ck),
                  kblk(cv, prev), kblk(cv, cur), kblk(cv, nxt), ctx_spec(cv),
                  pl.BlockSpec((1, SWA_GROUP * blk, LANES), lambda bi, h, i: (h, 0, 0))],
        out_specs=pl.BlockSpec((1, blk, gq), lambda bi, h, i: (bi, i, h)),
        out_shape=jax.ShapeDtypeStruct((b, t, SWA_HEADS * hd), BF16),
        compiler_params=_cparams("parallel", "parallel", "parallel"),
        name="swa",
    )(p0, p0, p0, p0, p0, p0, p0, p0, p0, sink_col)


def _diff_kernel(lam_ref, q_ref, k_ref, v_ref, g_ref, o_ref, s_ref, kx_ref, vx_ref,
                 *, row_off, n_qblocks, lambda_init):
    tq = DIFF_QBLOCK
    hd2 = 2 * DIFF_HD
    n_keys = k_ref.shape[1]
    lam = lam_ref[...]
    s01 = jnp.sum(lam[0:1] * lam[1:2], axis=1, keepdims=True)
    s23 = jnp.sum(lam[2:3] * lam[3:4], axis=1, keepdims=True)
    lam_full = jnp.exp(s01) - jnp.exp(s23) + lambda_init
    lane = lax.broadcasted_iota(jnp.int32, (1, hd2), 1)
    first = lane < DIFF_HD
    col0 = lane == 0
    zero = jnp.zeros((), BF16)

    def map_norms(x):
        x2 = x.astype(F32)
        x2 = x2 * x2
        return (jnp.sqrt(jnp.sum(jnp.where(first, x2, 0.0), axis=1, keepdims=True)),
                jnp.sqrt(jnp.sum(jnp.where(first, 0.0, x2), axis=1, keepdims=True)))

    kn0, kn1 = map_norms(k_ref[0])
    kmax0 = jnp.max(kn0, axis=0, keepdims=True)
    kmax1 = jnp.max(kn1, axis=0, keepdims=True)
    qn0, qn1 = map_norms(q_ref[0, row_off:, :])
    is_safe = jnp.maximum(jnp.max(qn0 * kmax0), jnp.max(qn1 * kmax1)) <= DIFF_SAFE_BOUND

    def q_block(j):
        q = q_ref[0, pl.ds(pl.multiple_of(row_off + j * tq, tq), tq), :]
        return q, jnp.where(first, q, zero), jnp.where(first, zero, q)

    def write_out(j, o):
        y = _rms(o) * g_ref[...] * (1.0 - lambda_init)
        o_ref[0, pl.ds(pl.multiple_of(j * tq, tq), tq), :] = y.astype(BF16)

    def scores_fast(j, slot):
        q, qa, qb = q_block(j)
        b0, b1 = map_norms(q)
        xa = jnp.where(col0, -(b0 * kmax0), 0.0).astype(BF16)
        xb = jnp.where(col0, -(b1 * kmax1), 0.0).astype(BF16)
        qq = jnp.concatenate([jnp.concatenate([qa, xa], axis=1), jnp.concatenate([qb, xb], axis=1)], axis=0)
        s_ref[slot] = _nt(qq, kx_ref[...])

    def finish_fast(j, slot):
        e = jnp.exp2(s_ref[slot]).astype(BF16)
        ov = _dot(e, vx_ref[...])
        p0 = ov[0:tq, 0:hd2] / ov[0:tq, hd2:2 * hd2]
        p1 = ov[tq:2 * tq, 0:hd2] / ov[tq:2 * tq, hd2:2 * hd2]
        write_out(j, p0 - lam_full * p1)

    def scores_max(j, slot):
        _, qa, qb = q_block(j)
        s_ref[slot] = _nt(jnp.concatenate([qa, qb], axis=0), k_ref[0])

    def finish_max(j, slot):
        s = s_ref[slot]
        e = jnp.exp2(s - jnp.max(s, axis=1, keepdims=True))
        z = jnp.sum(e, axis=1, keepdims=True)
        coef = lam_full * z[0:tq] / z[tq:2 * tq]
        a = e[0:tq] - coef * e[tq:2 * tq]
        write_out(j, _dot(a.astype(BF16), v_ref[0]) * (1.0 / z[0:tq]))

    def run(scores, finish):
        scores(0, 0)

        def body(jj, carry):
            j0 = 2 * jj
            scores(j0 + 1, 1)
            finish(j0, 0)
            scores(j0 + 2, 0)
            finish(j0 + 1, 1)
            return carry

        lax.fori_loop(0, n_qblocks // 2 - 1, body, 0)
        scores(n_qblocks - 1, 1)
        finish(n_qblocks - 2, 0)
        finish(n_qblocks - 1, 1)

    @pl.when(is_safe)
    def _():
        kx_ref[:, 0:hd2] = k_ref[0]
        kx_ref[:, hd2:2 * hd2] = jnp.broadcast_to(jnp.where(col0, 1.0, 0.0).astype(BF16), (n_keys, hd2))
        vx_ref[:, 0:hd2] = v_ref[0]
        vx_ref[:, hd2:2 * hd2] = jnp.ones((n_keys, hd2), BF16)
        run(scores_fast, finish_fast)

    @pl.when(jnp.logical_not(is_safe))
    def _():
        run(scores_max, finish_max)


def _diff_call(p1, lam, subln_g, n_ctx_rows, n_heads, lambda_init):
    b, t, _ = p1.shape
    s_len = t - n_ctx_rows
    hd2 = 2 * DIFF_HD
    n_qblocks = s_len // DIFF_QBLOCK
    assert n_qblocks >= 2 and n_qblocks % 2 == 0
    return pl.pallas_call(
        functools.partial(_diff_kernel, row_off=n_ctx_rows, n_qblocks=n_qblocks, lambda_init=lambda_init),
        grid=(b, n_heads),
        in_specs=[pl.BlockSpec((4, DIFF_HD), lambda bi, h: (0, 0)),
                  pl.BlockSpec((1, t, hd2), lambda bi, h: (bi, 0, h)),
                  pl.BlockSpec((1, t, hd2), lambda bi, h: (bi, 0, n_heads + h)),
                  pl.BlockSpec((1, t, hd2), lambda bi, h: (bi, 0, 2 * n_heads + h)),
                  pl.BlockSpec((1, hd2), lambda bi, h: (0, 0))],
        out_specs=pl.BlockSpec((1, s_len, hd2), lambda bi, h: (bi, 0, h)),
        out_shape=jax.ShapeDtypeStruct((b, s_len, n_heads * hd2), BF16),
        scratch_shapes=[pltpu.VMEM((2, 2 * DIFF_QBLOCK, t), F32), pltpu.VMEM((t, 2 * hd2), BF16),
                        pltpu.VMEM((t, 2 * hd2), BF16)],
        compiler_params=_cparams("parallel", "parallel"),
        name="diff_attn",
    )(lam, p1, p1, p1, subln_g.reshape(1, hd2))


def _outproj_kernel(*refs, n_y, n_row_refs, n_ctx_blocks, row_block_off):
    y_refs = refs[:n_y]
    w_refs = refs[n_y:2 * n_y]
    row_refs = refs[2 * n_y:2 * n_y + n_row_refs]
    mod_ref, g_ref, wr_ref, xn_ref, h_ref, aff_ref = refs[2 * n_y + n_row_refs:]
    acc = _dot(y_refs[0][0], w_refs[0][...])
    for yr, wr in zip(y_refs[1:], w_refs[1:]):
        acc = acc + _dot(yr[0], wr[...])
    xn = _load_rows(row_refs, n_ctx_blocks, row_block_off) + mod_ref[0, 0, 2:3, :] * acc
    xn_ref[0] = xn
    h = _rms(xn) * g_ref[...] * (1.0 + mod_ref[0, 0, 4:5, :]) + mod_ref[0, 0, 3:4, :]
    h_ref[0] = h.astype(BF16)
    h_hi, h_lo = _split2(h)
    w_hi, w_lo = _split2(wr_ref[...])
    logits = _nt(w_hi, h_hi) + (_nt(w_lo, h_hi) + _nt(w_hi, h_lo))
    ex = jnp.exp(logits - jnp.max(logits, axis=0, keepdims=True))
    aff_ref[0] = ex / jnp.sum(ex, axis=0, keepdims=True)


def _outproj_call(ys, ws, stream, mod, g, w_router_t, row_block_off, n_rows, n_ctx_blocks):
    b, d = mod.shape[0], mod.shape[3]
    n_y = len(ys)
    e = w_router_t.shape[0]
    row_specs, row_args = _row_specs(stream, n_ctx_blocks, row_block_off)
    in_specs = ([pl.BlockSpec((1, ROW_TILE, y.shape[2]), lambda bi, i: (bi, i, 0)) for y in ys]
                + [pl.BlockSpec(w.shape, lambda bi, i: (0, 0), pipeline_mode=pl.Buffered(1)) for w in ws]
                + row_specs
                + [pl.BlockSpec((1, 1, 6, d),
                                lambda bi, i: (bi, jnp.where(i + row_block_off < n_ctx_blocks, 0, 1), 0, 0)),
                   pl.BlockSpec((1, d), lambda bi, i: (0, 0)),
                   pl.BlockSpec((e, d), lambda bi, i: (0, 0))])
    return pl.pallas_call(
        functools.partial(_outproj_kernel, n_y=n_y, n_row_refs=len(row_args), n_ctx_blocks=n_ctx_blocks,
                          row_block_off=row_block_off),
        grid=(b, n_rows // ROW_TILE),
        in_specs=in_specs,
        out_specs=[pl.BlockSpec((1, ROW_TILE, d), lambda bi, i: (bi, i, 0)),
                   pl.BlockSpec((1, ROW_TILE, d), lambda bi, i: (bi, i, 0)),
                   pl.BlockSpec((1, e, ROW_TILE), lambda bi, i: (bi, 0, i))],
        out_shape=[jax.ShapeDtypeStruct((b, n_rows, d), F32),
                   jax.ShapeDtypeStruct((b, n_rows, d), BF16),
                   jax.ShapeDtypeStruct((b, e, n_rows), F32)],
        compiler_params=_cparams("parallel", "parallel"),
        name="out_proj",
    )(*ys, *ws, *row_args, mod, g.reshape(1, d), w_router_t)


def _prefix_lanes(m, upper):
    e, n = m.shape
    carry = jnp.zeros((e, 1), F32)
    outs = []
    for blk in range(n // LANES):
        mb = m[:, blk * LANES:(blk + 1) * LANES]
        inc = _dot(mb.astype(BF16), upper)
        outs.append(inc - mb + carry)
        carry = carry + jnp.sum(mb, axis=1, keepdims=True)
    return jnp.concatenate(outs, axis=1)


def _transpose_exact(x, eye):
    hi, mid, lo = _split3(x)
    return _nt(eye, hi) + (_nt(eye, mid) + _nt(eye, lo))


def _route_kernel(aff_ref, pos_ref, gate_ref, posn_ref, *, segs):
    n_exp = aff_ref.shape[1]
    li = lax.broadcasted_iota(jnp.int32, (LANES, LANES), 0)
    lj = lax.broadcasted_iota(jnp.int32, (LANES, LANES), 1)
    upper = jnp.where(li <= lj, 1.0, 0.0).astype(BF16)
    eye = jnp.where(li == lj, 1.0, 0.0).astype(BF16)
    for (off, n, cap, slot_off) in segs:
        a = aff_ref[0, :, off:off + n]
        bits = pltpu.bitcast(a, jnp.int32)

        def search(it, thr):
            cand = thr | jnp.left_shift(jnp.int32(1), 30 - it)
            cnt = jnp.sum(jnp.where(bits >= cand, 1.0, 0.0), axis=1, keepdims=True)
            return jnp.where(cnt >= cap, cand, thr)

        thr = lax.fori_loop(0, 31, search, jnp.zeros((n_exp, 1), jnp.int32))
        gt = jnp.where(bits > thr, 1.0, 0.0)
        eq = jnp.where(bits == thr, 1.0, 0.0)
        need = cap - jnp.sum(gt, axis=1, keepdims=True)
        sel = gt + eq * jnp.where(_prefix_lanes(eq, upper) < need, 1.0, 0.0)
        slot = _prefix_lanes(sel, upper)
        pos = jnp.where(sel > 0.0, slot + slot_off, -1.0)
        pos_ref[0, :, off:off + n] = pos.astype(jnp.int32)
        gate_ref[0, :, off:off + n] = sel * a
        for blk in range(n // LANES):
            cols = slice(blk * LANES, (blk + 1) * LANES)
            rows = slice(off + blk * LANES, off + (blk + 1) * LANES)
            posn_ref[0, rows, :] = _transpose_exact(pos[:, cols], eye).astype(jnp.int32)


def _route_call(aff_t, segs):
    b, e, t = aff_t.shape
    return pl.pallas_call(
        functools.partial(_route_kernel, segs=tuple(segs)),
        grid=(b,),
        in_specs=[pl.BlockSpec((1, e, t), lambda bi: (bi, 0, 0))],
        out_specs=[pl.BlockSpec((1, e, t), lambda bi: (bi, 0, 0)),
                   pl.BlockSpec((1, e, t), lambda bi: (bi, 0, 0)),
                   pl.BlockSpec((1, t, e), lambda bi: (bi, 0, 0))],
        out_shape=[jax.ShapeDtypeStruct((b, e, t), jnp.int32),
                   jax.ShapeDtypeStruct((b, e, t), F32),
                   jax.ShapeDtypeStruct((b, t, e), jnp.int32)],
        compiler_params=_cparams("parallel"),
        name="route",
    )(aff_t)


def _gather_kernel(pos_ref, gate_ref, h_ref, o_ref, g_ref, *, segs):
    e = pl.program_id(1)
    prow = pos_ref[0, pl.ds(e, 1), :]
    grow = gate_ref[0, pl.ds(e, 1), :]
    for (off, n, cap, slot_off) in segs:
        slots = lax.broadcasted_iota(jnp.int32, (cap, 1), 0) + slot_off
        hit = prow[:, off:off + n] == slots
        o_ref[0, 0, slot_off:slot_off + cap, :] = _dot(jnp.where(hit, 1.0, 0.0).astype(BF16),
                                                       h_ref[0, off:off + n, :]).astype(BF16)
        gsel = jnp.sum(jnp.where(hit, grow[:, off:off + n], 0.0), axis=1, keepdims=True)
        g_ref[0, 0, slot_off:slot_off + cap, :] = jnp.broadcast_to(gsel, (cap, LANES))


def _gather_call(pos, gate, h, segs, n_slots):
    b, e, t = pos.shape
    d = h.shape[2]
    return pl.pallas_call(
        functools.partial(_gather_kernel, segs=tuple(segs)),
        grid=(b, e),
        in_specs=[pl.BlockSpec((1, e, t), lambda bi, ei: (bi, 0, 0)),
                  pl.BlockSpec((1, e, t), lambda bi, ei: (bi, 0, 0)),
                  pl.BlockSpec((1, t, d), lambda bi, ei: (bi, 0, 0))],
        out_specs=[pl.BlockSpec((1, 1, n_slots, d), lambda bi, ei: (ei, bi, 0, 0)),
                   pl.BlockSpec((1, 1, n_slots, LANES), lambda bi, ei: (ei, bi, 0, 0))],
        out_shape=[jax.ShapeDtypeStruct((e, b, n_slots, d), BF16),
                   jax.ShapeDtypeStruct((e, b, n_slots, LANES), F32)],
        compiler_params=_cparams("parallel", "arbitrary"),
        name="gather",
    )(pos, gate, h)


def _ffn_up_kernel(x_ref, wg_ref, wu_ref, o_ref, wgb_ref, wub_ref):
    @pl.when(pl.program_id(2) == 0)
    def _():
        wgb_ref[...] = wg_ref[0, 0].astype(BF16)
        wub_ref[...] = wu_ref[0, 0].astype(BF16)

    x = x_ref[0]
    g = _dot(x, wgb_ref[...])
    u = _dot(x, wub_ref[...])
    o_ref[0] = (_silu(g) * u).astype(BF16)


def _ffn_down_kernel(a_ref, w_ref, gate_ref, o_ref, wb_ref):
    @pl.when(pl.program_id(2) == 0)
    def _():
        wb_ref[...] = w_ref[0, 0].astype(BF16)

    gate = jnp.concatenate([gate_ref[0]] * (o_ref.shape[2] // LANES), axis=1)
    o_ref[0] = (_dot(a_ref[0], wb_ref[...]) * gate).astype(BF16)


def _ffn_call(xs, gates, w_gu, w_dn, layer):
    e, m, d = xs.shape
    f = w_dn.shape[2]
    tm = _pick(m, (1152, 1024, 768, 512, 256, 128, 64, 32, 16, 8))
    tf = _pick(f, (512, 256, 128))
    nf = f // tf
    act = pl.pallas_call(
        _ffn_up_kernel,
        grid=(e, nf, m // tm),
        in_specs=[pl.BlockSpec((1, tm, d), lambda ei, j, i: (ei, i, 0)),
                  pl.BlockSpec((1, 1, d, tf), lambda ei, j, i: (layer, ei, 0, j)),
                  pl.BlockSpec((1, 1, d, tf), lambda ei, j, i: (layer, ei, 0, j + nf))],
        out_specs=pl.BlockSpec((1, tm, tf), lambda ei, j, i: (ei, i, j)),
        out_shape=jax.ShapeDtypeStruct((e, m, f), BF16),
        scratch_shapes=[pltpu.VMEM((d, tf), BF16), pltpu.VMEM((d, tf), BF16)],
        compiler_params=_cparams("parallel", "arbitrary", "arbitrary"),
        name="ffn_up",
    )(xs, w_gu, w_gu)
    tn = _pick(d, (512, 256, 128))
    return pl.pallas_call(
        _ffn_down_kernel,
        grid=(e, d // tn, m // tm),
        in_specs=[pl.BlockSpec((1, tm, f), lambda ei, j, i: (ei, i, 0)),
                  pl.BlockSpec((1, 1, f, tn), lambda ei, j, i: (layer, ei, 0, j)),
                  pl.BlockSpec((1, tm, LANES), lambda ei, j, i: (ei, i, 0))],
        out_specs=pl.BlockSpec((1, tm, tn), lambda ei, j, i: (ei, i, j)),
        out_shape=jax.ShapeDtypeStruct((e, m, d), BF16),
        scratch_shapes=[pltpu.VMEM((f, tn), BF16)],
        compiler_params=_cparams("parallel", "arbitrary", "arbitrary"),
        name="ffn_down",
    )(act, w_dn, gates)


def _combine_kernel(posn_ref, y_ref, x_ref, mod_ref, gf_ref, o_ref, *, cap_lat, cap_ctx, n_ctx_tiles, final_norm):
    n_exp = posn_ref.shape[2]
    lane = lax.broadcasted_iota(jnp.int32, (1, n_exp), 1)
    posn = posn_ref[0].astype(F32)

    def scatter(slot0, cap):
        slots = lax.broadcasted_iota(jnp.int32, (1, cap), 1).astype(F32) + slot0
        hots, rows = [], []
        for e in range(n_exp):
            pcol = jnp.sum(jnp.where(lane == e, posn, 0.0), axis=1, keepdims=True)
            hots.append(jnp.where(pcol == slots, 1.0, 0.0).astype(BF16))
            rows.append(y_ref[e, 0, slot0:slot0 + cap, :])
        return _dot(jnp.concatenate(hots, axis=1), jnp.concatenate(rows, axis=0))

    def finish(total, g2):
        out = x_ref[0] + g2 * total
        if final_norm:
            out = _rms(out) * gf_ref[...]
        o_ref[0] = out

    if n_ctx_tiles:
        i = pl.program_id(1)

        @pl.when(i < n_ctx_tiles)
        def _():
            finish(scatter(cap_lat, cap_ctx), mod_ref[0, 0, 5:6, :])

        @pl.when(i >= n_ctx_tiles)
        def _():
            finish(scatter(0, cap_lat), mod_ref[0, 1, 5:6, :])
    else:
        finish(scatter(0, cap_lat), mod_ref[0, 1, 5:6, :])


def _combine_call(posn, y, xa, mod, g_final, cap_lat, cap_ctx, n_ctx_rows, final_norm):
    b, t, d = xa.shape
    e = posn.shape[2]
    n_slots = y.shape[2]
    tt = ROW_TILE if n_ctx_rows else _pick(t, (512, 256))
    assert n_ctx_rows % tt == 0
    return pl.pallas_call(
        functools.partial(_combine_kernel, cap_lat=cap_lat, cap_ctx=cap_ctx, n_ctx_tiles=n_ctx_rows // tt,
                          final_norm=final_norm),
        grid=(b, t // tt),
        in_specs=[pl.BlockSpec((1, tt, e), lambda bi, i: (bi, i, 0)),
                  pl.BlockSpec((e, 1, n_slots, d), lambda bi, i: (0, bi, 0, 0), pipeline_mode=pl.Buffered(1)),
                  pl.BlockSpec((1, tt, d), lambda bi, i: (bi, i, 0)),
                  pl.BlockSpec((1, 2, 6, d), lambda bi, i: (bi, 0, 0, 0)),
                  pl.BlockSpec((1, d), lambda bi, i: (0, 0))],
        out_specs=pl.BlockSpec((1, tt, d), lambda bi, i: (bi, i, 0)),
        out_shape=jax.ShapeDtypeStruct((b, t, d), F32),
        compiler_params=_cparams("parallel", "arbitrary"),
        name="combine",
    )(posn, y, xa, mod, g_final.reshape(1, d))


def _rope_tables(n_tokens, n_ctx_rows, head_dim):
    rows = n_tokens // GRID_W
    r = jnp.repeat(jnp.arange(rows, dtype=F32), GRID_W)
    col = jnp.tile(jnp.arange(GRID_W, dtype=F32), rows)
    axis_dim = head_dim // 2
    inv = ROPE_THETA ** (-jnp.arange(0, axis_dim, 2, dtype=F32) / axis_dim)
    ar, ac = r[:, None] * inv, col[:, None] * inv
    cos_p = jnp.concatenate([jnp.cos(ar), jnp.cos(ar), jnp.cos(ac), jnp.cos(ac)], axis=-1)
    sin_p = jnp.concatenate([-jnp.sin(ar), jnp.sin(ar), -jnp.sin(ac), jnp.sin(ac)], axis=-1)
    reps = LANES // head_dim
    cos_p, sin_p = jnp.tile(cos_p, (1, reps)), jnp.tile(sin_p, (1, reps))
    cos_t = jnp.concatenate([jnp.ones((n_ctx_rows, LANES), F32), cos_p], axis=0)
    sin_t = jnp.concatenate([jnp.zeros((n_ctx_rows, LANES), F32), sin_p], axis=0)
    return cos_t, sin_t


def _moe(h2, aff_t, x_res, mod, g_final, w_gu, w_dn, layer, segs, cap_lat, cap_ctx, n_ctx_rows, final_norm):
    b, t, d = h2.shape
    e = aff_t.shape[1]
    n_slots = cap_lat + cap_ctx
    pos, gate, posn = _route_call(aff_t, segs)
    xs, gs = _gather_call(pos, gate, h2, segs, n_slots)
    y = _ffn_call(xs.reshape(e, b * n_slots, d), gs.reshape(e, b * n_slots, LANES), w_gu, w_dn, layer)
    return _combine_call(posn, y.reshape(e, b, n_slots, d), x_res, mod, g_final, cap_lat, cap_ctx, n_ctx_rows,
                         final_norm)


def kernel(x, c, ctx, c_ctx, w_ada, b_ada, g_norm_mix, g_norm_ffn, w_in_even, gla_gate_w, gla_gate_b, gla_norm_g, swa_sink, w_out_even, w_qkv_odd, diff_lambda, diff_subln_g, w_out_odd, w_router, w_gate_up, w_down, g_final):
    b, s_len, d = x.shape
    lc = ctx.shape[1]
    depth = w_ada.shape[0]
    assert depth == 2 and lc % ROW_TILE == 0 and s_len % ROW_TILE == 0 and s_len % GRID_W == 0
    t = lc + s_len
    ncb = lc // ROW_TILE
    gla_dk, gla_dv = d // 4 // GLA_HEADS, d // 2 // GLA_HEADS
    swa_hd = d // 2 // SWA_HEADS
    diff_heads = d // (2 * DIFF_HD)
    n_exp = w_router.shape[2]
    assert gla_dk == LANES and swa_hd == LANES and 2 * DIFF_HD == LANES

    n_rows = -(-(b + 1) // 8) * 8
    cc = jnp.concatenate([c, c_ctx[None, :], jnp.zeros((n_rows - b - 1, d), F32)], axis=0)
    m_all = _ada_call(cc, w_ada, b_ada).reshape(depth, n_rows, 6, d)

    def mod_for(layer):
        lat = m_all[layer, :b]
        ctxm = jnp.broadcast_to(m_all[layer, b][None], (b, 6, d))
        return jnp.stack([ctxm, lat], axis=1)

    xa = (ctx, x)

    mod0 = mod_for(0)
    w_in = w_in_even[0]
    sizes = (GLA_HEADS * gla_dk, GLA_HEADS * gla_dk, GLA_HEADS * gla_dv, GLA_HEADS * gla_dv,
             GLA_GATE_RANK, GLA_GATE_RANK, SWA_HEADS * swa_hd, SWA_KV_HEADS * swa_hd, SWA_KV_HEADS * swa_hd)
    offs = np.concatenate([[0], np.cumsum(sizes)])
    seg = lambda i: w_in[:, offs[i]:offs[i + 1]]
    w_main = jnp.concatenate([seg(0), seg(1), seg(2), seg(3), seg(6), seg(7), seg(8)], axis=1).astype(BF16)
    w_rank = jnp.concatenate([seg(4), seg(5), jnp.zeros((d, LANES - 2 * GLA_GATE_RANK), F32)],
                             axis=1).astype(BF16)
    col_aq, col_ak = 0, sizes[0]
    col_av = col_ak + sizes[1]
    col_ag = col_av + sizes[2]
    col_bq = col_ag + sizes[3]
    col_bk = col_bq + sizes[6]
    col_bv = col_bk + sizes[7]
    n_main = col_bv + sizes[8]
    rope0 = [(col_bq <= ch * COL_CHUNK < col_bv, 1.0) for ch in range(n_main // COL_CHUNK)]
    cos_b, sin_b = _rope_tables(s_len, lc, swa_hd)
    p0, r0 = _proj_call(xa, mod0, g_norm_mix[0], w_main, cos_b, sin_b, rope0, swa_hd // 4, ncb, wr=w_rank)

    gw = gla_gate_w[0]
    gw_pad = jnp.zeros((2, LANES, GLA_HEADS * gla_dk), F32)
    gw_pad = gw_pad.at[0, 0:GLA_GATE_RANK].set(gw[0]).at[1, GLA_GATE_RANK:2 * GLA_GATE_RANK].set(gw[1])
    a_out = _gla_call(p0, r0, gw_pad.astype(BF16), gla_gate_b[0].reshape(2, 1, -1), gla_norm_g[0],
                      _gla_constants(), lc, col_aq, col_ak, col_av, col_ag, gla_dk, gla_dv)
    sink_col = jnp.broadcast_to(
        jnp.repeat(swa_sink[0].reshape(SWA_KV_HEADS, SWA_GROUP), SWA_BLOCK, axis=1)[:, :, None],
        (SWA_KV_HEADS, SWA_GROUP * SWA_BLOCK, LANES))
    b_out = _swa_call(p0, sink_col, lc, col_bq, col_bk, col_bv, swa_hd)

    w_o = w_out_even[0].astype(BF16)
    n_a = GLA_HEADS * gla_dv
    w_router_t0 = jnp.transpose(w_router[0])
    xa1, h2, aff_t = _outproj_call([a_out, b_out], [w_o[:n_a], w_o[n_a:]], xa, mod0, g_norm_ffn[0],
                                   w_router_t0, 0, t, ncb)
    cap_lat = s_len * EC_FACTOR // n_exp
    cap_ctx = lc * EC_FACTOR // n_exp
    segs0 = [(lc, s_len, cap_lat, 0), (0, lc, cap_ctx, cap_lat)]
    xa2 = _moe(h2, aff_t, xa1, mod0, g_final, w_gate_up, w_down, 0, segs0, cap_lat, cap_ctx, lc, False)

    mod1 = mod_for(1)
    lambda_init = 0.8 - 0.6 * float(np.exp(-0.3 * 1))
    w_qkv = w_qkv_odd[0].astype(BF16)
    n_qkv = w_qkv.shape[1]
    q_scale = DIFF_HD ** -0.5 * float(np.log2(np.e))
    rope1 = [(ch * COL_CHUNK < 2 * d, q_scale if ch * COL_CHUNK < d else 1.0)
             for ch in range(n_qkv // COL_CHUNK)]
    cos_c, sin_c = _rope_tables(s_len, lc, DIFF_HD)
    (p1,) = _proj_call(xa2, mod1, g_norm_mix[1], w_qkv, cos_c, sin_c, rope1, DIFF_HD // 4, ncb)
    y1 = _diff_call(p1, diff_lambda[0], diff_subln_g[0], lc, diff_heads, lambda_init)
    x3, h2b, aff_tb = _outproj_call([y1], [w_out_odd[0].astype(BF16)], xa2, mod1, g_norm_ffn[1],
                                    jnp.transpose(w_router[1]), ncb, s_len, ncb)
    segs1 = [(0, s_len, cap_lat, 0)]
    return _moe(h2b, aff_tb, x3, mod1, g_final, w_gate_up, w_down, 1, segs1, cap_lat, 0, 0, True)
```

```python
import functools

import numpy as np
import jax
import jax.numpy as jnp
from jax import lax
from jax.experimental import pallas as pl
from jax.experimental.pallas import tpu as pltpu

F32 = jnp.float32
BF16 = jnp.bfloat16

GRID_W = 64
ROPE_THETA = 10000.0
NORM_EPS = 1e-6
GLA_HEADS = 4
GLA_GATE_RANK = 16
GLA_GATE_NORM = 16.0
SWA_HEADS = 8
SWA_KV_HEADS = 2
SWA_GROUP = SWA_HEADS // SWA_KV_HEADS
SWA_BLOCK = 128
DIFF_HD = 64
N_EXPERTS = 16
EC_FACTOR = 2

LANES = 128
ROW_TILE = 256
COL_CHUNK = 256
GLA_CHUNK = 128
GLA_LEVELS = 7
GLA_SAFE_RANGE = 60.0
DIFF_QBLOCK = 256
DIFF_SAFE_BOUND = 60.0
VMEM_LIMIT = 56 * 1024 * 1024
NEG_BIG = -1e30


def _cparams(*sem):
    return pltpu.CompilerParams(dimension_semantics=sem, vmem_limit_bytes=VMEM_LIMIT)


def _pick(n, cands):
    for c in cands:
        if n % c == 0:
            return c
    raise ValueError(f"no tile for {n} in {cands}")


def _nt(a, b):
    return lax.dot_general(a, b, (((1,), (1,)), ((), ())), preferred_element_type=F32)


def _tn(a, b):
    return lax.dot_general(a, b, (((0,), (0,)), ((), ())), preferred_element_type=F32)


def _dot(a, b):
    return jnp.dot(a, b, preferred_element_type=F32)


def _split3(a):
    hi = a.astype(BF16)
    r1 = a - hi.astype(F32)
    mid = r1.astype(BF16)
    lo = (r1 - mid.astype(F32)).astype(BF16)
    return hi, mid, lo


def _split2(a):
    hi = a.astype(BF16)
    lo = (a - hi.astype(F32)).astype(BF16)
    return hi, lo


def _silu(x):
    return x * (1.0 / (1.0 + jnp.exp(-x)))


def _rms(x, eps=NORM_EPS):
    return x * lax.rsqrt(jnp.mean(x * x, axis=-1, keepdims=True) + eps)


def _row_specs(stream, n_ctx_blocks, off=0):
    if isinstance(stream, tuple):
        ctx, x = stream
        d = x.shape[2]
        return ([pl.BlockSpec((1, ROW_TILE, d), lambda bi, i: (bi, jnp.minimum(i + off, n_ctx_blocks - 1), 0)),
                 pl.BlockSpec((1, ROW_TILE, d), lambda bi, i: (bi, jnp.maximum(i + off - n_ctx_blocks, 0), 0))],
                [ctx, x])
    d = stream.shape[2]
    return [pl.BlockSpec((1, ROW_TILE, d), lambda bi, i: (bi, i + off, 0))], [stream]


def _load_rows(row_refs, n_ctx_blocks, off=0):
    if len(row_refs) == 2:
        return jnp.where(pl.program_id(1) + off < n_ctx_blocks, row_refs[0][0], row_refs[1][0])
    return row_refs[0][0]


def _ada_kernel(c_ref, w_ref, b_ref, o_ref):
    s = _silu(c_ref[...])
    s_hi, s_lo = _split2(s)
    w_hi, w_lo = _split2(w_ref[0])
    acc = _dot(s_hi, w_hi) + (_dot(s_lo, w_hi) + _dot(s_hi, w_lo))
    o_ref[0] = acc + b_ref[0]


def _ada_call(cc, w_ada, b_ada):
    depth, d, n = w_ada.shape
    r = cc.shape[0]
    tn = _pick(n, (768, 512, 256, 128))
    return pl.pallas_call(
        _ada_kernel,
        grid=(depth, n // tn),
        in_specs=[pl.BlockSpec((r, d), lambda l, j: (0, 0)),
                  pl.BlockSpec((1, d, tn), lambda l, j: (l, 0, j)),
                  pl.BlockSpec((1, 1, tn), lambda l, j: (l, 0, j))],
        out_specs=pl.BlockSpec((1, r, tn), lambda l, j: (l, 0, j)),
        out_shape=jax.ShapeDtypeStruct((depth, r, n), F32),
        compiler_params=_cparams("parallel", "parallel"),
        name="adaln",
    )(cc, w_ada, b_ada.reshape(depth, 1, n))


def _proj_kernel(*refs, n_row_refs, n_ctx_blocks, chunk_ops, quarter, has_r):
    row_refs = refs[:n_row_refs]
    mod_ref, g_ref, w_ref, cos_ref, sin_ref = refs[n_row_refs:n_row_refs + 5]
    rest = refs[n_row_refs + 5:]
    if has_r:
        wr_ref, o_ref, r_ref = rest
    else:
        (o_ref,) = rest
    x = _load_rows(row_refs, n_ctx_blocks)
    y = _rms(x) * g_ref[...]
    h = (y * (1.0 + mod_ref[0, 0, 1:2, :]) + mod_ref[0, 0, 0:1, :]).astype(BF16)
    reps = COL_CHUNK // LANES
    cos_t = jnp.concatenate([cos_ref[...]] * reps, axis=1)
    sin_t = jnp.concatenate([sin_ref[...]] * reps, axis=1)
    lane = lax.broadcasted_iota(jnp.int32, (1, COL_CHUNK), 1)
    first = (lane % (2 * quarter)) < quarter
    for c, (rope, scale) in enumerate(chunk_ops):
        acc = _dot(h, w_ref[:, c * COL_CHUNK:(c + 1) * COL_CHUNK])
        if rope:
            partner = jnp.where(first, pltpu.roll(acc, COL_CHUNK - quarter, 1), pltpu.roll(acc, quarter, 1))
            acc = acc * cos_t + partner * sin_t
        if scale != 1.0:
            acc = acc * scale
        o_ref[0, :, c * COL_CHUNK:(c + 1) * COL_CHUNK] = acc.astype(BF16)
    if has_r:
        r_ref[0] = _dot(h, wr_ref[...]).astype(BF16)


def _proj_call(stream, mod, g, w, cos_t, sin_t, chunk_ops, quarter, n_ctx_blocks, wr=None):
    d, n = w.shape
    t = cos_t.shape[0]
    b = mod.shape[0]
    assert n == len(chunk_ops) * COL_CHUNK and t % ROW_TILE == 0
    has_r = wr is not None
    row_specs, row_args = _row_specs(stream, n_ctx_blocks)
    in_specs = row_specs + [
        pl.BlockSpec((1, 1, 6, d), lambda bi, i: (bi, jnp.where(i < n_ctx_blocks, 0, 1), 0, 0)),
        pl.BlockSpec((1, d), lambda bi, i: (0, 0)),
        pl.BlockSpec((d, n), lambda bi, i: (0, 0), pipeline_mode=pl.Buffered(1)),
        pl.BlockSpec((ROW_TILE, LANES), lambda bi, i: (i, 0)),
        pl.BlockSpec((ROW_TILE, LANES), lambda bi, i: (i, 0))]
    args = row_args + [mod, g.reshape(1, d), w, cos_t, sin_t]
    out_specs = [pl.BlockSpec((1, ROW_TILE, n), lambda bi, i: (bi, i, 0))]
    out_shape = [jax.ShapeDtypeStruct((b, t, n), BF16)]
    if has_r:
        in_specs.append(pl.BlockSpec((d, LANES), lambda bi, i: (0, 0)))
        args.append(wr)
        out_specs.append(pl.BlockSpec((1, ROW_TILE, LANES), lambda bi, i: (bi, i, 0)))
        out_shape.append(jax.ShapeDtypeStruct((b, t, LANES), BF16))
    return pl.pallas_call(
        functools.partial(_proj_kernel, n_row_refs=len(row_args), n_ctx_blocks=n_ctx_blocks,
                          chunk_ops=tuple(chunk_ops), quarter=quarter, has_r=has_r),
        grid=(b, t // ROW_TILE),
        in_specs=in_specs, out_specs=out_specs, out_shape=out_shape,
        compiler_params=_cparams("parallel", "parallel"),
        name="norm_proj",
    )(*args)


def _gla_constants():
    c = GLA_CHUNK
    idx = np.arange(c)
    cm = np.zeros((2, GLA_LEVELS + 2, c, c), np.float32)
    lmask = np.zeros((2, GLA_LEVELS + 2, c, c), np.float32)
    cm[0, 0] = (idx[None, :] <= idx[:, None])
    cm[1, 0] = (idx[None, :] >= idx[:, None])
    for l in range(GLA_LEVELS):
        s = c >> (l + 1)
        blk = idx // (2 * s)
        second = (idx % (2 * s)) >= s
        same = blk[:, None] == blk[None, :]
        sep_f = blk * 2 * s + s - 1
        cm[0, 1 + l] = (idx[None, :] <= sep_f[:, None])
        lmask[0, l] = same & second[:, None] & (~second)[None, :]
        sep_b = blk * 2 * s + s
        cm[1, 1 + l] = (idx[None, :] >= sep_b[:, None])
        lmask[1, l] = same & (~second)[:, None] & second[None, :]
    cm[:, GLA_LEVELS + 1] = 1.0
    lmask[:, GLA_LEVELS] = np.eye(c)
    lmask[:, GLA_LEVELS + 1] = cm[:, 0]
    return jnp.asarray(cm.reshape(2, (GLA_LEVELS + 2) * c, c), BF16), jnp.asarray(lmask, F32)


def _gla_kernel(q_ref, k_ref, v_ref, ag_ref, r_ref, gw_ref, gb_ref, ng_ref, cm_ref, lm_ref,
                o_ref, la_ref, of_ref, ob_ref, qe_ref, kd_ref, dec_ref, st_ref, *, n_ctx_chunks, n_chunks, q_scale):
    c = GLA_CHUNK
    r = r_ref[0]
    for d in range(2):
        z = _dot(r, gw_ref[d]) + gb_ref[d]
        la_ref[d] = (jnp.minimum(z, 0.0) - jnp.log(1.0 + jnp.exp(-jnp.abs(z)))) * (1.0 / GLA_GATE_NORM)
    st_ref[...] = jnp.zeros_like(st_ref)
    out_refs = (of_ref, ob_ref)

    def chunk_rows(ci):
        return pl.ds(pl.multiple_of(ci * c, c), c)

    min_tot = None
    for d in range(2):
        m = jnp.min(jnp.sum(la_ref[d].reshape(n_chunks, c, la_ref.shape[2]), axis=1))
        min_tot = m if min_tot is None else jnp.minimum(min_tot, m)
    is_safe = min_tot >= -GLA_SAFE_RANGE

    def prep_fast(ci, carry):
        rows = chunk_rows(ci)
        q = q_ref[0, rows, :].astype(F32) * q_scale
        k = k_ref[0, rows, :].astype(F32)
        v = v_ref[0, rows, :]
        for d in range(2):
            g_hi, g_lo = _split2(la_ref[d, rows, :])
            cm = cm_ref[d, 0:c, :]
            bc = _dot(cm, g_hi) + _dot(cm, g_lo)
            tot = bc[c - 1:c] if d == 0 else bc[0:1]
            qe = (q * jnp.exp(bc)).astype(BF16)
            kin = k * jnp.exp(-bc)
            att = lm_ref[d, GLA_LEVELS + 1] * _nt(qe, kin.astype(BF16))
            out_refs[d][rows, :] = _dot(att.astype(BF16), v)
            qe_ref[d, rows, :] = qe
            dec = jnp.exp(tot)
            kd_ref[d, rows, :] = (kin * dec).astype(BF16)
            dec_ref[d, pl.ds(pl.multiple_of(ci * 8, 8), 8), :] = jnp.broadcast_to(dec, (8, dec.shape[1]))
        return carry

    def prep(ci, carry):
        rows = chunk_rows(ci)
        q = q_ref[0, rows, :].astype(F32) * q_scale
        k = k_ref[0, rows, :].astype(F32)
        v = v_ref[0, rows, :]
        qk = _nt(q.astype(BF16), k.astype(BF16))
        for d in range(2):
            g_hi, g_lo = _split2(la_ref[d, rows, :])
            cm = cm_ref[d]
            cum = _dot(cm, g_hi) + _dot(cm, g_lo)
            bc = cum[0:c]
            tot = cum[(GLA_LEVELS + 1) * c:(GLA_LEVELS + 2) * c]
            att = lm_ref[d, GLA_LEVELS] * qk
            for l in range(GLA_LEVELS):
                ref_l = cum[(1 + l) * c:(2 + l) * c]
                eq = jnp.exp(jnp.minimum(bc - ref_l, 0.0))
                ek = jnp.exp(jnp.minimum(ref_l - bc, 0.0))
                att = att + lm_ref[d, l] * _nt((q * eq).astype(BF16), (k * ek).astype(BF16))
            out_refs[d][rows, :] = _dot(att.astype(BF16), v)
            qe_ref[d, rows, :] = (q * jnp.exp(bc)).astype(BF16)
            kd_ref[d, rows, :] = (k * jnp.exp(tot - bc)).astype(BF16)
            dec_ref[d, pl.ds(pl.multiple_of(ci * 8, 8), 8), :] = jnp.exp(tot[0:8])
        return carry

    @pl.when(is_safe)
    def _():
        lax.fori_loop(0, n_chunks, prep_fast, 0, unroll=2)

    @pl.when(jnp.logical_not(is_safe))
    def _():
        lax.fori_loop(0, n_chunks, prep, 0)

    def scan(t, carry):
        cb = jnp.where(t < n_ctx_chunks, n_ctx_chunks - 1 - t, n_chunks - 1 - (t - n_ctx_chunks))
        for d, ci in ((0, t), (1, cb)):
            rows = chunk_rows(ci)
            st = st_ref[d]
            out_refs[d][rows, :] += _nt(qe_ref[d, rows, :], st.astype(BF16))
            dec = dec_ref[d, pl.ds(pl.multiple_of(ci * 8, 8), 1), :]
            st_ref[d] = st * dec + _tn(v_ref[0, rows, :], kd_ref[d, rows, :])
        return carry

    lax.fori_loop(0, n_chunks, scan, 0)
    o = of_ref[...] + ob_ref[...]
    ag = ag_ref[0].astype(F32)
    o_ref[0] = (_rms(o) * ng_ref[...] * _silu(ag)).astype(BF16)


def _gla_call(p0, r, gw_pad, gb, norm_g, consts, n_ctx_rows, col_q, col_k, col_v, col_g, dk, dv):
    b, t, _ = p0.shape
    n_chunks = t // GLA_CHUNK
    cm, lm = consts
    kern = functools.partial(_gla_kernel, n_ctx_chunks=n_ctx_rows // GLA_CHUNK, n_chunks=n_chunks,
                             q_scale=float(dk) ** -0.5)
    full = lambda shape: pl.BlockSpec(shape, lambda bi, h: (0,) * len(shape))
    return pl.pallas_call(
        kern,
        grid=(b, GLA_HEADS),
        in_specs=[pl.BlockSpec((1, t, dk), lambda bi, h: (bi, 0, col_q // dk + h)),
                  pl.BlockSpec((1, t, dk), lambda bi, h: (bi, 0, col_k // dk + h)),
                  pl.BlockSpec((1, t, dv), lambda bi, h: (bi, 0, col_v // dv + h)),
                  pl.BlockSpec((1, t, dv), lambda bi, h: (bi, 0, col_g // dv + h)),
                  pl.BlockSpec((1, t, LANES), lambda bi, h: (bi, 0, 0)),
                  pl.BlockSpec((2, LANES, dk), lambda bi, h: (0, 0, h)),
                  pl.BlockSpec((2, 1, dk), lambda bi, h: (0, 0, h)),
                  full((1, dv)), full(cm.shape), full(lm.shape)],
        out_specs=pl.BlockSpec((1, t, dv), lambda bi, h: (bi, 0, h)),
        out_shape=jax.ShapeDtypeStruct((b, t, GLA_HEADS * dv), BF16),
        scratch_shapes=[pltpu.VMEM((2, t, dk), F32), pltpu.VMEM((t, dv), F32), pltpu.VMEM((t, dv), F32),
                        pltpu.VMEM((2, t, dk), BF16), pltpu.VMEM((2, t, dk), BF16),
                        pltpu.VMEM((2, n_chunks * 8, dk), F32), pltpu.VMEM((2, dv, dk), F32)],
        compiler_params=_cparams("parallel", "parallel"),
        name="gla",
    )(p0, p0, p0, p0, r, gw_pad, gb, norm_g.reshape(1, dv), cm, lm)


def _swa_kernel(q_ref, kp_ref, kc_ref, kn_ref, kx_ref, vp_ref, vc_ref, vn_ref, vx_ref, sink_ref, o_ref,
                *, n_ctx_blocks, n_blocks, scale):
    blk = SWA_BLOCK
    i = pl.program_id(2)
    q = q_ref[0]
    q4 = jnp.concatenate([q[:, g * blk:(g + 1) * blk] for g in range(SWA_GROUP)], axis=0)
    keys = jnp.concatenate([kp_ref[0], kc_ref[0], kn_ref[0], kx_ref[0]], axis=0)
    vals = jnp.concatenate([vp_ref[0], vc_ref[0], vn_ref[0], vx_ref[0]], axis=0)
    s = _nt(q4, keys) * scale
    nq, nk = s.shape
    r = lax.broadcasted_iota(jnp.int32, (nq, 1), 0) % blk
    col = lax.broadcasted_iota(jnp.int32, (1, nk), 1)
    is_ctx = i < n_ctx_blocks
    lo_s = jnp.where(is_ctx, nk, jnp.where(i == n_ctx_blocks, blk, 0))
    hi_s = jnp.where(is_ctx, -1, jnp.where(i == n_blocks - 1, 2 * blk - 1, 3 * blk - 1))
    lo = jnp.maximum(r, lo_s)
    hi = jnp.minimum(r + 2 * blk, hi_s)
    valid = ((col >= lo) & (col <= hi)) | (col >= 3 * blk)
    s = jnp.where(valid, s, NEG_BIG)
    sink = sink_ref[0, :, 0:1]
    m = jnp.maximum(jnp.max(s, axis=1, keepdims=True), sink)
    e = jnp.exp(s - m)
    den = jnp.sum(e, axis=1, keepdims=True) + jnp.exp(sink - m)
    o = _dot(e.astype(BF16), vals) / den
    o_ref[0] = jnp.concatenate([o[g * blk:(g + 1) * blk] for g in range(SWA_GROUP)], axis=1).astype(BF16)


def _swa_call(p0, sink_col, n_ctx_rows, col_q, col_k, col_v, hd):
    b, t, _ = p0.shape
    blk = SWA_BLOCK
    nb = t // blk
    ncb = n_ctx_rows // blk
    gq = SWA_GROUP * hd
    kblk = lambda colb, f: pl.BlockSpec((1, blk, hd), lambda bi, h, i: (bi, f(i), colb + h))
    prev = lambda i: jnp.clip(i - 1, ncb, nb - 1)
    cur = lambda i: i
    nxt = lambda i: jnp.clip(i + 1, ncb, nb - 1)
    ctx_spec = lambda colb: pl.BlockSpec((1, n_ctx_rows, hd), lambda bi, h, i: (bi, 0, colb + h))
    ck, cv = col_k // hd, col_v // hd
    return pl.pallas_call(
        functools.partial(_swa_kernel, n_ctx_blocks=ncb, n_blocks=nb, scale=float(hd) ** -0.5),
        grid=(b, SWA_KV_HEADS, nb),
        in_specs=[pl.BlockSpec((1, blk, gq), lambda bi, h, i: (bi, i, col_q // gq + h)),
                  kblk(ck, prev), kblk(ck, cur), kblk(ck, nxt), ctx_spec(ck),
                  kblk(cv, prev), kblk(cv, cur), kblk(cv, nxt), ctx_spec(cv),
                  pl.BlockSpec((1, SWA_GROUP * blk, LANES), lambda bi, h, i: (h, 0, 0))],
        out_specs=pl.BlockSpec((1, blk, gq), lambda bi, h, i: (bi, i, h)),
        out_shape=jax.ShapeDtypeStruct((b, t, SWA_HEADS * hd), BF16),
        compiler_params=_cparams("parallel", "parallel", "parallel"),
        name="swa",
    )(p0, p0, p0, p0, p0, p0, p0, p0, p0, sink_col)


def _diff_kernel(lam_ref, q_ref, k_ref, v_ref, g_ref, o_ref, s_ref, kx_ref, vx_ref,
                 *, row_off, n_qblocks, lambda_init):
    tq = DIFF_QBLOCK
    hd2 = 2 * DIFF_HD
    n_keys = k_ref.shape[1]
    lam = lam_ref[...]
    s01 = jnp.sum(lam[0:1] * lam[1:2], axis=1, keepdims=True)
    s23 = jnp.sum(lam[2:3] * lam[3:4], axis=1, keepdims=True)
    lam_full = jnp.exp(s01) - jnp.exp(s23) + lambda_init
    lane = lax.broadcasted_iota(jnp.int32, (1, hd2), 1)
    first = lane < DIFF_HD
    col0 = lane == 0
    zero = jnp.zeros((), BF16)

    def map_norms(x):
        x2 = x.astype(F32)
        x2 = x2 * x2
        return (jnp.sqrt(jnp.sum(jnp.where(first, x2, 0.0), axis=1, keepdims=True)),
                jnp.sqrt(jnp.sum(jnp.where(first, 0.0, x2), axis=1, keepdims=True)))

    kn0, kn1 = map_norms(k_ref[0])
    kmax0 = jnp.max(kn0, axis=0, keepdims=True)
    kmax1 = jnp.max(kn1, axis=0, keepdims=True)
    qn0, qn1 = map_norms(q_ref[0, row_off:, :])
    is_safe = jnp.maximum(jnp.max(qn0 * kmax0), jnp.max(qn1 * kmax1)) <= DIFF_SAFE_BOUND

    def q_block(j):
        q = q_ref[0, pl.ds(pl.multiple_of(row_off + j * tq, tq), tq), :]
        return q, jnp.where(first, q, zero), jnp.where(first, zero, q)

    def write_out(j, o):
        y = _rms(o) * g_ref[...] * (1.0 - lambda_init)
        o_ref[0, pl.ds(pl.multiple_of(j * tq, tq), tq), :] = y.astype(BF16)

    def scores_fast(j, slot):
        q, qa, qb = q_block(j)
        b0, b1 = map_norms(q)
        xa = jnp.where(col0, -(b0 * kmax0), 0.0).astype(BF16)
        xb = jnp.where(col0, -(b1 * kmax1), 0.0).astype(BF16)
        qq = jnp.concatenate([jnp.concatenate([qa, xa], axis=1), jnp.concatenate([qb, xb], axis=1)], axis=0)
        s_ref[slot] = _nt(qq, kx_ref[...])

    def finish_fast(j, slot):
        e = jnp.exp2(s_ref[slot]).astype(BF16)
        ov = _dot(e, vx_ref[...])
        p0 = ov[0:tq, 0:hd2] / ov[0:tq, hd2:2 * hd2]
        p1 = ov[tq:2 * tq, 0:hd2] / ov[tq:2 * tq, hd2:2 * hd2]
        write_out(j, p0 - lam_full * p1)

    def scores_max(j, slot):
        _, qa, qb = q_block(j)
        s_ref[slot] = _nt(jnp.concatenate([qa, qb], axis=0), k_ref[0])

    def finish_max(j, slot):
        s = s_ref[slot]
        e = jnp.exp2(s - jnp.max(s, axis=1, keepdims=True))
        z = jnp.sum(e, axis=1, keepdims=True)
        coef = lam_full * z[0:tq] / z[tq:2 * tq]
        a = e[0:tq] - coef * e[tq:2 * tq]
        write_out(j, _dot(a.astype(BF16), v_ref[0]) * (1.0 / z[0:tq]))

    def run(scores, finish):
        scores(0, 0)

        def body(jj, carry):
            j0 = 2 * jj
            scores(j0 + 1, 1)
            finish(j0, 0)
            scores(j0 + 2, 0)
            finish(j0 + 1, 1)
            return carry

        lax.fori_loop(0, n_qblocks // 2 - 1, body, 0)
        scores(n_qblocks - 1, 1)
        finish(n_qblocks - 2, 0)
        finish(n_qblocks - 1, 1)

    @pl.when(is_safe)
    def _():
        kx_ref[:, 0:hd2] = k_ref[0]
        kx_ref[:, hd2:2 * hd2] = jnp.broadcast_to(jnp.where(col0, 1.0, 0.0).astype(BF16), (n_keys, hd2))
        vx_ref[:, 0:hd2] = v_ref[0]
        vx_ref[:, hd2:2 * hd2] = jnp.ones((n_keys, hd2), BF16)
        run(scores_fast, finish_fast)

    @pl.when(jnp.logical_not(is_safe))
    def _():
        run(scores_max, finish_max)


def _diff_call(p1, lam, subln_g, n_ctx_rows, n_heads, lambda_init):
    b, t, _ = p1.shape
    s_len = t - n_ctx_rows
    hd2 = 2 * DIFF_HD
    n_qblocks = s_len // DIFF_QBLOCK
    assert n_qblocks >= 2 and n_qblocks % 2 == 0
    return pl.pallas_call(
        functools.partial(_diff_kernel, row_off=n_ctx_rows, n_qblocks=n_qblocks, lambda_init=lambda_init),
        grid=(b, n_heads),
        in_specs=[pl.BlockSpec((4, DIFF_HD), lambda bi, h: (0, 0)),
                  pl.BlockSpec((1, t, hd2), lambda bi, h: (bi, 0, h)),
                  pl.BlockSpec((1, t, hd2), lambda bi, h: (bi, 0, n_heads + h)),
                  pl.BlockSpec((1, t, hd2), lambda bi, h: (bi, 0, 2 * n_heads + h)),
                  pl.BlockSpec((1, hd2), lambda bi, h: (0, 0))],
        out_specs=pl.BlockSpec((1, s_len, hd2), lambda bi, h: (bi, 0, h)),
        out_shape=jax.ShapeDtypeStruct((b, s_len, n_heads * hd2), BF16),
        scratch_shapes=[pltpu.VMEM((2, 2 * DIFF_QBLOCK, t), F32), pltpu.VMEM((t, 2 * hd2), BF16),
                        pltpu.VMEM((t, 2 * hd2), BF16)],
        compiler_params=_cparams("parallel", "parallel"),
        name="diff_attn",
    )(lam, p1, p1, p1, subln_g.reshape(1, hd2))


def _outproj_kernel(*refs, n_y, n_row_refs, n_ctx_blocks, row_block_off):
    y_refs = refs[:n_y]
    w_refs = refs[n_y:2 * n_y]
    row_refs = refs[2 * n_y:2 * n_y + n_row_refs]
    mod_ref, g_ref, wr_ref, xn_ref, h_ref, aff_ref = refs[2 * n_y + n_row_refs:]
    acc = _dot(y_refs[0][0], w_refs[0][...])
    for yr, wr in zip(y_refs[1:], w_refs[1:]):
        acc = acc + _dot(yr[0], wr[...])
    xn = _load_rows(row_refs, n_ctx_blocks, row_block_off) + mod_ref[0, 0, 2:3, :] * acc
    xn_ref[0] = xn
    h = _rms(xn) * g_ref[...] * (1.0 + mod_ref[0, 0, 4:5, :]) + mod_ref[0, 0, 3:4, :]
    h_bf = h.astype(BF16)
    h_ref[0] = h_bf
    logits = _nt(wr_ref[...].astype(BF16), h_bf)
    ex = jnp.exp(logits - jnp.max(logits, axis=0, keepdims=True))
    aff_ref[0] = ex / jnp.sum(ex, axis=0, keepdims=True)


def _outproj_call(ys, ws, stream, mod, g, w_router_t, row_block_off, n_rows, n_ctx_blocks):
    b, d = mod.shape[0], mod.shape[3]
    n_y = len(ys)
    e = w_router_t.shape[0]
    row_specs, row_args = _row_specs(stream, n_ctx_blocks, row_block_off)
    in_specs = ([pl.BlockSpec((1, ROW_TILE, y.shape[2]), lambda bi, i: (bi, i, 0)) for y in ys]
                + [pl.BlockSpec(w.shape, lambda bi, i: (0, 0), pipeline_mode=pl.Buffered(1)) for w in ws]
                + row_specs
                + [pl.BlockSpec((1, 1, 6, d),
                                lambda bi, i: (bi, jnp.where(i + row_block_off < n_ctx_blocks, 0, 1), 0, 0)),
                   pl.BlockSpec((1, d), lambda bi, i: (0, 0)),
                   pl.BlockSpec((e, d), lambda bi, i: (0, 0))])
    return pl.pallas_call(
        functools.partial(_outproj_kernel, n_y=n_y, n_row_refs=len(row_args), n_ctx_blocks=n_ctx_blocks,
                          row_block_off=row_block_off),
        grid=(b, n_rows // ROW_TILE),
        in_specs=in_specs,
        out_specs=[pl.BlockSpec((1, ROW_TILE, d), lambda bi, i: (bi, i, 0)),
                   pl.BlockSpec((1, ROW_TILE, d), lambda bi, i: (bi, i, 0)),
                   pl.BlockSpec((1, e, ROW_TILE), lambda bi, i: (bi, 0, i))],
        out_shape=[jax.ShapeDtypeStruct((b, n_rows, d), F32),
                   jax.ShapeDtypeStruct((b, n_rows, d), BF16),
                   jax.ShapeDtypeStruct((b, e, n_rows), F32)],
        compiler_params=_cparams("parallel", "parallel"),
        name="out_proj",
    )(*ys, *ws, *row_args, mod, g.reshape(1, d), w_router_t)


def _prefix_lanes(m, upper):
    e, n = m.shape
    carry = jnp.zeros((e, 1), F32)
    outs = []
    for blk in range(n // LANES):
        mb = m[:, blk * LANES:(blk + 1) * LANES]
        inc = _dot(mb.astype(BF16), upper)
        outs.append(inc - mb + carry)
        carry = carry + jnp.sum(mb, axis=1, keepdims=True)
    return jnp.concatenate(outs, axis=1)


def _transpose_exact(x, eye):
    hi, mid, lo = _split3(x)
    return _nt(eye, hi) + (_nt(eye, mid) + _nt(eye, lo))


def _route_kernel(aff_ref, pos_ref, gate_ref, posn_ref, lo_ref, hi_ref, *, segs):
    n_exp = aff_ref.shape[1]
    lo_acc = jnp.zeros((n_exp, LANES), F32)
    hi_acc = jnp.zeros((n_exp, LANES), F32)
    li = lax.broadcasted_iota(jnp.int32, (LANES, LANES), 0)
    lj = lax.broadcasted_iota(jnp.int32, (LANES, LANES), 1)
    upper = jnp.where(li <= lj, 1.0, 0.0).astype(BF16)
    eye = jnp.where(li == lj, 1.0, 0.0).astype(BF16)
    for (off, n, cap, slot_off) in segs:
        a = aff_ref[0, :, off:off + n]
        bits = pltpu.bitcast(a, jnp.int32)

        def search(it, thr):
            cand = thr | jnp.left_shift(jnp.int32(1), 30 - it)
            cnt = jnp.sum(jnp.where(bits >= cand, 1.0, 0.0), axis=1, keepdims=True)
            return jnp.where(cnt >= cap, cand, thr)

        thr = lax.fori_loop(0, 31, search, jnp.zeros((n_exp, 1), jnp.int32))
        gt = jnp.where(bits > thr, 1.0, 0.0)
        eq = jnp.where(bits == thr, 1.0, 0.0)
        need = cap - jnp.sum(gt, axis=1, keepdims=True)
        sel = gt + eq * jnp.where(_prefix_lanes(eq, upper) < need, 1.0, 0.0)
        slot = _prefix_lanes(sel, upper)
        pos = jnp.where(sel > 0.0, slot + slot_off, -1.0)
        pos_ref[0, :, off:off + n] = pos.astype(jnp.int32)
        gate_ref[0, :, off:off + n] = sel * a
        for blk in range(n // LANES):
            cols = slice(blk * LANES, (blk + 1) * LANES)
            rows = slice(off + blk * LANES, off + (blk + 1) * LANES)
            posn_ref[0, rows, :] = _transpose_exact(pos[:, cols], eye).astype(jnp.int32)
        assert cap <= 256 and (off + n) // LANES <= LANES
        tok = lax.broadcasted_iota(jnp.int32, (n, LANES), 0)
        blk_start = (lax.broadcasted_iota(jnp.int32, (n, LANES), 1) - off // LANES) * LANES
        lo_acc = lo_acc + _dot(slot.astype(BF16), jnp.where(tok == blk_start, 1.0, 0.0).astype(BF16))
        hi_acc = hi_acc + _dot((slot + sel).astype(BF16),
                               jnp.where(tok == blk_start + (LANES - 1), 1.0, 0.0).astype(BF16))
    lo_ref[0] = lo_acc.astype(jnp.int32)
    hi_ref[0] = hi_acc.astype(jnp.int32)


def _route_call(aff_t, segs):
    b, e, t = aff_t.shape
    return pl.pallas_call(
        functools.partial(_route_kernel, segs=tuple(segs)),
        grid=(b,),
        in_specs=[pl.BlockSpec((1, e, t), lambda bi: (bi, 0, 0))],
        out_specs=[pl.BlockSpec((1, e, t), lambda bi: (bi, 0, 0)),
                   pl.BlockSpec((1, e, t), lambda bi: (bi, 0, 0)),
                   pl.BlockSpec((1, t, e), lambda bi: (bi, 0, 0)),
                   pl.BlockSpec((1, e, LANES), lambda bi: (bi, 0, 0)),
                   pl.BlockSpec((1, e, LANES), lambda bi: (bi, 0, 0))],
        out_shape=[jax.ShapeDtypeStruct((b, e, t), jnp.int32),
                   jax.ShapeDtypeStruct((b, e, t), F32),
                   jax.ShapeDtypeStruct((b, t, e), jnp.int32),
                   jax.ShapeDtypeStruct((b, e, LANES), jnp.int32),
                   jax.ShapeDtypeStruct((b, e, LANES), jnp.int32)],
        compiler_params=_cparams("parallel"),
        name="route",
    )(aff_t)


def _gather_kernel(lo_ref, pos_ref, gate_ref, h_ref, o_ref, g_ref, *, segs):
    e = pl.program_id(1)
    prow = pos_ref[0, pl.ds(e, 1), :]
    grow = gate_ref[0, pl.ds(e, 1), :]

    def emit(slot0, n_slot, tok0, n_tok):
        slots = lax.broadcasted_iota(jnp.int32, (n_slot, 1), 0) + slot0
        hit = prow[:, tok0:tok0 + n_tok] == slots
        o_ref[0, 0, slot0:slot0 + n_slot, :] = _dot(jnp.where(hit, 1.0, 0.0).astype(BF16),
                                                    h_ref[0, tok0:tok0 + n_tok, :]).astype(BF16)
        gsel = jnp.sum(jnp.where(hit, grow[:, tok0:tok0 + n_tok], 0.0), axis=1, keepdims=True)
        g_ref[0, 0, slot0:slot0 + n_slot, :] = jnp.broadcast_to(gsel, (n_slot, LANES))

    for (off, n, cap, slot_off) in segs:
        half = cap // 2
        k0 = -(-(n * 5 // 8) // LANES) * LANES
        k1 = (n * 3 // 8) // LANES * LANES
        if cap % 32 or k0 >= n or k1 <= 0:
            emit(slot_off, cap, off, n)
            continue
        fits = jnp.logical_and(lo_ref[0, e, (off + k0) // LANES] >= half, lo_ref[0, e, (off + k1) // LANES] <= half)

        @pl.when(fits)
        def _():
            emit(slot_off, half, off, k0)
            emit(slot_off + half, half, off + k1, n - k1)

        @pl.when(jnp.logical_not(fits))
        def _():
            emit(slot_off, cap, off, n)


def _gather_call(lo, pos, gate, h, segs, n_slots):
    b, e, t = pos.shape
    d = h.shape[2]
    return pl.pallas_call(
        functools.partial(_gather_kernel, segs=tuple(segs)),
        grid=(b, e),
        in_specs=[pl.BlockSpec((1, e, LANES), lambda bi, ei: (bi, 0, 0), memory_space=pltpu.SMEM),
                  pl.BlockSpec((1, e, t), lambda bi, ei: (bi, 0, 0)),
                  pl.BlockSpec((1, e, t), lambda bi, ei: (bi, 0, 0)),
                  pl.BlockSpec((1, t, d), lambda bi, ei: (bi, 0, 0))],
        out_specs=[pl.BlockSpec((1, 1, n_slots, d), lambda bi, ei: (ei, bi, 0, 0)),
                   pl.BlockSpec((1, 1, n_slots, LANES), lambda bi, ei: (ei, bi, 0, 0))],
        out_shape=[jax.ShapeDtypeStruct((e, b, n_slots, d), BF16),
                   jax.ShapeDtypeStruct((e, b, n_slots, LANES), F32)],
        compiler_params=_cparams("parallel", "arbitrary"),
        name="gather",
    )(lo, pos, gate, h)


def _ffn_up_kernel(x_ref, wg_ref, wu_ref, o_ref, wgb_ref, wub_ref):
    @pl.when(pl.program_id(2) == 0)
    def _():
        wgb_ref[...] = wg_ref[0, 0].astype(BF16)
        wub_ref[...] = wu_ref[0, 0].astype(BF16)

    x = x_ref[0]
    g = _dot(x, wgb_ref[...])
    u = _dot(x, wub_ref[...])
    o_ref[0] = (_silu(g) * u).astype(BF16)


def _ffn_down_kernel(a_ref, w_ref, gate_ref, o_ref, wb_ref):
    @pl.when(pl.program_id(2) == 0)
    def _():
        wb_ref[...] = w_ref[0, 0].astype(BF16)

    gate = jnp.concatenate([gate_ref[0]] * (o_ref.shape[2] // LANES), axis=1)
    o_ref[0] = (_dot(a_ref[0], wb_ref[...]) * gate).astype(BF16)


def _ffn_call(xs, gates, w_gu, w_dn, layer):
    e, m, d = xs.shape
    f = w_dn.shape[2]
    tm = _pick(m, (1152, 1024, 768, 512, 256, 128, 64, 32, 16, 8))
    tf = _pick(f, (512, 256, 128))
    nf = f // tf
    act = pl.pallas_call(
        _ffn_up_kernel,
        grid=(e, nf, m // tm),
        in_specs=[pl.BlockSpec((1, tm, d), lambda ei, j, i: (ei, i, 0)),
                  pl.BlockSpec((1, 1, d, tf), lambda ei, j, i: (layer, ei, 0, j)),
                  pl.BlockSpec((1, 1, d, tf), lambda ei, j, i: (layer, ei, 0, j + nf))],
        out_specs=pl.BlockSpec((1, tm, tf), lambda ei, j, i: (ei, i, j)),
        out_shape=jax.ShapeDtypeStruct((e, m, f), BF16),
        scratch_shapes=[pltpu.VMEM((d, tf), BF16), pltpu.VMEM((d, tf), BF16)],
        compiler_params=_cparams("parallel", "arbitrary", "arbitrary"),
        name="ffn_up",
    )(xs, w_gu, w_gu)
    tn = _pick(d, (512, 256, 128))
    return pl.pallas_call(
        _ffn_down_kernel,
        grid=(e, d // tn, m // tm),
        in_specs=[pl.BlockSpec((1, tm, f), lambda ei, j, i: (ei, i, 0)),
                  pl.BlockSpec((1, 1, f, tn), lambda ei, j, i: (layer, ei, 0, j)),
                  pl.BlockSpec((1, tm, LANES), lambda ei, j, i: (ei, i, 0))],
        out_specs=pl.BlockSpec((1, tm, tn), lambda ei, j, i: (ei, i, j)),
        out_shape=jax.ShapeDtypeStruct((e, m, d), BF16),
        scratch_shapes=[pltpu.VMEM((f, tn), BF16)],
        compiler_params=_cparams("parallel", "arbitrary", "arbitrary"),
        name="ffn_down",
    )(act, w_dn, gates)


def _combine_kernel(lo_ref, hi_ref, posn_ref, y_ref, x_ref, mod_ref, gf_ref, o_ref,
                    *, cap_lat, cap_ctx, n_ctx_tiles, blocks_per_tile, window, final_norm):
    n_exp = posn_ref.shape[2]
    i = pl.program_id(1)
    lane = lax.broadcasted_iota(jnp.int32, (1, n_exp), 1)
    posn = posn_ref[0].astype(F32)

    def pcol(e):
        return jnp.sum(jnp.where(lane == e, posn, 0.0), axis=1, keepdims=True)

    def scatter(slot0, cap):
        slots = lax.broadcasted_iota(jnp.int32, (1, cap), 1).astype(F32) + slot0
        hots = [jnp.where(pcol(e) == slots, 1.0, 0.0).astype(BF16) for e in range(n_exp)]
        rows = [y_ref[e, 0, slot0:slot0 + cap, :] for e in range(n_exp)]
        return _dot(jnp.concatenate(hots, axis=1), jnp.concatenate(rows, axis=0))

    def scatter_window(starts):
        slots = lax.broadcasted_iota(jnp.int32, (1, window), 1).astype(F32)
        hots = [jnp.where(pcol(e) - st.astype(F32) == slots, 1.0, 0.0).astype(BF16)
                for e, st in enumerate(starts)]
        rows = [y_ref[e, 0, pl.ds(pl.multiple_of(st, 16), window), :] for e, st in enumerate(starts)]
        return _dot(jnp.concatenate(hots, axis=1), jnp.concatenate(rows, axis=0))

    def finish(total, g2):
        out = x_ref[0] + g2 * total
        if final_norm:
            out = _rms(out) * gf_ref[...]
        o_ref[0] = out

    blk0 = i * blocks_per_tile
    starts = []
    fits = None
    for e in range(n_exp):
        lo = lo_ref[0, e, blk0]
        hi = hi_ref[0, e, blk0 + blocks_per_tile - 1]
        st = jnp.minimum(lax.shift_left(lax.shift_right_logical(lo, 4), 4), cap_lat - window)
        ok = hi - st <= window
        fits = ok if fits is None else jnp.logical_and(fits, ok)
        starts.append(st)
    is_lat = i >= n_ctx_tiles

    @pl.when(jnp.logical_and(is_lat, fits))
    def _():
        finish(scatter_window(starts), mod_ref[0, 1, 5:6, :])

    @pl.when(jnp.logical_and(is_lat, jnp.logical_not(fits)))
    def _():
        finish(scatter(0, cap_lat), mod_ref[0, 1, 5:6, :])

    if n_ctx_tiles:
        @pl.when(jnp.logical_not(is_lat))
        def _():
            finish(scatter(cap_lat, cap_ctx), mod_ref[0, 0, 5:6, :])


def _combine_call(lo, hi, posn, y, xa, mod, g_final, cap_lat, cap_ctx, n_ctx_rows, final_norm):
    b, t, d = xa.shape
    e = posn.shape[2]
    n_slots = y.shape[2]
    tt = ROW_TILE
    assert n_ctx_rows % tt == 0 and t % tt == 0
    window = min(cap_lat, max(16, 2 * tt * EC_FACTOR // e))
    smem = lambda: pl.BlockSpec((1, e, LANES), lambda bi, i: (bi, 0, 0), memory_space=pltpu.SMEM)
    return pl.pallas_call(
        functools.partial(_combine_kernel, cap_lat=cap_lat, cap_ctx=cap_ctx, n_ctx_tiles=n_ctx_rows // tt,
                          blocks_per_tile=tt // LANES, window=window, final_norm=final_norm),
        grid=(b, t // tt),
        in_specs=[smem(), smem(),
                  pl.BlockSpec((1, tt, e), lambda bi, i: (bi, i, 0)),
                  pl.BlockSpec((e, 1, n_slots, d), lambda bi, i: (0, bi, 0, 0), pipeline_mode=pl.Buffered(1)),
                  pl.BlockSpec((1, tt, d), lambda bi, i: (bi, i, 0)),
                  pl.BlockSpec((1, 2, 6, d), lambda bi, i: (bi, 0, 0, 0)),
                  pl.BlockSpec((1, d), lambda bi, i: (0, 0))],
        out_specs=pl.BlockSpec((1, tt, d), lambda bi, i: (bi, i, 0)),
        out_shape=jax.ShapeDtypeStruct((b, t, d), F32),
        compiler_params=_cparams("parallel", "arbitrary"),
        name="combine",
    )(lo, hi, posn, y, xa, mod, g_final.reshape(1, d))


def _rope_tables(n_tokens, n_ctx_rows, head_dim):
    rows = n_tokens // GRID_W
    r = jnp.repeat(jnp.arange(rows, dtype=F32), GRID_W)
    col = jnp.tile(jnp.arange(GRID_W, dtype=F32), rows)
    axis_dim = head_dim // 2
    inv = ROPE_THETA ** (-jnp.arange(0, axis_dim, 2, dtype=F32) / axis_dim)
    ar, ac = r[:, None] * inv, col[:, None] * inv
    cos_p = jnp.concatenate([jnp.cos(ar), jnp.cos(ar), jnp.cos(ac), jnp.cos(ac)], axis=-1)
    sin_p = jnp.concatenate([-jnp.sin(ar), jnp.sin(ar), -jnp.sin(ac), jnp.sin(ac)], axis=-1)
    reps = LANES // head_dim
    cos_p, sin_p = jnp.tile(cos_p, (1, reps)), jnp.tile(sin_p, (1, reps))
    cos_t = jnp.concatenate([jnp.ones((n_ctx_rows, LANES), F32), cos_p], axis=0)
    sin_t = jnp.concatenate([jnp.zeros((n_ctx_rows, LANES), F32), sin_p], axis=0)
    return cos_t, sin_t


def _moe(h2, aff_t, x_res, mod, g_final, w_gu, w_dn, layer, segs, cap_lat, cap_ctx, n_ctx_rows, final_norm):
    b, t, d = h2.shape
    e = aff_t.shape[1]
    n_slots = cap_lat + cap_ctx
    pos, gate, posn, lo, hi = _route_call(aff_t, segs)
    xs, gs = _gather_call(lo, pos, gate, h2, segs, n_slots)
    y = _ffn_call(xs.reshape(e, b * n_slots, d), gs.reshape(e, b * n_slots, LANES), w_gu, w_dn, layer)
    return _combine_call(lo, hi, posn, y.reshape(e, b, n_slots, d), x_res, mod, g_final, cap_lat, cap_ctx,
                         n_ctx_rows, final_norm)


def kernel(x, c, ctx, c_ctx, w_ada, b_ada, g_norm_mix, g_norm_ffn, w_in_even, gla_gate_w, gla_gate_b, gla_norm_g, swa_sink, w_out_even, w_qkv_odd, diff_lambda, diff_subln_g, w_out_odd, w_router, w_gate_up, w_down, g_final):
    b, s_len, d = x.shape
    lc = ctx.shape[1]
    depth = w_ada.shape[0]
    assert depth == 2 and lc % ROW_TILE == 0 and s_len % ROW_TILE == 0 and s_len % GRID_W == 0
    t = lc + s_len
    ncb = lc // ROW_TILE
    gla_dk, gla_dv = d // 4 // GLA_HEADS, d // 2 // GLA_HEADS
    swa_hd = d // 2 // SWA_HEADS
    diff_heads = d // (2 * DIFF_HD)
    n_exp = w_router.shape[2]
    assert gla_dk == LANES and swa_hd == LANES and 2 * DIFF_HD == LANES

    n_rows = -(-(b + 1) // 8) * 8
    cc = jnp.concatenate([c, c_ctx[None, :], jnp.zeros((n_rows - b - 1, d), F32)], axis=0)
    m_all = _ada_call(cc, w_ada, b_ada).reshape(depth, n_rows, 6, d)

    def mod_for(layer):
        lat = m_all[layer, :b]
        ctxm = jnp.broadcast_to(m_all[layer, b][None], (b, 6, d))
        return jnp.stack([ctxm, lat], axis=1)

    xa = (ctx, x)

    mod0 = mod_for(0)
    w_in = w_in_even[0]
    sizes = (GLA_HEADS * gla_dk, GLA_HEADS * gla_dk, GLA_HEADS * gla_dv, GLA_HEADS * gla_dv,
             GLA_GATE_RANK, GLA_GATE_RANK, SWA_HEADS * swa_hd, SWA_KV_HEADS * swa_hd, SWA_KV_HEADS * swa_hd)
    offs = np.concatenate([[0], np.cumsum(sizes)])
    seg = lambda i: w_in[:, offs[i]:offs[i + 1]]
    w_main = jnp.concatenate([seg(0), seg(1), seg(2), seg(3), seg(6), seg(7), seg(8)], axis=1).astype(BF16)
    w_rank = jnp.concatenate([seg(4), seg(5), jnp.zeros((d, LANES - 2 * GLA_GATE_RANK), F32)],
                             axis=1).astype(BF16)
    col_aq, col_ak = 0, sizes[0]
    col_av = col_ak + sizes[1]
    col_ag = col_av + sizes[2]
    col_bq = col_ag + sizes[3]
    col_bk = col_bq + sizes[6]
    col_bv = col_bk + sizes[7]
    n_main = col_bv + sizes[8]
    rope0 = [(col_bq <= ch * COL_CHUNK < col_bv, 1.0) for ch in range(n_main // COL_CHUNK)]
    cos_b, sin_b = _rope_tables(s_len, lc, swa_hd)
    p0, r0 = _proj_call(xa, mod0, g_norm_mix[0], w_main, cos_b, sin_b, rope0, swa_hd // 4, ncb, wr=w_rank)

    gw = gla_gate_w[0]
    gw_pad = jnp.zeros((2, LANES, GLA_HEADS * gla_dk), F32)
    gw_pad = gw_pad.at[0, 0:GLA_GATE_RANK].set(gw[0]).at[1, GLA_GATE_RANK:2 * GLA_GATE_RANK].set(gw[1])
    a_out = _gla_call(p0, r0, gw_pad.astype(BF16), gla_gate_b[0].reshape(2, 1, -1), gla_norm_g[0],
                      _gla_constants(), lc, col_aq, col_ak, col_av, col_ag, gla_dk, gla_dv)
    sink_col = jnp.broadcast_to(
        jnp.repeat(swa_sink[0].reshape(SWA_KV_HEADS, SWA_GROUP), SWA_BLOCK, axis=1)[:, :, None],
        (SWA_KV_HEADS, SWA_GROUP * SWA_BLOCK, LANES))
    b_out = _swa_call(p0, sink_col, lc, col_bq, col_bk, col_bv, swa_hd)

    w_o = w_out_even[0].astype(BF16)
    n_a = GLA_HEADS * gla_dv
    w_router_t0 = jnp.transpose(w_router[0])
    xa1, h2, aff_t = _outproj_call([a_out, b_out], [w_o[:n_a], w_o[n_a:]], xa, mod0, g_norm_ffn[0],
                                   w_router_t0, 0, t, ncb)
    cap_lat = s_len * EC_FACTOR // n_exp
    cap_ctx = lc * EC_FACTOR // n_exp
    segs0 = [(lc, s_len, cap_lat, 0), (0, lc, cap_ctx, cap_lat)]
    xa2 = _moe(h2, aff_t, xa1, mod0, g_final, w_gate_up, w_down, 0, segs0, cap_lat, cap_ctx, lc, False)

    mod1 = mod_for(1)
    lambda_init = 0.8 - 0.6 * float(np.exp(-0.3 * 1))
    w_qkv = w_qkv_odd[0].astype(BF16)
    n_qkv = w_qkv.shape[1]
    q_scale = DIFF_HD ** -0.5 * float(np.log2(np.e))
    rope1 = [(ch * COL_CHUNK < 2 * d, q_scale if ch * COL_CHUNK < d else 1.0)
             for ch in range(n_qkv // COL_CHUNK)]
    cos_c, sin_c = _rope_tables(s_len, lc, DIFF_HD)
    (p1,) = _proj_call(xa2, mod1, g_norm_mix[1], w_qkv, cos_c, sin_c, rope1, DIFF_HD // 4, ncb)
    y1 = _diff_call(p1, diff_lambda[0], diff_subln_g[0], lc, diff_heads, lambda_init)
    x3, h2b, aff_tb = _outproj_call([y1], [w_out_odd[0].astype(BF16)], xa2, mod1, g_norm_ffn[1],
                                    jnp.transpose(w_router[1]), ncb, s_len, ncb)
    segs1 = [(0, s_len, cap_lat, 0)]
    return _moe(h2b, aff_tb, x3, mod1, g_final, w_gate_up, w_down, 1, segs1, cap_lat, 0, 0, True)
```

```python
import functools

import numpy as np
import jax
import jax.numpy as jnp
from jax import lax
from jax.experimental import pallas as pl
from jax.experimental.pallas import tpu as pltpu

F32 = jnp.float32
BF16 = jnp.bfloat16

GRID_W = 64
ROPE_THETA = 10000.0
NORM_EPS = 1e-6
GLA_HEADS = 4
GLA_GATE_RANK = 16
GLA_GATE_NORM = 16.0
SWA_HEADS = 8
SWA_KV_HEADS = 2
SWA_GROUP = SWA_HEADS // SWA_KV_HEADS
SWA_WINDOW = 128
SWA_BLOCK = 128
DIFF_HD = 64
N_EXPERTS = 16
EC_FACTOR = 2

LANES = 128
ROW_TILE = 256
COL_CHUNK = 256
GLA_CHUNK = 128
GLA_LEVELS = 7
GLA_SAFE_RANGE = 60.0
DIFF_QBLOCK = 256
DIFF_SAFE_BOUND = 60.0
VMEM_LIMIT = 56 * 1024 * 1024
NEG_BIG = -1e30


def _cparams(*sem):
    return pltpu.CompilerParams(dimension_semantics=sem, vmem_limit_bytes=VMEM_LIMIT)


def _pick(n, cands):
    for c in cands:
        if n % c == 0:
            return c
    raise ValueError(f"no tile for {n} in {cands}")


def _nt(a, b):
    return lax.dot_general(a, b, (((1,), (1,)), ((), ())), preferred_element_type=F32)


def _tn(a, b):
    return lax.dot_general(a, b, (((0,), (0,)), ((), ())), preferred_element_type=F32)


def _dot(a, b):
    return jnp.dot(a, b, preferred_element_type=F32)


def _split3(a):
    hi = a.astype(BF16)
    r1 = a - hi.astype(F32)
    mid = r1.astype(BF16)
    lo = (r1 - mid.astype(F32)).astype(BF16)
    return hi, mid, lo


def _split2(a):
    hi = a.astype(BF16)
    lo = (a - hi.astype(F32)).astype(BF16)
    return hi, lo


def _silu(x):
    return x * (1.0 / (1.0 + jnp.exp(-x)))


def _rms(x, eps=NORM_EPS):
    return x * lax.rsqrt(jnp.mean(x * x, axis=-1, keepdims=True) + eps)


def _row_specs(stream, n_ctx_blocks, off=0):
    if isinstance(stream, tuple):
        ctx, x = stream
        d = x.shape[2]
        return ([pl.BlockSpec((1, ROW_TILE, d), lambda bi, i: (bi, jnp.minimum(i + off, n_ctx_blocks - 1), 0)),
                 pl.BlockSpec((1, ROW_TILE, d), lambda bi, i: (bi, jnp.maximum(i + off - n_ctx_blocks, 0), 0))],
                [ctx, x])
    d = stream.shape[2]
    return [pl.BlockSpec((1, ROW_TILE, d), lambda bi, i: (bi, i + off, 0))], [stream]


def _load_rows(row_refs, n_ctx_blocks, off=0):
    if len(row_refs) == 2:
        return jnp.where(pl.program_id(1) + off < n_ctx_blocks, row_refs[0][0], row_refs[1][0])
    return row_refs[0][0]


def _ada_kernel(c_ref, w_ref, b_ref, o_ref):
    s = _silu(c_ref[...])
    s_hi, s_lo = _split2(s)
    w_hi, w_lo = _split2(w_ref[0])
    acc = _dot(s_hi, w_hi) + (_dot(s_lo, w_hi) + _dot(s_hi, w_lo))
    o_ref[0] = acc + b_ref[0]


def _ada_call(cc, w_ada, b_ada):
    depth, d, n = w_ada.shape
    r = cc.shape[0]
    tn = _pick(n, (768, 512, 256, 128))
    return pl.pallas_call(
        _ada_kernel,
        grid=(depth, n // tn),
        in_specs=[pl.BlockSpec((r, d), lambda l, j: (0, 0)),
                  pl.BlockSpec((1, d, tn), lambda l, j: (l, 0, j)),
                  pl.BlockSpec((1, 1, tn), lambda l, j: (l, 0, j))],
        out_specs=pl.BlockSpec((1, r, tn), lambda l, j: (l, 0, j)),
        out_shape=jax.ShapeDtypeStruct((depth, r, n), F32),
        compiler_params=_cparams("parallel", "parallel"),
        name="adaln",
    )(cc, w_ada, b_ada.reshape(depth, 1, n))


def _proj_kernel(*refs, n_row_refs, n_ctx_blocks, chunk_ops, quarter, has_r):
    row_refs = refs[:n_row_refs]
    mod_ref, g_ref, w_ref, cos_ref, sin_ref = refs[n_row_refs:n_row_refs + 5]
    rest = refs[n_row_refs + 5:]
    if has_r:
        wr_ref, o_ref, r_ref = rest
    else:
        (o_ref,) = rest
    x = _load_rows(row_refs, n_ctx_blocks)
    y = _rms(x) * g_ref[...]
    h = (y * (1.0 + mod_ref[0, 0, 1:2, :]) + mod_ref[0, 0, 0:1, :]).astype(BF16)
    reps = COL_CHUNK // LANES
    cos_t = jnp.concatenate([cos_ref[...]] * reps, axis=1)
    sin_t = jnp.concatenate([sin_ref[...]] * reps, axis=1)
    lane = lax.broadcasted_iota(jnp.int32, (1, COL_CHUNK), 1)
    first = (lane % (2 * quarter)) < quarter
    for c, (rope, scale) in enumerate(chunk_ops):
        acc = _dot(h, w_ref[:, c * COL_CHUNK:(c + 1) * COL_CHUNK])
        if rope:
            partner = jnp.where(first, pltpu.roll(acc, COL_CHUNK - quarter, 1), pltpu.roll(acc, quarter, 1))
            acc = acc * cos_t + partner * sin_t
        if scale != 1.0:
            acc = acc * scale
        o_ref[0, :, c * COL_CHUNK:(c + 1) * COL_CHUNK] = acc.astype(BF16)
    if has_r:
        r_ref[0] = _dot(h, wr_ref[...]).astype(BF16)


def _proj_call(stream, mod, g, w, cos_t, sin_t, chunk_ops, quarter, n_ctx_blocks, wr=None):
    d, n = w.shape
    t = cos_t.shape[0]
    b = mod.shape[0]
    assert n == len(chunk_ops) * COL_CHUNK and t % ROW_TILE == 0
    has_r = wr is not None
    row_specs, row_args = _row_specs(stream, n_ctx_blocks)
    in_specs = row_specs + [
        pl.BlockSpec((1, 1, 6, d), lambda bi, i: (bi, jnp.where(i < n_ctx_blocks, 0, 1), 0, 0)),
        pl.BlockSpec((1, d), lambda bi, i: (0, 0)),
        pl.BlockSpec((d, n), lambda bi, i: (0, 0), pipeline_mode=pl.Buffered(1)),
        pl.BlockSpec((ROW_TILE, LANES), lambda bi, i: (i, 0)),
        pl.BlockSpec((ROW_TILE, LANES), lambda bi, i: (i, 0))]
    args = row_args + [mod, g.reshape(1, d), w, cos_t, sin_t]
    out_specs = [pl.BlockSpec((1, ROW_TILE, n), lambda bi, i: (bi, i, 0))]
    out_shape = [jax.ShapeDtypeStruct((b, t, n), BF16)]
    if has_r:
        in_specs.append(pl.BlockSpec((d, LANES), lambda bi, i: (0, 0)))
        args.append(wr)
        out_specs.append(pl.BlockSpec((1, ROW_TILE, LANES), lambda bi, i: (bi, i, 0)))
        out_shape.append(jax.ShapeDtypeStruct((b, t, LANES), BF16))
    return pl.pallas_call(
        functools.partial(_proj_kernel, n_row_refs=len(row_args), n_ctx_blocks=n_ctx_blocks,
                          chunk_ops=tuple(chunk_ops), quarter=quarter, has_r=has_r),
        grid=(b, t // ROW_TILE),
        in_specs=in_specs, out_specs=out_specs, out_shape=out_shape,
        compiler_params=_cparams("parallel", "parallel"),
        name="norm_proj",
    )(*args)


def _gla_constants():
    c = GLA_CHUNK
    idx = np.arange(c)
    cm = np.zeros((2, GLA_LEVELS + 2, c, c), np.float32)
    lmask = np.zeros((2, GLA_LEVELS + 2, c, c), np.float32)
    cm[0, 0] = (idx[None, :] <= idx[:, None])
    cm[1, 0] = (idx[None, :] >= idx[:, None])
    for l in range(GLA_LEVELS):
        s = c >> (l + 1)
        blk = idx // (2 * s)
        second = (idx % (2 * s)) >= s
        same = blk[:, None] == blk[None, :]
        sep_f = blk * 2 * s + s - 1
        cm[0, 1 + l] = (idx[None, :] <= sep_f[:, None])
        lmask[0, l] = same & second[:, None] & (~second)[None, :]
        sep_b = blk * 2 * s + s
        cm[1, 1 + l] = (idx[None, :] >= sep_b[:, None])
        lmask[1, l] = same & (~second)[:, None] & second[None, :]
    cm[:, GLA_LEVELS + 1] = 1.0
    lmask[:, GLA_LEVELS] = np.eye(c)
    lmask[:, GLA_LEVELS + 1] = cm[:, 0]
    return jnp.asarray(cm.reshape(2, (GLA_LEVELS + 2) * c, c), BF16), jnp.asarray(lmask, F32)


def _gla_kernel(q_ref, k_ref, v_ref, ag_ref, r_ref, gw_ref, gb_ref, ng_ref, cm_ref, lm_ref,
                o_ref, la_ref, of_ref, ob_ref, qe_ref, kd_ref, dec_ref, st_ref, *, n_ctx_chunks, n_chunks, q_scale):
    c = GLA_CHUNK
    r = r_ref[0]
    for d in range(2):
        z = _dot(r, gw_ref[d]) + gb_ref[d]
        la_ref[d] = (jnp.minimum(z, 0.0) - jnp.log(1.0 + jnp.exp(-jnp.abs(z)))) * (1.0 / GLA_GATE_NORM)
    st_ref[...] = jnp.zeros_like(st_ref)
    out_refs = (of_ref, ob_ref)

    def chunk_rows(ci):
        return pl.ds(pl.multiple_of(ci * c, c), c)

    min_tot = None
    for d in range(2):
        m = jnp.min(jnp.sum(la_ref[d].reshape(n_chunks, c, la_ref.shape[2]), axis=1))
        min_tot = m if min_tot is None else jnp.minimum(min_tot, m)
    is_safe = min_tot >= -GLA_SAFE_RANGE

    def prep_fast(cp, carry):
        jobs = [(2 * cp + u, d) for u in range(2) for d in range(2)]
        rows = [chunk_rows(ci) for ci, _ in jobs]
        qs = {u: q_ref[0, chunk_rows(2 * cp + u), :].astype(F32) * q_scale for u in range(2)}
        ks = {u: k_ref[0, chunk_rows(2 * cp + u), :].astype(F32) for u in range(2)}
        bcs = {}
        for j, (ci, d) in enumerate(jobs):
            g_hi, g_lo = _split2(la_ref[d, rows[j], :])
            cm = cm_ref[d, 0:c, :]
            bcs[j] = _dot(cm, g_hi) + _dot(cm, g_lo)
        qes, kins, decs = {}, {}, {}
        for j, (ci, d) in enumerate(jobs):
            bc = bcs[j]
            qes[j] = (qs[j // 2] * jnp.exp(bc)).astype(BF16)
            kins[j] = ks[j // 2] * jnp.exp(-bc)
            decs[j] = jnp.exp(bc[c - 1:c] if d == 0 else bc[0:1])
        atts = {j: _nt(qes[j], kins[j].astype(BF16)) for j in range(len(jobs))}
        for j, (ci, d) in enumerate(jobs):
            att = (lm_ref[d, GLA_LEVELS + 1] * atts[j]).astype(BF16)
            out_refs[d][rows[j], :] = _dot(att, v_ref[0, rows[j], :])
            qe_ref[d, rows[j], :] = qes[j]
            kd_ref[d, rows[j], :] = (kins[j] * decs[j]).astype(BF16)
            dec_ref[d, pl.ds(pl.multiple_of(ci * 8, 8), 8), :] = jnp.broadcast_to(decs[j], (8, decs[j].shape[1]))
        return carry

    def prep(ci, carry):
        rows = chunk_rows(ci)
        q = q_ref[0, rows, :].astype(F32) * q_scale
        k = k_ref[0, rows, :].astype(F32)
        v = v_ref[0, rows, :]
        qk = _nt(q.astype(BF16), k.astype(BF16))
        for d in range(2):
            g_hi, g_lo = _split2(la_ref[d, rows, :])
            cm = cm_ref[d]
            cum = _dot(cm, g_hi) + _dot(cm, g_lo)
            bc = cum[0:c]
            tot = cum[(GLA_LEVELS + 1) * c:(GLA_LEVELS + 2) * c]
            att = lm_ref[d, GLA_LEVELS] * qk
            for l in range(GLA_LEVELS):
                ref_l = cum[(1 + l) * c:(2 + l) * c]
                eq = jnp.exp(jnp.minimum(bc - ref_l, 0.0))
                ek = jnp.exp(jnp.minimum(ref_l - bc, 0.0))
                att = att + lm_ref[d, l] * _nt((q * eq).astype(BF16), (k * ek).astype(BF16))
            out_refs[d][rows, :] = _dot(att.astype(BF16), v)
            qe_ref[d, rows, :] = (q * jnp.exp(bc)).astype(BF16)
            kd_ref[d, rows, :] = (k * jnp.exp(tot - bc)).astype(BF16)
            dec_ref[d, pl.ds(pl.multiple_of(ci * 8, 8), 8), :] = jnp.exp(tot[0:8])
        return carry

    @pl.when(is_safe)
    def _():
        lax.fori_loop(0, n_chunks // 2, prep_fast, 0)

    @pl.when(jnp.logical_not(is_safe))
    def _():
        lax.fori_loop(0, n_chunks, prep, 0)

    def scan(t, carry):
        cb = jnp.where(t < n_ctx_chunks, n_ctx_chunks - 1 - t, n_chunks - 1 - (t - n_ctx_chunks))
        for d, ci in ((0, t), (1, cb)):
            rows = chunk_rows(ci)
            st = st_ref[d]
            out_refs[d][rows, :] += _nt(qe_ref[d, rows, :], st.astype(BF16))
            dec = dec_ref[d, pl.ds(pl.multiple_of(ci * 8, 8), 1), :]
            st_ref[d] = st * dec + _tn(v_ref[0, rows, :], kd_ref[d, rows, :])
        return carry

    lax.fori_loop(0, n_chunks, scan, 0)
    o = of_ref[...] + ob_ref[...]
    ag = ag_ref[0].astype(F32)
    o_ref[0] = (_rms(o) * ng_ref[...] * _silu(ag)).astype(BF16)


def _gla_call(p0, r, gw_pad, gb, norm_g, consts, n_ctx_rows, col_q, col_k, col_v, col_g, dk, dv):
    b, t, _ = p0.shape
    n_chunks = t // GLA_CHUNK
    cm, lm = consts
    kern = functools.partial(_gla_kernel, n_ctx_chunks=n_ctx_rows // GLA_CHUNK, n_chunks=n_chunks,
                             q_scale=float(dk) ** -0.5)
    full = lambda shape: pl.BlockSpec(shape, lambda bi, h: (0,) * len(shape))
    return pl.pallas_call(
        kern,
        grid=(b, GLA_HEADS),
        in_specs=[pl.BlockSpec((1, t, dk), lambda bi, h: (bi, 0, col_q // dk + h)),
                  pl.BlockSpec((1, t, dk), lambda bi, h: (bi, 0, col_k // dk + h)),
                  pl.BlockSpec((1, t, dv), lambda bi, h: (bi, 0, col_v // dv + h)),
                  pl.BlockSpec((1, t, dv), lambda bi, h: (bi, 0, col_g // dv + h)),
                  pl.BlockSpec((1, t, LANES), lambda bi, h: (bi, 0, 0)),
                  pl.BlockSpec((2, LANES, dk), lambda bi, h: (0, 0, h)),
                  pl.BlockSpec((2, 1, dk), lambda bi, h: (0, 0, h)),
                  full((1, dv)), full(cm.shape), full(lm.shape)],
        out_specs=pl.BlockSpec((1, t, dv), lambda bi, h: (bi, 0, h)),
        out_shape=jax.ShapeDtypeStruct((b, t, GLA_HEADS * dv), BF16),
        scratch_shapes=[pltpu.VMEM((2, t, dk), F32), pltpu.VMEM((t, dv), F32), pltpu.VMEM((t, dv), F32),
                        pltpu.VMEM((2, t, dk), BF16), pltpu.VMEM((2, t, dk), BF16),
                        pltpu.VMEM((2, n_chunks * 8, dk), F32), pltpu.VMEM((2, dv, dk), F32)],
        compiler_params=_cparams("parallel", "parallel"),
        name="gla",
    )(p0, p0, p0, p0, r, gw_pad, gb, norm_g.reshape(1, dv), cm, lm)


def _swa_kernel(q_ref, k_ref, v_ref, sink_ref, o_ref, s_ref, *, n_ctx_rows, n_blocks):
    blk = SWA_BLOCK
    win = 3 * blk
    ncb = n_ctx_rows // blk
    nlb = n_blocks - ncb
    sink = sink_ref[0, :, 0:1] * float(np.log2(np.e))
    rel0 = (lax.broadcasted_iota(jnp.int32, (blk, win), 1) - lax.broadcasted_iota(jnp.int32, (blk, win), 0))
    no_bias = jnp.zeros((blk, n_ctx_rows), F32)

    def key_rows(j):
        kb0 = jnp.clip(j - ncb - 1, 0, nlb - 3)
        return kb0, pl.ds(pl.multiple_of(n_ctx_rows + kb0 * blk, blk), win)

    def scores(j, slot):
        kb0, rows = key_rows(j)
        q = q_ref[0, pl.ds(pl.multiple_of(j * blk, blk), blk), :]
        q4 = jnp.concatenate([q[:, g * blk:(g + 1) * blk] for g in range(SWA_GROUP)], axis=0)
        keys = jnp.concatenate([k_ref[0, rows, :], k_ref[0, 0:n_ctx_rows, :]], axis=0)
        rel = rel0 + jnp.where(j < ncb, 4 * win, (kb0 - (j - ncb)) * blk)
        bias = jnp.concatenate([jnp.where(jnp.abs(rel) <= SWA_WINDOW, 0.0, NEG_BIG), no_bias], axis=1)
        s_ref[slot] = _nt(q4, keys) + jnp.concatenate([bias] * SWA_GROUP, axis=0)

    def finish(j, slot):
        _, rows = key_rows(j)
        vals = jnp.concatenate([v_ref[0, rows, :], v_ref[0, 0:n_ctx_rows, :]], axis=0)
        s = s_ref[slot]
        m = jnp.maximum(jnp.max(s, axis=1, keepdims=True), sink)
        e = jnp.exp2(s - m)
        den = jnp.sum(e, axis=1, keepdims=True) + jnp.exp2(sink - m)
        o = _dot(e.astype(BF16), vals) / den
        o_ref[0, pl.ds(pl.multiple_of(j * blk, blk), blk), :] = jnp.concatenate(
            [o[g * blk:(g + 1) * blk] for g in range(SWA_GROUP)], axis=1).astype(BF16)

    scores(0, 0)

    def body(jj, carry):
        j0 = 2 * jj
        scores(j0 + 1, 1)
        finish(j0, 0)
        scores(j0 + 2, 0)
        finish(j0 + 1, 1)
        return carry

    lax.fori_loop(0, n_blocks // 2 - 1, body, 0)
    scores(n_blocks - 1, 1)
    finish(n_blocks - 2, 0)
    finish(n_blocks - 1, 1)


def _swa_call(p0, sink_col, n_ctx_rows, col_q, col_k, col_v, hd):
    b, t, _ = p0.shape
    blk = SWA_BLOCK
    nb = t // blk
    gq = SWA_GROUP * hd
    assert nb % 2 == 0 and nb - n_ctx_rows // blk >= 3 and SWA_WINDOW <= blk
    return pl.pallas_call(
        functools.partial(_swa_kernel, n_ctx_rows=n_ctx_rows, n_blocks=nb),
        grid=(b, SWA_KV_HEADS),
        in_specs=[pl.BlockSpec((1, t, gq), lambda bi, h: (bi, 0, col_q // gq + h)),
                  pl.BlockSpec((1, t, hd), lambda bi, h: (bi, 0, col_k // hd + h)),
                  pl.BlockSpec((1, t, hd), lambda bi, h: (bi, 0, col_v // hd + h)),
                  pl.BlockSpec((1, SWA_GROUP * blk, LANES), lambda bi, h: (h, 0, 0))],
        out_specs=pl.BlockSpec((1, t, gq), lambda bi, h: (bi, 0, h)),
        out_shape=jax.ShapeDtypeStruct((b, t, SWA_HEADS * hd), BF16),
        scratch_shapes=[pltpu.VMEM((2, SWA_GROUP * blk, 3 * blk + n_ctx_rows), F32)],
        compiler_params=_cparams("parallel", "parallel"),
        name="swa",
    )(p0, p0, p0, sink_col)


def _diff_kernel(lam_ref, q_ref, k_ref, v_ref, g_ref, o_ref, s_ref, kx_ref, vx_ref,
                 *, row_off, n_qblocks, lambda_init):
    tq = DIFF_QBLOCK
    hd2 = 2 * DIFF_HD
    n_keys = k_ref.shape[1]
    lam = lam_ref[...]
    s01 = jnp.sum(lam[0:1] * lam[1:2], axis=1, keepdims=True)
    s23 = jnp.sum(lam[2:3] * lam[3:4], axis=1, keepdims=True)
    lam_full = jnp.exp(s01) - jnp.exp(s23) + lambda_init
    lane = lax.broadcasted_iota(jnp.int32, (1, hd2), 1)
    first = lane < DIFF_HD
    col0 = lane == 0
    zero = jnp.zeros((), BF16)

    def map_norms(x):
        x2 = x.astype(F32)
        x2 = x2 * x2
        return (jnp.sqrt(jnp.sum(jnp.where(first, x2, 0.0), axis=1, keepdims=True)),
                jnp.sqrt(jnp.sum(jnp.where(first, 0.0, x2), axis=1, keepdims=True)))

    kn0, kn1 = map_norms(k_ref[0])
    kmax0 = jnp.max(kn0, axis=0, keepdims=True)
    kmax1 = jnp.max(kn1, axis=0, keepdims=True)
    qn0, qn1 = map_norms(q_ref[0, row_off:, :])
    is_safe = jnp.maximum(jnp.max(qn0 * kmax0), jnp.max(qn1 * kmax1)) <= DIFF_SAFE_BOUND

    def q_block(j):
        q = q_ref[0, pl.ds(pl.multiple_of(row_off + j * tq, tq), tq), :]
        return q, jnp.where(first, q, zero), jnp.where(first, zero, q)

    def write_out(j, o):
        y = _rms(o) * g_ref[...] * (1.0 - lambda_init)
        o_ref[0, pl.ds(pl.multiple_of(j * tq, tq), tq), :] = y.astype(BF16)

    def scores_fast(j, slot):
        q, qa, qb = q_block(j)
        b0, b1 = map_norms(q)
        xa = jnp.where(col0, -(b0 * kmax0), 0.0).astype(BF16)
        xb = jnp.where(col0, -(b1 * kmax1), 0.0).astype(BF16)
        qq = jnp.concatenate([jnp.concatenate([qa, xa], axis=1), jnp.concatenate([qb, xb], axis=1)], axis=0)
        s_ref[slot] = _nt(qq, kx_ref[...])

    def finish_fast(j, slot):
        e = jnp.exp2(s_ref[slot]).astype(BF16)
        ov = _dot(e, vx_ref[...])
        p0 = ov[0:tq, 0:hd2] / ov[0:tq, hd2:2 * hd2]
        p1 = ov[tq:2 * tq, 0:hd2] / ov[tq:2 * tq, hd2:2 * hd2]
        write_out(j, p0 - lam_full * p1)

    def scores_max(j, slot):
        _, qa, qb = q_block(j)
        s_ref[slot] = _nt(jnp.concatenate([qa, qb], axis=0), k_ref[0])

    def finish_max(j, slot):
        s = s_ref[slot]
        e = jnp.exp2(s - jnp.max(s, axis=1, keepdims=True))
        z = jnp.sum(e, axis=1, keepdims=True)
        coef = lam_full * z[0:tq] / z[tq:2 * tq]
        a = e[0:tq] - coef * e[tq:2 * tq]
        write_out(j, _dot(a.astype(BF16), v_ref[0]) * (1.0 / z[0:tq]))

    def run(scores, finish):
        scores(0, 0)

        def body(jj, carry):
            j0 = 2 * jj
            scores(j0 + 1, 1)
            finish(j0, 0)
            scores(j0 + 2, 0)
            finish(j0 + 1, 1)
            return carry

        lax.fori_loop(0, n_qblocks // 2 - 1, body, 0)
        scores(n_qblocks - 1, 1)
        finish(n_qblocks - 2, 0)
        finish(n_qblocks - 1, 1)

    @pl.when(is_safe)
    def _():
        kx_ref[:, 0:hd2] = k_ref[0]
        kx_ref[:, hd2:2 * hd2] = jnp.broadcast_to(jnp.where(col0, 1.0, 0.0).astype(BF16), (n_keys, hd2))
        vx_ref[:, 0:hd2] = v_ref[0]
        vx_ref[:, hd2:2 * hd2] = jnp.ones((n_keys, hd2), BF16)
        run(scores_fast, finish_fast)

    @pl.when(jnp.logical_not(is_safe))
    def _():
        run(scores_max, finish_max)


def _diff_call(p1, lam, subln_g, n_ctx_rows, n_heads, lambda_init):
    b, t, _ = p1.shape
    s_len = t - n_ctx_rows
    hd2 = 2 * DIFF_HD
    n_qblocks = s_len // DIFF_QBLOCK
    assert n_qblocks >= 2 and n_qblocks % 2 == 0
    return pl.pallas_call(
        functools.partial(_diff_kernel, row_off=n_ctx_rows, n_qblocks=n_qblocks, lambda_init=lambda_init),
        grid=(b, n_heads),
        in_specs=[pl.BlockSpec((4, DIFF_HD), lambda bi, h: (0, 0)),
                  pl.BlockSpec((1, t, hd2), lambda bi, h: (bi, 0, h)),
                  pl.BlockSpec((1, t, hd2), lambda bi, h: (bi, 0, n_heads + h)),
                  pl.BlockSpec((1, t, hd2), lambda bi, h: (bi, 0, 2 * n_heads + h)),
                  pl.BlockSpec((1, hd2), lambda bi, h: (0, 0))],
        out_specs=pl.BlockSpec((1, s_len, hd2), lambda bi, h: (bi, 0, h)),
        out_shape=jax.ShapeDtypeStruct((b, s_len, n_heads * hd2), BF16),
        scratch_shapes=[pltpu.VMEM((2, 2 * DIFF_QBLOCK, t), F32), pltpu.VMEM((t, 2 * hd2), BF16),
                        pltpu.VMEM((t, 2 * hd2), BF16)],
        compiler_params=_cparams("parallel", "parallel"),
        name="diff_attn",
    )(lam, p1, p1, p1, subln_g.reshape(1, hd2))


def _outproj_kernel(*refs, n_y, n_row_refs, n_ctx_blocks, row_block_off):
    y_refs = refs[:n_y]
    w_refs = refs[n_y:2 * n_y]
    row_refs = refs[2 * n_y:2 * n_y + n_row_refs]
    mod_ref, g_ref, wr_ref, xn_ref, h_ref, aff_ref = refs[2 * n_y + n_row_refs:]
    acc = _dot(y_refs[0][0], w_refs[0][...])
    for yr, wr in zip(y_refs[1:], w_refs[1:]):
        acc = acc + _dot(yr[0], wr[...])
    xn = _load_rows(row_refs, n_ctx_blocks, row_block_off) + mod_ref[0, 0, 2:3, :] * acc
    xn_ref[0] = xn
    h = _rms(xn) * g_ref[...] * (1.0 + mod_ref[0, 0, 4:5, :]) + mod_ref[0, 0, 3:4, :]
    h_bf = h.astype(BF16)
    h_ref[0] = h_bf
    logits = _nt(wr_ref[...].astype(BF16), h_bf)
    ex = jnp.exp(logits - jnp.max(logits, axis=0, keepdims=True))
    aff_ref[0] = ex / jnp.sum(ex, axis=0, keepdims=True)


def _outproj_call(ys, ws, stream, mod, g, w_router_t, row_block_off, n_rows, n_ctx_blocks):
    b, d = mod.shape[0], mod.shape[3]
    n_y = len(ys)
    e = w_router_t.shape[0]
    row_specs, row_args = _row_specs(stream, n_ctx_blocks, row_block_off)
    in_specs = ([pl.BlockSpec((1, ROW_TILE, y.shape[2]), lambda bi, i: (bi, i, 0)) for y in ys]
                + [pl.BlockSpec(w.shape, lambda bi, i: (0, 0), pipeline_mode=pl.Buffered(1)) for w in ws]
                + row_specs
                + [pl.BlockSpec((1, 1, 6, d),
                                lambda bi, i: (bi, jnp.where(i + row_block_off < n_ctx_blocks, 0, 1), 0, 0)),
                   pl.BlockSpec((1, d), lambda bi, i: (0, 0)),
                   pl.BlockSpec((e, d), lambda bi, i: (0, 0))])
    return pl.pallas_call(
        functools.partial(_outproj_kernel, n_y=n_y, n_row_refs=len(row_args), n_ctx_blocks=n_ctx_blocks,
                          row_block_off=row_block_off),
        grid=(b, n_rows // ROW_TILE),
        in_specs=in_specs,
        out_specs=[pl.BlockSpec((1, ROW_TILE, d), lambda bi, i: (bi, i, 0)),
                   pl.BlockSpec((1, ROW_TILE, d), lambda bi, i: (bi, i, 0)),
                   pl.BlockSpec((1, e, ROW_TILE), lambda bi, i: (bi, 0, i))],
        out_shape=[jax.ShapeDtypeStruct((b, n_rows, d), F32),
                   jax.ShapeDtypeStruct((b, n_rows, d), BF16),
                   jax.ShapeDtypeStruct((b, e, n_rows), F32)],
        compiler_params=_cparams("parallel", "parallel"),
        name="out_proj",
    )(*ys, *ws, *row_args, mod, g.reshape(1, d), w_router_t)


def _prefix_lanes(m, upper):
    e, n = m.shape
    carry = jnp.zeros((e, 1), F32)
    outs = []
    for blk in range(n // LANES):
        mb = m[:, blk * LANES:(blk + 1) * LANES]
        inc = _dot(mb.astype(BF16), upper)
        outs.append(inc - mb + carry)
        carry = carry + jnp.sum(mb, axis=1, keepdims=True)
    return jnp.concatenate(outs, axis=1)


def _transpose_exact(x, eye):
    hi, mid, lo = _split3(x)
    return _nt(eye, hi) + (_nt(eye, mid) + _nt(eye, lo))


def _route_kernel(aff_ref, pos_ref, gate_ref, posn_ref, lo_ref, hi_ref, *, segs):
    n_exp = aff_ref.shape[1]
    lo_acc = jnp.zeros((n_exp, LANES), F32)
    hi_acc = jnp.zeros((n_exp, LANES), F32)
    li = lax.broadcasted_iota(jnp.int32, (LANES, LANES), 0)
    lj = lax.broadcasted_iota(jnp.int32, (LANES, LANES), 1)
    upper = jnp.where(li <= lj, 1.0, 0.0).astype(BF16)
    eye = jnp.where(li == lj, 1.0, 0.0).astype(BF16)
    for (off, n, cap, slot_off) in segs:
        a = aff_ref[0, :, off:off + n]
        bits = pltpu.bitcast(a, jnp.int32)

        def search(it, thr):
            cand = thr | jnp.left_shift(jnp.int32(1), 30 - it)
            cnt = jnp.sum(jnp.where(bits >= cand, 1.0, 0.0), axis=1, keepdims=True)
            return jnp.where(cnt >= cap, cand, thr)

        thr = lax.fori_loop(0, 31, search, jnp.zeros((n_exp, 1), jnp.int32))
        gt = jnp.where(bits > thr, 1.0, 0.0)
        eq = jnp.where(bits == thr, 1.0, 0.0)
        need = cap - jnp.sum(gt, axis=1, keepdims=True)
        sel = gt + eq * jnp.where(_prefix_lanes(eq, upper) < need, 1.0, 0.0)
        slot = _prefix_lanes(sel, upper)
        pos = jnp.where(sel > 0.0, slot + slot_off, -1.0)
        pos_ref[0, :, off:off + n] = pos.astype(jnp.int32)
        gate_ref[0, :, off:off + n] = sel * a
        for blk in range(n // LANES):
            cols = slice(blk * LANES, (blk + 1) * LANES)
            rows = slice(off + blk * LANES, off + (blk + 1) * LANES)
            posn_ref[0, rows, :] = _transpose_exact(pos[:, cols], eye).astype(jnp.int32)
        assert cap <= 256 and (off + n) // LANES <= LANES
        tok = lax.broadcasted_iota(jnp.int32, (n, LANES), 0)
        blk_start = (lax.broadcasted_iota(jnp.int32, (n, LANES), 1) - off // LANES) * LANES
        lo_acc = lo_acc + _dot(slot.astype(BF16), jnp.where(tok == blk_start, 1.0, 0.0).astype(BF16))
        hi_acc = hi_acc + _dot((slot + sel).astype(BF16),
                               jnp.where(tok == blk_start + (LANES - 1), 1.0, 0.0).astype(BF16))
    lo_ref[0] = lo_acc.astype(jnp.int32)
    hi_ref[0] = hi_acc.astype(jnp.int32)


def _route_call(aff_t, segs):
    b, e, t = aff_t.shape
    return pl.pallas_call(
        functools.partial(_route_kernel, segs=tuple(segs)),
        grid=(b,),
        in_specs=[pl.BlockSpec((1, e, t), lambda bi: (bi, 0, 0))],
        out_specs=[pl.BlockSpec((1, e, t), lambda bi: (bi, 0, 0)),
                   pl.BlockSpec((1, e, t), lambda bi: (bi, 0, 0)),
                   pl.BlockSpec((1, t, e), lambda bi: (bi, 0, 0)),
                   pl.BlockSpec((1, e, LANES), lambda bi: (bi, 0, 0)),
                   pl.BlockSpec((1, e, LANES), lambda bi: (bi, 0, 0))],
        out_shape=[jax.ShapeDtypeStruct((b, e, t), jnp.int32),
                   jax.ShapeDtypeStruct((b, e, t), F32),
                   jax.ShapeDtypeStruct((b, t, e), jnp.int32),
                   jax.ShapeDtypeStruct((b, e, LANES), jnp.int32),
                   jax.ShapeDtypeStruct((b, e, LANES), jnp.int32)],
        compiler_params=_cparams("parallel"),
        name="route",
    )(aff_t)


def _gather_kernel(lo_ref, pos_ref, gate_ref, h_ref, o_ref, g_ref, *, segs):
    e = pl.program_id(1)
    prow = pos_ref[0, pl.ds(e, 1), :]
    grow = gate_ref[0, pl.ds(e, 1), :]

    def emit(slot0, n_slot, tok0, n_tok):
        slots = lax.broadcasted_iota(jnp.int32, (n_slot, 1), 0) + slot0
        hit = prow[:, tok0:tok0 + n_tok] == slots
        o_ref[0, 0, slot0:slot0 + n_slot, :] = _dot(jnp.where(hit, 1.0, 0.0).astype(BF16),
                                                    h_ref[0, tok0:tok0 + n_tok, :]).astype(BF16)
        gsel = jnp.sum(jnp.where(hit, grow[:, tok0:tok0 + n_tok], 0.0), axis=1, keepdims=True)
        g_ref[0, 0, slot0:slot0 + n_slot, :] = jnp.broadcast_to(gsel, (n_slot, LANES))

    for (off, n, cap, slot_off) in segs:
        half = cap // 2
        k0 = -(-(n * 5 // 8) // LANES) * LANES
        k1 = (n * 3 // 8) // LANES * LANES
        if cap % 32 or k0 >= n or k1 <= 0:
            emit(slot_off, cap, off, n)
            continue
        fits = jnp.logical_and(lo_ref[0, e, (off + k0) // LANES] >= half, lo_ref[0, e, (off + k1) // LANES] <= half)

        @pl.when(fits)
        def _():
            emit(slot_off, half, off, k0)
            emit(slot_off + half, half, off + k1, n - k1)

        @pl.when(jnp.logical_not(fits))
        def _():
            emit(slot_off, cap, off, n)


def _gather_call(lo, pos, gate, h, segs, n_slots):
    b, e, t = pos.shape
    d = h.shape[2]
    return pl.pallas_call(
        functools.partial(_gather_kernel, segs=tuple(segs)),
        grid=(b, e),
        in_specs=[pl.BlockSpec((1, e, LANES), lambda bi, ei: (bi, 0, 0), memory_space=pltpu.SMEM),
                  pl.BlockSpec((1, e, t), lambda bi, ei: (bi, 0, 0)),
                  pl.BlockSpec((1, e, t), lambda bi, ei: (bi, 0, 0)),
                  pl.BlockSpec((1, t, d), lambda bi, ei: (bi, 0, 0))],
        out_specs=[pl.BlockSpec((1, 1, n_slots, d), lambda bi, ei: (ei, bi, 0, 0)),
                   pl.BlockSpec((1, 1, n_slots, LANES), lambda bi, ei: (ei, bi, 0, 0))],
        out_shape=[jax.ShapeDtypeStruct((e, b, n_slots, d), BF16),
                   jax.ShapeDtypeStruct((e, b, n_slots, LANES), F32)],
        compiler_params=_cparams("parallel", "arbitrary"),
        name="gather",
    )(lo, pos, gate, h)


def _ffn_up_kernel(x_ref, wg_ref, wu_ref, o_ref, wgb_ref, wub_ref):
    @pl.when(pl.program_id(2) == 0)
    def _():
        wgb_ref[...] = wg_ref[0, 0].astype(BF16)
        wub_ref[...] = wu_ref[0, 0].astype(BF16)

    x = x_ref[0]
    g = _dot(x, wgb_ref[...])
    u = _dot(x, wub_ref[...])
    o_ref[0] = (_silu(g) * u).astype(BF16)


def _ffn_down_kernel(a_ref, w_ref, gate_ref, o_ref, wb_ref):
    @pl.when(pl.program_id(2) == 0)
    def _():
        wb_ref[...] = w_ref[0, 0].astype(BF16)

    gate = jnp.concatenate([gate_ref[0]] * (o_ref.shape[2] // LANES), axis=1)
    o_ref[0] = (_dot(a_ref[0], wb_ref[...]) * gate).astype(BF16)


def _ffn_call(xs, gates, w_gu, w_dn, layer):
    e, m, d = xs.shape
    f = w_dn.shape[2]
    tm = _pick(m, (1152, 1024, 768, 512, 256, 128, 64, 32, 16, 8))
    tf = _pick(f, (512, 256, 128))
    nf = f // tf
    act = pl.pallas_call(
        _ffn_up_kernel,
        grid=(e, nf, m // tm),
        in_specs=[pl.BlockSpec((1, tm, d), lambda ei, j, i: (ei, i, 0)),
                  pl.BlockSpec((1, 1, d, tf), lambda ei, j, i: (layer, ei, 0, j)),
                  pl.BlockSpec((1, 1, d, tf), lambda ei, j, i: (layer, ei, 0, j + nf))],
        out_specs=pl.BlockSpec((1, tm, tf), lambda ei, j, i: (ei, i, j)),
        out_shape=jax.ShapeDtypeStruct((e, m, f), BF16),
        scratch_shapes=[pltpu.VMEM((d, tf), BF16), pltpu.VMEM((d, tf), BF16)],
        compiler_params=_cparams("parallel", "arbitrary", "arbitrary"),
        name="ffn_up",
    )(xs, w_gu, w_gu)
    tn = _pick(d, (512, 256, 128))
    return pl.pallas_call(
        _ffn_down_kernel,
        grid=(e, d // tn, m // tm),
        in_specs=[pl.BlockSpec((1, tm, f), lambda ei, j, i: (ei, i, 0)),
                  pl.BlockSpec((1, 1, f, tn), lambda ei, j, i: (layer, ei, 0, j)),
                  pl.BlockSpec((1, tm, LANES), lambda ei, j, i: (ei, i, 0))],
        out_specs=pl.BlockSpec((1, tm, tn), lambda ei, j, i: (ei, i, j)),
        out_shape=jax.ShapeDtypeStruct((e, m, d), BF16),
        scratch_shapes=[pltpu.VMEM((f, tn), BF16)],
        compiler_params=_cparams("parallel", "arbitrary", "arbitrary"),
        name="ffn_down",
    )(act, w_dn, gates)


def _combine_kernel(lo_ref, hi_ref, posn_ref, y_ref, x_ref, mod_ref, gf_ref, o_ref,
                    *, cap_lat, cap_ctx, n_ctx_tiles, blocks_per_tile, window, final_norm):
    n_exp = posn_ref.shape[2]
    i = pl.program_id(1)
    lane = lax.broadcasted_iota(jnp.int32, (1, n_exp), 1)
    posn = posn_ref[0].astype(F32)

    def pcol(e):
        return jnp.sum(jnp.where(lane == e, posn, 0.0), axis=1, keepdims=True)

    def scatter(slot0, cap):
        slots = lax.broadcasted_iota(jnp.int32, (1, cap), 1).astype(F32) + slot0
        hots = [jnp.where(pcol(e) == slots, 1.0, 0.0).astype(BF16) for e in range(n_exp)]
        rows = [y_ref[e, 0, slot0:slot0 + cap, :] for e in range(n_exp)]
        return _dot(jnp.concatenate(hots, axis=1), jnp.concatenate(rows, axis=0))

    def scatter_window(starts):
        w_tot = n_exp * window
        grp = lax.broadcasted_iota(jnp.int32, (n_exp, w_tot), 1) // window
        spread = jnp.where(grp == lax.broadcasted_iota(jnp.int32, (n_exp, w_tot), 0), 1.0, 0.0).astype(BF16)
        bpos = _dot(posn.astype(BF16), spread)
        wlane = lax.broadcasted_iota(jnp.int32, (1, w_tot), 1)
        target = (wlane % window).astype(F32)
        for e, st in enumerate(starts):
            target = jnp.where(wlane // window == e, target + st.astype(F32), target)
        onehot = jnp.where(bpos == target, 1.0, 0.0).astype(BF16)
        rows = [y_ref[e, 0, pl.ds(pl.multiple_of(st, 16), window), :] for e, st in enumerate(starts)]
        return _dot(onehot, jnp.concatenate(rows, axis=0))

    def finish(total, g2):
        out = x_ref[0] + g2 * total
        if final_norm:
            out = _rms(out) * gf_ref[...]
        o_ref[0] = out

    blk0 = i * blocks_per_tile
    starts = []
    fits = None
    for e in range(n_exp):
        lo = lo_ref[0, e, blk0]
        hi = hi_ref[0, e, blk0 + blocks_per_tile - 1]
        st = jnp.minimum(lax.shift_left(lax.shift_right_logical(lo, 4), 4), cap_lat - window)
        ok = hi - st <= window
        fits = ok if fits is None else jnp.logical_and(fits, ok)
        starts.append(st)
    is_lat = i >= n_ctx_tiles

    @pl.when(jnp.logical_and(is_lat, fits))
    def _():
        finish(scatter_window(starts), mod_ref[0, 1, 5:6, :])

    @pl.when(jnp.logical_and(is_lat, jnp.logical_not(fits)))
    def _():
        finish(scatter(0, cap_lat), mod_ref[0, 1, 5:6, :])

    if n_ctx_tiles:
        @pl.when(jnp.logical_not(is_lat))
        def _():
            finish(scatter(cap_lat, cap_ctx), mod_ref[0, 0, 5:6, :])


def _combine_call(lo, hi, posn, y, xa, mod, g_final, cap_lat, cap_ctx, n_ctx_rows, final_norm):
    b, t, d = xa.shape
    e = posn.shape[2]
    n_slots = y.shape[2]
    tt = ROW_TILE
    assert n_ctx_rows % tt == 0 and t % tt == 0
    window = min(cap_lat, max(16, 2 * tt * EC_FACTOR // e))
    smem = lambda: pl.BlockSpec((1, e, LANES), lambda bi, i: (bi, 0, 0), memory_space=pltpu.SMEM)
    return pl.pallas_call(
        functools.partial(_combine_kernel, cap_lat=cap_lat, cap_ctx=cap_ctx, n_ctx_tiles=n_ctx_rows // tt,
                          blocks_per_tile=tt // LANES, window=window, final_norm=final_norm),
        grid=(b, t // tt),
        in_specs=[smem(), smem(),
                  pl.BlockSpec((1, tt, e), lambda bi, i: (bi, i, 0)),
                  pl.BlockSpec((e, 1, n_slots, d), lambda bi, i: (0, bi, 0, 0), pipeline_mode=pl.Buffered(1)),
                  pl.BlockSpec((1, tt, d), lambda bi, i: (bi, i, 0)),
                  pl.BlockSpec((1, 2, 6, d), lambda bi, i: (bi, 0, 0, 0)),
                  pl.BlockSpec((1, d), lambda bi, i: (0, 0))],
        out_specs=pl.BlockSpec((1, tt, d), lambda bi, i: (bi, i, 0)),
        out_shape=jax.ShapeDtypeStruct((b, t, d), F32),
        compiler_params=_cparams("parallel", "arbitrary"),
        name="combine",
    )(lo, hi, posn, y, xa, mod, g_final.reshape(1, d))


def _rope_tables(n_tokens, n_ctx_rows, head_dim):
    rows = n_tokens // GRID_W
    r = jnp.repeat(jnp.arange(rows, dtype=F32), GRID_W)
    col = jnp.tile(jnp.arange(GRID_W, dtype=F32), rows)
    axis_dim = head_dim // 2
    inv = ROPE_THETA ** (-jnp.arange(0, axis_dim, 2, dtype=F32) / axis_dim)
    ar, ac = r[:, None] * inv, col[:, None] * inv
    cos_p = jnp.concatenate([jnp.cos(ar), jnp.cos(ar), jnp.cos(ac), jnp.cos(ac)], axis=-1)
    sin_p = jnp.concatenate([-jnp.sin(ar), jnp.sin(ar), -jnp.sin(ac), jnp.sin(ac)], axis=-1)
    reps = LANES // head_dim
    cos_p, sin_p = jnp.tile(cos_p, (1, reps)), jnp.tile(sin_p, (1, reps))
    cos_t = jnp.concatenate([jnp.ones((n_ctx_rows, LANES), F32), cos_p], axis=0)
    sin_t = jnp.concatenate([jnp.zeros((n_ctx_rows, LANES), F32), sin_p], axis=0)
    return cos_t, sin_t


def _moe(h2, aff_t, x_res, mod, g_final, w_gu, w_dn, layer, segs, cap_lat, cap_ctx, n_ctx_rows, final_norm):
    b, t, d = h2.shape
    e = aff_t.shape[1]
    n_slots = cap_lat + cap_ctx
    pos, gate, posn, lo, hi = _route_call(aff_t, segs)
    xs, gs = _gather_call(lo, pos, gate, h2, segs, n_slots)
    y = _ffn_call(xs.reshape(e, b * n_slots, d), gs.reshape(e, b * n_slots, LANES), w_gu, w_dn, layer)
    return _combine_call(lo, hi, posn, y.reshape(e, b, n_slots, d), x_res, mod, g_final, cap_lat, cap_ctx,
                         n_ctx_rows, final_norm)


def kernel(x, c, ctx, c_ctx, w_ada, b_ada, g_norm_mix, g_norm_ffn, w_in_even, gla_gate_w, gla_gate_b, gla_norm_g, swa_sink, w_out_even, w_qkv_odd, diff_lambda, diff_subln_g, w_out_odd, w_router, w_gate_up, w_down, g_final):
    b, s_len, d = x.shape
    lc = ctx.shape[1]
    depth = w_ada.shape[0]
    assert depth == 2 and lc % ROW_TILE == 0 and s_len % ROW_TILE == 0 and s_len % GRID_W == 0
    t = lc + s_len
    ncb = lc // ROW_TILE
    gla_dk, gla_dv = d // 4 // GLA_HEADS, d // 2 // GLA_HEADS
    swa_hd = d // 2 // SWA_HEADS
    diff_heads = d // (2 * DIFF_HD)
    n_exp = w_router.shape[2]
    assert gla_dk == LANES and swa_hd == LANES and 2 * DIFF_HD == LANES

    n_rows = -(-(b + 1) // 8) * 8
    cc = jnp.concatenate([c, c_ctx[None, :], jnp.zeros((n_rows - b - 1, d), F32)], axis=0)
    m_all = _ada_call(cc, w_ada, b_ada).reshape(depth, n_rows, 6, d)

    def mod_for(layer):
        lat = m_all[layer, :b]
        ctxm = jnp.broadcast_to(m_all[layer, b][None], (b, 6, d))
        return jnp.stack([ctxm, lat], axis=1)

    xa = (ctx, x)

    mod0 = mod_for(0)
    w_in = w_in_even[0]
    sizes = (GLA_HEADS * gla_dk, GLA_HEADS * gla_dk, GLA_HEADS * gla_dv, GLA_HEADS * gla_dv,
             GLA_GATE_RANK, GLA_GATE_RANK, SWA_HEADS * swa_hd, SWA_KV_HEADS * swa_hd, SWA_KV_HEADS * swa_hd)
    offs = np.concatenate([[0], np.cumsum(sizes)])
    seg = lambda i: w_in[:, offs[i]:offs[i + 1]]
    w_main = jnp.concatenate([seg(0), seg(1), seg(2), seg(3), seg(6), seg(7), seg(8)], axis=1).astype(BF16)
    w_rank = jnp.concatenate([seg(4), seg(5), jnp.zeros((d, LANES - 2 * GLA_GATE_RANK), F32)],
                             axis=1).astype(BF16)
    col_aq, col_ak = 0, sizes[0]
    col_av = col_ak + sizes[1]
    col_ag = col_av + sizes[2]
    col_bq = col_ag + sizes[3]
    col_bk = col_bq + sizes[6]
    col_bv = col_bk + sizes[7]
    n_main = col_bv + sizes[8]
    swa_q_scale = float(swa_hd) ** -0.5 * float(np.log2(np.e))
    rope0 = [(col_bq <= ch * COL_CHUNK < col_bv, swa_q_scale if col_bq <= ch * COL_CHUNK < col_bk else 1.0)
             for ch in range(n_main // COL_CHUNK)]
    cos_b, sin_b = _rope_tables(s_len, lc, swa_hd)
    p0, r0 = _proj_call(xa, mod0, g_norm_mix[0], w_main, cos_b, sin_b, rope0, swa_hd // 4, ncb, wr=w_rank)

    gw = gla_gate_w[0]
    gw_pad = jnp.zeros((2, LANES, GLA_HEADS * gla_dk), F32)
    gw_pad = gw_pad.at[0, 0:GLA_GATE_RANK].set(gw[0]).at[1, GLA_GATE_RANK:2 * GLA_GATE_RANK].set(gw[1])
    a_out = _gla_call(p0, r0, gw_pad.astype(BF16), gla_gate_b[0].reshape(2, 1, -1), gla_norm_g[0],
                      _gla_constants(), lc, col_aq, col_ak, col_av, col_ag, gla_dk, gla_dv)
    sink_col = jnp.broadcast_to(
        jnp.repeat(swa_sink[0].reshape(SWA_KV_HEADS, SWA_GROUP), SWA_BLOCK, axis=1)[:, :, None],
        (SWA_KV_HEADS, SWA_GROUP * SWA_BLOCK, LANES))
    b_out = _swa_call(p0, sink_col, lc, col_bq, col_bk, col_bv, swa_hd)

    w_o = w_out_even[0].astype(BF16)
    n_a = GLA_HEADS * gla_dv
    w_router_t0 = jnp.transpose(w_router[0])
    xa1, h2, aff_t = _outproj_call([a_out, b_out], [w_o[:n_a], w_o[n_a:]], xa, mod0, g_norm_ffn[0],
                                   w_router_t0, 0, t, ncb)
    cap_lat = s_len * EC_FACTOR // n_exp
    cap_ctx = lc * EC_FACTOR // n_exp
    segs0 = [(lc, s_len, cap_lat, 0), (0, lc, cap_ctx, cap_lat)]
    xa2 = _moe(h2, aff_t, xa1, mod0, g_final, w_gate_up, w_down, 0, segs0, cap_lat, cap_ctx, lc, False)

    mod1 = mod_for(1)
    lambda_init = 0.8 - 0.6 * float(np.exp(-0.3 * 1))
    w_qkv = w_qkv_odd[0].astype(BF16)
    n_qkv = w_qkv.shape[1]
    q_scale = DIFF_HD ** -0.5 * float(np.log2(np.e))
    rope1 = [(ch * COL_CHUNK < 2 * d, q_scale if ch * COL_CHUNK < d else 1.0)
             for ch in range(n_qkv // COL_CHUNK)]
    cos_c, sin_c = _rope_tables(s_len, lc, DIFF_HD)
    (p1,) = _proj_call(xa2, mod1, g_norm_mix[1], w_qkv, cos_c, sin_c, rope1, DIFF_HD // 4, ncb)
    y1 = _diff_call(p1, diff_lambda[0], diff_subln_g[0], lc, diff_heads, lambda_init)
    x3, h2b, aff_tb = _outproj_call([y1], [w_out_odd[0].astype(BF16)], xa2, mod1, g_norm_ffn[1],
                                    jnp.transpose(w_router[1]), ncb, s_len, ncb)
    segs1 = [(0, s_len, cap_lat, 0)]
    return _moe(h2b, aff_tb, x3, mod1, g_final, w_gate_up, w_down, 1, segs1, cap_lat, 0, 0, True)
```

```python
import functools

import numpy as np
import jax
import jax.numpy as jnp
from jax import lax
from jax.experimental import pallas as pl
from jax.experimental.pallas import tpu as pltpu

F32 = jnp.float32
BF16 = jnp.bfloat16

GRID_W = 64
ROPE_THETA = 10000.0
NORM_EPS = 1e-6
GLA_HEADS = 4
GLA_GATE_RANK = 16
GLA_GATE_NORM = 16.0
SWA_HEADS = 8
SWA_KV_HEADS = 2
SWA_GROUP = SWA_HEADS // SWA_KV_HEADS
SWA_WINDOW = 128
SWA_BLOCK = 128
DIFF_HD = 64
N_EXPERTS = 16
EC_FACTOR = 2

LANES = 128
ROW_TILE = 256
COL_CHUNK = 256
GLA_CHUNK = 128
GLA_LEVELS = 7
GLA_SAFE_RANGE = 60.0
DIFF_QBLOCK = 256
DIFF_SAFE_BOUND = 60.0
VMEM_LIMIT = 56 * 1024 * 1024
NEG_BIG = -1e30


def _cparams(*sem):
    return pltpu.CompilerParams(dimension_semantics=sem, vmem_limit_bytes=VMEM_LIMIT)


def _pick(n, cands):
    for c in cands:
        if n % c == 0:
            return c
    raise ValueError(f"no tile for {n} in {cands}")


def _nt(a, b):
    return lax.dot_general(a, b, (((1,), (1,)), ((), ())), preferred_element_type=F32)


def _tn(a, b):
    return lax.dot_general(a, b, (((0,), (0,)), ((), ())), preferred_element_type=F32)


def _dot(a, b):
    return jnp.dot(a, b, preferred_element_type=F32)


def _split3(a):
    hi = a.astype(BF16)
    r1 = a - hi.astype(F32)
    mid = r1.astype(BF16)
    lo = (r1 - mid.astype(F32)).astype(BF16)
    return hi, mid, lo


def _split2(a):
    hi = a.astype(BF16)
    lo = (a - hi.astype(F32)).astype(BF16)
    return hi, lo


def _silu(x):
    return x * (1.0 / (1.0 + jnp.exp(-x)))


def _rms(x, eps=NORM_EPS):
    return x * lax.rsqrt(jnp.mean(x * x, axis=-1, keepdims=True) + eps)


def _row_specs(stream, n_ctx_blocks, off=0):
    if isinstance(stream, tuple):
        ctx, x = stream
        d = x.shape[2]
        return ([pl.BlockSpec((1, ROW_TILE, d), lambda bi, i: (bi, jnp.minimum(i + off, n_ctx_blocks - 1), 0)),
                 pl.BlockSpec((1, ROW_TILE, d), lambda bi, i: (bi, jnp.maximum(i + off - n_ctx_blocks, 0), 0))],
                [ctx, x])
    d = stream.shape[2]
    return [pl.BlockSpec((1, ROW_TILE, d), lambda bi, i: (bi, i + off, 0))], [stream]


def _load_rows(row_refs, n_ctx_blocks, off=0):
    if len(row_refs) == 2:
        return jnp.where(pl.program_id(1) + off < n_ctx_blocks, row_refs[0][0], row_refs[1][0])
    return row_refs[0][0]


def _ada_kernel(c_ref, w_ref, b_ref, o_ref):
    s = _silu(c_ref[...])
    s_hi, s_lo = _split2(s)
    w_hi, w_lo = _split2(w_ref[0])
    acc = _dot(s_hi, w_hi) + (_dot(s_lo, w_hi) + _dot(s_hi, w_lo))
    o_ref[0] = acc + b_ref[0]


def _ada_call(cc, w_ada, b_ada):
    depth, d, n = w_ada.shape
    r = cc.shape[0]
    tn = _pick(n, (768, 512, 256, 128))
    return pl.pallas_call(
        _ada_kernel,
        grid=(depth, n // tn),
        in_specs=[pl.BlockSpec((r, d), lambda l, j: (0, 0)),
                  pl.BlockSpec((1, d, tn), lambda l, j: (l, 0, j)),
                  pl.BlockSpec((1, 1, tn), lambda l, j: (l, 0, j))],
        out_specs=pl.BlockSpec((1, r, tn), lambda l, j: (l, 0, j)),
        out_shape=jax.ShapeDtypeStruct((depth, r, n), F32),
        compiler_params=_cparams("parallel", "parallel"),
        name="adaln",
    )(cc, w_ada, b_ada.reshape(depth, 1, n))


def _proj_kernel(*refs, n_row_refs, n_ctx_blocks, chunk_ops, quarter, has_r):
    row_refs = refs[:n_row_refs]
    mod_ref, g_ref, w_ref, cos_ref, sin_ref = refs[n_row_refs:n_row_refs + 5]
    rest = refs[n_row_refs + 5:]
    if has_r:
        wr_ref, o_ref, r_ref = rest
    else:
        (o_ref,) = rest
    x = _load_rows(row_refs, n_ctx_blocks)
    y = _rms(x) * g_ref[...]
    h = (y * (1.0 + mod_ref[0, 0, 1:2, :]) + mod_ref[0, 0, 0:1, :]).astype(BF16)
    reps = COL_CHUNK // LANES
    cos_t = jnp.concatenate([cos_ref[...]] * reps, axis=1)
    sin_t = jnp.concatenate([sin_ref[...]] * reps, axis=1)
    lane = lax.broadcasted_iota(jnp.int32, (1, COL_CHUNK), 1)
    first = (lane % (2 * quarter)) < quarter
    for c, (rope, scale) in enumerate(chunk_ops):
        acc = _dot(h, w_ref[:, c * COL_CHUNK:(c + 1) * COL_CHUNK])
        if rope:
            partner = jnp.where(first, pltpu.roll(acc, COL_CHUNK - quarter, 1), pltpu.roll(acc, quarter, 1))
            acc = acc * cos_t + partner * sin_t
        if scale != 1.0:
            acc = acc * scale
        o_ref[0, :, c * COL_CHUNK:(c + 1) * COL_CHUNK] = acc.astype(BF16)
    if has_r:
        r_ref[0] = _dot(h, wr_ref[...]).astype(BF16)


def _proj_call(stream, mod, g, w, cos_t, sin_t, chunk_ops, quarter, n_ctx_blocks, wr=None):
    d, n = w.shape
    t = cos_t.shape[0]
    b = mod.shape[0]
    assert n == len(chunk_ops) * COL_CHUNK and t % ROW_TILE == 0
    has_r = wr is not None
    row_specs, row_args = _row_specs(stream, n_ctx_blocks)
    in_specs = row_specs + [
        pl.BlockSpec((1, 1, 6, d), lambda bi, i: (bi, jnp.where(i < n_ctx_blocks, 0, 1), 0, 0)),
        pl.BlockSpec((1, d), lambda bi, i: (0, 0)),
        pl.BlockSpec((d, n), lambda bi, i: (0, 0), pipeline_mode=pl.Buffered(1)),
        pl.BlockSpec((ROW_TILE, LANES), lambda bi, i: (i, 0)),
        pl.BlockSpec((ROW_TILE, LANES), lambda bi, i: (i, 0))]
    args = row_args + [mod, g.reshape(1, d), w, cos_t, sin_t]
    out_specs = [pl.BlockSpec((1, ROW_TILE, n), lambda bi, i: (bi, i, 0))]
    out_shape = [jax.ShapeDtypeStruct((b, t, n), BF16)]
    if has_r:
        in_specs.append(pl.BlockSpec((d, LANES), lambda bi, i: (0, 0)))
        args.append(wr)
        out_specs.append(pl.BlockSpec((1, ROW_TILE, LANES), lambda bi, i: (bi, i, 0)))
        out_shape.append(jax.ShapeDtypeStruct((b, t, LANES), BF16))
    return pl.pallas_call(
        functools.partial(_proj_kernel, n_row_refs=len(row_args), n_ctx_blocks=n_ctx_blocks,
                          chunk_ops=tuple(chunk_ops), quarter=quarter, has_r=has_r),
        grid=(b, t // ROW_TILE),
        in_specs=in_specs, out_specs=out_specs, out_shape=out_shape,
        compiler_params=_cparams("parallel", "parallel"),
        name="norm_proj",
    )(*args)


def _gla_constants():
    c = GLA_CHUNK
    idx = np.arange(c)
    cm = np.zeros((2, GLA_LEVELS + 2, c, c), np.float32)
    lmask = np.zeros((2, GLA_LEVELS + 2, c, c), np.float32)
    cm[0, 0] = (idx[None, :] <= idx[:, None])
    cm[1, 0] = (idx[None, :] >= idx[:, None])
    for l in range(GLA_LEVELS):
        s = c >> (l + 1)
        blk = idx // (2 * s)
        second = (idx % (2 * s)) >= s
        same = blk[:, None] == blk[None, :]
        sep_f = blk * 2 * s + s - 1
        cm[0, 1 + l] = (idx[None, :] <= sep_f[:, None])
        lmask[0, l] = same & second[:, None] & (~second)[None, :]
        sep_b = blk * 2 * s + s
        cm[1, 1 + l] = (idx[None, :] >= sep_b[:, None])
        lmask[1, l] = same & (~second)[:, None] & second[None, :]
    cm[:, GLA_LEVELS + 1] = 1.0
    lmask[:, GLA_LEVELS] = np.eye(c)
    lmask[:, GLA_LEVELS + 1] = cm[:, 0]
    return jnp.asarray(cm.reshape(2, (GLA_LEVELS + 2) * c, c), BF16), jnp.asarray(lmask, F32)


def _gla_kernel(q_ref, k_ref, v_ref, ag_ref, r_ref, gw_ref, gb_ref, ng_ref, cm_ref, lm_ref,
                o_ref, la_ref, of_ref, ob_ref, qe_ref, u_ref, ss_ref, dec_ref, st_ref,
                *, n_ctx_chunks, n_chunks, q_scale):
    c = GLA_CHUNK
    dv = st_ref.shape[1]
    r = r_ref[0]
    for d in range(2):
        z = _dot(r, gw_ref[d]) + gb_ref[d]
        la_ref[d] = (jnp.minimum(z, 0.0) - jnp.log(1.0 + jnp.exp(-jnp.abs(z)))) * (1.0 / GLA_GATE_NORM)
    st_ref[...] = jnp.zeros_like(st_ref)
    out_refs = (of_ref, ob_ref)

    def chunk_rows(ci):
        return pl.ds(pl.multiple_of(ci * c, c), c)

    def state_rows(ci):
        return pl.ds(pl.multiple_of(ci * dv, dv), dv)

    min_tot = None
    for d in range(2):
        m = jnp.min(jnp.sum(la_ref[d].reshape(n_chunks, c, la_ref.shape[2]), axis=1))
        min_tot = m if min_tot is None else jnp.minimum(min_tot, m)
    is_safe = min_tot >= -GLA_SAFE_RANGE

    def prep_fast(cp, carry):
        jobs = [(2 * cp + u, d) for u in range(2) for d in range(2)]
        rows = [chunk_rows(ci) for ci, _ in jobs]
        qs = {u: q_ref[0, chunk_rows(2 * cp + u), :].astype(F32) * q_scale for u in range(2)}
        ks = {u: k_ref[0, chunk_rows(2 * cp + u), :].astype(F32) for u in range(2)}
        bcs = {}
        for j, (ci, d) in enumerate(jobs):
            g_hi, g_lo = _split2(la_ref[d, rows[j], :])
            cm = cm_ref[d, 0:c, :]
            bcs[j] = _dot(cm, g_hi) + _dot(cm, g_lo)
        qes, kins, decs = {}, {}, {}
        for j, (ci, d) in enumerate(jobs):
            bc = bcs[j]
            qes[j] = (qs[j // 2] * jnp.exp(bc)).astype(BF16)
            kins[j] = ks[j // 2] * jnp.exp(-bc)
            decs[j] = jnp.exp(bc[c - 1:c] if d == 0 else bc[0:1])
        atts = {j: _nt(qes[j], kins[j].astype(BF16)) for j in range(len(jobs))}
        for j, (ci, d) in enumerate(jobs):
            att = (lm_ref[d, GLA_LEVELS + 1] * atts[j]).astype(BF16)
            out_refs[d][rows[j], :] = _dot(att, v_ref[0, rows[j], :])
            qe_ref[d, rows[j], :] = qes[j]
            u_ref[d, state_rows(ci), :] = _tn(v_ref[0, rows[j], :], (kins[j] * decs[j]).astype(BF16))
            dec_ref[d, pl.ds(pl.multiple_of(ci * 8, 8), 8), :] = jnp.broadcast_to(decs[j], (8, decs[j].shape[1]))
        return carry

    def prep(ci, carry):
        rows = chunk_rows(ci)
        q = q_ref[0, rows, :].astype(F32) * q_scale
        k = k_ref[0, rows, :].astype(F32)
        v = v_ref[0, rows, :]
        qk = _nt(q.astype(BF16), k.astype(BF16))
        for d in range(2):
            g_hi, g_lo = _split2(la_ref[d, rows, :])
            cm = cm_ref[d]
            cum = _dot(cm, g_hi) + _dot(cm, g_lo)
            bc = cum[0:c]
            tot = cum[(GLA_LEVELS + 1) * c:(GLA_LEVELS + 2) * c]
            att = lm_ref[d, GLA_LEVELS] * qk
            for l in range(GLA_LEVELS):
                ref_l = cum[(1 + l) * c:(2 + l) * c]
                eq = jnp.exp(jnp.minimum(bc - ref_l, 0.0))
                ek = jnp.exp(jnp.minimum(ref_l - bc, 0.0))
                att = att + lm_ref[d, l] * _nt((q * eq).astype(BF16), (k * ek).astype(BF16))
            out_refs[d][rows, :] = _dot(att.astype(BF16), v)
            qe_ref[d, rows, :] = (q * jnp.exp(bc)).astype(BF16)
            u_ref[d, state_rows(ci), :] = _tn(v, (k * jnp.exp(tot - bc)).astype(BF16))
            dec_ref[d, pl.ds(pl.multiple_of(ci * 8, 8), 8), :] = jnp.exp(tot[0:8])
        return carry

    @pl.when(is_safe)
    def _():
        lax.fori_loop(0, n_chunks // 2, prep_fast, 0)

    @pl.when(jnp.logical_not(is_safe))
    def _():
        lax.fori_loop(0, n_chunks, prep, 0)

    def scan(t, carry):
        cb = jnp.where(t < n_ctx_chunks, n_ctx_chunks - 1 - t, n_chunks - 1 - (t - n_ctx_chunks))
        for d, ci in ((0, t), (1, cb)):
            st = st_ref[d]
            ss_ref[d, state_rows(ci), :] = st.astype(BF16)
            dec = dec_ref[d, pl.ds(pl.multiple_of(ci * 8, 8), 1), :]
            st_ref[d] = st * dec + u_ref[d, state_rows(ci), :]
        return carry

    lax.fori_loop(0, n_chunks, scan, 0)

    def inter(ci, carry):
        rows = chunk_rows(ci)
        for d in range(2):
            out_refs[d][rows, :] += _nt(qe_ref[d, rows, :], ss_ref[d, state_rows(ci), :])
        return carry

    lax.fori_loop(0, n_chunks, inter, 0, unroll=2)
    o = of_ref[...] + ob_ref[...]
    ag = ag_ref[0].astype(F32)
    o_ref[0] = (_rms(o) * ng_ref[...] * _silu(ag)).astype(BF16)


def _gla_call(p0, r, gw_pad, gb, norm_g, consts, n_ctx_rows, col_q, col_k, col_v, col_g, dk, dv):
    b, t, _ = p0.shape
    n_chunks = t // GLA_CHUNK
    cm, lm = consts
    kern = functools.partial(_gla_kernel, n_ctx_chunks=n_ctx_rows // GLA_CHUNK, n_chunks=n_chunks,
                             q_scale=float(dk) ** -0.5)
    full = lambda shape: pl.BlockSpec(shape, lambda bi, h: (0,) * len(shape))
    return pl.pallas_call(
        kern,
        grid=(b, GLA_HEADS),
        in_specs=[pl.BlockSpec((1, t, dk), lambda bi, h: (bi, 0, col_q // dk + h)),
                  pl.BlockSpec((1, t, dk), lambda bi, h: (bi, 0, col_k // dk + h)),
                  pl.BlockSpec((1, t, dv), lambda bi, h: (bi, 0, col_v // dv + h)),
                  pl.BlockSpec((1, t, dv), lambda bi, h: (bi, 0, col_g // dv + h)),
                  pl.BlockSpec((1, t, LANES), lambda bi, h: (bi, 0, 0)),
                  pl.BlockSpec((2, LANES, dk), lambda bi, h: (0, 0, h)),
                  pl.BlockSpec((2, 1, dk), lambda bi, h: (0, 0, h)),
                  full((1, dv)), full(cm.shape), full(lm.shape)],
        out_specs=pl.BlockSpec((1, t, dv), lambda bi, h: (bi, 0, h)),
        out_shape=jax.ShapeDtypeStruct((b, t, GLA_HEADS * dv), BF16),
        scratch_shapes=[pltpu.VMEM((2, t, dk), F32), pltpu.VMEM((t, dv), F32), pltpu.VMEM((t, dv), F32),
                        pltpu.VMEM((2, t, dk), BF16), pltpu.VMEM((2, n_chunks * dv, dk), F32),
                        pltpu.VMEM((2, n_chunks * dv, dk), BF16),
                        pltpu.VMEM((2, n_chunks * 8, dk), F32), pltpu.VMEM((2, dv, dk), F32)],
        compiler_params=_cparams("parallel", "parallel"),
        name="gla",
    )(p0, p0, p0, p0, r, gw_pad, gb, norm_g.reshape(1, dv), cm, lm)


def _swa_kernel(q_ref, k_ref, v_ref, sink_ref, o_ref, s_ref, *, n_ctx_rows, n_blocks):
    blk = SWA_BLOCK
    win = 3 * blk
    ncb = n_ctx_rows // blk
    nlb = n_blocks - ncb
    sink = sink_ref[0, :, 0:1] * float(np.log2(np.e))
    rel0 = (lax.broadcasted_iota(jnp.int32, (blk, win), 1) - lax.broadcasted_iota(jnp.int32, (blk, win), 0))
    no_bias = jnp.zeros((blk, n_ctx_rows), F32)

    def key_rows(j):
        kb0 = jnp.clip(j - ncb - 1, 0, nlb - 3)
        return kb0, pl.ds(pl.multiple_of(n_ctx_rows + kb0 * blk, blk), win)

    def scores(j, slot):
        kb0, rows = key_rows(j)
        q = q_ref[0, pl.ds(pl.multiple_of(j * blk, blk), blk), :]
        q4 = jnp.concatenate([q[:, g * blk:(g + 1) * blk] for g in range(SWA_GROUP)], axis=0)
        keys = jnp.concatenate([k_ref[0, rows, :], k_ref[0, 0:n_ctx_rows, :]], axis=0)
        rel = rel0 + jnp.where(j < ncb, 4 * win, (kb0 - (j - ncb)) * blk)
        bias = jnp.concatenate([jnp.where(jnp.abs(rel) <= SWA_WINDOW, 0.0, NEG_BIG), no_bias], axis=1)
        s_ref[slot] = _nt(q4, keys) + jnp.concatenate([bias] * SWA_GROUP, axis=0)

    def finish(j, slot):
        _, rows = key_rows(j)
        vals = jnp.concatenate([v_ref[0, rows, :], v_ref[0, 0:n_ctx_rows, :]], axis=0)
        s = s_ref[slot]
        m = jnp.maximum(jnp.max(s, axis=1, keepdims=True), sink)
        e = jnp.exp2(s - m)
        den = jnp.sum(e, axis=1, keepdims=True) + jnp.exp2(sink - m)
        o = _dot(e.astype(BF16), vals) / den
        o_ref[0, pl.ds(pl.multiple_of(j * blk, blk), blk), :] = jnp.concatenate(
            [o[g * blk:(g + 1) * blk] for g in range(SWA_GROUP)], axis=1).astype(BF16)

    scores(0, 0)

    def body(jj, carry):
        j0 = 2 * jj
        scores(j0 + 1, 1)
        finish(j0, 0)
        scores(j0 + 2, 0)
        finish(j0 + 1, 1)
        return carry

    lax.fori_loop(0, n_blocks // 2 - 1, body, 0)
    scores(n_blocks - 1, 1)
    finish(n_blocks - 2, 0)
    finish(n_blocks - 1, 1)


def _swa_call(p0, sink_col, n_ctx_rows, col_q, col_k, col_v, hd):
    b, t, _ = p0.shape
    blk = SWA_BLOCK
    nb = t // blk
    gq = SWA_GROUP * hd
    assert nb % 2 == 0 and nb - n_ctx_rows // blk >= 3 and SWA_WINDOW <= blk
    return pl.pallas_call(
        functools.partial(_swa_kernel, n_ctx_rows=n_ctx_rows, n_blocks=nb),
        grid=(b, SWA_KV_HEADS),
        in_specs=[pl.BlockSpec((1, t, gq), lambda bi, h: (bi, 0, col_q // gq + h)),
                  pl.BlockSpec((1, t, hd), lambda bi, h: (bi, 0, col_k // hd + h)),
                  pl.BlockSpec((1, t, hd), lambda bi, h: (bi, 0, col_v // hd + h)),
                  pl.BlockSpec((1, SWA_GROUP * blk, LANES), lambda bi, h: (h, 0, 0))],
        out_specs=pl.BlockSpec((1, t, gq), lambda bi, h: (bi, 0, h)),
        out_shape=jax.ShapeDtypeStruct((b, t, SWA_HEADS * hd), BF16),
        scratch_shapes=[pltpu.VMEM((2, SWA_GROUP * blk, 3 * blk + n_ctx_rows), F32)],
        compiler_params=_cparams("parallel", "parallel"),
        name="swa",
    )(p0, p0, p0, sink_col)


def _diff_kernel(lam_ref, q_ref, k_ref, v_ref, g_ref, o_ref, s_ref, kx_ref, vx_ref,
                 *, row_off, n_qblocks, lambda_init):
    tq = DIFF_QBLOCK
    hd2 = 2 * DIFF_HD
    n_keys = k_ref.shape[1]
    lam = lam_ref[...]
    s01 = jnp.sum(lam[0:1] * lam[1:2], axis=1, keepdims=True)
    s23 = jnp.sum(lam[2:3] * lam[3:4], axis=1, keepdims=True)
    lam_full = jnp.exp(s01) - jnp.exp(s23) + lambda_init
    lane = lax.broadcasted_iota(jnp.int32, (1, hd2), 1)
    first = lane < DIFF_HD
    col0 = lane == 0
    zero = jnp.zeros((), BF16)

    def map_norms(x):
        x2 = x.astype(F32)
        x2 = x2 * x2
        return (jnp.sqrt(jnp.sum(jnp.where(first, x2, 0.0), axis=1, keepdims=True)),
                jnp.sqrt(jnp.sum(jnp.where(first, 0.0, x2), axis=1, keepdims=True)))

    kn0, kn1 = map_norms(k_ref[0])
    kmax0 = jnp.max(kn0, axis=0, keepdims=True)
    kmax1 = jnp.max(kn1, axis=0, keepdims=True)
    qn0, qn1 = map_norms(q_ref[0, row_off:, :])
    is_safe = jnp.maximum(jnp.max(qn0 * kmax0), jnp.max(qn1 * kmax1)) <= DIFF_SAFE_BOUND

    def q_block(j):
        q = q_ref[0, pl.ds(pl.multiple_of(row_off + j * tq, tq), tq), :]
        return q, jnp.where(first, q, zero), jnp.where(first, zero, q)

    def write_out(j, o):
        y = _rms(o) * g_ref[...] * (1.0 - lambda_init)
        o_ref[0, pl.ds(pl.multiple_of(j * tq, tq), tq), :] = y.astype(BF16)

    def scores_fast(j, slot):
        q, qa, qb = q_block(j)
        b0, b1 = map_norms(q)
        xa = jnp.where(col0, -(b0 * kmax0), 0.0).astype(BF16)
        xb = jnp.where(col0, -(b1 * kmax1), 0.0).astype(BF16)
        qq = jnp.concatenate([jnp.concatenate([qa, xa], axis=1), jnp.concatenate([qb, xb], axis=1)], axis=0)
        s_ref[slot] = _nt(qq, kx_ref[...])

    def finish_fast(j, slot):
        e = jnp.exp2(s_ref[slot]).astype(BF16)
        ov = _dot(e, vx_ref[...])
        p0 = ov[0:tq, 0:hd2] / ov[0:tq, hd2:2 * hd2]
        p1 = ov[tq:2 * tq, 0:hd2] / ov[tq:2 * tq, hd2:2 * hd2]
        write_out(j, p0 - lam_full * p1)

    def scores_max(j, slot):
        _, qa, qb = q_block(j)
        s_ref[slot] = _nt(jnp.concatenate([qa, qb], axis=0), k_ref[0])

    def finish_max(j, slot):
        s = s_ref[slot]
        e = jnp.exp2(s - jnp.max(s, axis=1, keepdims=True))
        z = jnp.sum(e, axis=1, keepdims=True)
        coef = lam_full * z[0:tq] / z[tq:2 * tq]
        a = e[0:tq] - coef * e[tq:2 * tq]
        write_out(j, _dot(a.astype(BF16), v_ref[0]) * (1.0 / z[0:tq]))

    def run(scores, finish):
        scores(0, 0)

        def body(jj, carry):
            j0 = 2 * jj
            scores(j0 + 1, 1)
            finish(j0, 0)
            scores(j0 + 2, 0)
            finish(j0 + 1, 1)
            return carry

        lax.fori_loop(0, n_qblocks // 2 - 1, body, 0)
        scores(n_qblocks - 1, 1)
        finish(n_qblocks - 2, 0)
        finish(n_qblocks - 1, 1)

    @pl.when(is_safe)
    def _():
        kx_ref[:, 0:hd2] = k_ref[0]
        kx_ref[:, hd2:2 * hd2] = jnp.broadcast_to(jnp.where(col0, 1.0, 0.0).astype(BF16), (n_keys, hd2))
        vx_ref[:, 0:hd2] = v_ref[0]
        vx_ref[:, hd2:2 * hd2] = jnp.ones((n_keys, hd2), BF16)
        run(scores_fast, finish_fast)

    @pl.when(jnp.logical_not(is_safe))
    def _():
        run(scores_max, finish_max)


def _diff_call(p1, lam, subln_g, n_ctx_rows, n_heads, lambda_init):
    b, t, _ = p1.shape
    s_len = t - n_ctx_rows
    hd2 = 2 * DIFF_HD
    n_qblocks = s_len // DIFF_QBLOCK
    assert n_qblocks >= 2 and n_qblocks % 2 == 0
    return pl.pallas_call(
        functools.partial(_diff_kernel, row_off=n_ctx_rows, n_qblocks=n_qblocks, lambda_init=lambda_init),
        grid=(b, n_heads),
        in_specs=[pl.BlockSpec((4, DIFF_HD), lambda bi, h: (0, 0)),
                  pl.BlockSpec((1, t, hd2), lambda bi, h: (bi, 0, h)),
                  pl.BlockSpec((1, t, hd2), lambda bi, h: (bi, 0, n_heads + h)),
                  pl.BlockSpec((1, t, hd2), lambda bi, h: (bi, 0, 2 * n_heads + h)),
                  pl.BlockSpec((1, hd2), lambda bi, h: (0, 0))],
        out_specs=pl.BlockSpec((1, s_len, hd2), lambda bi, h: (bi, 0, h)),
        out_shape=jax.ShapeDtypeStruct((b, s_len, n_heads * hd2), BF16),
        scratch_shapes=[pltpu.VMEM((2, 2 * DIFF_QBLOCK, t), F32), pltpu.VMEM((t, 2 * hd2), BF16),
                        pltpu.VMEM((t, 2 * hd2), BF16)],
        compiler_params=_cparams("parallel", "parallel"),
        name="diff_attn",
    )(lam, p1, p1, p1, subln_g.reshape(1, hd2))


def _outproj_kernel(*refs, n_y, n_row_refs, n_ctx_blocks, row_block_off):
    y_refs = refs[:n_y]
    w_refs = refs[n_y:2 * n_y]
    row_refs = refs[2 * n_y:2 * n_y + n_row_refs]
    mod_ref, g_ref, wr_ref, xn_ref, h_ref, aff_ref = refs[2 * n_y + n_row_refs:]
    n_exp = aff_ref.shape[1]
    acc = _dot(y_refs[0][0], w_refs[0][...])
    for yr, wr in zip(y_refs[1:], w_refs[1:]):
        acc = acc + _dot(yr[0], wr[...])
    xn = _load_rows(row_refs, n_ctx_blocks, row_block_off) + mod_ref[0, 0, 2:3, :] * acc
    xn_ref[0] = xn
    h = _rms(xn) * g_ref[...] * (1.0 + mod_ref[0, 0, 4:5, :]) + mod_ref[0, 0, 3:4, :]
    h_bf = h.astype(BF16)
    h_ref[0] = h_bf
    logits = _dot(h_bf, wr_ref[...])
    logits = jnp.where(lax.broadcasted_iota(jnp.int32, (1, LANES), 1) < n_exp, logits, NEG_BIG)
    ex = jnp.exp(logits - jnp.max(logits, axis=1, keepdims=True))
    aff = ex / jnp.sum(ex, axis=1, keepdims=True)
    aff_ref[0] = jnp.transpose(aff)[0:n_exp]


def _outproj_call(ys, ws, stream, mod, g, w_router, row_block_off, n_rows, n_ctx_blocks):
    b, d = mod.shape[0], mod.shape[3]
    n_y = len(ys)
    e = w_router.shape[1]
    w_router_pad = jnp.concatenate([w_router, jnp.zeros((d, LANES - e), F32)], axis=1).astype(BF16)
    row_specs, row_args = _row_specs(stream, n_ctx_blocks, row_block_off)
    in_specs = ([pl.BlockSpec((1, ROW_TILE, y.shape[2]), lambda bi, i: (bi, i, 0)) for y in ys]
                + [pl.BlockSpec(w.shape, lambda bi, i: (0, 0), pipeline_mode=pl.Buffered(1)) for w in ws]
                + row_specs
                + [pl.BlockSpec((1, 1, 6, d),
                                lambda bi, i: (bi, jnp.where(i + row_block_off < n_ctx_blocks, 0, 1), 0, 0)),
                   pl.BlockSpec((1, d), lambda bi, i: (0, 0)),
                   pl.BlockSpec((d, LANES), lambda bi, i: (0, 0))])
    return pl.pallas_call(
        functools.partial(_outproj_kernel, n_y=n_y, n_row_refs=len(row_args), n_ctx_blocks=n_ctx_blocks,
                          row_block_off=row_block_off),
        grid=(b, n_rows // ROW_TILE),
        in_specs=in_specs,
        out_specs=[pl.BlockSpec((1, ROW_TILE, d), lambda bi, i: (bi, i, 0)),
                   pl.BlockSpec((1, ROW_TILE, d), lambda bi, i: (bi, i, 0)),
                   pl.BlockSpec((1, e, ROW_TILE), lambda bi, i: (bi, 0, i))],
        out_shape=[jax.ShapeDtypeStruct((b, n_rows, d), F32),
                   jax.ShapeDtypeStruct((b, n_rows, d), BF16),
                   jax.ShapeDtypeStruct((b, e, n_rows), F32)],
        compiler_params=_cparams("parallel", "parallel"),
        name="out_proj",
    )(*ys, *ws, *row_args, mod, g.reshape(1, d), w_router_pad)


def _prefix_lanes(m, upper):
    e, n = m.shape
    carry = jnp.zeros((e, 1), F32)
    outs = []
    for blk in range(n // LANES):
        mb = m[:, blk * LANES:(blk + 1) * LANES]
        inc = _dot(mb.astype(BF16), upper)
        outs.append(inc - mb + carry)
        carry = carry + jnp.sum(mb, axis=1, keepdims=True)
    return jnp.concatenate(outs, axis=1)


def _transpose_exact(x, eye):
    hi, mid, lo = _split3(x)
    return _nt(eye, hi) + (_nt(eye, mid) + _nt(eye, lo))


def _route_kernel(aff_ref, pos_ref, gate_ref, posn_ref, lo_ref, hi_ref, *, segs):
    n_exp = aff_ref.shape[1]
    lo_acc = jnp.zeros((n_exp, LANES), F32)
    hi_acc = jnp.zeros((n_exp, LANES), F32)
    li = lax.broadcasted_iota(jnp.int32, (LANES, LANES), 0)
    lj = lax.broadcasted_iota(jnp.int32, (LANES, LANES), 1)
    upper = jnp.where(li <= lj, 1.0, 0.0).astype(BF16)
    eye = jnp.where(li == lj, 1.0, 0.0).astype(BF16)
    for (off, n, cap, slot_off) in segs:
        a = aff_ref[0, :, off:off + n]
        bits = pltpu.bitcast(a, jnp.int32)

        def search(it, thr):
            cand = thr | jnp.left_shift(jnp.int32(1), 30 - it)
            cnt = jnp.sum(jnp.where(bits >= cand, 1.0, 0.0), axis=1, keepdims=True)
            return jnp.where(cnt >= cap, cand, thr)

        thr = lax.fori_loop(0, 31, search, jnp.zeros((n_exp, 1), jnp.int32))
        gt = jnp.where(bits > thr, 1.0, 0.0)
        eq = jnp.where(bits == thr, 1.0, 0.0)
        need = cap - jnp.sum(gt, axis=1, keepdims=True)
        sel = gt + eq * jnp.where(_prefix_lanes(eq, upper) < need, 1.0, 0.0)
        slot = _prefix_lanes(sel, upper)
        pos = jnp.where(sel > 0.0, slot + slot_off, -1.0)
        pos_ref[0, :, off:off + n] = pos.astype(jnp.int32)
        gate_ref[0, :, off:off + n] = sel * a
        for blk in range(n // LANES):
            cols = slice(blk * LANES, (blk + 1) * LANES)
            rows = slice(off + blk * LANES, off + (blk + 1) * LANES)
            posn_ref[0, rows, :] = _transpose_exact(pos[:, cols], eye).astype(jnp.int32)
        assert cap <= 256 and (off + n) // LANES <= LANES
        tok = lax.broadcasted_iota(jnp.int32, (n, LANES), 0)
        blk_start = (lax.broadcasted_iota(jnp.int32, (n, LANES), 1) - off // LANES) * LANES
        lo_acc = lo_acc + _dot(slot.astype(BF16), jnp.where(tok == blk_start, 1.0, 0.0).astype(BF16))
        hi_acc = hi_acc + _dot((slot + sel).astype(BF16),
                               jnp.where(tok == blk_start + (LANES - 1), 1.0, 0.0).astype(BF16))
    lo_ref[0] = lo_acc.astype(jnp.int32)
    hi_ref[0] = hi_acc.astype(jnp.int32)


def _route_call(aff_t, segs):
    b, e, t = aff_t.shape
    return pl.pallas_call(
        functools.partial(_route_kernel, segs=tuple(segs)),
        grid=(b,),
        in_specs=[pl.BlockSpec((1, e, t), lambda bi: (bi, 0, 0))],
        out_specs=[pl.BlockSpec((1, e, t), lambda bi: (bi, 0, 0)),
                   pl.BlockSpec((1, e, t), lambda bi: (bi, 0, 0)),
                   pl.BlockSpec((1, t, e), lambda bi: (bi, 0, 0)),
                   pl.BlockSpec((1, e, LANES), lambda bi: (bi, 0, 0)),
                   pl.BlockSpec((1, e, LANES), lambda bi: (bi, 0, 0))],
        out_shape=[jax.ShapeDtypeStruct((b, e, t), jnp.int32),
                   jax.ShapeDtypeStruct((b, e, t), F32),
                   jax.ShapeDtypeStruct((b, t, e), jnp.int32),
                   jax.ShapeDtypeStruct((b, e, LANES), jnp.int32),
                   jax.ShapeDtypeStruct((b, e, LANES), jnp.int32)],
        compiler_params=_cparams("parallel"),
        name="route",
    )(aff_t)


def _gather_kernel(lo_ref, pos_ref, gate_ref, h_ref, o_ref, g_ref, *, segs):
    e = pl.program_id(1)
    prow = pos_ref[0, pl.ds(e, 1), :]
    grow = gate_ref[0, pl.ds(e, 1), :]

    def emit(slot0, n_slot, tok0, n_tok):
        slots = lax.broadcasted_iota(jnp.int32, (n_slot, 1), 0) + slot0
        hit = prow[:, tok0:tok0 + n_tok] == slots
        o_ref[0, 0, slot0:slot0 + n_slot, :] = _dot(jnp.where(hit, 1.0, 0.0).astype(BF16),
                                                    h_ref[0, tok0:tok0 + n_tok, :]).astype(BF16)
        gsel = jnp.sum(jnp.where(hit, grow[:, tok0:tok0 + n_tok], 0.0), axis=1, keepdims=True)
        g_ref[0, 0, slot0:slot0 + n_slot, :] = jnp.broadcast_to(gsel, (n_slot, LANES))

    for (off, n, cap, slot_off) in segs:
        half = cap // 2
        k0 = -(-(n * 5 // 8) // LANES) * LANES
        k1 = (n * 3 // 8) // LANES * LANES
        if cap % 32 or k0 >= n or k1 <= 0:
            emit(slot_off, cap, off, n)
            continue
        fits = jnp.logical_and(lo_ref[0, e, (off + k0) // LANES] >= half, lo_ref[0, e, (off + k1) // LANES] <= half)

        @pl.when(fits)
        def _():
            emit(slot_off, half, off, k0)
            emit(slot_off + half, half, off + k1, n - k1)

        @pl.when(jnp.logical_not(fits))
        def _():
            emit(slot_off, cap, off, n)


def _slot_window(tt, cap_lat, n_exp):
    return min(cap_lat, max(16, 2 * tt * EC_FACTOR // n_exp))


def _gather_call(lo, pos, gate, h, segs, n_slots):
    b, e, t = pos.shape
    d = h.shape[2]
    return pl.pallas_call(
        functools.partial(_gather_kernel, segs=tuple(segs)),
        grid=(b, e),
        in_specs=[pl.BlockSpec((1, e, LANES), lambda bi, ei: (bi, 0, 0), memory_space=pltpu.SMEM),
                  pl.BlockSpec((1, e, t), lambda bi, ei: (bi, 0, 0)),
                  pl.BlockSpec((1, e, t), lambda bi, ei: (bi, 0, 0)),
                  pl.BlockSpec((1, t, d), lambda bi, ei: (bi, 0, 0))],
        out_specs=[pl.BlockSpec((1, 1, n_slots, d), lambda bi, ei: (ei, bi, 0, 0)),
                   pl.BlockSpec((1, 1, n_slots, LANES), lambda bi, ei: (ei, bi, 0, 0))],
        out_shape=[jax.ShapeDtypeStruct((e, b, n_slots, d), BF16),
                   jax.ShapeDtypeStruct((e, b, n_slots, LANES), F32)],
        compiler_params=_cparams("parallel", "arbitrary"),
        name="gather",
    )(lo, pos, gate, h)


def _ffn_up_kernel(x_ref, wg_ref, wu_ref, o_ref, wgb_ref, wub_ref):
    @pl.when(pl.program_id(2) == 0)
    def _():
        wgb_ref[...] = wg_ref[0, 0].astype(BF16)
        wub_ref[...] = wu_ref[0, 0].astype(BF16)

    x = x_ref[0]
    g = _dot(x, wgb_ref[...])
    u = _dot(x, wub_ref[...])
    o_ref[0] = (_silu(g) * u).astype(BF16)


def _ffn_down_kernel(a_ref, w_ref, gate_ref, o_ref, wb_ref):
    @pl.when(pl.program_id(2) == 0)
    def _():
        wb_ref[...] = w_ref[0, 0].astype(BF16)

    gate = jnp.concatenate([gate_ref[0]] * (o_ref.shape[2] // LANES), axis=1)
    o_ref[0] = (_dot(a_ref[0], wb_ref[...]) * gate).astype(BF16)


def _ffn_call(xs, gates, w_gu, w_dn, layer):
    e, m, d = xs.shape
    f = w_dn.shape[2]
    tm = _pick(m, (1152, 1024, 768, 512, 256, 128, 64, 32, 16, 8))
    tf = _pick(f, (512, 256, 128))
    nf = f // tf
    act = pl.pallas_call(
        _ffn_up_kernel,
        grid=(e, nf, m // tm),
        in_specs=[pl.BlockSpec((1, tm, d), lambda ei, j, i: (ei, i, 0)),
                  pl.BlockSpec((1, 1, d, tf), lambda ei, j, i: (layer, ei, 0, j)),
                  pl.BlockSpec((1, 1, d, tf), lambda ei, j, i: (layer, ei, 0, j + nf))],
        out_specs=pl.BlockSpec((1, tm, tf), lambda ei, j, i: (ei, i, j)),
        out_shape=jax.ShapeDtypeStruct((e, m, f), BF16),
        scratch_shapes=[pltpu.VMEM((d, tf), BF16), pltpu.VMEM((d, tf), BF16)],
        compiler_params=_cparams("parallel", "arbitrary", "arbitrary"),
        name="ffn_up",
    )(xs, w_gu, w_gu)
    tn = _pick(d, (1024, 512, 256, 128))
    return pl.pallas_call(
        _ffn_down_kernel,
        grid=(e, d // tn, m // tm),
        in_specs=[pl.BlockSpec((1, tm, f), lambda ei, j, i: (ei, i, 0)),
                  pl.BlockSpec((1, 1, f, tn), lambda ei, j, i: (layer, ei, 0, j)),
                  pl.BlockSpec((1, tm, LANES), lambda ei, j, i: (ei, i, 0))],
        out_specs=pl.BlockSpec((1, tm, tn), lambda ei, j, i: (ei, i, j)),
        out_shape=jax.ShapeDtypeStruct((e, m, d), BF16),
        scratch_shapes=[pltpu.VMEM((f, tn), BF16)],
        compiler_params=_cparams("parallel", "arbitrary", "arbitrary"),
        name="ffn_down",
    )(act, w_dn, gates)


def _combine_kernel(lo_ref, hi_ref, posn_ref, y_ref, x_ref, mod_ref, gf_ref, o_ref,
                    *, cap_lat, cap_ctx, n_ctx_tiles, blocks_per_tile, window, final_norm):
    n_exp = posn_ref.shape[2]
    i = pl.program_id(1)
    lane = lax.broadcasted_iota(jnp.int32, (1, n_exp), 1)
    posn = posn_ref[0].astype(F32)

    def pcol(e):
        return jnp.sum(jnp.where(lane == e, posn, 0.0), axis=1, keepdims=True)

    def scatter(slot0, cap):
        slots = lax.broadcasted_iota(jnp.int32, (1, cap), 1).astype(F32) + slot0
        hots = [jnp.where(pcol(e) == slots, 1.0, 0.0).astype(BF16) for e in range(n_exp)]
        rows = [y_ref[e, 0, slot0:slot0 + cap, :] for e in range(n_exp)]
        return _dot(jnp.concatenate(hots, axis=1), jnp.concatenate(rows, axis=0))

    def scatter_window(starts):
        w_tot = n_exp * window
        grp = lax.broadcasted_iota(jnp.int32, (n_exp, w_tot), 1) // window
        spread = jnp.where(grp == lax.broadcasted_iota(jnp.int32, (n_exp, w_tot), 0), 1.0, 0.0).astype(BF16)
        bpos = _dot(posn.astype(BF16), spread)
        wlane = lax.broadcasted_iota(jnp.int32, (1, w_tot), 1)
        target = (wlane % window).astype(F32)
        for e, st in enumerate(starts):
            target = jnp.where(wlane // window == e, target + st.astype(F32), target)
        onehot = jnp.where(bpos == target, 1.0, 0.0).astype(BF16)
        rows = [y_ref[e, 0, pl.ds(pl.multiple_of(st, 16), window), :] for e, st in enumerate(starts)]
        return _dot(onehot, jnp.concatenate(rows, axis=0))

    def finish(total, g2):
        out = x_ref[0] + g2 * total
        if final_norm:
            out = _rms(out) * gf_ref[...]
        o_ref[0] = out

    blk0 = i * blocks_per_tile
    starts = []
    fits = None
    for e in range(n_exp):
        lo = lo_ref[0, e, blk0]
        hi = hi_ref[0, e, blk0 + blocks_per_tile - 1]
        st = jnp.minimum(lax.shift_left(lax.shift_right_logical(lo, 4), 4), cap_lat - window)
        ok = hi - st <= window
        fits = ok if fits is None else jnp.logical_and(fits, ok)
        starts.append(st)
    is_lat = i >= n_ctx_tiles

    @pl.when(jnp.logical_and(is_lat, fits))
    def _():
        finish(scatter_window(starts), mod_ref[0, 1, 5:6, :])

    @pl.when(jnp.logical_and(is_lat, jnp.logical_not(fits)))
    def _():
        finish(scatter(0, cap_lat), mod_ref[0, 1, 5:6, :])

    if n_ctx_tiles:
        @pl.when(jnp.logical_not(is_lat))
        def _():
            finish(scatter(cap_lat, cap_ctx), mod_ref[0, 0, 5:6, :])


def _combine_call(lo, hi, posn, y, xa, mod, g_final, cap_lat, cap_ctx, n_ctx_rows, final_norm):
    b, t, d = xa.shape
    e = posn.shape[2]
    n_slots = y.shape[2]
    tt = ROW_TILE
    assert n_ctx_rows % tt == 0 and t % tt == 0
    window = _slot_window(tt, cap_lat, e)
    smem = lambda: pl.BlockSpec((1, e, LANES), lambda bi, i: (bi, 0, 0), memory_space=pltpu.SMEM)
    return pl.pallas_call(
        functools.partial(_combine_kernel, cap_lat=cap_lat, cap_ctx=cap_ctx, n_ctx_tiles=n_ctx_rows // tt,
                          blocks_per_tile=tt // LANES, window=window, final_norm=final_norm),
        grid=(b, t // tt),
        in_specs=[smem(), smem(),
                  pl.BlockSpec((1, tt, e), lambda bi, i: (bi, i, 0)),
                  pl.BlockSpec((e, 1, n_slots, d), lambda bi, i: (0, bi, 0, 0), pipeline_mode=pl.Buffered(1)),
                  pl.BlockSpec((1, tt, d), lambda bi, i: (bi, i, 0)),
                  pl.BlockSpec((1, 2, 6, d), lambda bi, i: (bi, 0, 0, 0)),
                  pl.BlockSpec((1, d), lambda bi, i: (0, 0))],
        out_specs=pl.BlockSpec((1, tt, d), lambda bi, i: (bi, i, 0)),
        out_shape=jax.ShapeDtypeStruct((b, t, d), F32),
        compiler_params=_cparams("parallel", "arbitrary"),
        name="combine",
    )(lo, hi, posn, y, xa, mod, g_final.reshape(1, d))


def _rope_tables(n_tokens, n_ctx_rows, head_dim):
    rows = n_tokens // GRID_W
    r = jnp.repeat(jnp.arange(rows, dtype=F32), GRID_W)
    col = jnp.tile(jnp.arange(GRID_W, dtype=F32), rows)
    axis_dim = head_dim // 2
    inv = ROPE_THETA ** (-jnp.arange(0, axis_dim, 2, dtype=F32) / axis_dim)
    ar, ac = r[:, None] * inv, col[:, None] * inv
    cos_p = jnp.concatenate([jnp.cos(ar), jnp.cos(ar), jnp.cos(ac), jnp.cos(ac)], axis=-1)
    sin_p = jnp.concatenate([-jnp.sin(ar), jnp.sin(ar), -jnp.sin(ac), jnp.sin(ac)], axis=-1)
    reps = LANES // head_dim
    cos_p, sin_p = jnp.tile(cos_p, (1, reps)), jnp.tile(sin_p, (1, reps))
    cos_t = jnp.concatenate([jnp.ones((n_ctx_rows, LANES), F32), cos_p], axis=0)
    sin_t = jnp.concatenate([jnp.zeros((n_ctx_rows, LANES), F32), sin_p], axis=0)
    return cos_t, sin_t


def _moe(h2, aff_t, x_res, mod, g_final, w_gu, w_dn, layer, segs, cap_lat, cap_ctx, n_ctx_rows, final_norm):
    b, t, d = h2.shape
    e = aff_t.shape[1]
    n_slots = cap_lat + cap_ctx
    pos, gate, posn, lo, hi = _route_call(aff_t, segs)
    xs, gs = _gather_call(lo, pos, gate, h2, segs, n_slots)
    y = _ffn_call(xs.reshape(e, b * n_slots, d), gs.reshape(e, b * n_slots, LANES), w_gu, w_dn, layer)
    return _combine_call(lo, hi, posn, y.reshape(e, b, n_slots, d), x_res, mod, g_final, cap_lat, cap_ctx,
                         n_ctx_rows, final_norm)


def kernel(x, c, ctx, c_ctx, w_ada, b_ada, g_norm_mix, g_norm_ffn, w_in_even, gla_gate_w, gla_gate_b, gla_norm_g, swa_sink, w_out_even, w_qkv_odd, diff_lambda, diff_subln_g, w_out_odd, w_router, w_gate_up, w_down, g_final):
    b, s_len, d = x.shape
    lc = ctx.shape[1]
    depth = w_ada.shape[0]
    assert depth == 2 and lc % ROW_TILE == 0 and s_len % ROW_TILE == 0 and s_len % GRID_W == 0
    t = lc + s_len
    ncb = lc // ROW_TILE
    gla_dk, gla_dv = d // 4 // GLA_HEADS, d // 2 // GLA_HEADS
    swa_hd = d // 2 // SWA_HEADS
    diff_heads = d // (2 * DIFF_HD)
    n_exp = w_router.shape[2]
    assert gla_dk == LANES and swa_hd == LANES and 2 * DIFF_HD == LANES

    n_rows = -(-(b + 1) // 8) * 8
    cc = jnp.concatenate([c, c_ctx[None, :], jnp.zeros((n_rows - b - 1, d), F32)], axis=0)
    m_all = _ada_call(cc, w_ada, b_ada).reshape(depth, n_rows, 6, d)

    def mod_for(layer):
        lat = m_all[layer, :b]
        ctxm = jnp.broadcast_to(m_all[layer, b][None], (b, 6, d))
        return jnp.stack([ctxm, lat], axis=1)

    xa = (ctx, x)

    mod0 = mod_for(0)
    w_in = w_in_even[0]
    sizes = (GLA_HEADS * gla_dk, GLA_HEADS * gla_dk, GLA_HEADS * gla_dv, GLA_HEADS * gla_dv,
             GLA_GATE_RANK, GLA_GATE_RANK, SWA_HEADS * swa_hd, SWA_KV_HEADS * swa_hd, SWA_KV_HEADS * swa_hd)
    offs = np.concatenate([[0], np.cumsum(sizes)])
    seg = lambda i: w_in[:, offs[i]:offs[i + 1]]
    w_main = jnp.concatenate([seg(0), seg(1), seg(2), seg(3), seg(6), seg(7), seg(8)], axis=1).astype(BF16)
    w_rank = jnp.concatenate([seg(4), seg(5), jnp.zeros((d, LANES - 2 * GLA_GATE_RANK), F32)],
                             axis=1).astype(BF16)
    col_aq, col_ak = 0, sizes[0]
    col_av = col_ak + sizes[1]
    col_ag = col_av + sizes[2]
    col_bq = col_ag + sizes[3]
    col_bk = col_bq + sizes[6]
    col_bv = col_bk + sizes[7]
    n_main = col_bv + sizes[8]
    swa_q_scale = float(swa_hd) ** -0.5 * float(np.log2(np.e))
    rope0 = [(col_bq <= ch * COL_CHUNK < col_bv, swa_q_scale if col_bq <= ch * COL_CHUNK < col_bk else 1.0)
             for ch in range(n_main // COL_CHUNK)]
    cos_b, sin_b = _rope_tables(s_len, lc, swa_hd)
    p0, r0 = _proj_call(xa, mod0, g_norm_mix[0], w_main, cos_b, sin_b, rope0, swa_hd // 4, ncb, wr=w_rank)

    gw = gla_gate_w[0]
    gw_pad = jnp.zeros((2, LANES, GLA_HEADS * gla_dk), F32)
    gw_pad = gw_pad.at[0, 0:GLA_GATE_RANK].set(gw[0]).at[1, GLA_GATE_RANK:2 * GLA_GATE_RANK].set(gw[1])
    a_out = _gla_call(p0, r0, gw_pad.astype(BF16), gla_gate_b[0].reshape(2, 1, -1), gla_norm_g[0],
                      _gla_constants(), lc, col_aq, col_ak, col_av, col_ag, gla_dk, gla_dv)
    sink_col = jnp.broadcast_to(
        jnp.repeat(swa_sink[0].reshape(SWA_KV_HEADS, SWA_GROUP), SWA_BLOCK, axis=1)[:, :, None],
        (SWA_KV_HEADS, SWA_GROUP * SWA_BLOCK, LANES))
    b_out = _swa_call(p0, sink_col, lc, col_bq, col_bk, col_bv, swa_hd)

    w_o = w_out_even[0].astype(BF16)
    n_a = GLA_HEADS * gla_dv
    xa1, h2, aff_t = _outproj_call([a_out, b_out], [w_o[:n_a], w_o[n_a:]], xa, mod0, g_norm_ffn[0],
                                   w_router[0], 0, t, ncb)
    cap_lat = s_len * EC_FACTOR // n_exp
    cap_ctx = lc * EC_FACTOR // n_exp
    segs0 = [(lc, s_len, cap_lat, 0), (0, lc, cap_ctx, cap_lat)]
    xa2 = _moe(h2, aff_t, xa1, mod0, g_final, w_gate_up, w_down, 0, segs0, cap_lat, cap_ctx, lc, False)

    mod1 = mod_for(1)
    lambda_init = 0.8 - 0.6 * float(np.exp(-0.3 * 1))
    w_qkv = w_qkv_odd[0].astype(BF16)
    n_qkv = w_qkv.shape[1]
    q_scale = DIFF_HD ** -0.5 * float(np.log2(np.e))
    rope1 = [(ch * COL_CHUNK < 2 * d, q_scale if ch * COL_CHUNK < d else 1.0)
             for ch in range(n_qkv // COL_CHUNK)]
    cos_c, sin_c = _rope_tables(s_len, lc, DIFF_HD)
    (p1,) = _proj_call(xa2, mod1, g_norm_mix[1], w_qkv, cos_c, sin_c, rope1, DIFF_HD // 4, ncb)
    y1 = _diff_call(p1, diff_lambda[0], diff_subln_g[0], lc, diff_heads, lambda_init)
    x3, h2b, aff_tb = _outproj_call([y1], [w_out_odd[0].astype(BF16)], xa2, mod1, g_norm_ffn[1],
                                    w_router[1], ncb, s_len, ncb)
    segs1 = [(0, s_len, cap_lat, 0)]
    return _moe(h2b, aff_tb, x3, mod1, g_final, w_gate_up, w_down, 1, segs1, cap_lat, 0, 0, True)
```

```python
import functools

import numpy as np
import jax
import jax.numpy as jnp
from jax import lax
from jax.experimental import pallas as pl
from jax.experimental.pallas import tpu as pltpu

F32 = jnp.float32
BF16 = jnp.bfloat16

GRID_W = 64
ROPE_THETA = 10000.0
NORM_EPS = 1e-6
GLA_HEADS = 4
GLA_GATE_RANK = 16
GLA_GATE_NORM = 16.0
SWA_HEADS = 8
SWA_KV_HEADS = 2
SWA_GROUP = SWA_HEADS // SWA_KV_HEADS
SWA_WINDOW = 128
SWA_BLOCK = 128
DIFF_HD = 64
N_EXPERTS = 16
EC_FACTOR = 2

LANES = 128
ROW_TILE = 256
COL_CHUNK = 256
GLA_CHUNK = 128
GLA_LEVELS = 7
GLA_SAFE_RANGE = 60.0
DIFF_QBLOCK = 256
DIFF_SAFE_BOUND = 60.0
DIFF_BOUND_SLACK = 1.0 + 2.0 ** -6
VMEM_LIMIT = 56 * 1024 * 1024
NEG_BIG = -1e30


def _cparams(*sem):
    return pltpu.CompilerParams(dimension_semantics=sem, vmem_limit_bytes=VMEM_LIMIT)


def _pick(n, cands):
    for c in cands:
        if n % c == 0:
            return c
    raise ValueError(f"no tile for {n} in {cands}")


def _nt(a, b):
    return lax.dot_general(a, b, (((1,), (1,)), ((), ())), preferred_element_type=F32)


def _tn(a, b):
    return lax.dot_general(a, b, (((0,), (0,)), ((), ())), preferred_element_type=F32)


def _dot(a, b):
    return jnp.dot(a, b, preferred_element_type=F32)


def _split3(a):
    hi = a.astype(BF16)
    r1 = a - hi.astype(F32)
    mid = r1.astype(BF16)
    lo = (r1 - mid.astype(F32)).astype(BF16)
    return hi, mid, lo


def _split2(a):
    hi = a.astype(BF16)
    lo = (a - hi.astype(F32)).astype(BF16)
    return hi, lo


def _silu(x):
    return x * (1.0 / (1.0 + jnp.exp(-x)))


def _rms(x, eps=NORM_EPS):
    return x * lax.rsqrt(jnp.mean(x * x, axis=-1, keepdims=True) + eps)


def _row_specs(stream, n_ctx_blocks, off=0):
    if isinstance(stream, tuple):
        ctx, x = stream
        d = x.shape[2]
        return ([pl.BlockSpec((1, ROW_TILE, d), lambda bi, i: (bi, jnp.minimum(i + off, n_ctx_blocks - 1), 0)),
                 pl.BlockSpec((1, ROW_TILE, d), lambda bi, i: (bi, jnp.maximum(i + off - n_ctx_blocks, 0), 0))],
                [ctx, x])
    d = stream.shape[2]
    return [pl.BlockSpec((1, ROW_TILE, d), lambda bi, i: (bi, i + off, 0))], [stream]


def _load_rows(row_refs, n_ctx_blocks, off=0):
    if len(row_refs) == 2:
        return jnp.where(pl.program_id(1) + off < n_ctx_blocks, row_refs[0][0], row_refs[1][0])
    return row_refs[0][0]


def _ada_kernel(c_ref, w_ref, b_ref, o_ref):
    s = _silu(c_ref[...])
    s_hi, s_lo = _split2(s)
    w_hi, w_lo = _split2(w_ref[0])
    acc = _dot(s_hi, w_hi) + (_dot(s_lo, w_hi) + _dot(s_hi, w_lo))
    o_ref[0] = acc + b_ref[0]


def _ada_call(cc, w_ada, b_ada):
    depth, d, n = w_ada.shape
    r = cc.shape[0]
    tn = _pick(n, (768, 512, 256, 128))
    return pl.pallas_call(
        _ada_kernel,
        grid=(depth, n // tn),
        in_specs=[pl.BlockSpec((r, d), lambda l, j: (0, 0)),
                  pl.BlockSpec((1, d, tn), lambda l, j: (l, 0, j)),
                  pl.BlockSpec((1, 1, tn), lambda l, j: (l, 0, j))],
        out_specs=pl.BlockSpec((1, r, tn), lambda l, j: (l, 0, j)),
        out_shape=jax.ShapeDtypeStruct((depth, r, n), F32),
        compiler_params=_cparams("parallel", "parallel"),
        name="adaln",
    )(cc, w_ada, b_ada.reshape(depth, 1, n))


def _proj_kernel(*refs, n_row_refs, n_ctx_blocks, chunk_ops, quarter, has_r):
    row_refs = refs[:n_row_refs]
    mod_ref, g_ref, w_ref, cos_ref, sin_ref = refs[n_row_refs:n_row_refs + 5]
    rest = refs[n_row_refs + 5:]
    if has_r:
        wr_ref, o_ref, r_ref = rest
    else:
        (o_ref,) = rest
    x = _load_rows(row_refs, n_ctx_blocks)
    y = _rms(x) * g_ref[...]
    h = (y * (1.0 + mod_ref[0, 0, 1:2, :]) + mod_ref[0, 0, 0:1, :]).astype(BF16)
    reps = COL_CHUNK // LANES
    cos_t = jnp.concatenate([cos_ref[...]] * reps, axis=1)
    sin_t = jnp.concatenate([sin_ref[...]] * reps, axis=1)
    lane = lax.broadcasted_iota(jnp.int32, (1, COL_CHUNK), 1)
    first = (lane % (2 * quarter)) < quarter
    for c, (rope, scale) in enumerate(chunk_ops):
        acc = _dot(h, w_ref[:, c * COL_CHUNK:(c + 1) * COL_CHUNK])
        if rope:
            partner = jnp.where(first, pltpu.roll(acc, COL_CHUNK - quarter, 1), pltpu.roll(acc, quarter, 1))
            acc = acc * cos_t + partner * sin_t
        if scale != 1.0:
            acc = acc * scale
        o_ref[0, :, c * COL_CHUNK:(c + 1) * COL_CHUNK] = acc.astype(BF16)
    if has_r:
        r_ref[0] = _dot(h, wr_ref[...]).astype(BF16)


def _proj_call(stream, mod, g, w, cos_t, sin_t, chunk_ops, quarter, n_ctx_blocks, wr=None):
    d, n = w.shape
    t = cos_t.shape[0]
    b = mod.shape[0]
    assert n == len(chunk_ops) * COL_CHUNK and t % ROW_TILE == 0
    has_r = wr is not None
    row_specs, row_args = _row_specs(stream, n_ctx_blocks)
    in_specs = row_specs + [
        pl.BlockSpec((1, 1, 6, d), lambda bi, i: (bi, jnp.where(i < n_ctx_blocks, 0, 1), 0, 0)),
        pl.BlockSpec((1, d), lambda bi, i: (0, 0)),
        pl.BlockSpec((d, n), lambda bi, i: (0, 0), pipeline_mode=pl.Buffered(1)),
        pl.BlockSpec((ROW_TILE, LANES), lambda bi, i: (i, 0)),
        pl.BlockSpec((ROW_TILE, LANES), lambda bi, i: (i, 0))]
    args = row_args + [mod, g.reshape(1, d), w, cos_t, sin_t]
    out_specs = [pl.BlockSpec((1, ROW_TILE, n), lambda bi, i: (bi, i, 0))]
    out_shape = [jax.ShapeDtypeStruct((b, t, n), BF16)]
    if has_r:
        in_specs.append(pl.BlockSpec((d, LANES), lambda bi, i: (0, 0)))
        args.append(wr)
        out_specs.append(pl.BlockSpec((1, ROW_TILE, LANES), lambda bi, i: (bi, i, 0)))
        out_shape.append(jax.ShapeDtypeStruct((b, t, LANES), BF16))
    return pl.pallas_call(
        functools.partial(_proj_kernel, n_row_refs=len(row_args), n_ctx_blocks=n_ctx_blocks,
                          chunk_ops=tuple(chunk_ops), quarter=quarter, has_r=has_r),
        grid=(b, t // ROW_TILE),
        in_specs=in_specs, out_specs=out_specs, out_shape=out_shape,
        compiler_params=_cparams("parallel", "parallel"),
        name="norm_proj",
    )(*args)


def _gla_constants():
    c = GLA_CHUNK
    idx = np.arange(c)
    cm = np.zeros((2, GLA_LEVELS + 2, c, c), np.float32)
    lmask = np.zeros((2, GLA_LEVELS + 2, c, c), np.float32)
    cm[0, 0] = (idx[None, :] <= idx[:, None])
    cm[1, 0] = (idx[None, :] >= idx[:, None])
    for l in range(GLA_LEVELS):
        s = c >> (l + 1)
        blk = idx // (2 * s)
        second = (idx % (2 * s)) >= s
        same = blk[:, None] == blk[None, :]
        sep_f = blk * 2 * s + s - 1
        cm[0, 1 + l] = (idx[None, :] <= sep_f[:, None])
        lmask[0, l] = same & second[:, None] & (~second)[None, :]
        sep_b = blk * 2 * s + s
        cm[1, 1 + l] = (idx[None, :] >= sep_b[:, None])
        lmask[1, l] = same & (~second)[:, None] & second[None, :]
    cm[:, GLA_LEVELS + 1] = 1.0
    lmask[:, GLA_LEVELS] = np.eye(c)
    lmask[:, GLA_LEVELS + 1] = cm[:, 0]
    return jnp.asarray(cm.reshape(2, (GLA_LEVELS + 2) * c, c), BF16), jnp.asarray(lmask, F32)


def _gla_kernel(q_ref, k_ref, v_ref, ag_ref, r_ref, gw_ref, gb_ref, ng_ref, cm_ref, lm_ref,
                o_ref, la_ref, of_ref, ob_ref, qe_ref, u_ref, ss_ref, dec_ref, st_ref,
                *, n_ctx_chunks, n_chunks, q_scale):
    c = GLA_CHUNK
    dv = st_ref.shape[1]
    r = r_ref[0]
    for d in range(2):
        z = _dot(r, gw_ref[d]) + gb_ref[d]
        la_ref[d] = (jnp.minimum(z, 0.0) - jnp.log(1.0 + jnp.exp(-jnp.abs(z)))) * (1.0 / GLA_GATE_NORM)
    st_ref[...] = jnp.zeros_like(st_ref)
    out_refs = (of_ref, ob_ref)

    def chunk_rows(ci):
        return pl.ds(pl.multiple_of(ci * c, c), c)

    def state_rows(ci):
        return pl.ds(pl.multiple_of(ci * dv, dv), dv)

    min_tot = None
    for d in range(2):
        m = jnp.min(jnp.sum(la_ref[d].reshape(n_chunks, c, la_ref.shape[2]), axis=1))
        min_tot = m if min_tot is None else jnp.minimum(min_tot, m)
    is_safe = min_tot >= -GLA_SAFE_RANGE

    def prep_fast(cp, carry):
        jobs = [(2 * cp + u, d) for u in range(2) for d in range(2)]
        rows = [chunk_rows(ci) for ci, _ in jobs]
        qs = {u: q_ref[0, chunk_rows(2 * cp + u), :].astype(F32) * q_scale for u in range(2)}
        ks = {u: k_ref[0, chunk_rows(2 * cp + u), :].astype(F32) for u in range(2)}
        bcs = {}
        for j, (ci, d) in enumerate(jobs):
            g_hi, g_lo = _split2(la_ref[d, rows[j], :])
            cm = cm_ref[d, 0:c, :]
            bcs[j] = _dot(cm, g_hi) + _dot(cm, g_lo)
        qes, kins, decs = {}, {}, {}
        for j, (ci, d) in enumerate(jobs):
            bc = bcs[j]
            qes[j] = (qs[j // 2] * jnp.exp(bc)).astype(BF16)
            kins[j] = ks[j // 2] * jnp.exp(-bc)
            decs[j] = jnp.exp(bc[c - 1:c] if d == 0 else bc[0:1])
        atts = {j: _nt(qes[j], kins[j].astype(BF16)) for j in range(len(jobs))}
        for j, (ci, d) in enumerate(jobs):
            att = (lm_ref[d, GLA_LEVELS + 1] * atts[j]).astype(BF16)
            out_refs[d][rows[j], :] = _dot(att, v_ref[0, rows[j], :])
            qe_ref[d, rows[j], :] = qes[j]
            u_ref[d, state_rows(ci), :] = _tn(v_ref[0, rows[j], :], (kins[j] * decs[j]).astype(BF16))
            dec_ref[d, pl.ds(pl.multiple_of(ci * 8, 8), 8), :] = jnp.broadcast_to(decs[j], (8, decs[j].shape[1]))
        return carry

    def prep(ci, carry):
        rows = chunk_rows(ci)
        q = q_ref[0, rows, :].astype(F32) * q_scale
        k = k_ref[0, rows, :].astype(F32)
        v = v_ref[0, rows, :]
        qk = _nt(q.astype(BF16), k.astype(BF16))
        for d in range(2):
            g_hi, g_lo = _split2(la_ref[d, rows, :])
            cm = cm_ref[d]
            cum = _dot(cm, g_hi) + _dot(cm, g_lo)
            bc = cum[0:c]
            tot = cum[(GLA_LEVELS + 1) * c:(GLA_LEVELS + 2) * c]
            att = lm_ref[d, GLA_LEVELS] * qk
            for l in range(GLA_LEVELS):
                ref_l = cum[(1 + l) * c:(2 + l) * c]
                eq = jnp.exp(jnp.minimum(bc - ref_l, 0.0))
                ek = jnp.exp(jnp.minimum(ref_l - bc, 0.0))
                att = att + lm_ref[d, l] * _nt((q * eq).astype(BF16), (k * ek).astype(BF16))
            out_refs[d][rows, :] = _dot(att.astype(BF16), v)
            qe_ref[d, rows, :] = (q * jnp.exp(bc)).astype(BF16)
            u_ref[d, state_rows(ci), :] = _tn(v, (k * jnp.exp(tot - bc)).astype(BF16))
            dec_ref[d, pl.ds(pl.multiple_of(ci * 8, 8), 8), :] = jnp.exp(tot[0:8])
        return carry

    @pl.when(is_safe)
    def _():
        lax.fori_loop(0, n_chunks // 2, prep_fast, 0)

    @pl.when(jnp.logical_not(is_safe))
    def _():
        lax.fori_loop(0, n_chunks, prep, 0)

    def scan(t, carry):
        cb = jnp.where(t < n_ctx_chunks, n_ctx_chunks - 1 - t, n_chunks - 1 - (t - n_ctx_chunks))
        for d, ci in ((0, t), (1, cb)):
            st = st_ref[d]
            ss_ref[d, state_rows(ci), :] = st.astype(BF16)
            dec = dec_ref[d, pl.ds(pl.multiple_of(ci * 8, 8), 1), :]
            st_ref[d] = st * dec + u_ref[d, state_rows(ci), :]
        return carry

    lax.fori_loop(0, n_chunks, scan, 0)

    def inter(ci, carry):
        rows = chunk_rows(ci)
        for d in range(2):
            out_refs[d][rows, :] += _nt(qe_ref[d, rows, :], ss_ref[d, state_rows(ci), :])
        return carry

    lax.fori_loop(0, n_chunks, inter, 0, unroll=2)
    o = of_ref[...] + ob_ref[...]
    ag = ag_ref[0].astype(F32)
    o_ref[0] = (_rms(o) * ng_ref[...] * _silu(ag)).astype(BF16)


def _gla_call(p0, r, gw_pad, gb, norm_g, consts, n_ctx_rows, col_q, col_k, col_v, col_g, dk, dv):
    b, t, _ = p0.shape
    n_chunks = t // GLA_CHUNK
    cm, lm = consts
    kern = functools.partial(_gla_kernel, n_ctx_chunks=n_ctx_rows // GLA_CHUNK, n_chunks=n_chunks,
                             q_scale=float(dk) ** -0.5)
    full = lambda shape: pl.BlockSpec(shape, lambda bi, h: (0,) * len(shape))
    return pl.pallas_call(
        kern,
        grid=(b, GLA_HEADS),
        in_specs=[pl.BlockSpec((1, t, dk), lambda bi, h: (bi, 0, col_q // dk + h)),
                  pl.BlockSpec((1, t, dk), lambda bi, h: (bi, 0, col_k // dk + h)),
                  pl.BlockSpec((1, t, dv), lambda bi, h: (bi, 0, col_v // dv + h)),
                  pl.BlockSpec((1, t, dv), lambda bi, h: (bi, 0, col_g // dv + h)),
                  pl.BlockSpec((1, t, LANES), lambda bi, h: (bi, 0, 0)),
                  pl.BlockSpec((2, LANES, dk), lambda bi, h: (0, 0, h)),
                  pl.BlockSpec((2, 1, dk), lambda bi, h: (0, 0, h)),
                  full((1, dv)), full(cm.shape), full(lm.shape)],
        out_specs=pl.BlockSpec((1, t, dv), lambda bi, h: (bi, 0, h)),
        out_shape=jax.ShapeDtypeStruct((b, t, GLA_HEADS * dv), BF16),
        scratch_shapes=[pltpu.VMEM((2, t, dk), F32), pltpu.VMEM((t, dv), F32), pltpu.VMEM((t, dv), F32),
                        pltpu.VMEM((2, t, dk), BF16), pltpu.VMEM((2, n_chunks * dv, dk), F32),
                        pltpu.VMEM((2, n_chunks * dv, dk), BF16),
                        pltpu.VMEM((2, n_chunks * 8, dk), F32), pltpu.VMEM((2, dv, dk), F32)],
        compiler_params=_cparams("parallel", "parallel"),
        name="gla",
    )(p0, p0, p0, p0, r, gw_pad, gb, norm_g.reshape(1, dv), cm, lm)


def _swa_kernel(q_ref, k_ref, v_ref, sink_ref, o_ref, s_ref, *, n_ctx_rows, n_blocks):
    blk = SWA_BLOCK
    win = 3 * blk
    ncb = n_ctx_rows // blk
    nlb = n_blocks - ncb
    sink = sink_ref[0, :, 0:1] * float(np.log2(np.e))
    rel0 = (lax.broadcasted_iota(jnp.int32, (blk, win), 1) - lax.broadcasted_iota(jnp.int32, (blk, win), 0))
    no_bias = jnp.zeros((blk, n_ctx_rows), F32)

    def key_rows(j):
        kb0 = jnp.clip(j - ncb - 1, 0, nlb - 3)
        return kb0, pl.ds(pl.multiple_of(n_ctx_rows + kb0 * blk, blk), win)

    def scores(j, slot):
        kb0, rows = key_rows(j)
        q = q_ref[0, pl.ds(pl.multiple_of(j * blk, blk), blk), :]
        q4 = jnp.concatenate([q[:, g * blk:(g + 1) * blk] for g in range(SWA_GROUP)], axis=0)
        keys = jnp.concatenate([k_ref[0, rows, :], k_ref[0, 0:n_ctx_rows, :]], axis=0)
        rel = rel0 + jnp.where(j < ncb, 4 * win, (kb0 - (j - ncb)) * blk)
        bias = jnp.concatenate([jnp.where(jnp.abs(rel) <= SWA_WINDOW, 0.0, NEG_BIG), no_bias], axis=1)
        s_ref[slot] = _nt(q4, keys) + jnp.concatenate([bias] * SWA_GROUP, axis=0)

    def finish(j, slot):
        _, rows = key_rows(j)
        vals = jnp.concatenate([v_ref[0, rows, :], v_ref[0, 0:n_ctx_rows, :]], axis=0)
        s = s_ref[slot]
        m = jnp.maximum(jnp.max(s, axis=1, keepdims=True), sink)
        e = jnp.exp2(s - m)
        den = jnp.sum(e, axis=1, keepdims=True) + jnp.exp2(sink - m)
        o = _dot(e.astype(BF16), vals) / den
        o_ref[0, pl.ds(pl.multiple_of(j * blk, blk), blk), :] = jnp.concatenate(
            [o[g * blk:(g + 1) * blk] for g in range(SWA_GROUP)], axis=1).astype(BF16)

    scores(0, 0)

    def body(jj, carry):
        j0 = 2 * jj
        scores(j0 + 1, 1)
        finish(j0, 0)
        scores(j0 + 2, 0)
        finish(j0 + 1, 1)
        return carry

    lax.fori_loop(0, n_blocks // 2 - 1, body, 0)
    scores(n_blocks - 1, 1)
    finish(n_blocks - 2, 0)
    finish(n_blocks - 1, 1)


def _swa_call(p0, sink_col, n_ctx_rows, col_q, col_k, col_v, hd):
    b, t, _ = p0.shape
    blk = SWA_BLOCK
    nb = t // blk
    gq = SWA_GROUP * hd
    assert nb % 2 == 0 and nb - n_ctx_rows // blk >= 3 and SWA_WINDOW <= blk
    return pl.pallas_call(
        functools.partial(_swa_kernel, n_ctx_rows=n_ctx_rows, n_blocks=nb),
        grid=(b, SWA_KV_HEADS),
        in_specs=[pl.BlockSpec((1, t, gq), lambda bi, h: (bi, 0, col_q // gq + h)),
                  pl.BlockSpec((1, t, hd), lambda bi, h: (bi, 0, col_k // hd + h)),
                  pl.BlockSpec((1, t, hd), lambda bi, h: (bi, 0, col_v // hd + h)),
                  pl.BlockSpec((1, SWA_GROUP * blk, LANES), lambda bi, h: (h, 0, 0))],
        out_specs=pl.BlockSpec((1, t, gq), lambda bi, h: (bi, 0, h)),
        out_shape=jax.ShapeDtypeStruct((b, t, SWA_HEADS * hd), BF16),
        scratch_shapes=[pltpu.VMEM((2, SWA_GROUP * blk, 3 * blk + n_ctx_rows), F32)],
        compiler_params=_cparams("parallel", "parallel"),
        name="swa",
    )(p0, p0, p0, sink_col)


def _diff_kernel(lam_ref, q_ref, k_ref, v_ref, g_ref, o_ref, s_ref, kx_ref, vx_ref, qn_ref,
                 *, row_off, n_qblocks, lambda_init):
    tq = DIFF_QBLOCK
    hd2 = 2 * DIFF_HD
    n_keys = k_ref.shape[1]
    lam = lam_ref[...]
    s01 = jnp.sum(lam[0:1] * lam[1:2], axis=1, keepdims=True)
    s23 = jnp.sum(lam[2:3] * lam[3:4], axis=1, keepdims=True)
    lam_full = jnp.exp(s01) - jnp.exp(s23) + lambda_init
    lane = lax.broadcasted_iota(jnp.int32, (1, hd2), 1)
    first = lane < DIFF_HD
    col0 = lane == 0
    zero = jnp.zeros((), BF16)

    d_idx = lax.broadcasted_iota(jnp.int32, (hd2, 2 * hd2), 0)
    j_idx = lax.broadcasted_iota(jnp.int32, (hd2, 2 * hd2), 1)
    map_sum = jnp.where((d_idx < DIFF_HD) == (j_idx < hd2), 1.0, 0.0).astype(BF16)

    def sq_norms(x):
        xf = x.astype(F32)
        return _dot((xf * xf).astype(BF16), map_sum)

    kn = sq_norms(k_ref[0])
    kmax0 = jnp.sqrt(jnp.max(kn[:, 0:hd2], axis=0, keepdims=True)) * DIFF_BOUND_SLACK
    kmax1 = jnp.sqrt(jnp.max(kn[:, hd2:2 * hd2], axis=0, keepdims=True)) * DIFF_BOUND_SLACK
    qn_ref[...] = sq_norms(q_ref[0, row_off:, :])
    qmax0 = jnp.sqrt(jnp.max(qn_ref[:, 0:hd2], axis=0, keepdims=True))
    qmax1 = jnp.sqrt(jnp.max(qn_ref[:, hd2:2 * hd2], axis=0, keepdims=True))
    is_safe = jnp.max(jnp.maximum(qmax0 * kmax0, qmax1 * kmax1)) <= DIFF_SAFE_BOUND

    def q_block(j):
        q = q_ref[0, pl.ds(pl.multiple_of(row_off + j * tq, tq), tq), :]
        return q, jnp.where(first, q, zero), jnp.where(first, zero, q)

    def write_out(j, o):
        y = _rms(o) * g_ref[...] * (1.0 - lambda_init)
        o_ref[0, pl.ds(pl.multiple_of(j * tq, tq), tq), :] = y.astype(BF16)

    def scores_fast(j, slot):
        _, qa, qb = q_block(j)
        qn = jnp.sqrt(qn_ref[pl.ds(pl.multiple_of(j * tq, tq), tq), :])
        xa = jnp.where(col0, -(qn[:, 0:hd2] * kmax0), 0.0).astype(BF16)
        xb = jnp.where(col0, -(qn[:, hd2:2 * hd2] * kmax1), 0.0).astype(BF16)
        qq = jnp.concatenate([jnp.concatenate([qa, xa], axis=1), jnp.concatenate([qb, xb], axis=1)], axis=0)
        s_ref[slot] = _nt(qq, kx_ref[...])

    def finish_fast(j, slot):
        e = jnp.exp2(s_ref[slot]).astype(BF16)
        ov = _dot(e, vx_ref[...])
        p0 = ov[0:tq, 0:hd2] / ov[0:tq, hd2:2 * hd2]
        p1 = ov[tq:2 * tq, 0:hd2] / ov[tq:2 * tq, hd2:2 * hd2]
        write_out(j, p0 - lam_full * p1)

    def scores_max(j, slot):
        _, qa, qb = q_block(j)
        s_ref[slot] = _nt(jnp.concatenate([qa, qb], axis=0), k_ref[0])

    def finish_max(j, slot):
        s = s_ref[slot]
        e = jnp.exp2(s - jnp.max(s, axis=1, keepdims=True))
        z = jnp.sum(e, axis=1, keepdims=True)
        coef = lam_full * z[0:tq] / z[tq:2 * tq]
        a = e[0:tq] - coef * e[tq:2 * tq]
        write_out(j, _dot(a.astype(BF16), v_ref[0]) * (1.0 / z[0:tq]))

    def run(scores, finish):
        scores(0, 0)

        def body(jj, carry):
            j0 = 2 * jj
            scores(j0 + 1, 1)
            finish(j0, 0)
            scores(j0 + 2, 0)
            finish(j0 + 1, 1)
            return carry

        lax.fori_loop(0, n_qblocks // 2 - 1, body, 0)
        scores(n_qblocks - 1, 1)
        finish(n_qblocks - 2, 0)
        finish(n_qblocks - 1, 1)

    @pl.when(is_safe)
    def _():
        kx_ref[:, 0:hd2] = k_ref[0]
        kx_ref[:, hd2:2 * hd2] = jnp.broadcast_to(jnp.where(col0, 1.0, 0.0).astype(BF16), (n_keys, hd2))
        vx_ref[:, 0:hd2] = v_ref[0]
        vx_ref[:, hd2:2 * hd2] = jnp.ones((n_keys, hd2), BF16)
        run(scores_fast, finish_fast)

    @pl.when(jnp.logical_not(is_safe))
    def _():
        run(scores_max, finish_max)


def _diff_call(p1, lam, subln_g, n_ctx_rows, n_heads, lambda_init):
    b, t, _ = p1.shape
    s_len = t - n_ctx_rows
    hd2 = 2 * DIFF_HD
    n_qblocks = s_len // DIFF_QBLOCK
    assert n_qblocks >= 2 and n_qblocks % 2 == 0
    return pl.pallas_call(
        functools.partial(_diff_kernel, row_off=n_ctx_rows, n_qblocks=n_qblocks, lambda_init=lambda_init),
        grid=(b, n_heads),
        in_specs=[pl.BlockSpec((4, DIFF_HD), lambda bi, h: (0, 0)),
                  pl.BlockSpec((1, t, hd2), lambda bi, h: (bi, 0, h)),
                  pl.BlockSpec((1, t, hd2), lambda bi, h: (bi, 0, n_heads + h)),
                  pl.BlockSpec((1, t, hd2), lambda bi, h: (bi, 0, 2 * n_heads + h)),
                  pl.BlockSpec((1, hd2), lambda bi, h: (0, 0))],
        out_specs=pl.BlockSpec((1, s_len, hd2), lambda bi, h: (bi, 0, h)),
        out_shape=jax.ShapeDtypeStruct((b, s_len, n_heads * hd2), BF16),
        scratch_shapes=[pltpu.VMEM((2, 2 * DIFF_QBLOCK, t), F32), pltpu.VMEM((t, 2 * hd2), BF16),
                        pltpu.VMEM((t, 2 * hd2), BF16), pltpu.VMEM((s_len, 2 * hd2), F32)],
        compiler_params=_cparams("parallel", "parallel"),
        name="diff_attn",
    )(lam, p1, p1, p1, subln_g.reshape(1, hd2))


def _outproj_kernel(*refs, n_y, n_row_refs, n_ctx_blocks, row_block_off):
    y_refs = refs[:n_y]
    w_refs = refs[n_y:2 * n_y]
    row_refs = refs[2 * n_y:2 * n_y + n_row_refs]
    mod_ref, g_ref, wr_ref, xn_ref, h_ref, aff_ref = refs[2 * n_y + n_row_refs:]
    n_exp = aff_ref.shape[1]
    acc = _dot(y_refs[0][0], w_refs[0][...])
    for yr, wr in zip(y_refs[1:], w_refs[1:]):
        acc = acc + _dot(yr[0], wr[...])
    xn = _load_rows(row_refs, n_ctx_blocks, row_block_off) + mod_ref[0, 0, 2:3, :] * acc
    xn_ref[0] = xn
    h = _rms(xn) * g_ref[...] * (1.0 + mod_ref[0, 0, 4:5, :]) + mod_ref[0, 0, 3:4, :]
    h_bf = h.astype(BF16)
    h_ref[0] = h_bf
    logits = _dot(h_bf, wr_ref[...])
    logits = jnp.where(lax.broadcasted_iota(jnp.int32, (1, LANES), 1) < n_exp, logits, NEG_BIG)
    ex = jnp.exp(logits - jnp.max(logits, axis=1, keepdims=True))
    aff = ex / jnp.sum(ex, axis=1, keepdims=True)
    aff_ref[0] = jnp.transpose(aff)[0:n_exp]


def _outproj_call(ys, ws, stream, mod, g, w_router, row_block_off, n_rows, n_ctx_blocks):
    b, d = mod.shape[0], mod.shape[3]
    n_y = len(ys)
    e = w_router.shape[1]
    w_router_pad = jnp.concatenate([w_router, jnp.zeros((d, LANES - e), F32)], axis=1).astype(BF16)
    row_specs, row_args = _row_specs(stream, n_ctx_blocks, row_block_off)
    in_specs = ([pl.BlockSpec((1, ROW_TILE, y.shape[2]), lambda bi, i: (bi, i, 0)) for y in ys]
                + [pl.BlockSpec(w.shape, lambda bi, i: (0, 0), pipeline_mode=pl.Buffered(1)) for w in ws]
                + row_specs
                + [pl.BlockSpec((1, 1, 6, d),
                                lambda bi, i: (bi, jnp.where(i + row_block_off < n_ctx_blocks, 0, 1), 0, 0)),
                   pl.BlockSpec((1, d), lambda bi, i: (0, 0)),
                   pl.BlockSpec((d, LANES), lambda bi, i: (0, 0))])
    return pl.pallas_call(
        functools.partial(_outproj_kernel, n_y=n_y, n_row_refs=len(row_args), n_ctx_blocks=n_ctx_blocks,
                          row_block_off=row_block_off),
        grid=(b, n_rows // ROW_TILE),
        in_specs=in_specs,
        out_specs=[pl.BlockSpec((1, ROW_TILE, d), lambda bi, i: (bi, i, 0)),
                   pl.BlockSpec((1, ROW_TILE, d), lambda bi, i: (bi, i, 0)),
                   pl.BlockSpec((1, e, ROW_TILE), lambda bi, i: (bi, 0, i))],
        out_shape=[jax.ShapeDtypeStruct((b, n_rows, d), F32),
                   jax.ShapeDtypeStruct((b, n_rows, d), BF16),
                   jax.ShapeDtypeStruct((b, e, n_rows), F32)],
        compiler_params=_cparams("parallel", "parallel"),
        name="out_proj",
    )(*ys, *ws, *row_args, mod, g.reshape(1, d), w_router_pad)


def _prefix_lanes(m, upper):
    e, n = m.shape
    carry = jnp.zeros((e, 1), F32)
    outs = []
    for blk in range(n // LANES):
        mb = m[:, blk * LANES:(blk + 1) * LANES]
        inc = _dot(mb.astype(BF16), upper)
        outs.append(inc - mb + carry)
        carry = carry + jnp.sum(mb, axis=1, keepdims=True)
    return jnp.concatenate(outs, axis=1)


def _transpose_exact(x, eye):
    hi, mid, lo = _split3(x)
    return _nt(eye, hi) + (_nt(eye, mid) + _nt(eye, lo))


def _route_kernel(aff_ref, pos_ref, gate_ref, posn_ref, lo_ref, hi_ref, *, segs):
    n_exp = aff_ref.shape[1]
    lo_acc = jnp.zeros((n_exp, LANES), F32)
    hi_acc = jnp.zeros((n_exp, LANES), F32)
    li = lax.broadcasted_iota(jnp.int32, (LANES, LANES), 0)
    lj = lax.broadcasted_iota(jnp.int32, (LANES, LANES), 1)
    upper = jnp.where(li <= lj, 1.0, 0.0).astype(BF16)
    eye = jnp.where(li == lj, 1.0, 0.0).astype(BF16)
    all_bits = [pltpu.bitcast(aff_ref[0, :, off:off + n], jnp.int32) for (off, n, _, _) in segs]

    def search(it, thrs):
        out = []
        for bits, (_, _, cap, _), thr in zip(all_bits, segs, thrs):
            cand = thr | jnp.left_shift(jnp.int32(1), 30 - it)
            cnt = jnp.sum(jnp.where(bits >= cand, 1.0, 0.0), axis=1, keepdims=True)
            out.append(jnp.where(cnt >= cap, cand, thr))
        return tuple(out)

    all_thr = lax.fori_loop(0, 31, search, tuple(jnp.zeros((n_exp, 1), jnp.int32) for _ in segs))
    for (off, n, cap, slot_off), bits, thr in zip(segs, all_bits, all_thr):
        a = aff_ref[0, :, off:off + n]
        gt = jnp.where(bits > thr, 1.0, 0.0)
        eq = jnp.where(bits == thr, 1.0, 0.0)
        need = cap - jnp.sum(gt, axis=1, keepdims=True)
        sel = gt + eq * jnp.where(_prefix_lanes(eq, upper) < need, 1.0, 0.0)
        slot = _prefix_lanes(sel, upper)
        pos = jnp.where(sel > 0.0, slot + slot_off, -1.0)
        pos_ref[0, :, off:off + n] = pos.astype(jnp.int32)
        gate_ref[0, :, off:off + n] = sel * a
        for blk in range(n // LANES):
            cols = slice(blk * LANES, (blk + 1) * LANES)
            rows = slice(off + blk * LANES, off + (blk + 1) * LANES)
            posn_ref[0, rows, :] = _transpose_exact(pos[:, cols], eye).astype(jnp.int32)
        assert cap <= 256 and (off + n) // LANES <= LANES
        tok = lax.broadcasted_iota(jnp.int32, (n, LANES), 0)
        blk_start = (lax.broadcasted_iota(jnp.int32, (n, LANES), 1) - off // LANES) * LANES
        lo_acc = lo_acc + _dot(slot.astype(BF16), jnp.where(tok == blk_start, 1.0, 0.0).astype(BF16))
        hi_acc = hi_acc + _dot((slot + sel).astype(BF16),
                               jnp.where(tok == blk_start + (LANES - 1), 1.0, 0.0).astype(BF16))
    lo_ref[0] = lo_acc.astype(jnp.int32)
    hi_ref[0] = hi_acc.astype(jnp.int32)


def _route_call(aff_t, segs):
    b, e, t = aff_t.shape
    return pl.pallas_call(
        functools.partial(_route_kernel, segs=tuple(segs)),
        grid=(b,),
        in_specs=[pl.BlockSpec((1, e, t), lambda bi: (bi, 0, 0))],
        out_specs=[pl.BlockSpec((1, e, t), lambda bi: (bi, 0, 0)),
                   pl.BlockSpec((1, e, t), lambda bi: (bi, 0, 0)),
                   pl.BlockSpec((1, t, e), lambda bi: (bi, 0, 0)),
                   pl.BlockSpec((1, e, LANES), lambda bi: (bi, 0, 0)),
                   pl.BlockSpec((1, e, LANES), lambda bi: (bi, 0, 0))],
        out_shape=[jax.ShapeDtypeStruct((b, e, t), jnp.int32),
                   jax.ShapeDtypeStruct((b, e, t), F32),
                   jax.ShapeDtypeStruct((b, t, e), jnp.int32),
                   jax.ShapeDtypeStruct((b, e, LANES), jnp.int32),
                   jax.ShapeDtypeStruct((b, e, LANES), jnp.int32)],
        compiler_params=_cparams("parallel"),
        name="route",
    )(aff_t)


def _gather_kernel(lo_ref, pos_ref, gate_ref, h_ref, o_ref, g_ref, *, segs):
    e = pl.program_id(1)
    prow = pos_ref[0, pl.ds(e, 1), :]
    grow = gate_ref[0, pl.ds(e, 1), :]

    def emit(slot0, n_slot, tok0, n_tok):
        slots = lax.broadcasted_iota(jnp.int32, (n_slot, 1), 0) + slot0
        hit = prow[:, tok0:tok0 + n_tok] == slots
        o_ref[0, 0, slot0:slot0 + n_slot, :] = _dot(jnp.where(hit, 1.0, 0.0).astype(BF16),
                                                    h_ref[0, tok0:tok0 + n_tok, :]).astype(BF16)
        gsel = jnp.sum(jnp.where(hit, grow[:, tok0:tok0 + n_tok], 0.0), axis=1, keepdims=True)
        g_ref[0, 0, slot0:slot0 + n_slot, :] = jnp.broadcast_to(gsel, (n_slot, LANES))

    for (off, n, cap, slot_off) in segs:
        half = cap // 2
        k0 = -(-(n * 5 // 8) // LANES) * LANES
        k1 = (n * 3 // 8) // LANES * LANES
        if cap % 32 or k0 >= n or k1 <= 0:
            emit(slot_off, cap, off, n)
            continue
        fits = jnp.logical_and(lo_ref[0, e, (off + k0) // LANES] >= half, lo_ref[0, e, (off + k1) // LANES] <= half)

        @pl.when(fits)
        def _():
            emit(slot_off, half, off, k0)
            emit(slot_off + half, half, off + k1, n - k1)

        @pl.when(jnp.logical_not(fits))
        def _():
            emit(slot_off, cap, off, n)


def _slot_window(tt, cap_lat, n_exp):
    return min(cap_lat, max(16, 2 * tt * EC_FACTOR // n_exp))


def _gather_call(lo, pos, gate, h, segs, n_slots):
    b, e, t = pos.shape
    d = h.shape[2]
    return pl.pallas_call(
        functools.partial(_gather_kernel, segs=tuple(segs)),
        grid=(b, e),
        in_specs=[pl.BlockSpec((1, e, LANES), lambda bi, ei: (bi, 0, 0), memory_space=pltpu.SMEM),
                  pl.BlockSpec((1, e, t), lambda bi, ei: (bi, 0, 0)),
                  pl.BlockSpec((1, e, t), lambda bi, ei: (bi, 0, 0)),
                  pl.BlockSpec((1, t, d), lambda bi, ei: (bi, 0, 0))],
        out_specs=[pl.BlockSpec((1, 1, n_slots, d), lambda bi, ei: (ei, bi, 0, 0)),
                   pl.BlockSpec((1, 1, n_slots, LANES), lambda bi, ei: (ei, bi, 0, 0))],
        out_shape=[jax.ShapeDtypeStruct((e, b, n_slots, d), BF16),
                   jax.ShapeDtypeStruct((e, b, n_slots, LANES), F32)],
        compiler_params=_cparams("parallel", "arbitrary"),
        name="gather",
    )(lo, pos, gate, h)


def _ffn_up_kernel(x_ref, wg_ref, wu_ref, o_ref, wgb_ref, wub_ref):
    @pl.when(pl.program_id(2) == 0)
    def _():
        wgb_ref[...] = wg_ref[0, 0].astype(BF16)
        wub_ref[...] = wu_ref[0, 0].astype(BF16)

    x = x_ref[0]
    g = _dot(x, wgb_ref[...])
    u = _dot(x, wub_ref[...])
    o_ref[0] = (_silu(g) * u).astype(BF16)


def _ffn_down_kernel(a_ref, w_ref, gate_ref, o_ref, wb_ref):
    @pl.when(pl.program_id(2) == 0)
    def _():
        wb_ref[...] = w_ref[0, 0].astype(BF16)

    gate = jnp.concatenate([gate_ref[0]] * (o_ref.shape[2] // LANES), axis=1)
    o_ref[0] = (_dot(a_ref[0], wb_ref[...]) * gate).astype(BF16)


def _ffn_call(xs, gates, w_gu, w_dn, layer):
    e, m, d = xs.shape
    f = w_dn.shape[2]
    tm = _pick(m, (1152, 1024, 768, 512, 256, 128, 64, 32, 16, 8))
    tf = _pick(f, (512, 256, 128))
    nf = f // tf
    act = pl.pallas_call(
        _ffn_up_kernel,
        grid=(e, nf, m // tm),
        in_specs=[pl.BlockSpec((1, tm, d), lambda ei, j, i: (ei, i, 0)),
                  pl.BlockSpec((1, 1, d, tf), lambda ei, j, i: (layer, ei, 0, j)),
                  pl.BlockSpec((1, 1, d, tf), lambda ei, j, i: (layer, ei, 0, j + nf))],
        out_specs=pl.BlockSpec((1, tm, tf), lambda ei, j, i: (ei, i, j)),
        out_shape=jax.ShapeDtypeStruct((e, m, f), BF16),
        scratch_shapes=[pltpu.VMEM((d, tf), BF16), pltpu.VMEM((d, tf), BF16)],
        compiler_params=_cparams("parallel", "arbitrary", "arbitrary"),
        name="ffn_up",
    )(xs, w_gu, w_gu)
    tn = _pick(d, (1024, 512, 256, 128))
    return pl.pallas_call(
        _ffn_down_kernel,
        grid=(e, d // tn, m // tm),
        in_specs=[pl.BlockSpec((1, tm, f), lambda ei, j, i: (ei, i, 0)),
                  pl.BlockSpec((1, 1, f, tn), lambda ei, j, i: (layer, ei, 0, j)),
                  pl.BlockSpec((1, tm, LANES), lambda ei, j, i: (ei, i, 0))],
        out_specs=pl.BlockSpec((1, tm, tn), lambda ei, j, i: (ei, i, j)),
        out_shape=jax.ShapeDtypeStruct((e, m, d), BF16),
        scratch_shapes=[pltpu.VMEM((f, tn), BF16)],
        compiler_params=_cparams("parallel", "arbitrary", "arbitrary"),
        name="ffn_down",
    )(act, w_dn, gates)


def _combine_kernel(lo_ref, hi_ref, posn_ref, y_ref, x_ref, mod_ref, gf_ref, o_ref,
                    *, cap_lat, cap_ctx, n_ctx_tiles, blocks_per_tile, window, final_norm):
    n_exp = posn_ref.shape[2]
    i = pl.program_id(1)
    lane = lax.broadcasted_iota(jnp.int32, (1, n_exp), 1)
    posn = posn_ref[0].astype(F32)

    def pcol(e):
        return jnp.sum(jnp.where(lane == e, posn, 0.0), axis=1, keepdims=True)

    def scatter(slot0, cap):
        slots = lax.broadcasted_iota(jnp.int32, (1, cap), 1).astype(F32) + slot0
        hots = [jnp.where(pcol(e) == slots, 1.0, 0.0).astype(BF16) for e in range(n_exp)]
        rows = [y_ref[e, 0, slot0:slot0 + cap, :] for e in range(n_exp)]
        return _dot(jnp.concatenate(hots, axis=1), jnp.concatenate(rows, axis=0))

    def scatter_window(starts):
        w_tot = n_exp * window
        grp = lax.broadcasted_iota(jnp.int32, (n_exp, w_tot), 1) // window
        spread = jnp.where(grp == lax.broadcasted_iota(jnp.int32, (n_exp, w_tot), 0), 1.0, 0.0).astype(BF16)
        bpos = _dot(posn.astype(BF16), spread)
        wlane = lax.broadcasted_iota(jnp.int32, (1, w_tot), 1)
        target = (wlane % window).astype(F32)
        for e, st in enumerate(starts):
            target = jnp.where(wlane // window == e, target + st.astype(F32), target)
        onehot = jnp.where(bpos == target, 1.0, 0.0).astype(BF16)
        rows = [y_ref[e, 0, pl.ds(pl.multiple_of(st, 16), window), :] for e, st in enumerate(starts)]
        return _dot(onehot, jnp.concatenate(rows, axis=0))

    def finish(total, g2):
        out = x_ref[0] + g2 * total
        if final_norm:
            out = _rms(out) * gf_ref[...]
        o_ref[0] = out

    blk0 = i * blocks_per_tile
    starts = []
    fits = None
    for e in range(n_exp):
        lo = lo_ref[0, e, blk0]
        hi = hi_ref[0, e, blk0 + blocks_per_tile - 1]
        st = jnp.minimum(lax.shift_left(lax.shift_right_logical(lo, 4), 4), cap_lat - window)
        ok = hi - st <= window
        fits = ok if fits is None else jnp.logical_and(fits, ok)
        starts.append(st)
    is_lat = i >= n_ctx_tiles

    @pl.when(jnp.logical_and(is_lat, fits))
    def _():
        finish(scatter_window(starts), mod_ref[0, 1, 5:6, :])

    @pl.when(jnp.logical_and(is_lat, jnp.logical_not(fits)))
    def _():
        finish(scatter(0, cap_lat), mod_ref[0, 1, 5:6, :])

    if n_ctx_tiles:
        @pl.when(jnp.logical_not(is_lat))
        def _():
            finish(scatter(cap_lat, cap_ctx), mod_ref[0, 0, 5:6, :])


def _combine_call(lo, hi, posn, y, xa, mod, g_final, cap_lat, cap_ctx, n_ctx_rows, final_norm):
    b, t, d = xa.shape
    e = posn.shape[2]
    n_slots = y.shape[2]
    tt = ROW_TILE
    assert n_ctx_rows % tt == 0 and t % tt == 0
    window = _slot_window(tt, cap_lat, e)
    smem = lambda: pl.BlockSpec((1, e, LANES), lambda bi, i: (bi, 0, 0), memory_space=pltpu.SMEM)
    return pl.pallas_call(
        functools.partial(_combine_kernel, cap_lat=cap_lat, cap_ctx=cap_ctx, n_ctx_tiles=n_ctx_rows // tt,
                          blocks_per_tile=tt // LANES, window=window, final_norm=final_norm),
        grid=(b, t // tt),
        in_specs=[smem(), smem(),
                  pl.BlockSpec((1, tt, e), lambda bi, i: (bi, i, 0)),
                  pl.BlockSpec((e, 1, n_slots, d), lambda bi, i: (0, bi, 0, 0), pipeline_mode=pl.Buffered(1)),
                  pl.BlockSpec((1, tt, d), lambda bi, i: (bi, i, 0)),
                  pl.BlockSpec((1, 2, 6, d), lambda bi, i: (bi, 0, 0, 0)),
                  pl.BlockSpec((1, d), lambda bi, i: (0, 0))],
        out_specs=pl.BlockSpec((1, tt, d), lambda bi, i: (bi, i, 0)),
        out_shape=jax.ShapeDtypeStruct((b, t, d), F32),
        compiler_params=_cparams("parallel", "arbitrary"),
        name="combine",
    )(lo, hi, posn, y, xa, mod, g_final.reshape(1, d))


def _rope_tables(n_tokens, n_ctx_rows, head_dim):
    rows = n_tokens // GRID_W
    r = jnp.repeat(jnp.arange(rows, dtype=F32), GRID_W)
    col = jnp.tile(jnp.arange(GRID_W, dtype=F32), rows)
    axis_dim = head_dim // 2
    inv = ROPE_THETA ** (-jnp.arange(0, axis_dim, 2, dtype=F32) / axis_dim)
    ar, ac = r[:, None] * inv, col[:, None] * inv
    cos_p = jnp.concatenate([jnp.cos(ar), jnp.cos(ar), jnp.cos(ac), jnp.cos(ac)], axis=-1)
    sin_p = jnp.concatenate([-jnp.sin(ar), jnp.sin(ar), -jnp.sin(ac), jnp.sin(ac)], axis=-1)
    reps = LANES // head_dim
    cos_p, sin_p = jnp.tile(cos_p, (1, reps)), jnp.tile(sin_p, (1, reps))
    cos_t = jnp.concatenate([jnp.ones((n_ctx_rows, LANES), F32), cos_p], axis=0)
    sin_t = jnp.concatenate([jnp.zeros((n_ctx_rows, LANES), F32), sin_p], axis=0)
    return cos_t, sin_t


def _moe(h2, aff_t, x_res, mod, g_final, w_gu, w_dn, layer, segs, cap_lat, cap_ctx, n_ctx_rows, final_norm):
    b, t, d = h2.shape
    e = aff_t.shape[1]
    n_slots = cap_lat + cap_ctx
    pos, gate, posn, lo, hi = _route_call(aff_t, segs)
    xs, gs = _gather_call(lo, pos, gate, h2, segs, n_slots)
    y = _ffn_call(xs.reshape(e, b * n_slots, d), gs.reshape(e, b * n_slots, LANES), w_gu, w_dn, layer)
    return _combine_call(lo, hi, posn, y.reshape(e, b, n_slots, d), x_res, mod, g_final, cap_lat, cap_ctx,
                         n_ctx_rows, final_norm)


def kernel(x, c, ctx, c_ctx, w_ada, b_ada, g_norm_mix, g_norm_ffn, w_in_even, gla_gate_w, gla_gate_b, gla_norm_g, swa_sink, w_out_even, w_qkv_odd, diff_lambda, diff_subln_g, w_out_odd, w_router, w_gate_up, w_down, g_final):
    b, s_len, d = x.shape
    lc = ctx.shape[1]
    depth = w_ada.shape[0]
    assert depth == 2 and lc % ROW_TILE == 0 and s_len % ROW_TILE == 0 and s_len % GRID_W == 0
    t = lc + s_len
    ncb = lc // ROW_TILE
    gla_dk, gla_dv = d // 4 // GLA_HEADS, d // 2 // GLA_HEADS
    swa_hd = d // 2 // SWA_HEADS
    diff_heads = d // (2 * DIFF_HD)
    n_exp = w_router.shape[2]
    assert gla_dk == LANES and swa_hd == LANES and 2 * DIFF_HD == LANES

    n_rows = -(-(b + 1) // 8) * 8
    cc = jnp.concatenate([c, c_ctx[None, :], jnp.zeros((n_rows - b - 1, d), F32)], axis=0)
    m_all = _ada_call(cc, w_ada, b_ada).reshape(depth, n_rows, 6, d)

    def mod_for(layer):
        lat = m_all[layer, :b]
        ctxm = jnp.broadcast_to(m_all[layer, b][None], (b, 6, d))
        return jnp.stack([ctxm, lat], axis=1)

    xa = (ctx, x)

    mod0 = mod_for(0)
    w_in = w_in_even[0]
    sizes = (GLA_HEADS * gla_dk, GLA_HEADS * gla_dk, GLA_HEADS * gla_dv, GLA_HEADS * gla_dv,
             GLA_GATE_RANK, GLA_GATE_RANK, SWA_HEADS * swa_hd, SWA_KV_HEADS * swa_hd, SWA_KV_HEADS * swa_hd)
    offs = np.concatenate([[0], np.cumsum(sizes)])
    seg = lambda i: w_in[:, offs[i]:offs[i + 1]]
    w_main = jnp.concatenate([seg(0), seg(1), seg(2), seg(3), seg(6), seg(7), seg(8)], axis=1).astype(BF16)
    w_rank = jnp.concatenate([seg(4), seg(5), jnp.zeros((d, LANES - 2 * GLA_GATE_RANK), F32)],
                             axis=1).astype(BF16)
    col_aq, col_ak = 0, sizes[0]
    col_av = col_ak + sizes[1]
    col_ag = col_av + sizes[2]
    col_bq = col_ag + sizes[3]
    col_bk = col_bq + sizes[6]
    col_bv = col_bk + sizes[7]
    n_main = col_bv + sizes[8]
    swa_q_scale = float(swa_hd) ** -0.5 * float(np.log2(np.e))
    rope0 = [(col_bq <= ch * COL_CHUNK < col_bv, swa_q_scale if col_bq <= ch * COL_CHUNK < col_bk else 1.0)
             for ch in range(n_main // COL_CHUNK)]
    cos_b, sin_b = _rope_tables(s_len, lc, swa_hd)
    p0, r0 = _proj_call(xa, mod0, g_norm_mix[0], w_main, cos_b, sin_b, rope0, swa_hd // 4, ncb, wr=w_rank)

    gw = gla_gate_w[0]
    gw_pad = jnp.zeros((2, LANES, GLA_HEADS * gla_dk), F32)
    gw_pad = gw_pad.at[0, 0:GLA_GATE_RANK].set(gw[0]).at[1, GLA_GATE_RANK:2 * GLA_GATE_RANK].set(gw[1])
    a_out = _gla_call(p0, r0, gw_pad.astype(BF16), gla_gate_b[0].reshape(2, 1, -1), gla_norm_g[0],
                      _gla_constants(), lc, col_aq, col_ak, col_av, col_ag, gla_dk, gla_dv)
    sink_col = jnp.broadcast_to(
        jnp.repeat(swa_sink[0].reshape(SWA_KV_HEADS, SWA_GROUP), SWA_BLOCK, axis=1)[:, :, None],
        (SWA_KV_HEADS, SWA_GROUP * SWA_BLOCK, LANES))
    b_out = _swa_call(p0, sink_col, lc, col_bq, col_bk, col_bv, swa_hd)

    w_o = w_out_even[0].astype(BF16)
    n_a = GLA_HEADS * gla_dv
    xa1, h2, aff_t = _outproj_call([a_out, b_out], [w_o[:n_a], w_o[n_a:]], xa, mod0, g_norm_ffn[0],
                                   w_router[0], 0, t, ncb)
    cap_lat = s_len * EC_FACTOR // n_exp
    cap_ctx = lc * EC_FACTOR // n_exp
    segs0 = [(lc, s_len, cap_lat, 0), (0, lc, cap_ctx, cap_lat)]
    xa2 = _moe(h2, aff_t, xa1, mod0, g_final, w_gate_up, w_down, 0, segs0, cap_lat, cap_ctx, lc, False)

    mod1 = mod_for(1)
    lambda_init = 0.8 - 0.6 * float(np.exp(-0.3 * 1))
    w_qkv = w_qkv_odd[0].astype(BF16)
    n_qkv = w_qkv.shape[1]
    q_scale = DIFF_HD ** -0.5 * float(np.log2(np.e))
    rope1 = [(ch * COL_CHUNK < 2 * d, q_scale if ch * COL_CHUNK < d else 1.0)
             for ch in range(n_qkv // COL_CHUNK)]
    cos_c, sin_c = _rope_tables(s_len, lc, DIFF_HD)
    (p1,) = _proj_call(xa2, mod1, g_norm_mix[1], w_qkv, cos_c, sin_c, rope1, DIFF_HD // 4, ncb)
    y1 = _diff_call(p1, diff_lambda[0], diff_subln_g[0], lc, diff_heads, lambda_init)
    x3, h2b, aff_tb = _outproj_call([y1], [w_out_odd[0].astype(BF16)], xa2, mod1, g_norm_ffn[1],
                                    w_router[1], ncb, s_len, ncb)
    segs1 = [(0, s_len, cap_lat, 0)]
    return _moe(h2b, aff_tb, x3, mod1, g_final, w_gate_up, w_down, 1, segs1, cap_lat, 0, 0, True)
```

```python
import functools

import numpy as np
import jax
import jax.numpy as jnp
from jax import lax
from jax.experimental import pallas as pl
from jax.experimental.pallas import tpu as pltpu

F32 = jnp.float32
BF16 = jnp.bfloat16

GRID_W = 64
ROPE_THETA = 10000.0
NORM_EPS = 1e-6
GLA_HEADS = 4
GLA_GATE_RANK = 16
GLA_GATE_NORM = 16.0
SWA_HEADS = 8
SWA_KV_HEADS = 2
SWA_GROUP = SWA_HEADS // SWA_KV_HEADS
SWA_WINDOW = 128
SWA_BLOCK = 128
DIFF_HD = 64
N_EXPERTS = 16
EC_FACTOR = 2

LANES = 128
ROW_TILE = 256
COL_CHUNK = 256
GLA_CHUNK = 128
GLA_LEVELS = 7
GLA_SAFE_RANGE = 60.0
DIFF_QBLOCK = 256
DIFF_SAFE_BOUND = 60.0
DIFF_BOUND_SLACK = 1.0 + 2.0 ** -6
VMEM_LIMIT = 56 * 1024 * 1024
NEG_BIG = -1e30


def _cparams(*sem):
    return pltpu.CompilerParams(dimension_semantics=sem, vmem_limit_bytes=VMEM_LIMIT)


def _pick(n, cands):
    for c in cands:
        if n % c == 0:
            return c
    raise ValueError(f"no tile for {n} in {cands}")


def _nt(a, b):
    return lax.dot_general(a, b, (((1,), (1,)), ((), ())), preferred_element_type=F32)


def _tn(a, b):
    return lax.dot_general(a, b, (((0,), (0,)), ((), ())), preferred_element_type=F32)


def _dot(a, b):
    return jnp.dot(a, b, preferred_element_type=F32)


def _split3(a):
    hi = a.astype(BF16)
    r1 = a - hi.astype(F32)
    mid = r1.astype(BF16)
    lo = (r1 - mid.astype(F32)).astype(BF16)
    return hi, mid, lo


def _split2(a):
    hi = a.astype(BF16)
    lo = (a - hi.astype(F32)).astype(BF16)
    return hi, lo


def _silu(x):
    return x * (1.0 / (1.0 + jnp.exp(-x)))


def _rms(x, eps=NORM_EPS):
    return x * lax.rsqrt(jnp.mean(x * x, axis=-1, keepdims=True) + eps)


def _row_specs(stream, n_ctx_blocks, off=0):
    if isinstance(stream, tuple):
        ctx, x = stream
        d = x.shape[2]
        return ([pl.BlockSpec((1, ROW_TILE, d), lambda bi, i: (bi, jnp.minimum(i + off, n_ctx_blocks - 1), 0)),
                 pl.BlockSpec((1, ROW_TILE, d), lambda bi, i: (bi, jnp.maximum(i + off - n_ctx_blocks, 0), 0))],
                [ctx, x])
    d = stream.shape[2]
    return [pl.BlockSpec((1, ROW_TILE, d), lambda bi, i: (bi, i + off, 0))], [stream]


def _load_rows(row_refs, n_ctx_blocks, off=0):
    if len(row_refs) == 2:
        return jnp.where(pl.program_id(1) + off < n_ctx_blocks, row_refs[0][0], row_refs[1][0])
    return row_refs[0][0]


def _ada_kernel(c_ref, w_ref, b_ref, o_ref):
    s = _silu(c_ref[...])
    s_hi, s_lo = _split2(s)
    w_hi, w_lo = _split2(w_ref[0])
    acc = _dot(s_hi, w_hi) + (_dot(s_lo, w_hi) + _dot(s_hi, w_lo))
    o_ref[0] = acc + b_ref[0]


def _ada_call(cc, w_ada, b_ada):
    depth, d, n = w_ada.shape
    r = cc.shape[0]
    tn = _pick(n, (768, 512, 256, 128))
    return pl.pallas_call(
        _ada_kernel,
        grid=(depth, n // tn),
        in_specs=[pl.BlockSpec((r, d), lambda l, j: (0, 0)),
                  pl.BlockSpec((1, d, tn), lambda l, j: (l, 0, j)),
                  pl.BlockSpec((1, 1, tn), lambda l, j: (l, 0, j))],
        out_specs=pl.BlockSpec((1, r, tn), lambda l, j: (l, 0, j)),
        out_shape=jax.ShapeDtypeStruct((depth, r, n), F32),
        compiler_params=_cparams("parallel", "parallel"),
        name="adaln",
    )(cc, w_ada, b_ada.reshape(depth, 1, n))


def _proj_kernel(*refs, n_row_refs, n_ctx_blocks, chunk_ops, quarter, has_r):
    row_refs = refs[:n_row_refs]
    mod_ref, g_ref, w_ref, cos_ref, sin_ref = refs[n_row_refs:n_row_refs + 5]
    rest = refs[n_row_refs + 5:]
    if has_r:
        wr_ref, o_ref, r_ref = rest
    else:
        (o_ref,) = rest
    x = _load_rows(row_refs, n_ctx_blocks)
    y = _rms(x) * g_ref[...]
    h = (y * (1.0 + mod_ref[0, 0, 1:2, :]) + mod_ref[0, 0, 0:1, :]).astype(BF16)
    reps = COL_CHUNK // LANES
    cos_t = jnp.concatenate([cos_ref[...]] * reps, axis=1)
    sin_t = jnp.concatenate([sin_ref[...]] * reps, axis=1)
    lane = lax.broadcasted_iota(jnp.int32, (1, COL_CHUNK), 1)
    first = (lane % (2 * quarter)) < quarter
    for c, (rope, scale) in enumerate(chunk_ops):
        acc = _dot(h, w_ref[:, c * COL_CHUNK:(c + 1) * COL_CHUNK])
        if rope:
            partner = jnp.where(first, pltpu.roll(acc, COL_CHUNK - quarter, 1), pltpu.roll(acc, quarter, 1))
            acc = acc * cos_t + partner * sin_t
        if scale != 1.0:
            acc = acc * scale
        o_ref[0, :, c * COL_CHUNK:(c + 1) * COL_CHUNK] = acc.astype(BF16)
    if has_r:
        r_ref[0] = _dot(h, wr_ref[...]).astype(BF16)


def _proj_call(stream, mod, g, w, cos_t, sin_t, chunk_ops, quarter, n_ctx_blocks, wr=None):
    d, n = w.shape
    t = cos_t.shape[0]
    b = mod.shape[0]
    assert n == len(chunk_ops) * COL_CHUNK and t % ROW_TILE == 0
    has_r = wr is not None
    row_specs, row_args = _row_specs(stream, n_ctx_blocks)
    in_specs = row_specs + [
        pl.BlockSpec((1, 1, 6, d), lambda bi, i: (bi, jnp.where(i < n_ctx_blocks, 0, 1), 0, 0)),
        pl.BlockSpec((1, d), lambda bi, i: (0, 0)),
        pl.BlockSpec((d, n), lambda bi, i: (0, 0), pipeline_mode=pl.Buffered(1)),
        pl.BlockSpec((ROW_TILE, LANES), lambda bi, i: (i, 0)),
        pl.BlockSpec((ROW_TILE, LANES), lambda bi, i: (i, 0))]
    args = row_args + [mod, g.reshape(1, d), w, cos_t, sin_t]
    out_specs = [pl.BlockSpec((1, ROW_TILE, n), lambda bi, i: (bi, i, 0))]
    out_shape = [jax.ShapeDtypeStruct((b, t, n), BF16)]
    if has_r:
        in_specs.append(pl.BlockSpec((d, LANES), lambda bi, i: (0, 0)))
        args.append(wr)
        out_specs.append(pl.BlockSpec((1, ROW_TILE, LANES), lambda bi, i: (bi, i, 0)))
        out_shape.append(jax.ShapeDtypeStruct((b, t, LANES), BF16))
    return pl.pallas_call(
        functools.partial(_proj_kernel, n_row_refs=len(row_args), n_ctx_blocks=n_ctx_blocks,
                          chunk_ops=tuple(chunk_ops), quarter=quarter, has_r=has_r),
        grid=(b, t // ROW_TILE),
        in_specs=in_specs, out_specs=out_specs, out_shape=out_shape,
        compiler_params=_cparams("parallel", "parallel"),
        name="norm_proj",
    )(*args)


def _gla_constants():
    c = GLA_CHUNK
    idx = np.arange(c)
    cm = np.zeros((2, GLA_LEVELS + 2, c, c), np.float32)
    lmask = np.zeros((2, GLA_LEVELS + 2, c, c), np.float32)
    cm[0, 0] = (idx[None, :] <= idx[:, None])
    cm[1, 0] = (idx[None, :] >= idx[:, None])
    for l in range(GLA_LEVELS):
        s = c >> (l + 1)
        blk = idx // (2 * s)
        second = (idx % (2 * s)) >= s
        same = blk[:, None] == blk[None, :]
        sep_f = blk * 2 * s + s - 1
        cm[0, 1 + l] = (idx[None, :] <= sep_f[:, None])
        lmask[0, l] = same & second[:, None] & (~second)[None, :]
        sep_b = blk * 2 * s + s
        cm[1, 1 + l] = (idx[None, :] >= sep_b[:, None])
        lmask[1, l] = same & (~second)[:, None] & second[None, :]
    cm[:, GLA_LEVELS + 1] = 1.0
    lmask[:, GLA_LEVELS] = np.eye(c)
    lmask[:, GLA_LEVELS + 1] = cm[:, 0]
    return jnp.asarray(cm.reshape(2, (GLA_LEVELS + 2) * c, c), BF16), jnp.asarray(lmask, F32)


def _gla_kernel(q_ref, k_ref, v_ref, ag_ref, r_ref, gw_ref, gb_ref, ng_ref, cm_ref, lm_ref,
                o_ref, la_ref, of_ref, ob_ref, qe_ref, u_ref, ss_ref, dec_ref, st_ref,
                *, n_ctx_chunks, n_chunks, q_scale):
    c = GLA_CHUNK
    dv = st_ref.shape[1]
    r = r_ref[0]
    for d in range(2):
        z = _dot(r, gw_ref[d]) + gb_ref[d]
        la_ref[d] = (jnp.minimum(z, 0.0) - jnp.log(1.0 + jnp.exp(-jnp.abs(z)))) * (1.0 / GLA_GATE_NORM)
    st_ref[...] = jnp.zeros_like(st_ref)
    out_refs = (of_ref, ob_ref)

    def chunk_rows(ci):
        return pl.ds(pl.multiple_of(ci * c, c), c)

    def state_rows(ci):
        return pl.ds(pl.multiple_of(ci * dv, dv), dv)

    min_tot = None
    for d in range(2):
        m = jnp.min(jnp.sum(la_ref[d].reshape(n_chunks, c, la_ref.shape[2]), axis=1))
        min_tot = m if min_tot is None else jnp.minimum(min_tot, m)
    is_safe = min_tot >= -GLA_SAFE_RANGE

    def prep_fast(cp, carry):
        jobs = [(2 * cp + u, d) for u in range(2) for d in range(2)]
        rows = [chunk_rows(ci) for ci, _ in jobs]
        qs = {u: q_ref[0, chunk_rows(2 * cp + u), :].astype(F32) * q_scale for u in range(2)}
        ks = {u: k_ref[0, chunk_rows(2 * cp + u), :].astype(F32) for u in range(2)}
        bcs = {}
        for j, (ci, d) in enumerate(jobs):
            g_hi, g_lo = _split2(la_ref[d, rows[j], :])
            cm = cm_ref[d, 0:c, :]
            bcs[j] = _dot(cm, g_hi) + _dot(cm, g_lo)
        qes, kins, decs = {}, {}, {}
        for j, (ci, d) in enumerate(jobs):
            bc = bcs[j]
            qes[j] = (qs[j // 2] * jnp.exp(bc)).astype(BF16)
            kins[j] = ks[j // 2] * jnp.exp(-bc)
            decs[j] = jnp.exp(bc[c - 1:c] if d == 0 else bc[0:1])
        atts = {j: _nt(qes[j], kins[j].astype(BF16)) for j in range(len(jobs))}
        for j, (ci, d) in enumerate(jobs):
            att = (lm_ref[d, GLA_LEVELS + 1] * atts[j]).astype(BF16)
            out_refs[d][rows[j], :] = _dot(att, v_ref[0, rows[j], :])
            qe_ref[d, rows[j], :] = qes[j]
            u_ref[d, state_rows(ci), :] = _tn(v_ref[0, rows[j], :], (kins[j] * decs[j]).astype(BF16))
            dec_ref[d, pl.ds(pl.multiple_of(ci * 8, 8), 8), :] = jnp.broadcast_to(decs[j], (8, decs[j].shape[1]))
        return carry

    def prep(ci, carry):
        rows = chunk_rows(ci)
        q = q_ref[0, rows, :].astype(F32) * q_scale
        k = k_ref[0, rows, :].astype(F32)
        v = v_ref[0, rows, :]
        qk = _nt(q.astype(BF16), k.astype(BF16))
        for d in range(2):
            g_hi, g_lo = _split2(la_ref[d, rows, :])
            cm = cm_ref[d]
            cum = _dot(cm, g_hi) + _dot(cm, g_lo)
            bc = cum[0:c]
            tot = cum[(GLA_LEVELS + 1) * c:(GLA_LEVELS + 2) * c]
            att = lm_ref[d, GLA_LEVELS] * qk
            for l in range(GLA_LEVELS):
                ref_l = cum[(1 + l) * c:(2 + l) * c]
                eq = jnp.exp(jnp.minimum(bc - ref_l, 0.0))
                ek = jnp.exp(jnp.minimum(ref_l - bc, 0.0))
                att = att + lm_ref[d, l] * _nt((q * eq).astype(BF16), (k * ek).astype(BF16))
            out_refs[d][rows, :] = _dot(att.astype(BF16), v)
            qe_ref[d, rows, :] = (q * jnp.exp(bc)).astype(BF16)
            u_ref[d, state_rows(ci), :] = _tn(v, (k * jnp.exp(tot - bc)).astype(BF16))
            dec_ref[d, pl.ds(pl.multiple_of(ci * 8, 8), 8), :] = jnp.exp(tot[0:8])
        return carry

    @pl.when(is_safe)
    def _():
        lax.fori_loop(0, n_chunks // 2, prep_fast, 0)

    @pl.when(jnp.logical_not(is_safe))
    def _():
        lax.fori_loop(0, n_chunks, prep, 0)

    def scan(t, carry):
        cb = jnp.where(t < n_ctx_chunks, n_ctx_chunks - 1 - t, n_chunks - 1 - (t - n_ctx_chunks))
        for d, ci in ((0, t), (1, cb)):
            st = st_ref[d]
            ss_ref[d, state_rows(ci), :] = st.astype(BF16)
            dec = dec_ref[d, pl.ds(pl.multiple_of(ci * 8, 8), 1), :]
            st_ref[d] = st * dec + u_ref[d, state_rows(ci), :]
        return carry

    lax.fori_loop(0, n_chunks, scan, 0)

    def inter(ci, carry):
        rows = chunk_rows(ci)
        for d in range(2):
            out_refs[d][rows, :] += _nt(qe_ref[d, rows, :], ss_ref[d, state_rows(ci), :])
        return carry

    lax.fori_loop(0, n_chunks, inter, 0, unroll=2)
    o = of_ref[...] + ob_ref[...]
    ag = ag_ref[0].astype(F32)
    o_ref[0] = (_rms(o) * ng_ref[...] * _silu(ag)).astype(BF16)


def _gla_call(p0, r, gw_pad, gb, norm_g, consts, n_ctx_rows, col_q, col_k, col_v, col_g, dk, dv):
    b, t, _ = p0.shape
    n_chunks = t // GLA_CHUNK
    cm, lm = consts
    kern = functools.partial(_gla_kernel, n_ctx_chunks=n_ctx_rows // GLA_CHUNK, n_chunks=n_chunks,
                             q_scale=float(dk) ** -0.5)
    full = lambda shape: pl.BlockSpec(shape, lambda bi, h: (0,) * len(shape))
    return pl.pallas_call(
        kern,
        grid=(b, GLA_HEADS),
        in_specs=[pl.BlockSpec((1, t, dk), lambda bi, h: (bi, 0, col_q // dk + h)),
                  pl.BlockSpec((1, t, dk), lambda bi, h: (bi, 0, col_k // dk + h)),
                  pl.BlockSpec((1, t, dv), lambda bi, h: (bi, 0, col_v // dv + h)),
                  pl.BlockSpec((1, t, dv), lambda bi, h: (bi, 0, col_g // dv + h)),
                  pl.BlockSpec((1, t, LANES), lambda bi, h: (bi, 0, 0)),
                  pl.BlockSpec((2, LANES, dk), lambda bi, h: (0, 0, h)),
                  pl.BlockSpec((2, 1, dk), lambda bi, h: (0, 0, h)),
                  full((1, dv)), full(cm.shape), full(lm.shape)],
        out_specs=pl.BlockSpec((1, t, dv), lambda bi, h: (bi, 0, h)),
        out_shape=jax.ShapeDtypeStruct((b, t, GLA_HEADS * dv), BF16),
        scratch_shapes=[pltpu.VMEM((2, t, dk), F32), pltpu.VMEM((t, dv), F32), pltpu.VMEM((t, dv), F32),
                        pltpu.VMEM((2, t, dk), BF16), pltpu.VMEM((2, n_chunks * dv, dk), F32),
                        pltpu.VMEM((2, n_chunks * dv, dk), BF16),
                        pltpu.VMEM((2, n_chunks * 8, dk), F32), pltpu.VMEM((2, dv, dk), F32)],
        compiler_params=_cparams("parallel", "parallel"),
        name="gla",
    )(p0, p0, p0, p0, r, gw_pad, gb, norm_g.reshape(1, dv), cm, lm)


def _swa_kernel(q_ref, k_ref, v_ref, sink_ref, o_ref, s_ref, *, n_ctx_rows, n_blocks):
    blk = SWA_BLOCK
    win = 3 * blk
    ncb = n_ctx_rows // blk
    nlb = n_blocks - ncb
    sink = sink_ref[0, :, 0:1] * float(np.log2(np.e))
    rel0 = (lax.broadcasted_iota(jnp.int32, (blk, win), 1) - lax.broadcasted_iota(jnp.int32, (blk, win), 0))
    no_bias = jnp.zeros((blk, n_ctx_rows), F32)

    def key_rows(j):
        kb0 = jnp.clip(j - ncb - 1, 0, nlb - 3)
        return kb0, pl.ds(pl.multiple_of(n_ctx_rows + kb0 * blk, blk), win)

    def scores(j, slot):
        kb0, rows = key_rows(j)
        q = q_ref[0, pl.ds(pl.multiple_of(j * blk, blk), blk), :]
        q4 = jnp.concatenate([q[:, g * blk:(g + 1) * blk] for g in range(SWA_GROUP)], axis=0)
        keys = jnp.concatenate([k_ref[0, rows, :], k_ref[0, 0:n_ctx_rows, :]], axis=0)
        rel = rel0 + jnp.where(j < ncb, 4 * win, (kb0 - (j - ncb)) * blk)
        bias = jnp.concatenate([jnp.where(jnp.abs(rel) <= SWA_WINDOW, 0.0, NEG_BIG), no_bias], axis=1)
        s_ref[slot] = _nt(q4, keys) + jnp.concatenate([bias] * SWA_GROUP, axis=0)

    def finish(j, slot):
        _, rows = key_rows(j)
        vals = jnp.concatenate([v_ref[0, rows, :], v_ref[0, 0:n_ctx_rows, :]], axis=0)
        s = s_ref[slot]
        m = jnp.maximum(jnp.max(s, axis=1, keepdims=True), sink)
        e = jnp.exp2(s - m)
        den = jnp.sum(e, axis=1, keepdims=True) + jnp.exp2(sink - m)
        o = _dot(e.astype(BF16), vals) / den
        o_ref[0, pl.ds(pl.multiple_of(j * blk, blk), blk), :] = jnp.concatenate(
            [o[g * blk:(g + 1) * blk] for g in range(SWA_GROUP)], axis=1).astype(BF16)

    scores(0, 0)

    def body(jj, carry):
        j0 = 2 * jj
        scores(j0 + 1, 1)
        finish(j0, 0)
        scores(j0 + 2, 0)
        finish(j0 + 1, 1)
        return carry

    lax.fori_loop(0, n_blocks // 2 - 1, body, 0)
    scores(n_blocks - 1, 1)
    finish(n_blocks - 2, 0)
    finish(n_blocks - 1, 1)


def _swa_call(p0, sink_col, n_ctx_rows, col_q, col_k, col_v, hd):
    b, t, _ = p0.shape
    blk = SWA_BLOCK
    nb = t // blk
    gq = SWA_GROUP * hd
    assert nb % 2 == 0 and nb - n_ctx_rows // blk >= 3 and SWA_WINDOW <= blk
    return pl.pallas_call(
        functools.partial(_swa_kernel, n_ctx_rows=n_ctx_rows, n_blocks=nb),
        grid=(b, SWA_KV_HEADS),
        in_specs=[pl.BlockSpec((1, t, gq), lambda bi, h: (bi, 0, col_q // gq + h)),
                  pl.BlockSpec((1, t, hd), lambda bi, h: (bi, 0, col_k // hd + h)),
                  pl.BlockSpec((1, t, hd), lambda bi, h: (bi, 0, col_v // hd + h)),
                  pl.BlockSpec((1, SWA_GROUP * blk, LANES), lambda bi, h: (h, 0, 0))],
        out_specs=pl.BlockSpec((1, t, gq), lambda bi, h: (bi, 0, h)),
        out_shape=jax.ShapeDtypeStruct((b, t, SWA_HEADS * hd), BF16),
        scratch_shapes=[pltpu.VMEM((2, SWA_GROUP * blk, 3 * blk + n_ctx_rows), F32)],
        compiler_params=_cparams("parallel", "parallel"),
        name="swa",
    )(p0, p0, p0, sink_col)


def _diff_kernel(lam_ref, q_ref, k_ref, v_ref, g_ref, o_ref, s_ref, kx_ref, vx_ref, qn_ref,
                 *, row_off, n_qblocks, lambda_init):
    tq = DIFF_QBLOCK
    hd2 = 2 * DIFF_HD
    n_keys = k_ref.shape[1]
    lam = lam_ref[...]
    s01 = jnp.sum(lam[0:1] * lam[1:2], axis=1, keepdims=True)
    s23 = jnp.sum(lam[2:3] * lam[3:4], axis=1, keepdims=True)
    lam_full = jnp.exp(s01) - jnp.exp(s23) + lambda_init
    lane = lax.broadcasted_iota(jnp.int32, (1, hd2), 1)
    first = lane < DIFF_HD
    col0 = lane == 0
    zero = jnp.zeros((), BF16)

    d_idx = lax.broadcasted_iota(jnp.int32, (hd2, 2 * hd2), 0)
    j_idx = lax.broadcasted_iota(jnp.int32, (hd2, 2 * hd2), 1)
    map_sum = jnp.where((d_idx < DIFF_HD) == (j_idx < hd2), 1.0, 0.0).astype(BF16)

    def sq_norms(x):
        xf = x.astype(F32)
        return _dot((xf * xf).astype(BF16), map_sum)

    kn = sq_norms(k_ref[0])
    kmax0 = jnp.sqrt(jnp.max(kn[:, 0:hd2], axis=0, keepdims=True)) * DIFF_BOUND_SLACK
    kmax1 = jnp.sqrt(jnp.max(kn[:, hd2:2 * hd2], axis=0, keepdims=True)) * DIFF_BOUND_SLACK
    qn_ref[...] = sq_norms(q_ref[0, row_off:, :])
    qmax0 = jnp.sqrt(jnp.max(qn_ref[:, 0:hd2], axis=0, keepdims=True))
    qmax1 = jnp.sqrt(jnp.max(qn_ref[:, hd2:2 * hd2], axis=0, keepdims=True))
    is_safe = jnp.max(jnp.maximum(qmax0 * kmax0, qmax1 * kmax1)) <= DIFF_SAFE_BOUND

    def q_block(j):
        q = q_ref[0, pl.ds(pl.multiple_of(row_off + j * tq, tq), tq), :]
        return q, jnp.where(first, q, zero), jnp.where(first, zero, q)

    def write_out(j, o):
        y = _rms(o) * g_ref[...] * (1.0 - lambda_init)
        o_ref[0, pl.ds(pl.multiple_of(j * tq, tq), tq), :] = y.astype(BF16)

    def scores_fast(j, slot):
        _, qa, qb = q_block(j)
        qn = jnp.sqrt(qn_ref[pl.ds(pl.multiple_of(j * tq, tq), tq), :])
        xa = jnp.where(col0, -(qn[:, 0:hd2] * kmax0), 0.0).astype(BF16)
        xb = jnp.where(col0, -(qn[:, hd2:2 * hd2] * kmax1), 0.0).astype(BF16)
        qq = jnp.concatenate([jnp.concatenate([qa, xa], axis=1), jnp.concatenate([qb, xb], axis=1)], axis=0)
        s_ref[slot] = _nt(qq, kx_ref[...])

    def finish_fast(j, slot):
        e = jnp.exp2(s_ref[slot]).astype(BF16)
        ov = _dot(e, vx_ref[...])
        p0 = ov[0:tq, 0:hd2] / ov[0:tq, hd2:2 * hd2]
        p1 = ov[tq:2 * tq, 0:hd2] / ov[tq:2 * tq, hd2:2 * hd2]
        write_out(j, p0 - lam_full * p1)

    def scores_max(j, slot):
        _, qa, qb = q_block(j)
        s_ref[slot] = _nt(jnp.concatenate([qa, qb], axis=0), k_ref[0])

    def finish_max(j, slot):
        s = s_ref[slot]
        e = jnp.exp2(s - jnp.max(s, axis=1, keepdims=True))
        z = jnp.sum(e, axis=1, keepdims=True)
        coef = lam_full * z[0:tq] / z[tq:2 * tq]
        a = e[0:tq] - coef * e[tq:2 * tq]
        write_out(j, _dot(a.astype(BF16), v_ref[0]) * (1.0 / z[0:tq]))

    def run(scores, finish):
        scores(0, 0)

        def body(jj, carry):
            j0 = 2 * jj
            scores(j0 + 1, 1)
            finish(j0, 0)
            scores(j0 + 2, 0)
            finish(j0 + 1, 1)
            return carry

        lax.fori_loop(0, n_qblocks // 2 - 1, body, 0)
        scores(n_qblocks - 1, 1)
        finish(n_qblocks - 2, 0)
        finish(n_qblocks - 1, 1)

    @pl.when(is_safe)
    def _():
        kx_ref[:, 0:hd2] = k_ref[0]
        kx_ref[:, hd2:2 * hd2] = jnp.broadcast_to(jnp.where(col0, 1.0, 0.0).astype(BF16), (n_keys, hd2))
        vx_ref[:, 0:hd2] = v_ref[0]
        vx_ref[:, hd2:2 * hd2] = jnp.ones((n_keys, hd2), BF16)
        run(scores_fast, finish_fast)

    @pl.when(jnp.logical_not(is_safe))
    def _():
        run(scores_max, finish_max)


def _diff_call(p1, lam, subln_g, n_ctx_rows, n_heads, lambda_init):
    b, t, _ = p1.shape
    s_len = t - n_ctx_rows
    hd2 = 2 * DIFF_HD
    n_qblocks = s_len // DIFF_QBLOCK
    assert n_qblocks >= 2 and n_qblocks % 2 == 0
    return pl.pallas_call(
        functools.partial(_diff_kernel, row_off=n_ctx_rows, n_qblocks=n_qblocks, lambda_init=lambda_init),
        grid=(b, n_heads),
        in_specs=[pl.BlockSpec((4, DIFF_HD), lambda bi, h: (0, 0)),
                  pl.BlockSpec((1, t, hd2), lambda bi, h: (bi, 0, h)),
                  pl.BlockSpec((1, t, hd2), lambda bi, h: (bi, 0, n_heads + h)),
                  pl.BlockSpec((1, t, hd2), lambda bi, h: (bi, 0, 2 * n_heads + h)),
                  pl.BlockSpec((1, hd2), lambda bi, h: (0, 0))],
        out_specs=pl.BlockSpec((1, s_len, hd2), lambda bi, h: (bi, 0, h)),
        out_shape=jax.ShapeDtypeStruct((b, s_len, n_heads * hd2), BF16),
        scratch_shapes=[pltpu.VMEM((2, 2 * DIFF_QBLOCK, t), F32), pltpu.VMEM((t, 2 * hd2), BF16),
                        pltpu.VMEM((t, 2 * hd2), BF16), pltpu.VMEM((s_len, 2 * hd2), F32)],
        compiler_params=_cparams("parallel", "parallel"),
        name="diff_attn",
    )(lam, p1, p1, p1, subln_g.reshape(1, hd2))


def _outproj_kernel(*refs, n_y, n_row_refs, n_ctx_blocks, row_block_off):
    y_refs = refs[:n_y]
    w_refs = refs[n_y:2 * n_y]
    row_refs = refs[2 * n_y:2 * n_y + n_row_refs]
    mod_ref, g_ref, wr_ref, xn_ref, h_ref, aff_ref = refs[2 * n_y + n_row_refs:]
    n_exp = aff_ref.shape[1]
    acc = _dot(y_refs[0][0], w_refs[0][...])
    for yr, wr in zip(y_refs[1:], w_refs[1:]):
        acc = acc + _dot(yr[0], wr[...])
    xn = _load_rows(row_refs, n_ctx_blocks, row_block_off) + mod_ref[0, 0, 2:3, :] * acc
    xn_ref[0] = xn
    h = _rms(xn) * g_ref[...] * (1.0 + mod_ref[0, 0, 4:5, :]) + mod_ref[0, 0, 3:4, :]
    h_bf = h.astype(BF16)
    h_ref[0] = h_bf
    logits = _dot(h_bf, wr_ref[...])
    logits = jnp.where(lax.broadcasted_iota(jnp.int32, (1, LANES), 1) < n_exp, logits, NEG_BIG)
    ex = jnp.exp(logits - jnp.max(logits, axis=1, keepdims=True))
    aff = ex / jnp.sum(ex, axis=1, keepdims=True)
    aff_ref[0] = jnp.transpose(aff)[0:n_exp]


def _outproj_call(ys, ws, stream, mod, g, w_router, row_block_off, n_rows, n_ctx_blocks):
    b, d = mod.shape[0], mod.shape[3]
    n_y = len(ys)
    e = w_router.shape[1]
    w_router_pad = jnp.concatenate([w_router, jnp.zeros((d, LANES - e), F32)], axis=1).astype(BF16)
    row_specs, row_args = _row_specs(stream, n_ctx_blocks, row_block_off)
    in_specs = ([pl.BlockSpec((1, ROW_TILE, y.shape[2]), lambda bi, i: (bi, i, 0)) for y in ys]
                + [pl.BlockSpec(w.shape, lambda bi, i: (0, 0), pipeline_mode=pl.Buffered(1)) for w in ws]
                + row_specs
                + [pl.BlockSpec((1, 1, 6, d),
                                lambda bi, i: (bi, jnp.where(i + row_block_off < n_ctx_blocks, 0, 1), 0, 0)),
                   pl.BlockSpec((1, d), lambda bi, i: (0, 0)),
                   pl.BlockSpec((d, LANES), lambda bi, i: (0, 0))])
    return pl.pallas_call(
        functools.partial(_outproj_kernel, n_y=n_y, n_row_refs=len(row_args), n_ctx_blocks=n_ctx_blocks,
                          row_block_off=row_block_off),
        grid=(b, n_rows // ROW_TILE),
        in_specs=in_specs,
        out_specs=[pl.BlockSpec((1, ROW_TILE, d), lambda bi, i: (bi, i, 0)),
                   pl.BlockSpec((1, ROW_TILE, d), lambda bi, i: (bi, i, 0)),
                   pl.BlockSpec((1, e, ROW_TILE), lambda bi, i: (bi, 0, i))],
        out_shape=[jax.ShapeDtypeStruct((b, n_rows, d), F32),
                   jax.ShapeDtypeStruct((b, n_rows, d), BF16),
                   jax.ShapeDtypeStruct((b, e, n_rows), F32)],
        compiler_params=_cparams("parallel", "parallel"),
        name="out_proj",
    )(*ys, *ws, *row_args, mod, g.reshape(1, d), w_router_pad)


def _prefix_lanes(m, upper):
    e, n = m.shape
    carry = jnp.zeros((e, 1), F32)
    outs = []
    for blk in range(n // LANES):
        mb = m[:, blk * LANES:(blk + 1) * LANES]
        inc = _dot(mb.astype(BF16), upper)
        outs.append(inc - mb + carry)
        carry = carry + jnp.sum(mb, axis=1, keepdims=True)
    return jnp.concatenate(outs, axis=1)


def _transpose_exact(x, eye):
    hi, mid, lo = _split3(x)
    return _nt(eye, hi) + (_nt(eye, mid) + _nt(eye, lo))


def _route_kernel(aff_ref, pos_ref, gate_ref, posn_ref, lo_ref, hi_ref, *, segs):
    n_exp = aff_ref.shape[1]
    lo_acc = jnp.zeros((n_exp, LANES), F32)
    hi_acc = jnp.zeros((n_exp, LANES), F32)
    li = lax.broadcasted_iota(jnp.int32, (LANES, LANES), 0)
    lj = lax.broadcasted_iota(jnp.int32, (LANES, LANES), 1)
    upper = jnp.where(li <= lj, 1.0, 0.0).astype(BF16)
    eye = jnp.where(li == lj, 1.0, 0.0).astype(BF16)
    all_bits = [pltpu.bitcast(aff_ref[0, :, off:off + n], jnp.int32) for (off, n, _, _) in segs]

    def search(it, thrs):
        out = []
        for bits, (_, _, cap, _), thr in zip(all_bits, segs, thrs):
            cand = thr | jnp.left_shift(jnp.int32(1), 30 - it)
            cnt = jnp.sum(jnp.where(bits >= cand, 1.0, 0.0), axis=1, keepdims=True)
            out.append(jnp.where(cnt >= cap, cand, thr))
        return tuple(out)

    all_thr = lax.fori_loop(0, 31, search, tuple(jnp.zeros((n_exp, 1), jnp.int32) for _ in segs))
    for (off, n, cap, slot_off), bits, thr in zip(segs, all_bits, all_thr):
        a = aff_ref[0, :, off:off + n]
        gt = jnp.where(bits > thr, 1.0, 0.0)
        eq = jnp.where(bits == thr, 1.0, 0.0)
        need = cap - jnp.sum(gt, axis=1, keepdims=True)
        sel = gt + eq * jnp.where(_prefix_lanes(eq, upper) < need, 1.0, 0.0)
        slot = _prefix_lanes(sel, upper)
        pos = jnp.where(sel > 0.0, slot + slot_off, -1.0)
        pos_ref[0, :, off:off + n] = pos.astype(jnp.int32)
        gate_ref[0, :, off:off + n] = sel * a
        for blk in range(n // LANES):
            cols = slice(blk * LANES, (blk + 1) * LANES)
            rows = slice(off + blk * LANES, off + (blk + 1) * LANES)
            posn_ref[0, rows, :] = _transpose_exact(pos[:, cols], eye).astype(jnp.int32)
        assert cap <= 256 and (off + n) // LANES <= LANES
        tok = lax.broadcasted_iota(jnp.int32, (n, LANES), 0)
        blk_start = (lax.broadcasted_iota(jnp.int32, (n, LANES), 1) - off // LANES) * LANES
        lo_acc = lo_acc + _dot(slot.astype(BF16), jnp.where(tok == blk_start, 1.0, 0.0).astype(BF16))
        hi_acc = hi_acc + _dot((slot + sel).astype(BF16),
                               jnp.where(tok == blk_start + (LANES - 1), 1.0, 0.0).astype(BF16))
    lo_ref[0] = lo_acc.astype(jnp.int32)
    hi_ref[0] = hi_acc.astype(jnp.int32)


def _route_call(aff_t, segs):
    b, e, t = aff_t.shape
    return pl.pallas_call(
        functools.partial(_route_kernel, segs=tuple(segs)),
        grid=(b,),
        in_specs=[pl.BlockSpec((1, e, t), lambda bi: (bi, 0, 0))],
        out_specs=[pl.BlockSpec((1, e, t), lambda bi: (bi, 0, 0)),
                   pl.BlockSpec((1, e, t), lambda bi: (bi, 0, 0)),
                   pl.BlockSpec((1, t, e), lambda bi: (bi, 0, 0)),
                   pl.BlockSpec((1, e, LANES), lambda bi: (bi, 0, 0)),
                   pl.BlockSpec((1, e, LANES), lambda bi: (bi, 0, 0))],
        out_shape=[jax.ShapeDtypeStruct((b, e, t), jnp.int32),
                   jax.ShapeDtypeStruct((b, e, t), F32),
                   jax.ShapeDtypeStruct((b, t, e), jnp.int32),
                   jax.ShapeDtypeStruct((b, e, LANES), jnp.int32),
                   jax.ShapeDtypeStruct((b, e, LANES), jnp.int32)],
        compiler_params=_cparams("parallel"),
        name="route",
    )(aff_t)


def _gather_kernel(lo_ref, hi_ref, pos_ref, gate_ref, h_ref, o_ref, g_ref,
                   *, n_ctx_rows, n_lat_rows, cap_lat, sblk, win, margin):
    sb = pl.program_id(1)
    n_exp = pos_ref.shape[1]
    n_lat_blocks = cap_lat // sblk
    tok_per_blk = n_lat_rows // n_lat_blocks

    def emit(slot0, tok0, n_tok):
        rows = pl.ds(tok0, n_tok)
        slots = lax.broadcasted_iota(jnp.int32, (sblk, 1), 0) + slot0
        hits = [pos_ref[0, e:e + 1, rows] == slots for e in range(n_exp)]
        onehot = jnp.concatenate([jnp.where(h, 1.0, 0.0).astype(BF16) for h in hits], axis=0)
        res = _dot(onehot, h_ref[0, rows, :])
        for e in range(n_exp):
            o_ref[e, 0] = res[e * sblk:(e + 1) * sblk].astype(BF16)
            gsel = jnp.sum(jnp.where(hits[e], gate_ref[0, e:e + 1, rows], 0.0), axis=1, keepdims=True)
            g_ref[e, 0] = jnp.broadcast_to(gsel, (sblk, LANES))

    slot0 = sb * sblk
    w0 = jnp.clip(sb * tok_per_blk - margin, 0, n_lat_rows - win)
    tok0 = pl.multiple_of(n_ctx_rows + w0, LANES)
    first_blk = (n_ctx_rows + w0) // LANES
    fits = None
    for e in range(n_exp):
        ok = jnp.logical_and(lo_ref[0, e, first_blk] <= slot0,
                             hi_ref[0, e, first_blk + win // LANES - 1] >= slot0 + sblk)
        fits = ok if fits is None else jnp.logical_and(fits, ok)
    is_lat = sb < n_lat_blocks

    @pl.when(jnp.logical_and(is_lat, fits))
    def _():
        emit(slot0, tok0, win)

    @pl.when(jnp.logical_and(is_lat, jnp.logical_not(fits)))
    def _():
        emit(slot0, n_ctx_rows, n_lat_rows)

    if n_ctx_rows:
        @pl.when(jnp.logical_not(is_lat))
        def _():
            emit(cap_lat, 0, n_ctx_rows)


def _slot_window(tt, cap_lat, n_exp):
    return min(cap_lat, max(16, 2 * tt * EC_FACTOR // n_exp))


def _gather_call(lo, hi, pos, gate, h, cap_lat, cap_ctx, n_ctx_rows):
    b, e, t = pos.shape
    d = h.shape[2]
    n_slots = cap_lat + cap_ctx
    n_lat_rows = t - n_ctx_rows
    sblk = cap_ctx if cap_ctx else 32
    assert cap_lat % sblk == 0 and sblk % 16 == 0
    n_lat_blocks = cap_lat // sblk
    tok_per_blk = n_lat_rows // n_lat_blocks
    margin = tok_per_blk
    win = min(n_lat_rows, 3 * tok_per_blk)
    assert tok_per_blk % LANES == 0 and win % LANES == 0
    smem = lambda: pl.BlockSpec((1, e, LANES), lambda bi, si: (bi, 0, 0), memory_space=pltpu.SMEM)
    return pl.pallas_call(
        functools.partial(_gather_kernel, n_ctx_rows=n_ctx_rows, n_lat_rows=n_lat_rows, cap_lat=cap_lat,
                          sblk=sblk, win=win, margin=margin),
        grid=(b, n_slots // sblk),
        in_specs=[smem(), smem(),
                  pl.BlockSpec((1, e, t), lambda bi, si: (bi, 0, 0)),
                  pl.BlockSpec((1, e, t), lambda bi, si: (bi, 0, 0)),
                  pl.BlockSpec((1, t, d), lambda bi, si: (bi, 0, 0))],
        out_specs=[pl.BlockSpec((e, 1, sblk, d), lambda bi, si: (0, bi, si, 0)),
                   pl.BlockSpec((e, 1, sblk, LANES), lambda bi, si: (0, bi, si, 0))],
        out_shape=[jax.ShapeDtypeStruct((e, b, n_slots, d), BF16),
                   jax.ShapeDtypeStruct((e, b, n_slots, LANES), F32)],
        compiler_params=_cparams("parallel", "arbitrary"),
        name="gather",
    )(lo, hi, pos, gate, h)


def _ffn_up_kernel(x_ref, wg_ref, wu_ref, o_ref, wgb_ref, wub_ref):
    @pl.when(pl.program_id(2) == 0)
    def _():
        wgb_ref[...] = wg_ref[0, 0].astype(BF16)
        wub_ref[...] = wu_ref[0, 0].astype(BF16)

    x = x_ref[0]
    g = _dot(x, wgb_ref[...])
    u = _dot(x, wub_ref[...])
    o_ref[0] = (_silu(g) * u).astype(BF16)


def _ffn_down_kernel(a_ref, w_ref, gate_ref, o_ref, wb_ref):
    @pl.when(pl.program_id(2) == 0)
    def _():
        wb_ref[...] = w_ref[0, 0].astype(BF16)

    gate = jnp.concatenate([gate_ref[0]] * (o_ref.shape[2] // LANES), axis=1)
    o_ref[0] = (_dot(a_ref[0], wb_ref[...]) * gate).astype(BF16)


def _ffn_call(xs, gates, w_gu, w_dn, layer):
    e, m, d = xs.shape
    f = w_dn.shape[2]
    tm = _pick(m, (1152, 1024, 768, 512, 256, 128, 64, 32, 16, 8))
    tf = _pick(f, (512, 256, 128))
    nf = f // tf
    act = pl.pallas_call(
        _ffn_up_kernel,
        grid=(e, nf, m // tm),
        in_specs=[pl.BlockSpec((1, tm, d), lambda ei, j, i: (ei, i, 0)),
                  pl.BlockSpec((1, 1, d, tf), lambda ei, j, i: (layer, ei, 0, j)),
                  pl.BlockSpec((1, 1, d, tf), lambda ei, j, i: (layer, ei, 0, j + nf))],
        out_specs=pl.BlockSpec((1, tm, tf), lambda ei, j, i: (ei, i, j)),
        out_shape=jax.ShapeDtypeStruct((e, m, f), BF16),
        scratch_shapes=[pltpu.VMEM((d, tf), BF16), pltpu.VMEM((d, tf), BF16)],
        compiler_params=_cparams("parallel", "arbitrary", "arbitrary"),
        name="ffn_up",
    )(xs, w_gu, w_gu)
    tn = _pick(d, (1024, 512, 256, 128))
    return pl.pallas_call(
        _ffn_down_kernel,
        grid=(e, d // tn, m // tm),
        in_specs=[pl.BlockSpec((1, tm, f), lambda ei, j, i: (ei, i, 0)),
                  pl.BlockSpec((1, 1, f, tn), lambda ei, j, i: (layer, ei, 0, j)),
                  pl.BlockSpec((1, tm, LANES), lambda ei, j, i: (ei, i, 0))],
        out_specs=pl.BlockSpec((1, tm, tn), lambda ei, j, i: (ei, i, j)),
        out_shape=jax.ShapeDtypeStruct((e, m, d), BF16),
        scratch_shapes=[pltpu.VMEM((f, tn), BF16)],
        compiler_params=_cparams("parallel", "arbitrary", "arbitrary"),
        name="ffn_down",
    )(act, w_dn, gates)


def _combine_kernel(lo_ref, hi_ref, posn_ref, y_ref, x_ref, mod_ref, gf_ref, o_ref,
                    *, cap_lat, cap_ctx, n_ctx_tiles, blocks_per_tile, window, final_norm):
    n_exp = posn_ref.shape[2]
    i = pl.program_id(1)
    lane = lax.broadcasted_iota(jnp.int32, (1, n_exp), 1)
    posn = posn_ref[0].astype(F32)

    def pcol(e):
        return jnp.sum(jnp.where(lane == e, posn, 0.0), axis=1, keepdims=True)

    def scatter(slot0, cap):
        slots = lax.broadcasted_iota(jnp.int32, (1, cap), 1).astype(F32) + slot0
        hots = [jnp.where(pcol(e) == slots, 1.0, 0.0).astype(BF16) for e in range(n_exp)]
        rows = [y_ref[e, 0, slot0:slot0 + cap, :] for e in range(n_exp)]
        return _dot(jnp.concatenate(hots, axis=1), jnp.concatenate(rows, axis=0))

    def scatter_window(starts):
        w_tot = n_exp * window
        grp = lax.broadcasted_iota(jnp.int32, (n_exp, w_tot), 1) // window
        spread = jnp.where(grp == lax.broadcasted_iota(jnp.int32, (n_exp, w_tot), 0), 1.0, 0.0).astype(BF16)
        bpos = _dot(posn.astype(BF16), spread)
        wlane = lax.broadcasted_iota(jnp.int32, (1, w_tot), 1)
        target = (wlane % window).astype(F32)
        for e, st in enumerate(starts):
            target = jnp.where(wlane // window == e, target + st.astype(F32), target)
        onehot = jnp.where(bpos == target, 1.0, 0.0).astype(BF16)
        rows = [y_ref[e, 0, pl.ds(pl.multiple_of(st, 16), window), :] for e, st in enumerate(starts)]
        return _dot(onehot, jnp.concatenate(rows, axis=0))

    def finish(total, g2):
        out = x_ref[0] + g2 * total
        if final_norm:
            out = _rms(out) * gf_ref[...]
        o_ref[0] = out

    blk0 = i * blocks_per_tile
    starts = []
    fits = None
    for e in range(n_exp):
        lo = lo_ref[0, e, blk0]
        hi = hi_ref[0, e, blk0 + blocks_per_tile - 1]
        st = jnp.minimum(lax.shift_left(lax.shift_right_logical(lo, 4), 4), cap_lat - window)
        ok = hi - st <= window
        fits = ok if fits is None else jnp.logical_and(fits, ok)
        starts.append(st)
    is_lat = i >= n_ctx_tiles

    @pl.when(jnp.logical_and(is_lat, fits))
    def _():
        finish(scatter_window(starts), mod_ref[0, 1, 5:6, :])

    @pl.when(jnp.logical_and(is_lat, jnp.logical_not(fits)))
    def _():
        finish(scatter(0, cap_lat), mod_ref[0, 1, 5:6, :])

    if n_ctx_tiles:
        @pl.when(jnp.logical_not(is_lat))
        def _():
            finish(scatter(cap_lat, cap_ctx), mod_ref[0, 0, 5:6, :])


def _combine_call(lo, hi, posn, y, xa, mod, g_final, cap_lat, cap_ctx, n_ctx_rows, final_norm):
    b, t, d = xa.shape
    e = posn.shape[2]
    n_slots = y.shape[2]
    tt = ROW_TILE
    assert n_ctx_rows % tt == 0 and t % tt == 0
    window = _slot_window(tt, cap_lat, e)
    smem = lambda: pl.BlockSpec((1, e, LANES), lambda bi, i: (bi, 0, 0), memory_space=pltpu.SMEM)
    return pl.pallas_call(
        functools.partial(_combine_kernel, cap_lat=cap_lat, cap_ctx=cap_ctx, n_ctx_tiles=n_ctx_rows // tt,
                          blocks_per_tile=tt // LANES, window=window, final_norm=final_norm),
        grid=(b, t // tt),
        in_specs=[smem(), smem(),
                  pl.BlockSpec((1, tt, e), lambda bi, i: (bi, i, 0)),
                  pl.BlockSpec((e, 1, n_slots, d), lambda bi, i: (0, bi, 0, 0), pipeline_mode=pl.Buffered(1)),
                  pl.BlockSpec((1, tt, d), lambda bi, i: (bi, i, 0)),
                  pl.BlockSpec((1, 2, 6, d), lambda bi, i: (bi, 0, 0, 0)),
                  pl.BlockSpec((1, d), lambda bi, i: (0, 0))],
        out_specs=pl.BlockSpec((1, tt, d), lambda bi, i: (bi, i, 0)),
        out_shape=jax.ShapeDtypeStruct((b, t, d), F32),
        compiler_params=_cparams("parallel", "arbitrary"),
        name="combine",
    )(lo, hi, posn, y, xa, mod, g_final.reshape(1, d))


def _rope_tables(n_tokens, n_ctx_rows, head_dim):
    rows = n_tokens // GRID_W
    r = jnp.repeat(jnp.arange(rows, dtype=F32), GRID_W)
    col = jnp.tile(jnp.arange(GRID_W, dtype=F32), rows)
    axis_dim = head_dim // 2
    inv = ROPE_THETA ** (-jnp.arange(0, axis_dim, 2, dtype=F32) / axis_dim)
    ar, ac = r[:, None] * inv, col[:, None] * inv
    cos_p = jnp.concatenate([jnp.cos(ar), jnp.cos(ar), jnp.cos(ac), jnp.cos(ac)], axis=-1)
    sin_p = jnp.concatenate([-jnp.sin(ar), jnp.sin(ar), -jnp.sin(ac), jnp.sin(ac)], axis=-1)
    reps = LANES // head_dim
    cos_p, sin_p = jnp.tile(cos_p, (1, reps)), jnp.tile(sin_p, (1, reps))
    cos_t = jnp.concatenate([jnp.ones((n_ctx_rows, LANES), F32), cos_p], axis=0)
    sin_t = jnp.concatenate([jnp.zeros((n_ctx_rows, LANES), F32), sin_p], axis=0)
    return cos_t, sin_t


def _moe(h2, aff_t, x_res, mod, g_final, w_gu, w_dn, layer, segs, cap_lat, cap_ctx, n_ctx_rows, final_norm):
    b, t, d = h2.shape
    e = aff_t.shape[1]
    n_slots = cap_lat + cap_ctx
    pos, gate, posn, lo, hi = _route_call(aff_t, segs)
    xs, gs = _gather_call(lo, hi, pos, gate, h2, cap_lat, cap_ctx, n_ctx_rows)
    y = _ffn_call(xs.reshape(e, b * n_slots, d), gs.reshape(e, b * n_slots, LANES), w_gu, w_dn, layer)
    return _combine_call(lo, hi, posn, y.reshape(e, b, n_slots, d), x_res, mod, g_final, cap_lat, cap_ctx,
                         n_ctx_rows, final_norm)


def kernel(x, c, ctx, c_ctx, w_ada, b_ada, g_norm_mix, g_norm_ffn, w_in_even, gla_gate_w, gla_gate_b, gla_norm_g, swa_sink, w_out_even, w_qkv_odd, diff_lambda, diff_subln_g, w_out_odd, w_router, w_gate_up, w_down, g_final):
    b, s_len, d = x.shape
    lc = ctx.shape[1]
    depth = w_ada.shape[0]
    assert depth == 2 and lc % ROW_TILE == 0 and s_len % ROW_TILE == 0 and s_len % GRID_W == 0
    t = lc + s_len
    ncb = lc // ROW_TILE
    gla_dk, gla_dv = d // 4 // GLA_HEADS, d // 2 // GLA_HEADS
    swa_hd = d // 2 // SWA_HEADS
    diff_heads = d // (2 * DIFF_HD)
    n_exp = w_router.shape[2]
    assert gla_dk == LANES and swa_hd == LANES and 2 * DIFF_HD == LANES

    n_rows = -(-(b + 1) // 8) * 8
    cc = jnp.concatenate([c, c_ctx[None, :], jnp.zeros((n_rows - b - 1, d), F32)], axis=0)
    m_all = _ada_call(cc, w_ada, b_ada).reshape(depth, n_rows, 6, d)

    def mod_for(layer):
        lat = m_all[layer, :b]
        ctxm = jnp.broadcast_to(m_all[layer, b][None], (b, 6, d))
        return jnp.stack([ctxm, lat], axis=1)

    xa = (ctx, x)

    mod0 = mod_for(0)
    w_in = w_in_even[0]
    sizes = (GLA_HEADS * gla_dk, GLA_HEADS * gla_dk, GLA_HEADS * gla_dv, GLA_HEADS * gla_dv,
             GLA_GATE_RANK, GLA_GATE_RANK, SWA_HEADS * swa_hd, SWA_KV_HEADS * swa_hd, SWA_KV_HEADS * swa_hd)
    offs = np.concatenate([[0], np.cumsum(sizes)])
    seg = lambda i: w_in[:, offs[i]:offs[i + 1]]
    w_main = jnp.concatenate([seg(0), seg(1), seg(2), seg(3), seg(6), seg(7), seg(8)], axis=1).astype(BF16)
    w_rank = jnp.concatenate([seg(4), seg(5), jnp.zeros((d, LANES - 2 * GLA_GATE_RANK), F32)],
                             axis=1).astype(BF16)
    col_aq, col_ak = 0, sizes[0]
    col_av = col_ak + sizes[1]
    col_ag = col_av + sizes[2]
    col_bq = col_ag + sizes[3]
    col_bk = col_bq + sizes[6]
    col_bv = col_bk + sizes[7]
    n_main = col_bv + sizes[8]
    swa_q_scale = float(swa_hd) ** -0.5 * float(np.log2(np.e))
    rope0 = [(col_bq <= ch * COL_CHUNK < col_bv, swa_q_scale if col_bq <= ch * COL_CHUNK < col_bk else 1.0)
             for ch in range(n_main // COL_CHUNK)]
    cos_b, sin_b = _rope_tables(s_len, lc, swa_hd)
    p0, r0 = _proj_call(xa, mod0, g_norm_mix[0], w_main, cos_b, sin_b, rope0, swa_hd // 4, ncb, wr=w_rank)

    gw = gla_gate_w[0]
    gw_pad = jnp.zeros((2, LANES, GLA_HEADS * gla_dk), F32)
    gw_pad = gw_pad.at[0, 0:GLA_GATE_RANK].set(gw[0]).at[1, GLA_GATE_RANK:2 * GLA_GATE_RANK].set(gw[1])
    a_out = _gla_call(p0, r0, gw_pad.astype(BF16), gla_gate_b[0].reshape(2, 1, -1), gla_norm_g[0],
                      _gla_constants(), lc, col_aq, col_ak, col_av, col_ag, gla_dk, gla_dv)
    sink_col = jnp.broadcast_to(
        jnp.repeat(swa_sink[0].reshape(SWA_KV_HEADS, SWA_GROUP), SWA_BLOCK, axis=1)[:, :, None],
        (SWA_KV_HEADS, SWA_GROUP * SWA_BLOCK, LANES))
    b_out = _swa_call(p0, sink_col, lc, col_bq, col_bk, col_bv, swa_hd)

    w_o = w_out_even[0].astype(BF16)
    n_a = GLA_HEADS * gla_dv
    xa1, h2, aff_t = _outproj_call([a_out, b_out], [w_o[:n_a], w_o[n_a:]], xa, mod0, g_norm_ffn[0],
                                   w_router[0], 0, t, ncb)
    cap_lat = s_len * EC_FACTOR // n_exp
    cap_ctx = lc * EC_FACTOR // n_exp
    segs0 = [(lc, s_len, cap_lat, 0), (0, lc, cap_ctx, cap_lat)]
    xa2 = _moe(h2, aff_t, xa1, mod0, g_final, w_gate_up, w_down, 0, segs0, cap_lat, cap_ctx, lc, False)

    mod1 = mod_for(1)
    lambda_init = 0.8 - 0.6 * float(np.exp(-0.3 * 1))
    w_qkv = w_qkv_odd[0].astype(BF16)
    n_qkv = w_qkv.shape[1]
    q_scale = DIFF_HD ** -0.5 * float(np.log2(np.e))
    rope1 = [(ch * COL_CHUNK < 2 * d, q_scale if ch * COL_CHUNK < d else 1.0)
             for ch in range(n_qkv // COL_CHUNK)]
    cos_c, sin_c = _rope_tables(s_len, lc, DIFF_HD)
    (p1,) = _proj_call(xa2, mod1, g_norm_mix[1], w_qkv, cos_c, sin_c, rope1, DIFF_HD // 4, ncb)
    y1 = _diff_call(p1, diff_lambda[0], diff_subln_g[0], lc, diff_heads, lambda_init)
    x3, h2b, aff_tb = _outproj_call([y1], [w_out_odd[0].astype(BF16)], xa2, mod1, g_norm_ffn[1],
                                    w_router[1], ncb, s_len, ncb)
    segs1 = [(0, s_len, cap_lat, 0)]
    return _moe(h2b, aff_tb, x3, mod1, g_final, w_gate_up, w_down, 1, segs1, cap_lat, 0, 0, True)
```

```python
import functools

import numpy as np
import jax
import jax.numpy as jnp
from jax import lax
from jax.experimental import pallas as pl
from jax.experimental.pallas import tpu as pltpu

F32 = jnp.float32
BF16 = jnp.bfloat16

GRID_W = 64
ROPE_THETA = 10000.0
NORM_EPS = 1e-6
GLA_HEADS = 4
GLA_GATE_RANK = 16
GLA_GATE_NORM = 16.0
SWA_HEADS = 8
SWA_KV_HEADS = 2
SWA_GROUP = SWA_HEADS // SWA_KV_HEADS
SWA_WINDOW = 128
SWA_BLOCK = 128
DIFF_HD = 64
N_EXPERTS = 16
EC_FACTOR = 2

LANES = 128
ROW_TILE = 256
COL_CHUNK = 256
GLA_CHUNK = 128
GLA_LEVELS = 7
GLA_SAFE_RANGE = 60.0
DIFF_QBLOCK = 256
DIFF_SAFE_BOUND = 60.0
DIFF_BOUND_SLACK = 1.0 + 2.0 ** -6
VMEM_LIMIT = 56 * 1024 * 1024
NEG_BIG = -1e30


def _cparams(*sem):
    return pltpu.CompilerParams(dimension_semantics=sem, vmem_limit_bytes=VMEM_LIMIT)


def _pick(n, cands):
    for c in cands:
        if n % c == 0:
            return c
    raise ValueError(f"no tile for {n} in {cands}")


def _nt(a, b):
    return lax.dot_general(a, b, (((1,), (1,)), ((), ())), preferred_element_type=F32)


def _tn(a, b):
    return lax.dot_general(a, b, (((0,), (0,)), ((), ())), preferred_element_type=F32)


def _dot(a, b):
    return jnp.dot(a, b, preferred_element_type=F32)


def _split3(a):
    hi = a.astype(BF16)
    r1 = a - hi.astype(F32)
    mid = r1.astype(BF16)
    lo = (r1 - mid.astype(F32)).astype(BF16)
    return hi, mid, lo


def _split2(a):
    hi = a.astype(BF16)
    lo = (a - hi.astype(F32)).astype(BF16)
    return hi, lo


def _silu(x):
    return x * (1.0 / (1.0 + jnp.exp(-x)))


def _rms(x, eps=NORM_EPS):
    return x * lax.rsqrt(jnp.mean(x * x, axis=-1, keepdims=True) + eps)


def _row_specs(stream, n_ctx_blocks, off=0):
    if isinstance(stream, tuple):
        ctx, x = stream
        d = x.shape[2]
        return ([pl.BlockSpec((1, ROW_TILE, d), lambda bi, i: (bi, jnp.minimum(i + off, n_ctx_blocks - 1), 0)),
                 pl.BlockSpec((1, ROW_TILE, d), lambda bi, i: (bi, jnp.maximum(i + off - n_ctx_blocks, 0), 0))],
                [ctx, x])
    d = stream.shape[2]
    return [pl.BlockSpec((1, ROW_TILE, d), lambda bi, i: (bi, i + off, 0))], [stream]


def _load_rows(row_refs, n_ctx_blocks, off=0):
    if len(row_refs) == 2:
        return jnp.where(pl.program_id(1) + off < n_ctx_blocks, row_refs[0][0], row_refs[1][0])
    return row_refs[0][0]


def _ada_kernel(c_ref, w_ref, b_ref, o_ref):
    s = _silu(c_ref[...])
    s_hi, s_lo = _split2(s)
    w_hi, w_lo = _split2(w_ref[0])
    acc = _dot(s_hi, w_hi) + (_dot(s_lo, w_hi) + _dot(s_hi, w_lo))
    o_ref[0] = acc + b_ref[0]


def _ada_call(cc, w_ada, b_ada):
    depth, d, n = w_ada.shape
    r = cc.shape[0]
    tn = _pick(n, (768, 512, 256, 128))
    return pl.pallas_call(
        _ada_kernel,
        grid=(depth, n // tn),
        in_specs=[pl.BlockSpec((r, d), lambda l, j: (0, 0)),
                  pl.BlockSpec((1, d, tn), lambda l, j: (l, 0, j)),
                  pl.BlockSpec((1, 1, tn), lambda l, j: (l, 0, j))],
        out_specs=pl.BlockSpec((1, r, tn), lambda l, j: (l, 0, j)),
        out_shape=jax.ShapeDtypeStruct((depth, r, n), F32),
        compiler_params=_cparams("parallel", "parallel"),
        name="adaln",
    )(cc, w_ada, b_ada.reshape(depth, 1, n))


def _proj_kernel(*refs, n_row_refs, n_ctx_blocks, chunk_ops, quarter, has_r):
    row_refs = refs[:n_row_refs]
    mod_ref, g_ref, w_ref, cos_ref, sin_ref = refs[n_row_refs:n_row_refs + 5]
    rest = refs[n_row_refs + 5:]
    if has_r:
        wr_ref, o_ref, r_ref = rest
    else:
        (o_ref,) = rest
    x = _load_rows(row_refs, n_ctx_blocks)
    y = _rms(x) * g_ref[...]
    h = (y * (1.0 + mod_ref[0, 0, 1:2, :]) + mod_ref[0, 0, 0:1, :]).astype(BF16)
    reps = COL_CHUNK // LANES
    cos_t = jnp.concatenate([cos_ref[...]] * reps, axis=1)
    sin_t = jnp.concatenate([sin_ref[...]] * reps, axis=1)
    lane = lax.broadcasted_iota(jnp.int32, (1, COL_CHUNK), 1)
    first = (lane % (2 * quarter)) < quarter
    for c, (rope, scale) in enumerate(chunk_ops):
        acc = _dot(h, w_ref[:, c * COL_CHUNK:(c + 1) * COL_CHUNK])
        if rope:
            partner = jnp.where(first, pltpu.roll(acc, COL_CHUNK - quarter, 1), pltpu.roll(acc, quarter, 1))
            acc = acc * cos_t + partner * sin_t
        if scale != 1.0:
            acc = acc * scale
        o_ref[0, :, c * COL_CHUNK:(c + 1) * COL_CHUNK] = acc.astype(BF16)
    if has_r:
        r_ref[0] = _dot(h, wr_ref[...]).astype(BF16)


def _proj_call(stream, mod, g, w, cos_t, sin_t, chunk_ops, quarter, n_ctx_blocks, wr=None):
    d, n = w.shape
    t = cos_t.shape[0]
    b = mod.shape[0]
    assert n == len(chunk_ops) * COL_CHUNK and t % ROW_TILE == 0
    has_r = wr is not None
    row_specs, row_args = _row_specs(stream, n_ctx_blocks)
    in_specs = row_specs + [
        pl.BlockSpec((1, 1, 6, d), lambda bi, i: (bi, jnp.where(i < n_ctx_blocks, 0, 1), 0, 0)),
        pl.BlockSpec((1, d), lambda bi, i: (0, 0)),
        pl.BlockSpec((d, n), lambda bi, i: (0, 0), pipeline_mode=pl.Buffered(1)),
        pl.BlockSpec((ROW_TILE, LANES), lambda bi, i: (i, 0)),
        pl.BlockSpec((ROW_TILE, LANES), lambda bi, i: (i, 0))]
    args = row_args + [mod, g.reshape(1, d), w, cos_t, sin_t]
    out_specs = [pl.BlockSpec((1, ROW_TILE, n), lambda bi, i: (bi, i, 0))]
    out_shape = [jax.ShapeDtypeStruct((b, t, n), BF16)]
    if has_r:
        in_specs.append(pl.BlockSpec((d, LANES), lambda bi, i: (0, 0)))
        args.append(wr)
        out_specs.append(pl.BlockSpec((1, ROW_TILE, LANES), lambda bi, i: (bi, i, 0)))
        out_shape.append(jax.ShapeDtypeStruct((b, t, LANES), BF16))
    return pl.pallas_call(
        functools.partial(_proj_kernel, n_row_refs=len(row_args), n_ctx_blocks=n_ctx_blocks,
                          chunk_ops=tuple(chunk_ops), quarter=quarter, has_r=has_r),
        grid=(b, t // ROW_TILE),
        in_specs=in_specs, out_specs=out_specs, out_shape=out_shape,
        compiler_params=_cparams("parallel", "parallel"),
        name="norm_proj",
    )(*args)


def _gla_constants():
    c = GLA_CHUNK
    idx = np.arange(c)
    cm = np.zeros((2, GLA_LEVELS + 2, c, c), np.float32)
    lmask = np.zeros((2, GLA_LEVELS + 2, c, c), np.float32)
    cm[0, 0] = (idx[None, :] <= idx[:, None])
    cm[1, 0] = (idx[None, :] >= idx[:, None])
    for l in range(GLA_LEVELS):
        s = c >> (l + 1)
        blk = idx // (2 * s)
        second = (idx % (2 * s)) >= s
        same = blk[:, None] == blk[None, :]
        sep_f = blk * 2 * s + s - 1
        cm[0, 1 + l] = (idx[None, :] <= sep_f[:, None])
        lmask[0, l] = same & second[:, None] & (~second)[None, :]
        sep_b = blk * 2 * s + s
        cm[1, 1 + l] = (idx[None, :] >= sep_b[:, None])
        lmask[1, l] = same & (~second)[:, None] & second[None, :]
    cm[:, GLA_LEVELS + 1] = 1.0
    lmask[:, GLA_LEVELS] = np.eye(c)
    lmask[:, GLA_LEVELS + 1] = cm[:, 0]
    return jnp.asarray(cm.reshape(2, (GLA_LEVELS + 2) * c, c), BF16), jnp.asarray(lmask, F32)


def _gla_kernel(q_ref, k_ref, v_ref, ag_ref, r_ref, gw_ref, gb_ref, ng_ref, cm_ref, lm_ref,
                o_ref, la_ref, of_ref, ob_ref, qe_ref, u_ref, ss_ref, dec_ref, st_ref,
                *, n_ctx_chunks, n_chunks, q_scale):
    c = GLA_CHUNK
    dv = st_ref.shape[1]
    r = r_ref[0]
    for d in range(2):
        z = _dot(r, gw_ref[d]) + gb_ref[d]
        la_ref[d] = (jnp.minimum(z, 0.0) - jnp.log(1.0 + jnp.exp(-jnp.abs(z)))) * (1.0 / GLA_GATE_NORM)
    st_ref[...] = jnp.zeros_like(st_ref)
    out_refs = (of_ref, ob_ref)

    def chunk_rows(ci):
        return pl.ds(pl.multiple_of(ci * c, c), c)

    def state_rows(ci):
        return pl.ds(pl.multiple_of(ci * dv, dv), dv)

    min_tot = None
    for d in range(2):
        m = jnp.min(jnp.sum(la_ref[d].reshape(n_chunks, c, la_ref.shape[2]), axis=1))
        min_tot = m if min_tot is None else jnp.minimum(min_tot, m)
    is_safe = min_tot >= -GLA_SAFE_RANGE

    group = 3 if n_chunks % 3 == 0 else 2

    def prep_fast(cp, carry):
        jobs = [(group * cp + u, d) for u in range(group) for d in range(2)]
        rows = [chunk_rows(ci) for ci, _ in jobs]
        qs = {u: q_ref[0, chunk_rows(group * cp + u), :].astype(F32) * q_scale for u in range(group)}
        ks = {u: k_ref[0, chunk_rows(group * cp + u), :].astype(F32) for u in range(group)}
        bcs = {}
        for j, (ci, d) in enumerate(jobs):
            g_hi, g_lo = _split2(la_ref[d, rows[j], :])
            cm = cm_ref[d, 0:c, :]
            bcs[j] = _dot(cm, g_hi) + _dot(cm, g_lo)
        qes, kins, decs = {}, {}, {}
        for j, (ci, d) in enumerate(jobs):
            bc = bcs[j]
            qes[j] = (qs[j // 2] * jnp.exp(bc)).astype(BF16)
            kins[j] = ks[j // 2] * jnp.exp(-bc)
            decs[j] = jnp.exp(bc[c - 1:c] if d == 0 else bc[0:1])
        atts = {j: _nt(qes[j], kins[j].astype(BF16)) for j in range(len(jobs))}
        for j, (ci, d) in enumerate(jobs):
            att = (lm_ref[d, GLA_LEVELS + 1] * atts[j]).astype(BF16)
            out_refs[d][rows[j], :] = _dot(att, v_ref[0, rows[j], :])
            qe_ref[d, rows[j], :] = qes[j]
            u_ref[d, state_rows(ci), :] = _tn(v_ref[0, rows[j], :], (kins[j] * decs[j]).astype(BF16))
            dec_ref[d, pl.ds(pl.multiple_of(ci * 8, 8), 8), :] = jnp.broadcast_to(decs[j], (8, decs[j].shape[1]))
        return carry

    def prep(ci, carry):
        rows = chunk_rows(ci)
        q = q_ref[0, rows, :].astype(F32) * q_scale
        k = k_ref[0, rows, :].astype(F32)
        v = v_ref[0, rows, :]
        qk = _nt(q.astype(BF16), k.astype(BF16))
        for d in range(2):
            g_hi, g_lo = _split2(la_ref[d, rows, :])
            cm = cm_ref[d]
            cum = _dot(cm, g_hi) + _dot(cm, g_lo)
            bc = cum[0:c]
            tot = cum[(GLA_LEVELS + 1) * c:(GLA_LEVELS + 2) * c]
            att = lm_ref[d, GLA_LEVELS] * qk
            for l in range(GLA_LEVELS):
                ref_l = cum[(1 + l) * c:(2 + l) * c]
                eq = jnp.exp(jnp.minimum(bc - ref_l, 0.0))
                ek = jnp.exp(jnp.minimum(ref_l - bc, 0.0))
                att = att + lm_ref[d, l] * _nt((q * eq).astype(BF16), (k * ek).astype(BF16))
            out_refs[d][rows, :] = _dot(att.astype(BF16), v)
            qe_ref[d, rows, :] = (q * jnp.exp(bc)).astype(BF16)
            u_ref[d, state_rows(ci), :] = _tn(v, (k * jnp.exp(tot - bc)).astype(BF16))
            dec_ref[d, pl.ds(pl.multiple_of(ci * 8, 8), 8), :] = jnp.exp(tot[0:8])
        return carry

    @pl.when(is_safe)
    def _():
        lax.fori_loop(0, n_chunks // group, prep_fast, 0)

    @pl.when(jnp.logical_not(is_safe))
    def _():
        lax.fori_loop(0, n_chunks, prep, 0)

    def scan(t, carry):
        cb = jnp.where(t < n_ctx_chunks, n_ctx_chunks - 1 - t, n_chunks - 1 - (t - n_ctx_chunks))
        for d, ci in ((0, t), (1, cb)):
            st = st_ref[d]
            ss_ref[d, state_rows(ci), :] = st.astype(BF16)
            dec = dec_ref[d, pl.ds(pl.multiple_of(ci * 8, 8), 1), :]
            st_ref[d] = st * dec + u_ref[d, state_rows(ci), :]
        return carry

    lax.fori_loop(0, n_chunks, scan, 0)

    def inter(ci, carry):
        rows = chunk_rows(ci)
        for d in range(2):
            out_refs[d][rows, :] += _nt(qe_ref[d, rows, :], ss_ref[d, state_rows(ci), :])
        return carry

    lax.fori_loop(0, n_chunks, inter, 0, unroll=2 * group)
    o = of_ref[...] + ob_ref[...]
    ag = ag_ref[0].astype(F32)
    o_ref[0] = (_rms(o) * ng_ref[...] * _silu(ag)).astype(BF16)


def _gla_call(p0, r, gw_pad, gb, norm_g, consts, n_ctx_rows, col_q, col_k, col_v, col_g, dk, dv):
    b, t, _ = p0.shape
    n_chunks = t // GLA_CHUNK
    cm, lm = consts
    kern = functools.partial(_gla_kernel, n_ctx_chunks=n_ctx_rows // GLA_CHUNK, n_chunks=n_chunks,
                             q_scale=float(dk) ** -0.5)
    full = lambda shape: pl.BlockSpec(shape, lambda bi, h: (0,) * len(shape))
    return pl.pallas_call(
        kern,
        grid=(b, GLA_HEADS),
        in_specs=[pl.BlockSpec((1, t, dk), lambda bi, h: (bi, 0, col_q // dk + h)),
                  pl.BlockSpec((1, t, dk), lambda bi, h: (bi, 0, col_k // dk + h)),
                  pl.BlockSpec((1, t, dv), lambda bi, h: (bi, 0, col_v // dv + h)),
                  pl.BlockSpec((1, t, dv), lambda bi, h: (bi, 0, col_g // dv + h)),
                  pl.BlockSpec((1, t, LANES), lambda bi, h: (bi, 0, 0)),
                  pl.BlockSpec((2, LANES, dk), lambda bi, h: (0, 0, h)),
                  pl.BlockSpec((2, 1, dk), lambda bi, h: (0, 0, h)),
                  full((1, dv)), full(cm.shape), full(lm.shape)],
        out_specs=pl.BlockSpec((1, t, dv), lambda bi, h: (bi, 0, h)),
        out_shape=jax.ShapeDtypeStruct((b, t, GLA_HEADS * dv), BF16),
        scratch_shapes=[pltpu.VMEM((2, t, dk), F32), pltpu.VMEM((t, dv), F32), pltpu.VMEM((t, dv), F32),
                        pltpu.VMEM((2, t, dk), BF16), pltpu.VMEM((2, n_chunks * dv, dk), F32),
                        pltpu.VMEM((2, n_chunks * dv, dk), BF16),
                        pltpu.VMEM((2, n_chunks * 8, dk), F32), pltpu.VMEM((2, dv, dk), F32)],
        compiler_params=_cparams("parallel", "parallel"),
        name="gla",
    )(p0, p0, p0, p0, r, gw_pad, gb, norm_g.reshape(1, dv), cm, lm)


def _swa_kernel(q_ref, k_ref, v_ref, sink_ref, o_ref, s_ref, *, n_ctx_rows, n_blocks):
    blk = SWA_BLOCK
    win = 3 * blk
    ncb = n_ctx_rows // blk
    nlb = n_blocks - ncb
    sink = sink_ref[0, :, 0:1] * float(np.log2(np.e))
    rel0 = (lax.broadcasted_iota(jnp.int32, (blk, win), 1) - lax.broadcasted_iota(jnp.int32, (blk, win), 0))
    no_bias = jnp.zeros((blk, n_ctx_rows), F32)

    def key_rows(j):
        kb0 = jnp.clip(j - ncb - 1, 0, nlb - 3)
        return kb0, pl.ds(pl.multiple_of(n_ctx_rows + kb0 * blk, blk), win)

    def scores(j, slot):
        kb0, rows = key_rows(j)
        q = q_ref[0, pl.ds(pl.multiple_of(j * blk, blk), blk), :]
        q4 = jnp.concatenate([q[:, g * blk:(g + 1) * blk] for g in range(SWA_GROUP)], axis=0)
        keys = jnp.concatenate([k_ref[0, rows, :], k_ref[0, 0:n_ctx_rows, :]], axis=0)
        rel = rel0 + jnp.where(j < ncb, 4 * win, (kb0 - (j - ncb)) * blk)
        bias = jnp.concatenate([jnp.where(jnp.abs(rel) <= SWA_WINDOW, 0.0, NEG_BIG), no_bias], axis=1)
        s_ref[slot] = _nt(q4, keys) + jnp.concatenate([bias] * SWA_GROUP, axis=0)

    def finish(j, slot):
        _, rows = key_rows(j)
        vals = jnp.concatenate([v_ref[0, rows, :], v_ref[0, 0:n_ctx_rows, :]], axis=0)
        s = s_ref[slot]
        m = jnp.maximum(jnp.max(s, axis=1, keepdims=True), sink)
        e = jnp.exp2(s - m)
        den = jnp.sum(e, axis=1, keepdims=True) + jnp.exp2(sink - m)
        o = _dot(e.astype(BF16), vals) / den
        o_ref[0, pl.ds(pl.multiple_of(j * blk, blk), blk), :] = jnp.concatenate(
            [o[g * blk:(g + 1) * blk] for g in range(SWA_GROUP)], axis=1).astype(BF16)

    scores(0, 0)

    def body(jj, carry):
        j0 = 2 * jj
        scores(j0 + 1, 1)
        finish(j0, 0)
        scores(j0 + 2, 0)
        finish(j0 + 1, 1)
        return carry

    lax.fori_loop(0, n_blocks // 2 - 1, body, 0)
    scores(n_blocks - 1, 1)
    finish(n_blocks - 2, 0)
    finish(n_blocks - 1, 1)


def _swa_call(p0, sink_col, n_ctx_rows, col_q, col_k, col_v, hd):
    b, t, _ = p0.shape
    blk = SWA_BLOCK
    nb = t // blk
    gq = SWA_GROUP * hd
    assert nb % 2 == 0 and nb - n_ctx_rows // blk >= 3 and SWA_WINDOW <= blk
    return pl.pallas_call(
        functools.partial(_swa_kernel, n_ctx_rows=n_ctx_rows, n_blocks=nb),
        grid=(b, SWA_KV_HEADS),
        in_specs=[pl.BlockSpec((1, t, gq), lambda bi, h: (bi, 0, col_q // gq + h)),
                  pl.BlockSpec((1, t, hd), lambda bi, h: (bi, 0, col_k // hd + h)),
                  pl.BlockSpec((1, t, hd), lambda bi, h: (bi, 0, col_v // hd + h)),
                  pl.BlockSpec((1, SWA_GROUP * blk, LANES), lambda bi, h: (h, 0, 0))],
        out_specs=pl.BlockSpec((1, t, gq), lambda bi, h: (bi, 0, h)),
        out_shape=jax.ShapeDtypeStruct((b, t, SWA_HEADS * hd), BF16),
        scratch_shapes=[pltpu.VMEM((2, SWA_GROUP * blk, 3 * blk + n_ctx_rows), F32)],
        compiler_params=_cparams("parallel", "parallel"),
        name="swa",
    )(p0, p0, p0, sink_col)


def _diff_kernel(lam_ref, q_ref, k_ref, v_ref, g_ref, o_ref, s_ref, kx_ref, vx_ref, qn_ref,
                 *, row_off, n_qblocks, lambda_init):
    tq = DIFF_QBLOCK
    hd2 = 2 * DIFF_HD
    n_keys = k_ref.shape[1]
    lam = lam_ref[...]
    s01 = jnp.sum(lam[0:1] * lam[1:2], axis=1, keepdims=True)
    s23 = jnp.sum(lam[2:3] * lam[3:4], axis=1, keepdims=True)
    lam_full = jnp.exp(s01) - jnp.exp(s23) + lambda_init
    lane = lax.broadcasted_iota(jnp.int32, (1, hd2), 1)
    first = lane < DIFF_HD
    col0 = lane == 0
    zero = jnp.zeros((), BF16)

    d_idx = lax.broadcasted_iota(jnp.int32, (hd2, 2 * hd2), 0)
    j_idx = lax.broadcasted_iota(jnp.int32, (hd2, 2 * hd2), 1)
    map_sum = jnp.where((d_idx < DIFF_HD) == (j_idx < hd2), 1.0, 0.0).astype(BF16)

    def sq_norms(x):
        xf = x.astype(F32)
        return _dot((xf * xf).astype(BF16), map_sum)

    kn = sq_norms(k_ref[0])
    kmax0 = jnp.sqrt(jnp.max(kn[:, 0:hd2], axis=0, keepdims=True)) * DIFF_BOUND_SLACK
    kmax1 = jnp.sqrt(jnp.max(kn[:, hd2:2 * hd2], axis=0, keepdims=True)) * DIFF_BOUND_SLACK
    qn_ref[...] = sq_norms(q_ref[0, row_off:, :])
    qmax0 = jnp.sqrt(jnp.max(qn_ref[:, 0:hd2], axis=0, keepdims=True))
    qmax1 = jnp.sqrt(jnp.max(qn_ref[:, hd2:2 * hd2], axis=0, keepdims=True))
    is_safe = jnp.max(jnp.maximum(qmax0 * kmax0, qmax1 * kmax1)) <= DIFF_SAFE_BOUND

    def q_block(j):
        q = q_ref[0, pl.ds(pl.multiple_of(row_off + j * tq, tq), tq), :]
        return q, jnp.where(first, q, zero), jnp.where(first, zero, q)

    def write_out(j, o):
        y = _rms(o) * g_ref[...] * (1.0 - lambda_init)
        o_ref[0, pl.ds(pl.multiple_of(j * tq, tq), tq), :] = y.astype(BF16)

    def scores_fast(j, slot):
        _, qa, qb = q_block(j)
        qn = jnp.sqrt(qn_ref[pl.ds(pl.multiple_of(j * tq, tq), tq), :])
        xa = jnp.where(col0, -(qn[:, 0:hd2] * kmax0), 0.0).astype(BF16)
        xb = jnp.where(col0, -(qn[:, hd2:2 * hd2] * kmax1), 0.0).astype(BF16)
        qq = jnp.concatenate([jnp.concatenate([qa, xa], axis=1), jnp.concatenate([qb, xb], axis=1)], axis=0)
        s_ref[slot] = _nt(qq, kx_ref[...])

    def finish_fast(j, slot):
        e = jnp.exp2(s_ref[slot]).astype(BF16)
        ov = _dot(e, vx_ref[...])
        p0 = ov[0:tq, 0:hd2] / ov[0:tq, hd2:2 * hd2]
        p1 = ov[tq:2 * tq, 0:hd2] / ov[tq:2 * tq, hd2:2 * hd2]
        write_out(j, p0 - lam_full * p1)

    def scores_max(j, slot):
        _, qa, qb = q_block(j)
        s_ref[slot] = _nt(jnp.concatenate([qa, qb], axis=0), k_ref[0])

    def finish_max(j, slot):
        s = s_ref[slot]
        e = jnp.exp2(s - jnp.max(s, axis=1, keepdims=True))
        z = jnp.sum(e, axis=1, keepdims=True)
        coef = lam_full * z[0:tq] / z[tq:2 * tq]
        a = e[0:tq] - coef * e[tq:2 * tq]
        write_out(j, _dot(a.astype(BF16), v_ref[0]) * (1.0 / z[0:tq]))

    def run(scores, finish):
        scores(0, 0)

        def body(jj, carry):
            j0 = 2 * jj
            scores(j0 + 1, 1)
            finish(j0, 0)
            scores(j0 + 2, 0)
            finish(j0 + 1, 1)
            return carry

        lax.fori_loop(0, n_qblocks // 2 - 1, body, 0)
        scores(n_qblocks - 1, 1)
        finish(n_qblocks - 2, 0)
        finish(n_qblocks - 1, 1)

    @pl.when(is_safe)
    def _():
        kx_ref[:, 0:hd2] = k_ref[0]
        kx_ref[:, hd2:2 * hd2] = jnp.broadcast_to(jnp.where(col0, 1.0, 0.0).astype(BF16), (n_keys, hd2))
        vx_ref[:, 0:hd2] = v_ref[0]
        vx_ref[:, hd2:2 * hd2] = jnp.ones((n_keys, hd2), BF16)
        run(scores_fast, finish_fast)

    @pl.when(jnp.logical_not(is_safe))
    def _():
        run(scores_max, finish_max)


def _diff_call(p1, lam, subln_g, n_ctx_rows, n_heads, lambda_init):
    b, t, _ = p1.shape
    s_len = t - n_ctx_rows
    hd2 = 2 * DIFF_HD
    n_qblocks = s_len // DIFF_QBLOCK
    assert n_qblocks >= 2 and n_qblocks % 2 == 0
    return pl.pallas_call(
        functools.partial(_diff_kernel, row_off=n_ctx_rows, n_qblocks=n_qblocks, lambda_init=lambda_init),
        grid=(b, n_heads),
        in_specs=[pl.BlockSpec((4, DIFF_HD), lambda bi, h: (0, 0)),
                  pl.BlockSpec((1, t, hd2), lambda bi, h: (bi, 0, h)),
                  pl.BlockSpec((1, t, hd2), lambda bi, h: (bi, 0, n_heads + h)),
                  pl.BlockSpec((1, t, hd2), lambda bi, h: (bi, 0, 2 * n_heads + h)),
                  pl.BlockSpec((1, hd2), lambda bi, h: (0, 0))],
        out_specs=pl.BlockSpec((1, s_len, hd2), lambda bi, h: (bi, 0, h)),
        out_shape=jax.ShapeDtypeStruct((b, s_len, n_heads * hd2), BF16),
        scratch_shapes=[pltpu.VMEM((2, 2 * DIFF_QBLOCK, t), F32), pltpu.VMEM((t, 2 * hd2), BF16),
                        pltpu.VMEM((t, 2 * hd2), BF16), pltpu.VMEM((s_len, 2 * hd2), F32)],
        compiler_params=_cparams("parallel", "parallel"),
        name="diff_attn",
    )(lam, p1, p1, p1, subln_g.reshape(1, hd2))


def _outproj_kernel(*refs, n_y, n_row_refs, n_ctx_blocks, row_block_off):
    y_refs = refs[:n_y]
    w_refs = refs[n_y:2 * n_y]
    row_refs = refs[2 * n_y:2 * n_y + n_row_refs]
    mod_ref, g_ref, wr_ref, xn_ref, h_ref, aff_ref = refs[2 * n_y + n_row_refs:]
    n_exp = aff_ref.shape[1]
    acc = _dot(y_refs[0][0], w_refs[0][...])
    for yr, wr in zip(y_refs[1:], w_refs[1:]):
        acc = acc + _dot(yr[0], wr[...])
    xn = _load_rows(row_refs, n_ctx_blocks, row_block_off) + mod_ref[0, 0, 2:3, :] * acc
    xn_ref[0] = xn
    h = _rms(xn) * g_ref[...] * (1.0 + mod_ref[0, 0, 4:5, :]) + mod_ref[0, 0, 3:4, :]
    h_bf = h.astype(BF16)
    h_ref[0] = h_bf
    logits = _dot(h_bf, wr_ref[...])
    logits = jnp.where(lax.broadcasted_iota(jnp.int32, (1, LANES), 1) < n_exp, logits, NEG_BIG)
    ex = jnp.exp(logits - jnp.max(logits, axis=1, keepdims=True))
    aff = ex / jnp.sum(ex, axis=1, keepdims=True)
    aff_ref[0] = jnp.transpose(aff)[0:n_exp]


def _outproj_call(ys, ws, stream, mod, g, w_router, row_block_off, n_rows, n_ctx_blocks):
    b, d = mod.shape[0], mod.shape[3]
    n_y = len(ys)
    e = w_router.shape[1]
    w_router_pad = jnp.concatenate([w_router, jnp.zeros((d, LANES - e), F32)], axis=1).astype(BF16)
    row_specs, row_args = _row_specs(stream, n_ctx_blocks, row_block_off)
    in_specs = ([pl.BlockSpec((1, ROW_TILE, y.shape[2]), lambda bi, i: (bi, i, 0)) for y in ys]
                + [pl.BlockSpec(w.shape, lambda bi, i: (0, 0), pipeline_mode=pl.Buffered(1)) for w in ws]
                + row_specs
                + [pl.BlockSpec((1, 1, 6, d),
                                lambda bi, i: (bi, jnp.where(i + row_block_off < n_ctx_blocks, 0, 1), 0, 0)),
                   pl.BlockSpec((1, d), lambda bi, i: (0, 0)),
                   pl.BlockSpec((d, LANES), lambda bi, i: (0, 0))])
    return pl.pallas_call(
        functools.partial(_outproj_kernel, n_y=n_y, n_row_refs=len(row_args), n_ctx_blocks=n_ctx_blocks,
                          row_block_off=row_block_off),
        grid=(b, n_rows // ROW_TILE),
        in_specs=in_specs,
        out_specs=[pl.BlockSpec((1, ROW_TILE, d), lambda bi, i: (bi, i, 0)),
                   pl.BlockSpec((1, ROW_TILE, d), lambda bi, i: (bi, i, 0)),
                   pl.BlockSpec((1, e, ROW_TILE), lambda bi, i: (bi, 0, i))],
        out_shape=[jax.ShapeDtypeStruct((b, n_rows, d), F32),
                   jax.ShapeDtypeStruct((b, n_rows, d), BF16),
                   jax.ShapeDtypeStruct((b, e, n_rows), F32)],
        compiler_params=_cparams("parallel", "parallel"),
        name="out_proj",
    )(*ys, *ws, *row_args, mod, g.reshape(1, d), w_router_pad)


def _prefix_lanes(m, upper):
    e, n = m.shape
    carry = jnp.zeros((e, 1), F32)
    outs = []
    for blk in range(n // LANES):
        mb = m[:, blk * LANES:(blk + 1) * LANES]
        inc = _dot(mb.astype(BF16), upper)
        outs.append(inc - mb + carry)
        carry = carry + jnp.sum(mb, axis=1, keepdims=True)
    return jnp.concatenate(outs, axis=1)


def _transpose_exact(x, eye):
    hi, mid, lo = _split3(x)
    return _nt(eye, hi) + (_nt(eye, mid) + _nt(eye, lo))


def _route_kernel(aff_ref, pos_ref, gate_ref, posn_ref, lo_ref, hi_ref, *, segs):
    n_exp = aff_ref.shape[1]
    lo_acc = jnp.zeros((n_exp, LANES), F32)
    hi_acc = jnp.zeros((n_exp, LANES), F32)
    li = lax.broadcasted_iota(jnp.int32, (LANES, LANES), 0)
    lj = lax.broadcasted_iota(jnp.int32, (LANES, LANES), 1)
    upper = jnp.where(li <= lj, 1.0, 0.0).astype(BF16)
    eye = jnp.where(li == lj, 1.0, 0.0).astype(BF16)
    all_bits = [pltpu.bitcast(aff_ref[0, :, off:off + n], jnp.int32) for (off, n, _, _) in segs]

    def search(it, thrs):
        out = []
        for bits, (_, _, cap, _), thr in zip(all_bits, segs, thrs):
            cand = thr | jnp.left_shift(jnp.int32(1), 30 - it)
            cnt = jnp.sum(jnp.where(bits >= cand, 1.0, 0.0), axis=1, keepdims=True)
            out.append(jnp.where(cnt >= cap, cand, thr))
        return tuple(out)

    all_thr = lax.fori_loop(0, 31, search, tuple(jnp.zeros((n_exp, 1), jnp.int32) for _ in segs))
    for (off, n, cap, slot_off), bits, thr in zip(segs, all_bits, all_thr):
        a = aff_ref[0, :, off:off + n]
        gt = jnp.where(bits > thr, 1.0, 0.0)
        eq = jnp.where(bits == thr, 1.0, 0.0)
        need = cap - jnp.sum(gt, axis=1, keepdims=True)
        sel = gt + eq * jnp.where(_prefix_lanes(eq, upper) < need, 1.0, 0.0)
        slot = _prefix_lanes(sel, upper)
        pos = jnp.where(sel > 0.0, slot + slot_off, -1.0)
        pos_ref[0, :, off:off + n] = pos.astype(jnp.int32)
        gate_ref[0, :, off:off + n] = sel * a
        for blk in range(n // LANES):
            cols = slice(blk * LANES, (blk + 1) * LANES)
            rows = slice(off + blk * LANES, off + (blk + 1) * LANES)
            posn_ref[0, rows, :] = _transpose_exact(pos[:, cols], eye).astype(jnp.int32)
        assert cap <= 256 and (off + n) // LANES <= LANES
        tok = lax.broadcasted_iota(jnp.int32, (n, LANES), 0)
        blk_start = (lax.broadcasted_iota(jnp.int32, (n, LANES), 1) - off // LANES) * LANES
        lo_acc = lo_acc + _dot(slot.astype(BF16), jnp.where(tok == blk_start, 1.0, 0.0).astype(BF16))
        hi_acc = hi_acc + _dot((slot + sel).astype(BF16),
                               jnp.where(tok == blk_start + (LANES - 1), 1.0, 0.0).astype(BF16))
    lo_ref[0] = lo_acc.astype(jnp.int32)
    hi_ref[0] = hi_acc.astype(jnp.int32)


def _route_call(aff_t, segs):
    b, e, t = aff_t.shape
    return pl.pallas_call(
        functools.partial(_route_kernel, segs=tuple(segs)),
        grid=(b,),
        in_specs=[pl.BlockSpec((1, e, t), lambda bi: (bi, 0, 0))],
        out_specs=[pl.BlockSpec((1, e, t), lambda bi: (bi, 0, 0)),
                   pl.BlockSpec((1, e, t), lambda bi: (bi, 0, 0)),
                   pl.BlockSpec((1, t, e), lambda bi: (bi, 0, 0)),
                   pl.BlockSpec((1, e, LANES), lambda bi: (bi, 0, 0)),
                   pl.BlockSpec((1, e, LANES), lambda bi: (bi, 0, 0))],
        out_shape=[jax.ShapeDtypeStruct((b, e, t), jnp.int32),
                   jax.ShapeDtypeStruct((b, e, t), F32),
                   jax.ShapeDtypeStruct((b, t, e), jnp.int32),
                   jax.ShapeDtypeStruct((b, e, LANES), jnp.int32),
                   jax.ShapeDtypeStruct((b, e, LANES), jnp.int32)],
        compiler_params=_cparams("parallel"),
        name="route",
    )(aff_t)


def _gather_kernel(lo_ref, hi_ref, pos_ref, gate_ref, h_ref, o_ref, g_ref,
                   *, n_ctx_rows, n_lat_rows, cap_lat, sblk, win, margin):
    sb = pl.program_id(1)
    n_exp = pos_ref.shape[1]
    n_lat_blocks = cap_lat // sblk
    tok_per_blk = n_lat_rows // n_lat_blocks

    def emit(slot0, tok0, n_tok):
        rows = pl.ds(tok0, n_tok)
        slots = lax.broadcasted_iota(jnp.int32, (sblk, 1), 0) + slot0
        hits = [pos_ref[0, e:e + 1, rows] == slots for e in range(n_exp)]
        onehot = jnp.concatenate([jnp.where(h, 1.0, 0.0).astype(BF16) for h in hits], axis=0)
        res = _dot(onehot, h_ref[0, rows, :])
        for e in range(n_exp):
            o_ref[e, 0] = res[e * sblk:(e + 1) * sblk].astype(BF16)
            gsel = jnp.sum(jnp.where(hits[e], gate_ref[0, e:e + 1, rows], 0.0), axis=1, keepdims=True)
            g_ref[e, 0] = jnp.broadcast_to(gsel, (sblk, LANES))

    slot0 = sb * sblk
    w0 = jnp.clip(sb * tok_per_blk - margin, 0, n_lat_rows - win)
    tok0 = pl.multiple_of(n_ctx_rows + w0, LANES)
    first_blk = (n_ctx_rows + w0) // LANES
    fits = None
    for e in range(n_exp):
        ok = jnp.logical_and(lo_ref[0, e, first_blk] <= slot0,
                             hi_ref[0, e, first_blk + win // LANES - 1] >= slot0 + sblk)
        fits = ok if fits is None else jnp.logical_and(fits, ok)
    is_lat = sb < n_lat_blocks

    @pl.when(jnp.logical_and(is_lat, fits))
    def _():
        emit(slot0, tok0, win)

    @pl.when(jnp.logical_and(is_lat, jnp.logical_not(fits)))
    def _():
        emit(slot0, n_ctx_rows, n_lat_rows)

    if n_ctx_rows:
        @pl.when(jnp.logical_not(is_lat))
        def _():
            emit(cap_lat, 0, n_ctx_rows)


def _slot_window(tt, cap_lat, n_exp):
    return min(cap_lat, max(16, 2 * tt * EC_FACTOR // n_exp))


def _gather_call(lo, hi, pos, gate, h, cap_lat, cap_ctx, n_ctx_rows):
    b, e, t = pos.shape
    d = h.shape[2]
    n_slots = cap_lat + cap_ctx
    n_lat_rows = t - n_ctx_rows
    sblk = cap_ctx if cap_ctx else 32
    assert cap_lat % sblk == 0 and sblk % 16 == 0
    n_lat_blocks = cap_lat // sblk
    tok_per_blk = n_lat_rows // n_lat_blocks
    margin = tok_per_blk
    win = min(n_lat_rows, 3 * tok_per_blk)
    assert tok_per_blk % LANES == 0 and win % LANES == 0
    smem = lambda: pl.BlockSpec((1, e, LANES), lambda bi, si: (bi, 0, 0), memory_space=pltpu.SMEM)
    return pl.pallas_call(
        functools.partial(_gather_kernel, n_ctx_rows=n_ctx_rows, n_lat_rows=n_lat_rows, cap_lat=cap_lat,
                          sblk=sblk, win=win, margin=margin),
        grid=(b, n_slots // sblk),
        in_specs=[smem(), smem(),
                  pl.BlockSpec((1, e, t), lambda bi, si: (bi, 0, 0)),
                  pl.BlockSpec((1, e, t), lambda bi, si: (bi, 0, 0)),
                  pl.BlockSpec((1, t, d), lambda bi, si: (bi, 0, 0))],
        out_specs=[pl.BlockSpec((e, 1, sblk, d), lambda bi, si: (0, bi, si, 0)),
                   pl.BlockSpec((e, 1, sblk, LANES), lambda bi, si: (0, bi, si, 0))],
        out_shape=[jax.ShapeDtypeStruct((e, b, n_slots, d), BF16),
                   jax.ShapeDtypeStruct((e, b, n_slots, LANES), F32)],
        compiler_params=_cparams("parallel", "arbitrary"),
        name="gather",
    )(lo, hi, pos, gate, h)


def _ffn_up_kernel(x_ref, wg_ref, wu_ref, o_ref, wgb_ref, wub_ref):
    @pl.when(pl.program_id(2) == 0)
    def _():
        wgb_ref[...] = wg_ref[0, 0].astype(BF16)
        wub_ref[...] = wu_ref[0, 0].astype(BF16)

    x = x_ref[0]
    g = _dot(x, wgb_ref[...])
    u = _dot(x, wub_ref[...])
    o_ref[0] = (_silu(g) * u).astype(BF16)


def _ffn_down_kernel(a_ref, w_ref, gate_ref, o_ref, wb_ref):
    @pl.when(pl.program_id(2) == 0)
    def _():
        wb_ref[...] = w_ref[0, 0].astype(BF16)

    gate = jnp.concatenate([gate_ref[0]] * (o_ref.shape[2] // LANES), axis=1)
    o_ref[0] = (_dot(a_ref[0], wb_ref[...]) * gate).astype(BF16)


def _ffn_call(xs, gates, w_gu, w_dn, layer):
    e, m, d = xs.shape
    f = w_dn.shape[2]
    tm = _pick(m, (1152, 1024, 768, 512, 256, 128, 64, 32, 16, 8))
    tf = _pick(f, (512, 256, 128))
    nf = f // tf
    act = pl.pallas_call(
        _ffn_up_kernel,
        grid=(e, nf, m // tm),
        in_specs=[pl.BlockSpec((1, tm, d), lambda ei, j, i: (ei, i, 0)),
                  pl.BlockSpec((1, 1, d, tf), lambda ei, j, i: (layer, ei, 0, j)),
                  pl.BlockSpec((1, 1, d, tf), lambda ei, j, i: (layer, ei, 0, j + nf))],
        out_specs=pl.BlockSpec((1, tm, tf), lambda ei, j, i: (ei, i, j)),
        out_shape=jax.ShapeDtypeStruct((e, m, f), BF16),
        scratch_shapes=[pltpu.VMEM((d, tf), BF16), pltpu.VMEM((d, tf), BF16)],
        compiler_params=_cparams("parallel", "arbitrary", "arbitrary"),
        name="ffn_up",
    )(xs, w_gu, w_gu)
    tn = _pick(d, (1024, 512, 256, 128))
    return pl.pallas_call(
        _ffn_down_kernel,
        grid=(e, d // tn, m // tm),
        in_specs=[pl.BlockSpec((1, tm, f), lambda ei, j, i: (ei, i, 0)),
                  pl.BlockSpec((1, 1, f, tn), lambda ei, j, i: (layer, ei, 0, j)),
                  pl.BlockSpec((1, tm, LANES), lambda ei, j, i: (ei, i, 0))],
        out_specs=pl.BlockSpec((1, tm, tn), lambda ei, j, i: (ei, i, j)),
        out_shape=jax.ShapeDtypeStruct((e, m, d), BF16),
        scratch_shapes=[pltpu.VMEM((f, tn), BF16)],
        compiler_params=_cparams("parallel", "arbitrary", "arbitrary"),
        name="ffn_down",
    )(act, w_dn, gates)


def _combine_kernel(lo_ref, hi_ref, posn_ref, y_ref, x_ref, mod_ref, gf_ref, o_ref,
                    *, cap_lat, cap_ctx, n_ctx_tiles, blocks_per_tile, window, final_norm):
    n_exp = posn_ref.shape[2]
    i = pl.program_id(1)
    lane = lax.broadcasted_iota(jnp.int32, (1, n_exp), 1)
    posn = posn_ref[0].astype(F32)

    def pcol(e):
        return jnp.sum(jnp.where(lane == e, posn, 0.0), axis=1, keepdims=True)

    def scatter(slot0, cap):
        slots = lax.broadcasted_iota(jnp.int32, (1, cap), 1).astype(F32) + slot0
        hots = [jnp.where(pcol(e) == slots, 1.0, 0.0).astype(BF16) for e in range(n_exp)]
        rows = [y_ref[e, 0, slot0:slot0 + cap, :] for e in range(n_exp)]
        return _dot(jnp.concatenate(hots, axis=1), jnp.concatenate(rows, axis=0))

    def scatter_window(starts):
        w_tot = n_exp * window
        grp = lax.broadcasted_iota(jnp.int32, (n_exp, w_tot), 1) // window
        spread = jnp.where(grp == lax.broadcasted_iota(jnp.int32, (n_exp, w_tot), 0), 1.0, 0.0).astype(BF16)
        bpos = _dot(posn.astype(BF16), spread)
        wlane = lax.broadcasted_iota(jnp.int32, (1, w_tot), 1)
        target = (wlane % window).astype(F32)
        for e, st in enumerate(starts):
            target = jnp.where(wlane // window == e, target + st.astype(F32), target)
        onehot = jnp.where(bpos == target, 1.0, 0.0).astype(BF16)
        rows = [y_ref[e, 0, pl.ds(pl.multiple_of(st, 16), window), :] for e, st in enumerate(starts)]
        return _dot(onehot, jnp.concatenate(rows, axis=0))

    def finish(total, g2):
        out = x_ref[0] + g2 * total
        if final_norm:
            out = _rms(out) * gf_ref[...]
        o_ref[0] = out

    blk0 = i * blocks_per_tile
    starts = []
    fits = None
    for e in range(n_exp):
        lo = lo_ref[0, e, blk0]
        hi = hi_ref[0, e, blk0 + blocks_per_tile - 1]
        st = jnp.minimum(lax.shift_left(lax.shift_right_logical(lo, 4), 4), cap_lat - window)
        ok = hi - st <= window
        fits = ok if fits is None else jnp.logical_and(fits, ok)
        starts.append(st)
    is_lat = i >= n_ctx_tiles

    @pl.when(jnp.logical_and(is_lat, fits))
    def _():
        finish(scatter_window(starts), mod_ref[0, 1, 5:6, :])

    @pl.when(jnp.logical_and(is_lat, jnp.logical_not(fits)))
    def _():
        finish(scatter(0, cap_lat), mod_ref[0, 1, 5:6, :])

    if n_ctx_tiles:
        @pl.when(jnp.logical_not(is_lat))
        def _():
            finish(scatter(cap_lat, cap_ctx), mod_ref[0, 0, 5:6, :])


def _combine_call(lo, hi, posn, y, xa, mod, g_final, cap_lat, cap_ctx, n_ctx_rows, final_norm):
    b, t, d = xa.shape
    e = posn.shape[2]
    n_slots = y.shape[2]
    tt = ROW_TILE
    assert n_ctx_rows % tt == 0 and t % tt == 0
    window = _slot_window(tt, cap_lat, e)
    smem = lambda: pl.BlockSpec((1, e, LANES), lambda bi, i: (bi, 0, 0), memory_space=pltpu.SMEM)
    return pl.pallas_call(
        functools.partial(_combine_kernel, cap_lat=cap_lat, cap_ctx=cap_ctx, n_ctx_tiles=n_ctx_rows // tt,
                          blocks_per_tile=tt // LANES, window=window, final_norm=final_norm),
        grid=(b, t // tt),
        in_specs=[smem(), smem(),
                  pl.BlockSpec((1, tt, e), lambda bi, i: (bi, i, 0)),
                  pl.BlockSpec((e, 1, n_slots, d), lambda bi, i: (0, bi, 0, 0)),
                  pl.BlockSpec((1, tt, d), lambda bi, i: (bi, i, 0)),
                  pl.BlockSpec((1, 2, 6, d), lambda bi, i: (bi, 0, 0, 0)),
                  pl.BlockSpec((1, d), lambda bi, i: (0, 0))],
        out_specs=pl.BlockSpec((1, tt, d), lambda bi, i: (bi, i, 0)),
        out_shape=jax.ShapeDtypeStruct((b, t, d), F32),
        compiler_params=_cparams("parallel", "arbitrary"),
        name="combine",
    )(lo, hi, posn, y, xa, mod, g_final.reshape(1, d))


def _rope_tables(n_tokens, n_ctx_rows, head_dim):
    rows = n_tokens // GRID_W
    r = jnp.repeat(jnp.arange(rows, dtype=F32), GRID_W)
    col = jnp.tile(jnp.arange(GRID_W, dtype=F32), rows)
    axis_dim = head_dim // 2
    inv = ROPE_THETA ** (-jnp.arange(0, axis_dim, 2, dtype=F32) / axis_dim)
    ar, ac = r[:, None] * inv, col[:, None] * inv
    cos_p = jnp.concatenate([jnp.cos(ar), jnp.cos(ar), jnp.cos(ac), jnp.cos(ac)], axis=-1)
    sin_p = jnp.concatenate([-jnp.sin(ar), jnp.sin(ar), -jnp.sin(ac), jnp.sin(ac)], axis=-1)
    reps = LANES // head_dim
    cos_p, sin_p = jnp.tile(cos_p, (1, reps)), jnp.tile(sin_p, (1, reps))
    cos_t = jnp.concatenate([jnp.ones((n_ctx_rows, LANES), F32), cos_p], axis=0)
    sin_t = jnp.concatenate([jnp.zeros((n_ctx_rows, LANES), F32), sin_p], axis=0)
    return cos_t, sin_t


def _moe(h2, aff_t, x_res, mod, g_final, w_gu, w_dn, layer, segs, cap_lat, cap_ctx, n_ctx_rows, final_norm):
    b, t, d = h2.shape
    e = aff_t.shape[1]
    n_slots = cap_lat + cap_ctx
    pos, gate, posn, lo, hi = _route_call(aff_t, segs)
    xs, gs = _gather_call(lo, hi, pos, gate, h2, cap_lat, cap_ctx, n_ctx_rows)
    y = _ffn_call(xs.reshape(e, b * n_slots, d), gs.reshape(e, b * n_slots, LANES), w_gu, w_dn, layer)
    return _combine_call(lo, hi, posn, y.reshape(e, b, n_slots, d), x_res, mod, g_final, cap_lat, cap_ctx,
                         n_ctx_rows, final_norm)


def kernel(x, c, ctx, c_ctx, w_ada, b_ada, g_norm_mix, g_norm_ffn, w_in_even, gla_gate_w, gla_gate_b, gla_norm_g, swa_sink, w_out_even, w_qkv_odd, diff_lambda, diff_subln_g, w_out_odd, w_router, w_gate_up, w_down, g_final):
    b, s_len, d = x.shape
    lc = ctx.shape[1]
    depth = w_ada.shape[0]
    assert depth == 2 and lc % ROW_TILE == 0 and s_len % ROW_TILE == 0 and s_len % GRID_W == 0
    t = lc + s_len
    ncb = lc // ROW_TILE
    gla_dk, gla_dv = d // 4 // GLA_HEADS, d // 2 // GLA_HEADS
    swa_hd = d // 2 // SWA_HEADS
    diff_heads = d // (2 * DIFF_HD)
    n_exp = w_router.shape[2]
    assert gla_dk == LANES and swa_hd == LANES and 2 * DIFF_HD == LANES

    n_rows = -(-(b + 1) // 8) * 8
    cc = jnp.concatenate([c, c_ctx[None, :], jnp.zeros((n_rows - b - 1, d), F32)], axis=0)
    m_all = _ada_call(cc, w_ada, b_ada).reshape(depth, n_rows, 6, d)

    def mod_for(layer):
        lat = m_all[layer, :b]
        ctxm = jnp.broadcast_to(m_all[layer, b][None], (b, 6, d))
        return jnp.stack([ctxm, lat], axis=1)

    xa = (ctx, x)

    mod0 = mod_for(0)
    w_in = w_in_even[0]
    sizes = (GLA_HEADS * gla_dk, GLA_HEADS * gla_dk, GLA_HEADS * gla_dv, GLA_HEADS * gla_dv,
             GLA_GATE_RANK, GLA_GATE_RANK, SWA_HEADS * swa_hd, SWA_KV_HEADS * swa_hd, SWA_KV_HEADS * swa_hd)
    offs = np.concatenate([[0], np.cumsum(sizes)])
    seg = lambda i: w_in[:, offs[i]:offs[i + 1]]
    w_main = jnp.concatenate([seg(0), seg(1), seg(2), seg(3), seg(6), seg(7), seg(8)], axis=1).astype(BF16)
    w_rank = jnp.concatenate([seg(4), seg(5), jnp.zeros((d, LANES - 2 * GLA_GATE_RANK), F32)],
                             axis=1).astype(BF16)
    col_aq, col_ak = 0, sizes[0]
    col_av = col_ak + sizes[1]
    col_ag = col_av + sizes[2]
    col_bq = col_ag + sizes[3]
    col_bk = col_bq + sizes[6]
    col_bv = col_bk + sizes[7]
    n_main = col_bv + sizes[8]
    swa_q_scale = float(swa_hd) ** -0.5 * float(np.log2(np.e))
    rope0 = [(col_bq <= ch * COL_CHUNK < col_bv, swa_q_scale if col_bq <= ch * COL_CHUNK < col_bk else 1.0)
             for ch in range(n_main // COL_CHUNK)]
    cos_b, sin_b = _rope_tables(s_len, lc, swa_hd)
    p0, r0 = _proj_call(xa, mod0, g_norm_mix[0], w_main, cos_b, sin_b, rope0, swa_hd // 4, ncb, wr=w_rank)

    gw = gla_gate_w[0]
    gw_pad = jnp.zeros((2, LANES, GLA_HEADS * gla_dk), F32)
    gw_pad = gw_pad.at[0, 0:GLA_GATE_RANK].set(gw[0]).at[1, GLA_GATE_RANK:2 * GLA_GATE_RANK].set(gw[1])
    a_out = _gla_call(p0, r0, gw_pad.astype(BF16), gla_gate_b[0].reshape(2, 1, -1), gla_norm_g[0],
                      _gla_constants(), lc, col_aq, col_ak, col_av, col_ag, gla_dk, gla_dv)
    sink_col = jnp.broadcast_to(
        jnp.repeat(swa_sink[0].reshape(SWA_KV_HEADS, SWA_GROUP), SWA_BLOCK, axis=1)[:, :, None],
        (SWA_KV_HEADS, SWA_GROUP * SWA_BLOCK, LANES))
    b_out = _swa_call(p0, sink_col, lc, col_bq, col_bk, col_bv, swa_hd)

    w_o = w_out_even[0].astype(BF16)
    n_a = GLA_HEADS * gla_dv
    xa1, h2, aff_t = _outproj_call([a_out, b_out], [w_o[:n_a], w_o[n_a:]], xa, mod0, g_norm_ffn[0],
                                   w_router[0], 0, t, ncb)
    cap_lat = s_len * EC_FACTOR // n_exp
    cap_ctx = lc * EC_FACTOR // n_exp
    segs0 = [(lc, s_len, cap_lat, 0), (0, lc, cap_ctx, cap_lat)]
    xa2 = _moe(h2, aff_t, xa1, mod0, g_final, w_gate_up, w_down, 0, segs0, cap_lat, cap_ctx, lc, False)

    mod1 = mod_for(1)
    lambda_init = 0.8 - 0.6 * float(np.exp(-0.3 * 1))
    w_qkv = w_qkv_odd[0].astype(BF16)
    n_qkv = w_qkv.shape[1]
    q_scale = DIFF_HD ** -0.5 * float(np.log2(np.e))
    rope1 = [(ch * COL_CHUNK < 2 * d, q_scale if ch * COL_CHUNK < d else 1.0)
             for ch in range(n_qkv // COL_CHUNK)]
    cos_c, sin_c = _rope_tables(s_len, lc, DIFF_HD)
    (p1,) = _proj_call(xa2, mod1, g_norm_mix[1], w_qkv, cos_c, sin_c, rope1, DIFF_HD // 4, ncb)
    y1 = _diff_call(p1, diff_lambda[0], diff_subln_g[0], lc, diff_heads, lambda_init)
    x3, h2b, aff_tb = _outproj_call([y1], [w_out_odd[0].astype(BF16)], xa2, mod1, g_norm_ffn[1],
                                    w_router[1], ncb, s_len, ncb)
    segs1 = [(0, s_len, cap_lat, 0)]
    return _moe(h2b, aff_tb, x3, mod1, g_final, w_gate_up, w_down, 1, segs1, cap_lat, 0, 0, True)
```

```python
import functools

import numpy as np
import jax
import jax.numpy as jnp
from jax import lax
from jax.experimental import pallas as pl
from jax.experimental.pallas import tpu as pltpu

F32 = jnp.float32
BF16 = jnp.bfloat16

GRID_W = 64
ROPE_THETA = 10000.0
NORM_EPS = 1e-6
GLA_HEADS = 4
GLA_GATE_RANK = 16
GLA_GATE_NORM = 16.0
SWA_HEADS = 8
SWA_KV_HEADS = 2
SWA_GROUP = SWA_HEADS // SWA_KV_HEADS
SWA_WINDOW = 128
SWA_BLOCK = 128
DIFF_HD = 64
N_EXPERTS = 16
EC_FACTOR = 2

LANES = 128
ROW_TILE = 256
COL_CHUNK = 256
GLA_CHUNK = 128
GLA_LEVELS = 7
GLA_SAFE_RANGE = 60.0
DIFF_QBLOCK = 256
DIFF_SAFE_BOUND = 60.0
DIFF_BOUND_SLACK = 1.0 + 2.0 ** -6
VMEM_LIMIT = 56 * 1024 * 1024
NEG_BIG = -1e30


def _cparams(*sem):
    return pltpu.CompilerParams(dimension_semantics=sem, vmem_limit_bytes=VMEM_LIMIT)


def _pick(n, cands):
    for c in cands:
        if n % c == 0:
            return c
    raise ValueError(f"no tile for {n} in {cands}")


def _nt(a, b):
    return lax.dot_general(a, b, (((1,), (1,)), ((), ())), preferred_element_type=F32)


def _tn(a, b):
    return lax.dot_general(a, b, (((0,), (0,)), ((), ())), preferred_element_type=F32)


def _dot(a, b):
    return jnp.dot(a, b, preferred_element_type=F32)


def _split3(a):
    hi = a.astype(BF16)
    r1 = a - hi.astype(F32)
    mid = r1.astype(BF16)
    lo = (r1 - mid.astype(F32)).astype(BF16)
    return hi, mid, lo


def _split2(a):
    hi = a.astype(BF16)
    lo = (a - hi.astype(F32)).astype(BF16)
    return hi, lo


def _silu(x):
    return x * (1.0 / (1.0 + jnp.exp(-x)))


def _rms(x, eps=NORM_EPS):
    return x * lax.rsqrt(jnp.mean(x * x, axis=-1, keepdims=True) + eps)


def _row_specs(stream, n_ctx_blocks, off=0, stride=1):
    if isinstance(stream, tuple):
        assert stride == 1
        ctx, x = stream
        d = x.shape[2]
        return ([pl.BlockSpec((1, ROW_TILE, d), lambda bi, i: (bi, jnp.minimum(i + off, n_ctx_blocks - 1), 0)),
                 pl.BlockSpec((1, ROW_TILE, d), lambda bi, i: (bi, jnp.maximum(i + off - n_ctx_blocks, 0), 0))],
                [ctx, x])
    d = stream.shape[2]
    return [pl.BlockSpec((1, ROW_TILE, d), lambda bi, i: (bi, i * stride + off, 0))], [stream]


def _load_rows(row_refs, n_ctx_blocks, off=0):
    if len(row_refs) == 2:
        return jnp.where(pl.program_id(1) + off < n_ctx_blocks, row_refs[0][0], row_refs[1][0])
    return row_refs[0][0]


def _ada_kernel(c_ref, w_ref, b_ref, o_ref):
    s = _silu(c_ref[...])
    s_hi, s_lo = _split2(s)
    w_hi, w_lo = _split2(w_ref[0])
    acc = _dot(s_hi, w_hi) + (_dot(s_lo, w_hi) + _dot(s_hi, w_lo))
    o_ref[0] = acc + b_ref[0]


def _ada_call(cc, w_ada, b_ada):
    depth, d, n = w_ada.shape
    r = cc.shape[0]
    tn = _pick(n, (768, 512, 256, 128))
    return pl.pallas_call(
        _ada_kernel,
        grid=(depth, n // tn),
        in_specs=[pl.BlockSpec((r, d), lambda l, j: (0, 0)),
                  pl.BlockSpec((1, d, tn), lambda l, j: (l, 0, j)),
                  pl.BlockSpec((1, 1, tn), lambda l, j: (l, 0, j))],
        out_specs=pl.BlockSpec((1, r, tn), lambda l, j: (l, 0, j)),
        out_shape=jax.ShapeDtypeStruct((depth, r, n), F32),
        compiler_params=_cparams("parallel", "parallel"),
        name="adaln",
    )(cc, w_ada, b_ada.reshape(depth, 1, n))


def _proj_kernel(*refs, n_row_refs, n_ctx_blocks, chunk_ops, quarter, has_r):
    row_refs = refs[:n_row_refs]
    mod_ref, g_ref, w_ref, cos_ref, sin_ref = refs[n_row_refs:n_row_refs + 5]
    rest = refs[n_row_refs + 5:]
    if has_r:
        wr_ref, o_ref, r_ref = rest
    else:
        (o_ref,) = rest
    x = _load_rows(row_refs, n_ctx_blocks)
    y = _rms(x) * g_ref[...]
    h = (y * (1.0 + mod_ref[0, 0, 1:2, :]) + mod_ref[0, 0, 0:1, :]).astype(BF16)
    reps = COL_CHUNK // LANES
    cos_t = jnp.concatenate([cos_ref[...]] * reps, axis=1)
    sin_t = jnp.concatenate([sin_ref[...]] * reps, axis=1)
    lane = lax.broadcasted_iota(jnp.int32, (1, COL_CHUNK), 1)
    first = (lane % (2 * quarter)) < quarter
    for c, (rope, scale) in enumerate(chunk_ops):
        acc = _dot(h, w_ref[:, c * COL_CHUNK:(c + 1) * COL_CHUNK])
        if rope:
            partner = jnp.where(first, pltpu.roll(acc, COL_CHUNK - quarter, 1), pltpu.roll(acc, quarter, 1))
            acc = acc * cos_t + partner * sin_t
        if scale != 1.0:
            acc = acc * scale
        o_ref[0, :, c * COL_CHUNK:(c + 1) * COL_CHUNK] = acc.astype(BF16)
    if has_r:
        r_ref[0] = _dot(h, wr_ref[...]).astype(BF16)


def _proj_call(stream, mod, g, w, cos_t, sin_t, chunk_ops, quarter, n_ctx_blocks, wr=None):
    d, n = w.shape
    t = cos_t.shape[0]
    b = mod.shape[0]
    assert n == len(chunk_ops) * COL_CHUNK and t % ROW_TILE == 0
    has_r = wr is not None
    row_specs, row_args = _row_specs(stream, n_ctx_blocks)
    in_specs = row_specs + [
        pl.BlockSpec((1, 1, 6, d), lambda bi, i: (bi, jnp.where(i < n_ctx_blocks, 0, 1), 0, 0)),
        pl.BlockSpec((1, d), lambda bi, i: (0, 0)),
        pl.BlockSpec((d, n), lambda bi, i: (0, 0), pipeline_mode=pl.Buffered(1)),
        pl.BlockSpec((ROW_TILE, LANES), lambda bi, i: (i, 0)),
        pl.BlockSpec((ROW_TILE, LANES), lambda bi, i: (i, 0))]
    args = row_args + [mod, g.reshape(1, d), w, cos_t, sin_t]
    out_specs = [pl.BlockSpec((1, ROW_TILE, n), lambda bi, i: (bi, i, 0))]
    out_shape = [jax.ShapeDtypeStruct((b, t, n), BF16)]
    if has_r:
        in_specs.append(pl.BlockSpec((d, LANES), lambda bi, i: (0, 0)))
        args.append(wr)
        out_specs.append(pl.BlockSpec((1, ROW_TILE, LANES), lambda bi, i: (bi, i, 0)))
        out_shape.append(jax.ShapeDtypeStruct((b, t, LANES), BF16))
    return pl.pallas_call(
        functools.partial(_proj_kernel, n_row_refs=len(row_args), n_ctx_blocks=n_ctx_blocks,
                          chunk_ops=tuple(chunk_ops), quarter=quarter, has_r=has_r),
        grid=(b, t // ROW_TILE),
        in_specs=in_specs, out_specs=out_specs, out_shape=out_shape,
        compiler_params=_cparams("parallel", "parallel"),
        name="norm_proj",
    )(*args)


def _gla_constants():
    c = GLA_CHUNK
    idx = np.arange(c)
    cm = np.zeros((2, GLA_LEVELS + 2, c, c), np.float32)
    lmask = np.zeros((2, GLA_LEVELS + 2, c, c), np.float32)
    cm[0, 0] = (idx[None, :] <= idx[:, None])
    cm[1, 0] = (idx[None, :] >= idx[:, None])
    for l in range(GLA_LEVELS):
        s = c >> (l + 1)
        blk = idx // (2 * s)
        second = (idx % (2 * s)) >= s
        same = blk[:, None] == blk[None, :]
        sep_f = blk * 2 * s + s - 1
        cm[0, 1 + l] = (idx[None, :] <= sep_f[:, None])
        lmask[0, l] = same & second[:, None] & (~second)[None, :]
        sep_b = blk * 2 * s + s
        cm[1, 1 + l] = (idx[None, :] >= sep_b[:, None])
        lmask[1, l] = same & (~second)[:, None] & second[None, :]
    cm[:, GLA_LEVELS + 1] = 1.0
    lmask[:, GLA_LEVELS] = np.eye(c)
    lmask[:, GLA_LEVELS + 1] = cm[:, 0]
    return jnp.asarray(cm.reshape(2, (GLA_LEVELS + 2) * c, c), BF16), jnp.asarray(lmask, F32)


def _gla_kernel(q_ref, k_ref, v_ref, ag_ref, r_ref, gw_ref, gb_ref, ng_ref, cm_ref, lm_ref,
                o_ref, la_ref, of_ref, ob_ref, qe_ref, u_ref, ss_ref, dec_ref, st_ref,
                *, n_ctx_chunks, n_chunks, q_scale):
    c = GLA_CHUNK
    dv = st_ref.shape[1]
    r = r_ref[0]
    for d in range(2):
        z = _dot(r, gw_ref[d]) + gb_ref[d]
        la_ref[d] = (jnp.minimum(z, 0.0) - jnp.log(1.0 + jnp.exp(-jnp.abs(z)))) * (1.0 / GLA_GATE_NORM)
    st_ref[...] = jnp.zeros_like(st_ref)
    out_refs = (of_ref, ob_ref)

    def chunk_rows(ci):
        return pl.ds(pl.multiple_of(ci * c, c), c)

    def state_rows(ci):
        return pl.ds(pl.multiple_of(ci * dv, dv), dv)

    min_tot = None
    for d in range(2):
        m = jnp.min(jnp.sum(la_ref[d].reshape(n_chunks, c, la_ref.shape[2]), axis=1))
        min_tot = m if min_tot is None else jnp.minimum(min_tot, m)
    is_safe = min_tot >= -GLA_SAFE_RANGE

    group = 3 if n_chunks % 3 == 0 else 2

    def prep_fast(cp, carry):
        jobs = [(group * cp + u, d) for u in range(group) for d in range(2)]
        rows = [chunk_rows(ci) for ci, _ in jobs]
        qs = {u: q_ref[0, chunk_rows(group * cp + u), :].astype(F32) * q_scale for u in range(group)}
        ks = {u: k_ref[0, chunk_rows(group * cp + u), :].astype(F32) for u in range(group)}
        bcs = {}
        for j, (ci, d) in enumerate(jobs):
            g_hi, g_lo = _split2(la_ref[d, rows[j], :])
            cm = cm_ref[d, 0:c, :]
            bcs[j] = _dot(cm, g_hi) + _dot(cm, g_lo)
        qes, kins, decs = {}, {}, {}
        for j, (ci, d) in enumerate(jobs):
            bc = bcs[j]
            qes[j] = (qs[j // 2] * jnp.exp(bc)).astype(BF16)
            kins[j] = ks[j // 2] * jnp.exp(-bc)
            decs[j] = jnp.exp(bc[c - 1:c] if d == 0 else bc[0:1])
        atts = {j: _nt(qes[j], kins[j].astype(BF16)) for j in range(len(jobs))}
        for j, (ci, d) in enumerate(jobs):
            att = (lm_ref[d, GLA_LEVELS + 1] * atts[j]).astype(BF16)
            out_refs[d][rows[j], :] = _dot(att, v_ref[0, rows[j], :])
            qe_ref[d, rows[j], :] = qes[j]
            u_ref[d, state_rows(ci), :] = _tn(v_ref[0, rows[j], :], (kins[j] * decs[j]).astype(BF16))
            dec_ref[d, pl.ds(pl.multiple_of(ci * 8, 8), 8), :] = jnp.broadcast_to(decs[j], (8, decs[j].shape[1]))
        return carry

    def prep(ci, carry):
        rows = chunk_rows(ci)
        q = q_ref[0, rows, :].astype(F32) * q_scale
        k = k_ref[0, rows, :].astype(F32)
        v = v_ref[0, rows, :]
        qk = _nt(q.astype(BF16), k.astype(BF16))
        for d in range(2):
            g_hi, g_lo = _split2(la_ref[d, rows, :])
            cm = cm_ref[d]
            cum = _dot(cm, g_hi) + _dot(cm, g_lo)
            bc = cum[0:c]
            tot = cum[(GLA_LEVELS + 1) * c:(GLA_LEVELS + 2) * c]
            att = lm_ref[d, GLA_LEVELS] * qk
            for l in range(GLA_LEVELS):
                ref_l = cum[(1 + l) * c:(2 + l) * c]
                eq = jnp.exp(jnp.minimum(bc - ref_l, 0.0))
                ek = jnp.exp(jnp.minimum(ref_l - bc, 0.0))
                att = att + lm_ref[d, l] * _nt((q * eq).astype(BF16), (k * ek).astype(BF16))
            out_refs[d][rows, :] = _dot(att.astype(BF16), v)
            qe_ref[d, rows, :] = (q * jnp.exp(bc)).astype(BF16)
            u_ref[d, state_rows(ci), :] = _tn(v, (k * jnp.exp(tot - bc)).astype(BF16))
            dec_ref[d, pl.ds(pl.multiple_of(ci * 8, 8), 8), :] = jnp.exp(tot[0:8])
        return carry

    @pl.when(is_safe)
    def _():
        lax.fori_loop(0, n_chunks // group, prep_fast, 0)

    @pl.when(jnp.logical_not(is_safe))
    def _():
        lax.fori_loop(0, n_chunks, prep, 0)

    def scan(t, carry):
        cb = jnp.where(t < n_ctx_chunks, n_ctx_chunks - 1 - t, n_chunks - 1 - (t - n_ctx_chunks))
        for d, ci in ((0, t), (1, cb)):
            st = st_ref[d]
            ss_ref[d, state_rows(ci), :] = st.astype(BF16)
            dec = dec_ref[d, pl.ds(pl.multiple_of(ci * 8, 8), 1), :]
            st_ref[d] = st * dec + u_ref[d, state_rows(ci), :]
        return carry

    lax.fori_loop(0, n_chunks, scan, 0)

    def inter(ci, carry):
        rows = chunk_rows(ci)
        for d in range(2):
            out_refs[d][rows, :] += _nt(qe_ref[d, rows, :], ss_ref[d, state_rows(ci), :])
        return carry

    lax.fori_loop(0, n_chunks, inter, 0, unroll=2 * group)
    o = of_ref[...] + ob_ref[...]
    ag = ag_ref[0].astype(F32)
    o_ref[0] = (_rms(o) * ng_ref[...] * _silu(ag)).astype(BF16)


def _gla_call(p0, r, gw_pad, gb, norm_g, consts, n_ctx_rows, col_q, col_k, col_v, col_g, dk, dv):
    b, t, _ = p0.shape
    n_chunks = t // GLA_CHUNK
    cm, lm = consts
    kern = functools.partial(_gla_kernel, n_ctx_chunks=n_ctx_rows // GLA_CHUNK, n_chunks=n_chunks,
                             q_scale=float(dk) ** -0.5)
    full = lambda shape: pl.BlockSpec(shape, lambda bi, h: (0,) * len(shape))
    return pl.pallas_call(
        kern,
        grid=(b, GLA_HEADS),
        in_specs=[pl.BlockSpec((1, t, dk), lambda bi, h: (bi, 0, col_q // dk + h)),
                  pl.BlockSpec((1, t, dk), lambda bi, h: (bi, 0, col_k // dk + h)),
                  pl.BlockSpec((1, t, dv), lambda bi, h: (bi, 0, col_v // dv + h)),
                  pl.BlockSpec((1, t, dv), lambda bi, h: (bi, 0, col_g // dv + h)),
                  pl.BlockSpec((1, t, LANES), lambda bi, h: (bi, 0, 0)),
                  pl.BlockSpec((2, LANES, dk), lambda bi, h: (0, 0, h)),
                  pl.BlockSpec((2, 1, dk), lambda bi, h: (0, 0, h)),
                  full((1, dv)), full(cm.shape), full(lm.shape)],
        out_specs=pl.BlockSpec((1, t, dv), lambda bi, h: (bi, 0, h)),
        out_shape=jax.ShapeDtypeStruct((b, t, GLA_HEADS * dv), BF16),
        scratch_shapes=[pltpu.VMEM((2, t, dk), F32), pltpu.VMEM((t, dv), F32), pltpu.VMEM((t, dv), F32),
                        pltpu.VMEM((2, t, dk), BF16), pltpu.VMEM((2, n_chunks * dv, dk), F32),
                        pltpu.VMEM((2, n_chunks * dv, dk), BF16),
                        pltpu.VMEM((2, n_chunks * 8, dk), F32), pltpu.VMEM((2, dv, dk), F32)],
        compiler_params=_cparams("parallel", "parallel"),
        name="gla",
    )(p0, p0, p0, p0, r, gw_pad, gb, norm_g.reshape(1, dv), cm, lm)


def _swa_kernel(q_ref, k_ref, v_ref, sink_ref, o_ref, s_ref, *, n_ctx_rows, n_blocks):
    blk = SWA_BLOCK
    win = 3 * blk
    ncb = n_ctx_rows // blk
    nlb = n_blocks - ncb
    sink = sink_ref[0, :, 0:1] * float(np.log2(np.e))
    rel0 = (lax.broadcasted_iota(jnp.int32, (blk, win), 1) - lax.broadcasted_iota(jnp.int32, (blk, win), 0))
    no_bias = jnp.zeros((blk, n_ctx_rows), F32)

    def key_rows(j):
        kb0 = jnp.clip(j - ncb - 1, 0, nlb - 3)
        return kb0, pl.ds(pl.multiple_of(n_ctx_rows + kb0 * blk, blk), win)

    def scores(j, slot):
        kb0, rows = key_rows(j)
        q = q_ref[0, pl.ds(pl.multiple_of(j * blk, blk), blk), :]
        q4 = jnp.concatenate([q[:, g * blk:(g + 1) * blk] for g in range(SWA_GROUP)], axis=0)
        keys = jnp.concatenate([k_ref[0, rows, :], k_ref[0, 0:n_ctx_rows, :]], axis=0)
        rel = rel0 + jnp.where(j < ncb, 4 * win, (kb0 - (j - ncb)) * blk)
        bias = jnp.concatenate([jnp.where(jnp.abs(rel) <= SWA_WINDOW, 0.0, NEG_BIG), no_bias], axis=1)
        s_ref[slot] = _nt(q4, keys) + jnp.concatenate([bias] * SWA_GROUP, axis=0)

    def finish(j, slot):
        _, rows = key_rows(j)
        vals = jnp.concatenate([v_ref[0, rows, :], v_ref[0, 0:n_ctx_rows, :]], axis=0)
        s = s_ref[slot]
        m = jnp.maximum(jnp.max(s, axis=1, keepdims=True), sink)
        e = jnp.exp2(s - m)
        den = jnp.sum(e, axis=1, keepdims=True) + jnp.exp2(sink - m)
        o = _dot(e.astype(BF16), vals) / den
        o_ref[0, pl.ds(pl.multiple_of(j * blk, blk), blk), :] = jnp.concatenate(
            [o[g * blk:(g + 1) * blk] for g in range(SWA_GROUP)], axis=1).astype(BF16)

    scores(0, 0)

    def body(jj, carry):
        j0 = 2 * jj
        scores(j0 + 1, 1)
        finish(j0, 0)
        scores(j0 + 2, 0)
        finish(j0 + 1, 1)
        return carry

    lax.fori_loop(0, n_blocks // 2 - 1, body, 0)
    scores(n_blocks - 1, 1)
    finish(n_blocks - 2, 0)
    finish(n_blocks - 1, 1)


def _swa_call(p0, sink_col, n_ctx_rows, col_q, col_k, col_v, hd):
    b, t, _ = p0.shape
    blk = SWA_BLOCK
    nb = t // blk
    gq = SWA_GROUP * hd
    assert nb % 2 == 0 and nb - n_ctx_rows // blk >= 3 and SWA_WINDOW <= blk
    return pl.pallas_call(
        functools.partial(_swa_kernel, n_ctx_rows=n_ctx_rows, n_blocks=nb),
        grid=(b, SWA_KV_HEADS),
        in_specs=[pl.BlockSpec((1, t, gq), lambda bi, h: (bi, 0, col_q // gq + h)),
                  pl.BlockSpec((1, t, hd), lambda bi, h: (bi, 0, col_k // hd + h)),
                  pl.BlockSpec((1, t, hd), lambda bi, h: (bi, 0, col_v // hd + h)),
                  pl.BlockSpec((1, SWA_GROUP * blk, LANES), lambda bi, h: (h, 0, 0))],
        out_specs=pl.BlockSpec((1, t, gq), lambda bi, h: (bi, 0, h)),
        out_shape=jax.ShapeDtypeStruct((b, t, SWA_HEADS * hd), BF16),
        scratch_shapes=[pltpu.VMEM((2, SWA_GROUP * blk, 3 * blk + n_ctx_rows), F32)],
        compiler_params=_cparams("parallel", "parallel"),
        name="swa",
    )(p0, p0, p0, sink_col)


def _diff_kernel(lam_ref, q_ref, k_ref, v_ref, g_ref, o_ref, s_ref, kx_ref, vx_ref, qn_ref,
                 *, row_off, n_qblocks, lambda_init):
    tq = DIFF_QBLOCK
    hd2 = 2 * DIFF_HD
    n_keys = k_ref.shape[1]
    lam = lam_ref[...]
    s01 = jnp.sum(lam[0:1] * lam[1:2], axis=1, keepdims=True)
    s23 = jnp.sum(lam[2:3] * lam[3:4], axis=1, keepdims=True)
    lam_full = jnp.exp(s01) - jnp.exp(s23) + lambda_init
    lane = lax.broadcasted_iota(jnp.int32, (1, hd2), 1)
    first = lane < DIFF_HD
    col0 = lane == 0
    zero = jnp.zeros((), BF16)

    d_idx = lax.broadcasted_iota(jnp.int32, (hd2, 2 * hd2), 0)
    j_idx = lax.broadcasted_iota(jnp.int32, (hd2, 2 * hd2), 1)
    map_sum = jnp.where((d_idx < DIFF_HD) == (j_idx < hd2), 1.0, 0.0).astype(BF16)

    def sq_norms(x):
        xf = x.astype(F32)
        return _dot((xf * xf).astype(BF16), map_sum)

    kn = sq_norms(k_ref[0])
    kmax0 = jnp.sqrt(jnp.max(kn[:, 0:hd2], axis=0, keepdims=True)) * DIFF_BOUND_SLACK
    kmax1 = jnp.sqrt(jnp.max(kn[:, hd2:2 * hd2], axis=0, keepdims=True)) * DIFF_BOUND_SLACK
    qn_ref[...] = sq_norms(q_ref[0, row_off:, :])
    qmax0 = jnp.sqrt(jnp.max(qn_ref[:, 0:hd2], axis=0, keepdims=True))
    qmax1 = jnp.sqrt(jnp.max(qn_ref[:, hd2:2 * hd2], axis=0, keepdims=True))
    is_safe = jnp.max(jnp.maximum(qmax0 * kmax0, qmax1 * kmax1)) <= DIFF_SAFE_BOUND

    def q_block(j):
        q = q_ref[0, pl.ds(pl.multiple_of(row_off + j * tq, tq), tq), :]
        return q, jnp.where(first, q, zero), jnp.where(first, zero, q)

    def write_out(j, o):
        y = _rms(o) * g_ref[...] * (1.0 - lambda_init)
        o_ref[0, pl.ds(pl.multiple_of(j * tq, tq), tq), :] = y.astype(BF16)

    def scores_fast(j, slot):
        _, qa, qb = q_block(j)
        qn = jnp.sqrt(qn_ref[pl.ds(pl.multiple_of(j * tq, tq), tq), :])
        xa = jnp.where(col0, -(qn[:, 0:hd2] * kmax0), 0.0).astype(BF16)
        xb = jnp.where(col0, -(qn[:, hd2:2 * hd2] * kmax1), 0.0).astype(BF16)
        qq = jnp.concatenate([jnp.concatenate([qa, xa], axis=1), jnp.concatenate([qb, xb], axis=1)], axis=0)
        s_ref[slot] = _nt(qq, kx_ref[...])

    def finish_fast(j, slot):
        e = jnp.exp2(s_ref[slot]).astype(BF16)
        ov = _dot(e, vx_ref[...])
        p0 = ov[0:tq, 0:hd2] / ov[0:tq, hd2:2 * hd2]
        p1 = ov[tq:2 * tq, 0:hd2] / ov[tq:2 * tq, hd2:2 * hd2]
        write_out(j, p0 - lam_full * p1)

    def scores_max(j, slot):
        _, qa, qb = q_block(j)
        s_ref[slot] = _nt(jnp.concatenate([qa, qb], axis=0), k_ref[0])

    def finish_max(j, slot):
        s = s_ref[slot]
        e = jnp.exp2(s - jnp.max(s, axis=1, keepdims=True))
        z = jnp.sum(e, axis=1, keepdims=True)
        coef = lam_full * z[0:tq] / z[tq:2 * tq]
        a = e[0:tq] - coef * e[tq:2 * tq]
        write_out(j, _dot(a.astype(BF16), v_ref[0]) * (1.0 / z[0:tq]))

    def run(scores, finish):
        scores(0, 0)

        def body(jj, carry):
            j0 = 2 * jj
            scores(j0 + 1, 1)
            finish(j0, 0)
            scores(j0 + 2, 0)
            finish(j0 + 1, 1)
            return carry

        lax.fori_loop(0, n_qblocks // 2 - 1, body, 0)
        scores(n_qblocks - 1, 1)
        finish(n_qblocks - 2, 0)
        finish(n_qblocks - 1, 1)

    @pl.when(is_safe)
    def _():
        kx_ref[:, 0:hd2] = k_ref[0]
        kx_ref[:, hd2:2 * hd2] = jnp.broadcast_to(jnp.where(col0, 1.0, 0.0).astype(BF16), (n_keys, hd2))
        vx_ref[:, 0:hd2] = v_ref[0]
        vx_ref[:, hd2:2 * hd2] = jnp.ones((n_keys, hd2), BF16)
        run(scores_fast, finish_fast)

    @pl.when(jnp.logical_not(is_safe))
    def _():
        run(scores_max, finish_max)


def _diff_call(p1, lam, subln_g, n_ctx_rows, n_heads, lambda_init):
    b, t, _ = p1.shape
    s_len = t - n_ctx_rows
    hd2 = 2 * DIFF_HD
    n_qblocks = s_len // DIFF_QBLOCK
    assert n_qblocks >= 2 and n_qblocks % 2 == 0
    return pl.pallas_call(
        functools.partial(_diff_kernel, row_off=n_ctx_rows, n_qblocks=n_qblocks, lambda_init=lambda_init),
        grid=(b, n_heads),
        in_specs=[pl.BlockSpec((4, DIFF_HD), lambda bi, h: (0, 0)),
                  pl.BlockSpec((1, t, hd2), lambda bi, h: (bi, 0, h)),
                  pl.BlockSpec((1, t, hd2), lambda bi, h: (bi, 0, n_heads + h)),
                  pl.BlockSpec((1, t, hd2), lambda bi, h: (bi, 0, 2 * n_heads + h)),
                  pl.BlockSpec((1, hd2), lambda bi, h: (0, 0))],
        out_specs=pl.BlockSpec((1, s_len, hd2), lambda bi, h: (bi, 0, h)),
        out_shape=jax.ShapeDtypeStruct((b, s_len, n_heads * hd2), BF16),
        scratch_shapes=[pltpu.VMEM((2, 2 * DIFF_QBLOCK, t), F32), pltpu.VMEM((t, 2 * hd2), BF16),
                        pltpu.VMEM((t, 2 * hd2), BF16), pltpu.VMEM((s_len, 2 * hd2), F32)],
        compiler_params=_cparams("parallel", "parallel"),
        name="diff_attn",
    )(lam, p1, p1, p1, subln_g.reshape(1, hd2))


def _outproj_kernel(*refs, n_y, n_sub, n_row_refs, n_ctx_blocks, row_block_off):
    y_refs = refs[:n_y]
    w_refs = refs[n_y:2 * n_y]
    row_refs = refs[2 * n_y:2 * n_y + n_sub * n_row_refs]
    mod_ref, g_ref, wr_ref, xn_ref, h_ref, aff_ref = refs[2 * n_y + n_sub * n_row_refs:]
    n_exp = aff_ref.shape[1]
    subs = [slice(u * ROW_TILE, (u + 1) * ROW_TILE) for u in range(n_sub)]
    accs = []
    for rs in subs:
        acc = _dot(y_refs[0][0, rs, :], w_refs[0][...])
        for yr, wr in zip(y_refs[1:], w_refs[1:]):
            acc = acc + _dot(yr[0, rs, :], wr[...])
        accs.append(acc)
    for u, rs in enumerate(subs):
        x_rows = _load_rows(row_refs[u * n_row_refs:(u + 1) * n_row_refs], n_ctx_blocks, row_block_off)
        xn = x_rows + mod_ref[0, 0, 2:3, :] * accs[u]
        xn_ref[0, rs, :] = xn
        h = _rms(xn) * g_ref[...] * (1.0 + mod_ref[0, 0, 4:5, :]) + mod_ref[0, 0, 3:4, :]
        h_bf = h.astype(BF16)
        h_ref[0, rs, :] = h_bf
        logits = _dot(h_bf, wr_ref[...])
        logits = jnp.where(lax.broadcasted_iota(jnp.int32, (1, LANES), 1) < n_exp, logits, NEG_BIG)
        ex = jnp.exp(logits - jnp.max(logits, axis=1, keepdims=True))
        aff = ex / jnp.sum(ex, axis=1, keepdims=True)
        aff_ref[0, :, rs] = jnp.transpose(aff)[0:n_exp]


def _outproj_call(ys, ws, stream, mod, g, w_router, row_block_off, n_rows, n_ctx_blocks):
    b, d = mod.shape[0], mod.shape[3]
    n_y = len(ys)
    e = w_router.shape[1]
    w_router_pad = jnp.concatenate([w_router, jnp.zeros((d, LANES - e), F32)], axis=1).astype(BF16)
    all_latent = row_block_off >= n_ctx_blocks and not isinstance(stream, tuple)
    n_sub = 2 if all_latent and n_rows % (2 * ROW_TILE) == 0 else 1
    tile = n_sub * ROW_TILE
    row_specs, row_args = [], []
    for u in range(n_sub):
        specs_u, args_u = _row_specs(stream, n_ctx_blocks, row_block_off + u, n_sub)
        row_specs += specs_u
        row_args += args_u
    in_specs = ([pl.BlockSpec((1, tile, y.shape[2]), lambda bi, i: (bi, i, 0)) for y in ys]
                + [pl.BlockSpec(w.shape, lambda bi, i: (0, 0), pipeline_mode=pl.Buffered(1)) for w in ws]
                + row_specs
                + [pl.BlockSpec((1, 1, 6, d),
                                lambda bi, i: (bi, jnp.where(i * n_sub + row_block_off < n_ctx_blocks, 0, 1), 0, 0)),
                   pl.BlockSpec((1, d), lambda bi, i: (0, 0)),
                   pl.BlockSpec((d, LANES), lambda bi, i: (0, 0))])
    return pl.pallas_call(
        functools.partial(_outproj_kernel, n_y=n_y, n_sub=n_sub, n_row_refs=len(row_args) // n_sub,
                          n_ctx_blocks=n_ctx_blocks, row_block_off=row_block_off),
        grid=(b, n_rows // tile),
        in_specs=in_specs,
        out_specs=[pl.BlockSpec((1, tile, d), lambda bi, i: (bi, i, 0)),
                   pl.BlockSpec((1, tile, d), lambda bi, i: (bi, i, 0)),
                   pl.BlockSpec((1, e, tile), lambda bi, i: (bi, 0, i))],
        out_shape=[jax.ShapeDtypeStruct((b, n_rows, d), F32),
                   jax.ShapeDtypeStruct((b, n_rows, d), BF16),
                   jax.ShapeDtypeStruct((b, e, n_rows), F32)],
        compiler_params=_cparams("parallel", "parallel"),
        name="out_proj",
    )(*ys, *ws, *row_args, mod, g.reshape(1, d), w_router_pad)


def _prefix_lanes(m, upper):
    e, n = m.shape
    carry = jnp.zeros((e, 1), F32)
    outs = []
    for blk in range(n // LANES):
        mb = m[:, blk * LANES:(blk + 1) * LANES]
        inc = _dot(mb.astype(BF16), upper)
        outs.append(inc - mb + carry)
        carry = carry + jnp.sum(mb, axis=1, keepdims=True)
    return jnp.concatenate(outs, axis=1)


def _transpose_exact(x, eye):
    hi, mid, lo = _split3(x)
    return _nt(eye, hi) + (_nt(eye, mid) + _nt(eye, lo))


def _route_kernel(aff_ref, pos_ref, gate_ref, posn_ref, lo_ref, hi_ref, *, segs):
    n_exp = aff_ref.shape[1]
    lo_acc = jnp.zeros((n_exp, LANES), F32)
    hi_acc = jnp.zeros((n_exp, LANES), F32)
    li = lax.broadcasted_iota(jnp.int32, (LANES, LANES), 0)
    lj = lax.broadcasted_iota(jnp.int32, (LANES, LANES), 1)
    upper = jnp.where(li <= lj, 1.0, 0.0).astype(BF16)
    eye = jnp.where(li == lj, 1.0, 0.0).astype(BF16)
    all_bits = [pltpu.bitcast(aff_ref[0, :, off:off + n], jnp.int32) for (off, n, _, _) in segs]

    def search(it, thrs):
        out = []
        for bits, (_, _, cap, _), thr in zip(all_bits, segs, thrs):
            cand = thr | jnp.left_shift(jnp.int32(1), 30 - it)
            cnt = jnp.sum(jnp.where(bits >= cand, 1.0, 0.0), axis=1, keepdims=True)
            out.append(jnp.where(cnt >= cap, cand, thr))
        return tuple(out)

    all_thr = lax.fori_loop(0, 31, search, tuple(jnp.zeros((n_exp, 1), jnp.int32) for _ in segs))
    for (off, n, cap, slot_off), bits, thr in zip(segs, all_bits, all_thr):
        a = aff_ref[0, :, off:off + n]
        gt = jnp.where(bits > thr, 1.0, 0.0)
        eq = jnp.where(bits == thr, 1.0, 0.0)
        need = cap - jnp.sum(gt, axis=1, keepdims=True)
        sel = gt + eq * jnp.where(_prefix_lanes(eq, upper) < need, 1.0, 0.0)
        slot = _prefix_lanes(sel, upper)
        pos = jnp.where(sel > 0.0, slot + slot_off, -1.0)
        pos_ref[0, :, off:off + n] = pos.astype(jnp.int32)
        gate_ref[0, :, off:off + n] = sel * a
        for blk in range(n // LANES):
            cols = slice(blk * LANES, (blk + 1) * LANES)
            rows = slice(off + blk * LANES, off + (blk + 1) * LANES)
            posn_ref[0, rows, :] = _transpose_exact(pos[:, cols], eye).astype(jnp.int32)
        assert cap <= 256 and (off + n) // LANES <= LANES
        tok = lax.broadcasted_iota(jnp.int32, (n, LANES), 0)
        blk_start = (lax.broadcasted_iota(jnp.int32, (n, LANES), 1) - off // LANES) * LANES
        lo_acc = lo_acc + _dot(slot.astype(BF16), jnp.where(tok == blk_start, 1.0, 0.0).astype(BF16))
        hi_acc = hi_acc + _dot((slot + sel).astype(BF16),
                               jnp.where(tok == blk_start + (LANES - 1), 1.0, 0.0).astype(BF16))
    lo_ref[0] = lo_acc.astype(jnp.int32)
    hi_ref[0] = hi_acc.astype(jnp.int32)


def _route_call(aff_t, segs):
    b, e, t = aff_t.shape
    return pl.pallas_call(
        functools.partial(_route_kernel, segs=tuple(segs)),
        grid=(b,),
        in_specs=[pl.BlockSpec((1, e, t), lambda bi: (bi, 0, 0))],
        out_specs=[pl.BlockSpec((1, e, t), lambda bi: (bi, 0, 0)),
                   pl.BlockSpec((1, e, t), lambda bi: (bi, 0, 0)),
                   pl.BlockSpec((1, t, e), lambda bi: (bi, 0, 0)),
                   pl.BlockSpec((1, e, LANES), lambda bi: (bi, 0, 0)),
                   pl.BlockSpec((1, e, LANES), lambda bi: (bi, 0, 0))],
        out_shape=[jax.ShapeDtypeStruct((b, e, t), jnp.int32),
                   jax.ShapeDtypeStruct((b, e, t), F32),
                   jax.ShapeDtypeStruct((b, t, e), jnp.int32),
                   jax.ShapeDtypeStruct((b, e, LANES), jnp.int32),
                   jax.ShapeDtypeStruct((b, e, LANES), jnp.int32)],
        compiler_params=_cparams("parallel"),
        name="route",
    )(aff_t)


def _gather_kernel(lo_ref, hi_ref, pos_ref, gate_ref, h_ref, o_ref, g_ref,
                   *, n_ctx_rows, n_lat_rows, cap_lat, sblk, win, margin):
    sb = pl.program_id(1)
    n_exp = pos_ref.shape[1]
    n_lat_blocks = cap_lat // sblk
    tok_per_blk = n_lat_rows // n_lat_blocks

    def emit(slot0, tok0, n_tok):
        rows = pl.ds(tok0, n_tok)
        slots = lax.broadcasted_iota(jnp.int32, (sblk, 1), 0) + slot0
        hits = [pos_ref[0, e:e + 1, rows] == slots for e in range(n_exp)]
        onehot = jnp.concatenate([jnp.where(h, 1.0, 0.0).astype(BF16) for h in hits], axis=0)
        res = _dot(onehot, h_ref[0, rows, :])
        for e in range(n_exp):
            o_ref[e, 0] = res[e * sblk:(e + 1) * sblk].astype(BF16)
            gsel = jnp.sum(jnp.where(hits[e], gate_ref[0, e:e + 1, rows], 0.0), axis=1, keepdims=True)
            g_ref[e, 0] = jnp.broadcast_to(gsel, (sblk, LANES))

    slot0 = sb * sblk
    w0 = jnp.clip(sb * tok_per_blk - margin, 0, n_lat_rows - win)
    tok0 = pl.multiple_of(n_ctx_rows + w0, LANES)
    first_blk = (n_ctx_rows + w0) // LANES
    fits = None
    for e in range(n_exp):
        ok = jnp.logical_and(lo_ref[0, e, first_blk] <= slot0,
                             hi_ref[0, e, first_blk + win // LANES - 1] >= slot0 + sblk)
        fits = ok if fits is None else jnp.logical_and(fits, ok)
    is_lat = sb < n_lat_blocks

    @pl.when(jnp.logical_and(is_lat, fits))
    def _():
        emit(slot0, tok0, win)

    @pl.when(jnp.logical_and(is_lat, jnp.logical_not(fits)))
    def _():
        emit(slot0, n_ctx_rows, n_lat_rows)

    if n_ctx_rows:
        @pl.when(jnp.logical_not(is_lat))
        def _():
            emit(cap_lat, 0, n_ctx_rows)


def _slot_window(tt, cap_lat, n_exp):
    return min(cap_lat, max(16, 2 * tt * EC_FACTOR // n_exp))


def _gather_call(lo, hi, pos, gate, h, cap_lat, cap_ctx, n_ctx_rows):
    b, e, t = pos.shape
    d = h.shape[2]
    n_slots = cap_lat + cap_ctx
    n_lat_rows = t - n_ctx_rows
    sblk = cap_ctx if cap_ctx else 32
    assert cap_lat % sblk == 0 and sblk % 16 == 0
    n_lat_blocks = cap_lat // sblk
    tok_per_blk = n_lat_rows // n_lat_blocks
    margin = tok_per_blk
    win = min(n_lat_rows, 3 * tok_per_blk)
    assert tok_per_blk % LANES == 0 and win % LANES == 0
    smem = lambda: pl.BlockSpec((1, e, LANES), lambda bi, si: (bi, 0, 0), memory_space=pltpu.SMEM)
    return pl.pallas_call(
        functools.partial(_gather_kernel, n_ctx_rows=n_ctx_rows, n_lat_rows=n_lat_rows, cap_lat=cap_lat,
                          sblk=sblk, win=win, margin=margin),
        grid=(b, n_slots // sblk),
        in_specs=[smem(), smem(),
                  pl.BlockSpec((1, e, t), lambda bi, si: (bi, 0, 0)),
                  pl.BlockSpec((1, e, t), lambda bi, si: (bi, 0, 0)),
                  pl.BlockSpec((1, t, d), lambda bi, si: (bi, 0, 0))],
        out_specs=[pl.BlockSpec((e, 1, sblk, d), lambda bi, si: (0, bi, si, 0)),
                   pl.BlockSpec((e, 1, sblk, LANES), lambda bi, si: (0, bi, si, 0))],
        out_shape=[jax.ShapeDtypeStruct((e, b, n_slots, d), BF16),
                   jax.ShapeDtypeStruct((e, b, n_slots, LANES), F32)],
        compiler_params=_cparams("parallel", "arbitrary"),
        name="gather",
    )(lo, hi, pos, gate, h)


def _ffn_up_kernel(x_ref, wg_ref, wu_ref, o_ref, wgb_ref, wub_ref):
    @pl.when(pl.program_id(2) == 0)
    def _():
        wgb_ref[...] = wg_ref[0, 0].astype(BF16)
        wub_ref[...] = wu_ref[0, 0].astype(BF16)

    x = x_ref[0]
    g = _dot(x, wgb_ref[...])
    u = _dot(x, wub_ref[...])
    o_ref[0] = (_silu(g) * u).astype(BF16)


def _ffn_down_kernel(a_ref, w_ref, gate_ref, o_ref, wb_ref):
    @pl.when(pl.program_id(2) == 0)
    def _():
        wb_ref[...] = w_ref[0, 0].astype(BF16)

    gate = jnp.concatenate([gate_ref[0]] * (o_ref.shape[2] // LANES), axis=1)
    o_ref[0] = (_dot(a_ref[0], wb_ref[...]) * gate).astype(BF16)


def _ffn_call(xs, gates, w_gu, w_dn, layer):
    e, m, d = xs.shape
    f = w_dn.shape[2]
    tm = _pick(m, (1152, 1024, 768, 512, 256, 128, 64, 32, 16, 8))
    tf = _pick(f, (512, 256, 128))
    nf = f // tf
    act = pl.pallas_call(
        _ffn_up_kernel,
        grid=(e, nf, m // tm),
        in_specs=[pl.BlockSpec((1, tm, d), lambda ei, j, i: (ei, i, 0)),
                  pl.BlockSpec((1, 1, d, tf), lambda ei, j, i: (layer, ei, 0, j)),
                  pl.BlockSpec((1, 1, d, tf), lambda ei, j, i: (layer, ei, 0, j + nf))],
        out_specs=pl.BlockSpec((1, tm, tf), lambda ei, j, i: (ei, i, j)),
        out_shape=jax.ShapeDtypeStruct((e, m, f), BF16),
        scratch_shapes=[pltpu.VMEM((d, tf), BF16), pltpu.VMEM((d, tf), BF16)],
        compiler_params=_cparams("parallel", "arbitrary", "arbitrary"),
        name="ffn_up",
    )(xs, w_gu, w_gu)
    tn = _pick(d, (1024, 512, 256, 128))
    return pl.pallas_call(
        _ffn_down_kernel,
        grid=(e, d // tn, m // tm),
        in_specs=[pl.BlockSpec((1, tm, f), lambda ei, j, i: (ei, i, 0)),
                  pl.BlockSpec((1, 1, f, tn), lambda ei, j, i: (layer, ei, 0, j)),
                  pl.BlockSpec((1, tm, LANES), lambda ei, j, i: (ei, i, 0))],
        out_specs=pl.BlockSpec((1, tm, tn), lambda ei, j, i: (ei, i, j)),
        out_shape=jax.ShapeDtypeStruct((e, m, d), BF16),
        scratch_shapes=[pltpu.VMEM((f, tn), BF16)],
        compiler_params=_cparams("parallel", "arbitrary", "arbitrary"),
        name="ffn_down",
    )(act, w_dn, gates)


def _combine_kernel(lo_ref, hi_ref, posn_ref, y_ref, x_ref, mod_ref, gf_ref, o_ref,
                    *, cap_lat, cap_ctx, n_ctx_tiles, blocks_per_tile, window, final_norm):
    n_exp = posn_ref.shape[2]
    i = pl.program_id(1)
    lane = lax.broadcasted_iota(jnp.int32, (1, n_exp), 1)
    posn = posn_ref[0].astype(F32)

    def pcol(e):
        return jnp.sum(jnp.where(lane == e, posn, 0.0), axis=1, keepdims=True)

    def scatter(slot0, cap):
        slots = lax.broadcasted_iota(jnp.int32, (1, cap), 1).astype(F32) + slot0
        hots = [jnp.where(pcol(e) == slots, 1.0, 0.0).astype(BF16) for e in range(n_exp)]
        rows = [y_ref[e, 0, slot0:slot0 + cap, :] for e in range(n_exp)]
        return _dot(jnp.concatenate(hots, axis=1), jnp.concatenate(rows, axis=0))

    def scatter_window(starts):
        w_tot = n_exp * window
        grp = lax.broadcasted_iota(jnp.int32, (n_exp, w_tot), 1) // window
        spread = jnp.where(grp == lax.broadcasted_iota(jnp.int32, (n_exp, w_tot), 0), 1.0, 0.0).astype(BF16)
        bpos = _dot(posn.astype(BF16), spread)
        wlane = lax.broadcasted_iota(jnp.int32, (1, w_tot), 1)
        target = (wlane % window).astype(F32)
        for e, st in enumerate(starts):
            target = jnp.where(wlane // window == e, target + st.astype(F32), target)
        onehot = jnp.where(bpos == target, 1.0, 0.0).astype(BF16)
        rows = [y_ref[e, 0, pl.ds(pl.multiple_of(st, 16), window), :] for e, st in enumerate(starts)]
        return _dot(onehot, jnp.concatenate(rows, axis=0))

    def finish(total, g2):
        out = x_ref[0] + g2 * total
        if final_norm:
            out = _rms(out) * gf_ref[...]
        o_ref[0] = out

    blk0 = i * blocks_per_tile
    starts = []
    fits = None
    for e in range(n_exp):
        lo = lo_ref[0, e, blk0]
        hi = hi_ref[0, e, blk0 + blocks_per_tile - 1]
        st = jnp.minimum(lax.shift_left(lax.shift_right_logical(lo, 4), 4), cap_lat - window)
        ok = hi - st <= window
        fits = ok if fits is None else jnp.logical_and(fits, ok)
        starts.append(st)
    is_lat = i >= n_ctx_tiles

    @pl.when(jnp.logical_and(is_lat, fits))
    def _():
        finish(scatter_window(starts), mod_ref[0, 1, 5:6, :])

    @pl.when(jnp.logical_and(is_lat, jnp.logical_not(fits)))
    def _():
        finish(scatter(0, cap_lat), mod_ref[0, 1, 5:6, :])

    if n_ctx_tiles:
        @pl.when(jnp.logical_not(is_lat))
        def _():
            finish(scatter(cap_lat, cap_ctx), mod_ref[0, 0, 5:6, :])


def _combine_call(lo, hi, posn, y, xa, mod, g_final, cap_lat, cap_ctx, n_ctx_rows, final_norm):
    b, t, d = xa.shape
    e = posn.shape[2]
    n_slots = y.shape[2]
    tt = ROW_TILE
    assert n_ctx_rows % tt == 0 and t % tt == 0
    window = _slot_window(tt, cap_lat, e)
    smem = lambda: pl.BlockSpec((1, e, LANES), lambda bi, i: (bi, 0, 0), memory_space=pltpu.SMEM)
    return pl.pallas_call(
        functools.partial(_combine_kernel, cap_lat=cap_lat, cap_ctx=cap_ctx, n_ctx_tiles=n_ctx_rows // tt,
                          blocks_per_tile=tt // LANES, window=window, final_norm=final_norm),
        grid=(b, t // tt),
        in_specs=[smem(), smem(),
                  pl.BlockSpec((1, tt, e), lambda bi, i: (bi, i, 0)),
                  pl.BlockSpec((e, 1, n_slots, d), lambda bi, i: (0, bi, 0, 0)),
                  pl.BlockSpec((1, tt, d), lambda bi, i: (bi, i, 0)),
                  pl.BlockSpec((1, 2, 6, d), lambda bi, i: (bi, 0, 0, 0)),
                  pl.BlockSpec((1, d), lambda bi, i: (0, 0))],
        out_specs=pl.BlockSpec((1, tt, d), lambda bi, i: (bi, i, 0)),
        out_shape=jax.ShapeDtypeStruct((b, t, d), F32),
        compiler_params=_cparams("parallel", "arbitrary"),
        name="combine",
    )(lo, hi, posn, y, xa, mod, g_final.reshape(1, d))


def _rope_tables(n_tokens, n_ctx_rows, head_dim):
    rows = n_tokens // GRID_W
    r = jnp.repeat(jnp.arange(rows, dtype=F32), GRID_W)
    col = jnp.tile(jnp.arange(GRID_W, dtype=F32), rows)
    axis_dim = head_dim // 2
    inv = ROPE_THETA ** (-jnp.arange(0, axis_dim, 2, dtype=F32) / axis_dim)
    ar, ac = r[:, None] * inv, col[:, None] * inv
    cos_p = jnp.concatenate([jnp.cos(ar), jnp.cos(ar), jnp.cos(ac), jnp.cos(ac)], axis=-1)
    sin_p = jnp.concatenate([-jnp.sin(ar), jnp.sin(ar), -jnp.sin(ac), jnp.sin(ac)], axis=-1)
    reps = LANES // head_dim
    cos_p, sin_p = jnp.tile(cos_p, (1, reps)), jnp.tile(sin_p, (1, reps))
    cos_t = jnp.concatenate([jnp.ones((n_ctx_rows, LANES), F32), cos_p], axis=0)
    sin_t = jnp.concatenate([jnp.zeros((n_ctx_rows, LANES), F32), sin_p], axis=0)
    return cos_t, sin_t


def _moe(h2, aff_t, x_res, mod, g_final, w_gu, w_dn, layer, segs, cap_lat, cap_ctx, n_ctx_rows, final_norm):
    b, t, d = h2.shape
    e = aff_t.shape[1]
    n_slots = cap_lat + cap_ctx
    pos, gate, posn, lo, hi = _route_call(aff_t, segs)
    xs, gs = _gather_call(lo, hi, pos, gate, h2, cap_lat, cap_ctx, n_ctx_rows)
    y = _ffn_call(xs.reshape(e, b * n_slots, d), gs.reshape(e, b * n_slots, LANES), w_gu, w_dn, layer)
    return _combine_call(lo, hi, posn, y.reshape(e, b, n_slots, d), x_res, mod, g_final, cap_lat, cap_ctx,
                         n_ctx_rows, final_norm)


def kernel(x, c, ctx, c_ctx, w_ada, b_ada, g_norm_mix, g_norm_ffn, w_in_even, gla_gate_w, gla_gate_b, gla_norm_g, swa_sink, w_out_even, w_qkv_odd, diff_lambda, diff_subln_g, w_out_odd, w_router, w_gate_up, w_down, g_final):
    b, s_len, d = x.shape
    lc = ctx.shape[1]
    depth = w_ada.shape[0]
    assert depth == 2 and lc % ROW_TILE == 0 and s_len % ROW_TILE == 0 and s_len % GRID_W == 0
    t = lc + s_len
    ncb = lc // ROW_TILE
    gla_dk, gla_dv = d // 4 // GLA_HEADS, d // 2 // GLA_HEADS
    swa_hd = d // 2 // SWA_HEADS
    diff_heads = d // (2 * DIFF_HD)
    n_exp = w_router.shape[2]
    assert gla_dk == LANES and swa_hd == LANES and 2 * DIFF_HD == LANES

    n_rows = -(-(b + 1) // 8) * 8
    cc = jnp.concatenate([c, c_ctx[None, :], jnp.zeros((n_rows - b - 1, d), F32)], axis=0)
    m_all = _ada_call(cc, w_ada, b_ada).reshape(depth, n_rows, 6, d)

    def mod_for(layer):
        lat = m_all[layer, :b]
        ctxm = jnp.broadcast_to(m_all[layer, b][None], (b, 6, d))
        return jnp.stack([ctxm, lat], axis=1)

    xa = (ctx, x)

    mod0 = mod_for(0)
    w_in = w_in_even[0]
    sizes = (GLA_HEADS * gla_dk, GLA_HEADS * gla_dk, GLA_HEADS * gla_dv, GLA_HEADS * gla_dv,
             GLA_GATE_RANK, GLA_GATE_RANK, SWA_HEADS * swa_hd, SWA_KV_HEADS * swa_hd, SWA_KV_HEADS * swa_hd)
    offs = np.concatenate([[0], np.cumsum(sizes)])
    seg = lambda i: w_in[:, offs[i]:offs[i + 1]]
    w_main = jnp.concatenate([seg(0), seg(1), seg(2), seg(3), seg(6), seg(7), seg(8)], axis=1).astype(BF16)
    w_rank = jnp.concatenate([seg(4), seg(5), jnp.zeros((d, LANES - 2 * GLA_GATE_RANK), F32)],
                             axis=1).astype(BF16)
    col_aq, col_ak = 0, sizes[0]
    col_av = col_ak + sizes[1]
    col_ag = col_av + sizes[2]
    col_bq = col_ag + sizes[3]
    col_bk = col_bq + sizes[6]
    col_bv = col_bk + sizes[7]
    n_main = col_bv + sizes[8]
    swa_q_scale = float(swa_hd) ** -0.5 * float(np.log2(np.e))
    rope0 = [(col_bq <= ch * COL_CHUNK < col_bv, swa_q_scale if col_bq <= ch * COL_CHUNK < col_bk else 1.0)
             for ch in range(n_main // COL_CHUNK)]
    cos_b, sin_b = _rope_tables(s_len, lc, swa_hd)
    p0, r0 = _proj_call(xa, mod0, g_norm_mix[0], w_main, cos_b, sin_b, rope0, swa_hd // 4, ncb, wr=w_rank)

    gw = gla_gate_w[0]
    gw_pad = jnp.zeros((2, LANES, GLA_HEADS * gla_dk), F32)
    gw_pad = gw_pad.at[0, 0:GLA_GATE_RANK].set(gw[0]).at[1, GLA_GATE_RANK:2 * GLA_GATE_RANK].set(gw[1])
    a_out = _gla_call(p0, r0, gw_pad.astype(BF16), gla_gate_b[0].reshape(2, 1, -1), gla_norm_g[0],
                      _gla_constants(), lc, col_aq, col_ak, col_av, col_ag, gla_dk, gla_dv)
    sink_col = jnp.broadcast_to(
        jnp.repeat(swa_sink[0].reshape(SWA_KV_HEADS, SWA_GROUP), SWA_BLOCK, axis=1)[:, :, None],
        (SWA_KV_HEADS, SWA_GROUP * SWA_BLOCK, LANES))
    b_out = _swa_call(p0, sink_col, lc, col_bq, col_bk, col_bv, swa_hd)

    w_o = w_out_even[0].astype(BF16)
    n_a = GLA_HEADS * gla_dv
    xa1, h2, aff_t = _outproj_call([a_out, b_out], [w_o[:n_a], w_o[n_a:]], xa, mod0, g_norm_ffn[0],
                                   w_router[0], 0, t, ncb)
    cap_lat = s_len * EC_FACTOR // n_exp
    cap_ctx = lc * EC_FACTOR // n_exp
    segs0 = [(lc, s_len, cap_lat, 0), (0, lc, cap_ctx, cap_lat)]
    xa2 = _moe(h2, aff_t, xa1, mod0, g_final, w_gate_up, w_down, 0, segs0, cap_lat, cap_ctx, lc, False)

    mod1 = mod_for(1)
    lambda_init = 0.8 - 0.6 * float(np.exp(-0.3 * 1))
    w_qkv = w_qkv_odd[0].astype(BF16)
    n_qkv = w_qkv.shape[1]
    q_scale = DIFF_HD ** -0.5 * float(np.log2(np.e))
    rope1 = [(ch * COL_CHUNK < 2 * d, q_scale if ch * COL_CHUNK < d else 1.0)
             for ch in range(n_qkv // COL_CHUNK)]
    cos_c, sin_c = _rope_tables(s_len, lc, DIFF_HD)
    (p1,) = _proj_call(xa2, mod1, g_norm_mix[1], w_qkv, cos_c, sin_c, rope1, DIFF_HD // 4, ncb)
    y1 = _diff_call(p1, diff_lambda[0], diff_subln_g[0], lc, diff_heads, lambda_init)
    x3, h2b, aff_tb = _outproj_call([y1], [w_out_odd[0].astype(BF16)], xa2, mod1, g_norm_ffn[1],
                                    w_router[1], ncb, s_len, ncb)
    segs1 = [(0, s_len, cap_lat, 0)]
    return _moe(h2b, aff_tb, x3, mod1, g_final, w_gate_up, w_down, 1, segs1, cap_lat, 0, 0, True)
```

```python
import functools

import numpy as np
import jax
import jax.numpy as jnp
from jax import lax
from jax.experimental import pallas as pl
from jax.experimental.pallas import tpu as pltpu

F32 = jnp.float32
BF16 = jnp.bfloat16

GRID_W = 64
ROPE_THETA = 10000.0
NORM_EPS = 1e-6
GLA_HEADS = 4
GLA_GATE_RANK = 16
GLA_GATE_NORM = 16.0
SWA_HEADS = 8
SWA_KV_HEADS = 2
SWA_GROUP = SWA_HEADS // SWA_KV_HEADS
SWA_WINDOW = 128
SWA_BLOCK = 128
DIFF_HD = 64
N_EXPERTS = 16
EC_FACTOR = 2

LANES = 128
ROW_TILE = 256
COL_CHUNK = 256
GLA_CHUNK = 256
GLA_LEVELS = 8
GLA_SAFE_RANGE = 60.0
DIFF_QBLOCK = 256
DIFF_SAFE_BOUND = 60.0
DIFF_BOUND_SLACK = 1.0 + 2.0 ** -6
VMEM_LIMIT = 56 * 1024 * 1024
NEG_BIG = -1e30


def _cparams(*sem):
    return pltpu.CompilerParams(dimension_semantics=sem, vmem_limit_bytes=VMEM_LIMIT)


def _pick(n, cands):
    for c in cands:
        if n % c == 0:
            return c
    raise ValueError(f"no tile for {n} in {cands}")


def _nt(a, b):
    return lax.dot_general(a, b, (((1,), (1,)), ((), ())), preferred_element_type=F32)


def _tn(a, b):
    return lax.dot_general(a, b, (((0,), (0,)), ((), ())), preferred_element_type=F32)


def _dot(a, b):
    return jnp.dot(a, b, preferred_element_type=F32)


def _split3(a):
    hi = a.astype(BF16)
    r1 = a - hi.astype(F32)
    mid = r1.astype(BF16)
    lo = (r1 - mid.astype(F32)).astype(BF16)
    return hi, mid, lo


def _split2(a):
    hi = a.astype(BF16)
    lo = (a - hi.astype(F32)).astype(BF16)
    return hi, lo


def _silu(x):
    return x * (1.0 / (1.0 + jnp.exp(-x)))


def _rms(x, eps=NORM_EPS):
    return x * lax.rsqrt(jnp.mean(x * x, axis=-1, keepdims=True) + eps)


def _row_specs(stream, n_ctx_blocks, off=0, stride=1):
    if isinstance(stream, tuple):
        assert stride == 1
        ctx, x = stream
        d = x.shape[2]
        return ([pl.BlockSpec((1, ROW_TILE, d), lambda bi, i: (bi, jnp.minimum(i + off, n_ctx_blocks - 1), 0)),
                 pl.BlockSpec((1, ROW_TILE, d), lambda bi, i: (bi, jnp.maximum(i + off - n_ctx_blocks, 0), 0))],
                [ctx, x])
    d = stream.shape[2]
    return [pl.BlockSpec((1, ROW_TILE, d), lambda bi, i: (bi, i * stride + off, 0))], [stream]


def _load_rows(row_refs, n_ctx_blocks, off=0):
    if len(row_refs) == 2:
        return jnp.where(pl.program_id(1) + off < n_ctx_blocks, row_refs[0][0], row_refs[1][0])
    return row_refs[0][0]


def _ada_kernel(c_ref, w_ref, b_ref, o_ref):
    s = _silu(c_ref[...])
    s_hi, s_lo = _split2(s)
    w_hi, w_lo = _split2(w_ref[0])
    acc = _dot(s_hi, w_hi) + (_dot(s_lo, w_hi) + _dot(s_hi, w_lo))
    o_ref[0] = acc + b_ref[0]


def _ada_call(cc, w_ada, b_ada):
    depth, d, n = w_ada.shape
    r = cc.shape[0]
    tn = _pick(n, (768, 512, 256, 128))
    return pl.pallas_call(
        _ada_kernel,
        grid=(depth, n // tn),
        in_specs=[pl.BlockSpec((r, d), lambda l, j: (0, 0)),
                  pl.BlockSpec((1, d, tn), lambda l, j: (l, 0, j)),
                  pl.BlockSpec((1, 1, tn), lambda l, j: (l, 0, j))],
        out_specs=pl.BlockSpec((1, r, tn), lambda l, j: (l, 0, j)),
        out_shape=jax.ShapeDtypeStruct((depth, r, n), F32),
        compiler_params=_cparams("parallel", "parallel"),
        name="adaln",
    )(cc, w_ada, b_ada.reshape(depth, 1, n))


def _proj_kernel(*refs, n_row_refs, n_ctx_blocks, chunk_ops, quarter, has_r):
    row_refs = refs[:n_row_refs]
    mod_ref, g_ref, w_ref, cos_ref, sin_ref = refs[n_row_refs:n_row_refs + 5]
    rest = refs[n_row_refs + 5:]
    if has_r:
        wr_ref, o_ref, r_ref = rest
    else:
        (o_ref,) = rest
    x = _load_rows(row_refs, n_ctx_blocks)
    y = _rms(x) * g_ref[...]
    h = (y * (1.0 + mod_ref[0, 0, 1:2, :]) + mod_ref[0, 0, 0:1, :]).astype(BF16)
    reps = COL_CHUNK // LANES
    cos_t = jnp.concatenate([cos_ref[...]] * reps, axis=1)
    sin_t = jnp.concatenate([sin_ref[...]] * reps, axis=1)
    lane = lax.broadcasted_iota(jnp.int32, (1, COL_CHUNK), 1)
    first = (lane % (2 * quarter)) < quarter
    for c, (rope, scale) in enumerate(chunk_ops):
        acc = _dot(h, w_ref[:, c * COL_CHUNK:(c + 1) * COL_CHUNK])
        if rope:
            partner = jnp.where(first, pltpu.roll(acc, COL_CHUNK - quarter, 1), pltpu.roll(acc, quarter, 1))
            acc = acc * cos_t + partner * sin_t
        if scale != 1.0:
            acc = acc * scale
        o_ref[0, :, c * COL_CHUNK:(c + 1) * COL_CHUNK] = acc.astype(BF16)
    if has_r:
        r_ref[0] = _dot(h, wr_ref[...]).astype(BF16)


def _proj_call(stream, mod, g, w, cos_t, sin_t, chunk_ops, quarter, n_ctx_blocks, wr=None):
    d, n = w.shape
    t = cos_t.shape[0]
    b = mod.shape[0]
    assert n == len(chunk_ops) * COL_CHUNK and t % ROW_TILE == 0
    has_r = wr is not None
    row_specs, row_args = _row_specs(stream, n_ctx_blocks)
    in_specs = row_specs + [
        pl.BlockSpec((1, 1, 6, d), lambda bi, i: (bi, jnp.where(i < n_ctx_blocks, 0, 1), 0, 0)),
        pl.BlockSpec((1, d), lambda bi, i: (0, 0)),
        pl.BlockSpec((d, n), lambda bi, i: (0, 0), pipeline_mode=pl.Buffered(1)),
        pl.BlockSpec((ROW_TILE, LANES), lambda bi, i: (i, 0)),
        pl.BlockSpec((ROW_TILE, LANES), lambda bi, i: (i, 0))]
    args = row_args + [mod, g.reshape(1, d), w, cos_t, sin_t]
    out_specs = [pl.BlockSpec((1, ROW_TILE, n), lambda bi, i: (bi, i, 0))]
    out_shape = [jax.ShapeDtypeStruct((b, t, n), BF16)]
    if has_r:
        in_specs.append(pl.BlockSpec((d, LANES), lambda bi, i: (0, 0)))
        args.append(wr)
        out_specs.append(pl.BlockSpec((1, ROW_TILE, LANES), lambda bi, i: (bi, i, 0)))
        out_shape.append(jax.ShapeDtypeStruct((b, t, LANES), BF16))
    return pl.pallas_call(
        functools.partial(_proj_kernel, n_row_refs=len(row_args), n_ctx_blocks=n_ctx_blocks,
                          chunk_ops=tuple(chunk_ops), quarter=quarter, has_r=has_r),
        grid=(b, t // ROW_TILE),
        in_specs=in_specs, out_specs=out_specs, out_shape=out_shape,
        compiler_params=_cparams("parallel", "parallel"),
        name="norm_proj",
    )(*args)


def _gla_constants():
    c = GLA_CHUNK
    idx = np.arange(c)
    cm = np.zeros((2, GLA_LEVELS + 2, c, c), np.float32)
    lmask = np.zeros((2, GLA_LEVELS + 2, c, c), np.float32)
    cm[0, 0] = (idx[None, :] <= idx[:, None])
    cm[1, 0] = (idx[None, :] >= idx[:, None])
    for l in range(GLA_LEVELS):
        s = c >> (l + 1)
        blk = idx // (2 * s)
        second = (idx % (2 * s)) >= s
        same = blk[:, None] == blk[None, :]
        sep_f = blk * 2 * s + s - 1
        cm[0, 1 + l] = (idx[None, :] <= sep_f[:, None])
        lmask[0, l] = same & second[:, None] & (~second)[None, :]
        sep_b = blk * 2 * s + s
        cm[1, 1 + l] = (idx[None, :] >= sep_b[:, None])
        lmask[1, l] = same & (~second)[:, None] & second[None, :]
    cm[:, GLA_LEVELS + 1] = 1.0
    lmask[:, GLA_LEVELS] = np.eye(c)
    lmask[:, GLA_LEVELS + 1] = cm[:, 0]
    return jnp.asarray(cm.reshape(2, (GLA_LEVELS + 2) * c, c), BF16), jnp.asarray(lmask, F32)


def _gla_kernel(q_ref, k_ref, v_ref, ag_ref, r_ref, gw_ref, gb_ref, ng_ref, cm_ref, lm_ref,
                o_ref, la_ref, of_ref, ob_ref, qe_ref, u_ref, ss_ref, dec_ref, st_ref,
                *, n_ctx_chunks, n_chunks, q_scale):
    c = GLA_CHUNK
    dv = st_ref.shape[1]
    r = r_ref[0]
    for d in range(2):
        z = _dot(r, gw_ref[d]) + gb_ref[d]
        la_ref[d] = (jnp.minimum(z, 0.0) - jnp.log(1.0 + jnp.exp(-jnp.abs(z)))) * (1.0 / GLA_GATE_NORM)
    st_ref[...] = jnp.zeros_like(st_ref)
    out_refs = (of_ref, ob_ref)

    def chunk_rows(ci):
        return pl.ds(pl.multiple_of(ci * c, c), c)

    def state_rows(ci):
        return pl.ds(pl.multiple_of(ci * dv, dv), dv)

    min_tot = None
    for d in range(2):
        m = jnp.min(jnp.sum(la_ref[d].reshape(n_chunks, c, la_ref.shape[2]), axis=1))
        min_tot = m if min_tot is None else jnp.minimum(min_tot, m)
    is_safe = min_tot >= -GLA_SAFE_RANGE

    group = 3 if n_chunks % 3 == 0 else 2

    def prep_fast(cp, carry):
        jobs = [(group * cp + u, d) for u in range(group) for d in range(2)]
        rows = [chunk_rows(ci) for ci, _ in jobs]
        qs = {u: q_ref[0, chunk_rows(group * cp + u), :].astype(F32) * q_scale for u in range(group)}
        ks = {u: k_ref[0, chunk_rows(group * cp + u), :].astype(F32) for u in range(group)}
        bcs = {}
        for j, (ci, d) in enumerate(jobs):
            g_hi, g_lo = _split2(la_ref[d, rows[j], :])
            cm = cm_ref[d, 0:c, :]
            bcs[j] = _dot(cm, g_hi) + _dot(cm, g_lo)
        qes, kins, decs = {}, {}, {}
        for j, (ci, d) in enumerate(jobs):
            bc = bcs[j]
            qes[j] = (qs[j // 2] * jnp.exp(bc)).astype(BF16)
            kins[j] = ks[j // 2] * jnp.exp(-bc)
            decs[j] = jnp.exp(bc[c - 1:c] if d == 0 else bc[0:1])
        atts = {j: _nt(qes[j], kins[j].astype(BF16)) for j in range(len(jobs))}
        for j, (ci, d) in enumerate(jobs):
            att = (lm_ref[d, GLA_LEVELS + 1] * atts[j]).astype(BF16)
            out_refs[d][rows[j], :] = _dot(att, v_ref[0, rows[j], :])
            qe_ref[d, rows[j], :] = qes[j]
            u_ref[d, state_rows(ci), :] = _tn(v_ref[0, rows[j], :], (kins[j] * decs[j]).astype(BF16))
            dec_ref[d, pl.ds(pl.multiple_of(ci * 8, 8), 8), :] = jnp.broadcast_to(decs[j], (8, decs[j].shape[1]))
        return carry

    def prep(ci, carry):
        rows = chunk_rows(ci)
        q = q_ref[0, rows, :].astype(F32) * q_scale
        k = k_ref[0, rows, :].astype(F32)
        v = v_ref[0, rows, :]
        qk = _nt(q.astype(BF16), k.astype(BF16))
        for d in range(2):
            g_hi, g_lo = _split2(la_ref[d, rows, :])
            cm = cm_ref[d]
            cum = _dot(cm, g_hi) + _dot(cm, g_lo)
            bc = cum[0:c]
            tot = cum[(GLA_LEVELS + 1) * c:(GLA_LEVELS + 2) * c]
            att = lm_ref[d, GLA_LEVELS] * qk
            for l in range(GLA_LEVELS):
                ref_l = cum[(1 + l) * c:(2 + l) * c]
                eq = jnp.exp(jnp.minimum(bc - ref_l, 0.0))
                ek = jnp.exp(jnp.minimum(ref_l - bc, 0.0))
                att = att + lm_ref[d, l] * _nt((q * eq).astype(BF16), (k * ek).astype(BF16))
            out_refs[d][rows, :] = _dot(att.astype(BF16), v)
            qe_ref[d, rows, :] = (q * jnp.exp(bc)).astype(BF16)
            u_ref[d, state_rows(ci), :] = _tn(v, (k * jnp.exp(tot - bc)).astype(BF16))
            dec_ref[d, pl.ds(pl.multiple_of(ci * 8, 8), 8), :] = jnp.exp(tot[0:8])
        return carry

    @pl.when(is_safe)
    def _():
        lax.fori_loop(0, n_chunks // group, prep_fast, 0)

    @pl.when(jnp.logical_not(is_safe))
    def _():
        lax.fori_loop(0, n_chunks, prep, 0)

    def scan(t, carry):
        cb = jnp.where(t < n_ctx_chunks, n_ctx_chunks - 1 - t, n_chunks - 1 - (t - n_ctx_chunks))
        for d, ci in ((0, t), (1, cb)):
            st = st_ref[d]
            ss_ref[d, state_rows(ci), :] = st.astype(BF16)
            dec = dec_ref[d, pl.ds(pl.multiple_of(ci * 8, 8), 1), :]
            st_ref[d] = st * dec + u_ref[d, state_rows(ci), :]
        return carry

    lax.fori_loop(0, n_chunks, scan, 0)

    def inter(ci, carry):
        rows = chunk_rows(ci)
        for d in range(2):
            out_refs[d][rows, :] += _nt(qe_ref[d, rows, :], ss_ref[d, state_rows(ci), :])
        return carry

    lax.fori_loop(0, n_chunks, inter, 0, unroll=2 * group)
    o = of_ref[...] + ob_ref[...]
    ag = ag_ref[0].astype(F32)
    o_ref[0] = (_rms(o) * ng_ref[...] * _silu(ag)).astype(BF16)


def _gla_call(p0, r, gw_pad, gb, norm_g, consts, n_ctx_rows, col_q, col_k, col_v, col_g, dk, dv):
    b, t, _ = p0.shape
    n_chunks = t // GLA_CHUNK
    cm, lm = consts
    kern = functools.partial(_gla_kernel, n_ctx_chunks=n_ctx_rows // GLA_CHUNK, n_chunks=n_chunks,
                             q_scale=float(dk) ** -0.5)
    full = lambda shape: pl.BlockSpec(shape, lambda bi, h: (0,) * len(shape))
    return pl.pallas_call(
        kern,
        grid=(b, GLA_HEADS),
        in_specs=[pl.BlockSpec((1, t, dk), lambda bi, h: (bi, 0, col_q // dk + h)),
                  pl.BlockSpec((1, t, dk), lambda bi, h: (bi, 0, col_k // dk + h)),
                  pl.BlockSpec((1, t, dv), lambda bi, h: (bi, 0, col_v // dv + h)),
                  pl.BlockSpec((1, t, dv), lambda bi, h: (bi, 0, col_g // dv + h)),
                  pl.BlockSpec((1, t, LANES), lambda bi, h: (bi, 0, 0)),
                  pl.BlockSpec((2, LANES, dk), lambda bi, h: (0, 0, h)),
                  pl.BlockSpec((2, 1, dk), lambda bi, h: (0, 0, h)),
                  full((1, dv)), full(cm.shape), full(lm.shape)],
        out_specs=pl.BlockSpec((1, t, dv), lambda bi, h: (bi, 0, h)),
        out_shape=jax.ShapeDtypeStruct((b, t, GLA_HEADS * dv), BF16),
        scratch_shapes=[pltpu.VMEM((2, t, dk), F32), pltpu.VMEM((t, dv), F32), pltpu.VMEM((t, dv), F32),
                        pltpu.VMEM((2, t, dk), BF16), pltpu.VMEM((2, n_chunks * dv, dk), F32),
                        pltpu.VMEM((2, n_chunks * dv, dk), BF16),
                        pltpu.VMEM((2, n_chunks * 8, dk), F32), pltpu.VMEM((2, dv, dk), F32)],
        compiler_params=_cparams("parallel", "parallel"),
        name="gla",
    )(p0, p0, p0, p0, r, gw_pad, gb, norm_g.reshape(1, dv), cm, lm)


def _swa_kernel(q_ref, k_ref, v_ref, sink_ref, o_ref, s_ref, *, n_ctx_rows, n_blocks):
    blk = SWA_BLOCK
    win = 3 * blk
    ncb = n_ctx_rows // blk
    nlb = n_blocks - ncb
    sink = sink_ref[0, :, 0:1] * float(np.log2(np.e))
    rel0 = (lax.broadcasted_iota(jnp.int32, (blk, win), 1) - lax.broadcasted_iota(jnp.int32, (blk, win), 0))
    no_bias = jnp.zeros((blk, n_ctx_rows), F32)

    def key_rows(j):
        kb0 = jnp.clip(j - ncb - 1, 0, nlb - 3)
        return kb0, pl.ds(pl.multiple_of(n_ctx_rows + kb0 * blk, blk), win)

    def scores(j, slot):
        kb0, rows = key_rows(j)
        q = q_ref[0, pl.ds(pl.multiple_of(j * blk, blk), blk), :]
        q4 = jnp.concatenate([q[:, g * blk:(g + 1) * blk] for g in range(SWA_GROUP)], axis=0)
        keys = jnp.concatenate([k_ref[0, rows, :], k_ref[0, 0:n_ctx_rows, :]], axis=0)
        rel = rel0 + jnp.where(j < ncb, 4 * win, (kb0 - (j - ncb)) * blk)
        bias = jnp.concatenate([jnp.where(jnp.abs(rel) <= SWA_WINDOW, 0.0, NEG_BIG), no_bias], axis=1)
        s_ref[slot] = _nt(q4, keys) + jnp.concatenate([bias] * SWA_GROUP, axis=0)

    def finish(j, slot):
        _, rows = key_rows(j)
        vals = jnp.concatenate([v_ref[0, rows, :], v_ref[0, 0:n_ctx_rows, :]], axis=0)
        s = s_ref[slot]
        m = jnp.maximum(jnp.max(s, axis=1, keepdims=True), sink)
        e = jnp.exp2(s - m)
        den = jnp.sum(e, axis=1, keepdims=True) + jnp.exp2(sink - m)
        o = _dot(e.astype(BF16), vals) / den
        o_ref[0, pl.ds(pl.multiple_of(j * blk, blk), blk), :] = jnp.concatenate(
            [o[g * blk:(g + 1) * blk] for g in range(SWA_GROUP)], axis=1).astype(BF16)

    scores(0, 0)

    def body(jj, carry):
        j0 = 2 * jj
        scores(j0 + 1, 1)
        finish(j0, 0)
        scores(j0 + 2, 0)
        finish(j0 + 1, 1)
        return carry

    lax.fori_loop(0, n_blocks // 2 - 1, body, 0)
    scores(n_blocks - 1, 1)
    finish(n_blocks - 2, 0)
    finish(n_blocks - 1, 1)


def _swa_call(p0, sink_col, n_ctx_rows, col_q, col_k, col_v, hd):
    b, t, _ = p0.shape
    blk = SWA_BLOCK
    nb = t // blk
    gq = SWA_GROUP * hd
    assert nb % 2 == 0 and nb - n_ctx_rows // blk >= 3 and SWA_WINDOW <= blk
    return pl.pallas_call(
        functools.partial(_swa_kernel, n_ctx_rows=n_ctx_rows, n_blocks=nb),
        grid=(b, SWA_KV_HEADS),
        in_specs=[pl.BlockSpec((1, t, gq), lambda bi, h: (bi, 0, col_q // gq + h)),
                  pl.BlockSpec((1, t, hd), lambda bi, h: (bi, 0, col_k // hd + h)),
                  pl.BlockSpec((1, t, hd), lambda bi, h: (bi, 0, col_v // hd + h)),
                  pl.BlockSpec((1, SWA_GROUP * blk, LANES), lambda bi, h: (h, 0, 0))],
        out_specs=pl.BlockSpec((1, t, gq), lambda bi, h: (bi, 0, h)),
        out_shape=jax.ShapeDtypeStruct((b, t, SWA_HEADS * hd), BF16),
        scratch_shapes=[pltpu.VMEM((2, SWA_GROUP * blk, 3 * blk + n_ctx_rows), F32)],
        compiler_params=_cparams("parallel", "parallel"),
        name="swa",
    )(p0, p0, p0, sink_col)


def _diff_kernel(lam_ref, q_ref, k_ref, v_ref, g_ref, o_ref, s_ref, kx_ref, vx_ref, qn_ref,
                 *, row_off, n_qblocks, lambda_init):
    tq = DIFF_QBLOCK
    hd2 = 2 * DIFF_HD
    n_keys = k_ref.shape[1]
    lam = lam_ref[...]
    s01 = jnp.sum(lam[0:1] * lam[1:2], axis=1, keepdims=True)
    s23 = jnp.sum(lam[2:3] * lam[3:4], axis=1, keepdims=True)
    lam_full = jnp.exp(s01) - jnp.exp(s23) + lambda_init
    lane = lax.broadcasted_iota(jnp.int32, (1, hd2), 1)
    first = lane < DIFF_HD
    col0 = lane == 0
    zero = jnp.zeros((), BF16)

    d_idx = lax.broadcasted_iota(jnp.int32, (hd2, 2 * hd2), 0)
    j_idx = lax.broadcasted_iota(jnp.int32, (hd2, 2 * hd2), 1)
    map_sum = jnp.where((d_idx < DIFF_HD) == (j_idx < hd2), 1.0, 0.0).astype(BF16)

    def sq_norms(x):
        xf = x.astype(F32)
        return _dot((xf * xf).astype(BF16), map_sum)

    kn = sq_norms(k_ref[0])
    kmax0 = jnp.sqrt(jnp.max(kn[:, 0:hd2], axis=0, keepdims=True)) * DIFF_BOUND_SLACK
    kmax1 = jnp.sqrt(jnp.max(kn[:, hd2:2 * hd2], axis=0, keepdims=True)) * DIFF_BOUND_SLACK
    qn_ref[...] = sq_norms(q_ref[0, row_off:, :])
    qmax0 = jnp.sqrt(jnp.max(qn_ref[:, 0:hd2], axis=0, keepdims=True))
    qmax1 = jnp.sqrt(jnp.max(qn_ref[:, hd2:2 * hd2], axis=0, keepdims=True))
    is_safe = jnp.max(jnp.maximum(qmax0 * kmax0, qmax1 * kmax1)) <= DIFF_SAFE_BOUND

    def q_block(j):
        q = q_ref[0, pl.ds(pl.multiple_of(row_off + j * tq, tq), tq), :]
        return q, jnp.where(first, q, zero), jnp.where(first, zero, q)

    def write_out(j, o):
        y = _rms(o) * g_ref[...] * (1.0 - lambda_init)
        o_ref[0, pl.ds(pl.multiple_of(j * tq, tq), tq), :] = y.astype(BF16)

    def scores_fast(j, slot):
        _, qa, qb = q_block(j)
        qn = jnp.sqrt(qn_ref[pl.ds(pl.multiple_of(j * tq, tq), tq), :])
        xa = jnp.where(col0, -(qn[:, 0:hd2] * kmax0), 0.0).astype(BF16)
        xb = jnp.where(col0, -(qn[:, hd2:2 * hd2] * kmax1), 0.0).astype(BF16)
        qq = jnp.concatenate([jnp.concatenate([qa, xa], axis=1), jnp.concatenate([qb, xb], axis=1)], axis=0)
        s_ref[slot] = _nt(qq, kx_ref[...])

    def finish_fast(j, slot):
        e = jnp.exp2(s_ref[slot]).astype(BF16)
        ov = _dot(e, vx_ref[...])
        p0 = ov[0:tq, 0:hd2] / ov[0:tq, hd2:2 * hd2]
        p1 = ov[tq:2 * tq, 0:hd2] / ov[tq:2 * tq, hd2:2 * hd2]
        write_out(j, p0 - lam_full * p1)

    def scores_max(j, slot):
        _, qa, qb = q_block(j)
        s_ref[slot] = _nt(jnp.concatenate([qa, qb], axis=0), k_ref[0])

    def finish_max(j, slot):
        s = s_ref[slot]
        e = jnp.exp2(s - jnp.max(s, axis=1, keepdims=True))
        z = jnp.sum(e, axis=1, keepdims=True)
        coef = lam_full * z[0:tq] / z[tq:2 * tq]
        a = e[0:tq] - coef * e[tq:2 * tq]
        write_out(j, _dot(a.astype(BF16), v_ref[0]) * (1.0 / z[0:tq]))

    def run(scores, finish):
        scores(0, 0)

        def body(jj, carry):
            j0 = 2 * jj
            scores(j0 + 1, 1)
            finish(j0, 0)
            scores(j0 + 2, 0)
            finish(j0 + 1, 1)
            return carry

        lax.fori_loop(0, n_qblocks // 2 - 1, body, 0)
        scores(n_qblocks - 1, 1)
        finish(n_qblocks - 2, 0)
        finish(n_qblocks - 1, 1)

    @pl.when(is_safe)
    def _():
        kx_ref[:, 0:hd2] = k_ref[0]
        kx_ref[:, hd2:2 * hd2] = jnp.broadcast_to(jnp.where(col0, 1.0, 0.0).astype(BF16), (n_keys, hd2))
        vx_ref[:, 0:hd2] = v_ref[0]
        vx_ref[:, hd2:2 * hd2] = jnp.ones((n_keys, hd2), BF16)
        run(scores_fast, finish_fast)

    @pl.when(jnp.logical_not(is_safe))
    def _():
        run(scores_max, finish_max)


def _diff_call(p1, lam, subln_g, n_ctx_rows, n_heads, lambda_init):
    b, t, _ = p1.shape
    s_len = t - n_ctx_rows
    hd2 = 2 * DIFF_HD
    n_qblocks = s_len // DIFF_QBLOCK
    assert n_qblocks >= 2 and n_qblocks % 2 == 0
    return pl.pallas_call(
        functools.partial(_diff_kernel, row_off=n_ctx_rows, n_qblocks=n_qblocks, lambda_init=lambda_init),
        grid=(b, n_heads),
        in_specs=[pl.BlockSpec((4, DIFF_HD), lambda bi, h: (0, 0)),
                  pl.BlockSpec((1, t, hd2), lambda bi, h: (bi, 0, h)),
                  pl.BlockSpec((1, t, hd2), lambda bi, h: (bi, 0, n_heads + h)),
                  pl.BlockSpec((1, t, hd2), lambda bi, h: (bi, 0, 2 * n_heads + h)),
                  pl.BlockSpec((1, hd2), lambda bi, h: (0, 0))],
        out_specs=pl.BlockSpec((1, s_len, hd2), lambda bi, h: (bi, 0, h)),
        out_shape=jax.ShapeDtypeStruct((b, s_len, n_heads * hd2), BF16),
        scratch_shapes=[pltpu.VMEM((2, 2 * DIFF_QBLOCK, t), F32), pltpu.VMEM((t, 2 * hd2), BF16),
                        pltpu.VMEM((t, 2 * hd2), BF16), pltpu.VMEM((s_len, 2 * hd2), F32)],
        compiler_params=_cparams("parallel", "parallel"),
        name="diff_attn",
    )(lam, p1, p1, p1, subln_g.reshape(1, hd2))


def _outproj_kernel(*refs, n_y, n_sub, n_row_refs, n_ctx_blocks, row_block_off):
    y_refs = refs[:n_y]
    w_refs = refs[n_y:2 * n_y]
    row_refs = refs[2 * n_y:2 * n_y + n_sub * n_row_refs]
    mod_ref, g_ref, wr_ref, xn_ref, h_ref, aff_ref = refs[2 * n_y + n_sub * n_row_refs:]
    n_exp = aff_ref.shape[1]
    subs = [slice(u * ROW_TILE, (u + 1) * ROW_TILE) for u in range(n_sub)]
    accs = []
    for rs in subs:
        acc = _dot(y_refs[0][0, rs, :], w_refs[0][...])
        for yr, wr in zip(y_refs[1:], w_refs[1:]):
            acc = acc + _dot(yr[0, rs, :], wr[...])
        accs.append(acc)
    for u, rs in enumerate(subs):
        x_rows = _load_rows(row_refs[u * n_row_refs:(u + 1) * n_row_refs], n_ctx_blocks, row_block_off)
        xn = x_rows + mod_ref[0, 0, 2:3, :] * accs[u]
        xn_ref[0, rs, :] = xn
        h = _rms(xn) * g_ref[...] * (1.0 + mod_ref[0, 0, 4:5, :]) + mod_ref[0, 0, 3:4, :]
        h_bf = h.astype(BF16)
        h_ref[0, rs, :] = h_bf
        logits = _dot(h_bf, wr_ref[...])
        logits = jnp.where(lax.broadcasted_iota(jnp.int32, (1, LANES), 1) < n_exp, logits, NEG_BIG)
        ex = jnp.exp(logits - jnp.max(logits, axis=1, keepdims=True))
        aff = ex / jnp.sum(ex, axis=1, keepdims=True)
        aff_ref[0, :, rs] = jnp.transpose(aff)[0:n_exp]


def _outproj_call(ys, ws, stream, mod, g, w_router, row_block_off, n_rows, n_ctx_blocks):
    b, d = mod.shape[0], mod.shape[3]
    n_y = len(ys)
    e = w_router.shape[1]
    w_router_pad = jnp.concatenate([w_router, jnp.zeros((d, LANES - e), F32)], axis=1).astype(BF16)
    all_latent = row_block_off >= n_ctx_blocks and not isinstance(stream, tuple)
    n_sub = 2 if all_latent and n_rows % (2 * ROW_TILE) == 0 else 1
    tile = n_sub * ROW_TILE
    row_specs, row_args = [], []
    for u in range(n_sub):
        specs_u, args_u = _row_specs(stream, n_ctx_blocks, row_block_off + u, n_sub)
        row_specs += specs_u
        row_args += args_u
    in_specs = ([pl.BlockSpec((1, tile, y.shape[2]), lambda bi, i: (bi, i, 0)) for y in ys]
                + [pl.BlockSpec(w.shape, lambda bi, i: (0, 0), pipeline_mode=pl.Buffered(1)) for w in ws]
                + row_specs
                + [pl.BlockSpec((1, 1, 6, d),
                                lambda bi, i: (bi, jnp.where(i * n_sub + row_block_off < n_ctx_blocks, 0, 1), 0, 0)),
                   pl.BlockSpec((1, d), lambda bi, i: (0, 0)),
                   pl.BlockSpec((d, LANES), lambda bi, i: (0, 0))])
    return pl.pallas_call(
        functools.partial(_outproj_kernel, n_y=n_y, n_sub=n_sub, n_row_refs=len(row_args) // n_sub,
                          n_ctx_blocks=n_ctx_blocks, row_block_off=row_block_off),
        grid=(b, n_rows // tile),
        in_specs=in_specs,
        out_specs=[pl.BlockSpec((1, tile, d), lambda bi, i: (bi, i, 0)),
                   pl.BlockSpec((1, tile, d), lambda bi, i: (bi, i, 0)),
                   pl.BlockSpec((1, e, tile), lambda bi, i: (bi, 0, i))],
        out_shape=[jax.ShapeDtypeStruct((b, n_rows, d), F32),
                   jax.ShapeDtypeStruct((b, n_rows, d), BF16),
                   jax.ShapeDtypeStruct((b, e, n_rows), F32)],
        compiler_params=_cparams("parallel", "parallel"),
        name="out_proj",
    )(*ys, *ws, *row_args, mod, g.reshape(1, d), w_router_pad)


def _prefix_lanes(m, upper):
    e, n = m.shape
    carry = jnp.zeros((e, 1), F32)
    outs = []
    for blk in range(n // LANES):
        mb = m[:, blk * LANES:(blk + 1) * LANES]
        inc = _dot(mb.astype(BF16), upper)
        outs.append(inc - mb + carry)
        carry = carry + jnp.sum(mb, axis=1, keepdims=True)
    return jnp.concatenate(outs, axis=1)


def _transpose_exact(x, eye):
    hi, mid, lo = _split3(x)
    return _nt(eye, hi) + (_nt(eye, mid) + _nt(eye, lo))


def _route_kernel(aff_ref, pos_ref, gate_ref, posn_ref, lo_ref, hi_ref, *, segs):
    n_exp = aff_ref.shape[1]
    lo_acc = jnp.zeros((n_exp, LANES), F32)
    hi_acc = jnp.zeros((n_exp, LANES), F32)
    li = lax.broadcasted_iota(jnp.int32, (LANES, LANES), 0)
    lj = lax.broadcasted_iota(jnp.int32, (LANES, LANES), 1)
    upper = jnp.where(li <= lj, 1.0, 0.0).astype(BF16)
    eye = jnp.where(li == lj, 1.0, 0.0).astype(BF16)
    all_bits = [pltpu.bitcast(aff_ref[0, :, off:off + n], jnp.int32) for (off, n, _, _) in segs]

    def search(it, thrs):
        out = []
        for bits, (_, _, cap, _), thr in zip(all_bits, segs, thrs):
            cand = thr | jnp.left_shift(jnp.int32(1), 30 - it)
            cnt = jnp.sum(jnp.where(bits >= cand, 1.0, 0.0), axis=1, keepdims=True)
            out.append(jnp.where(cnt >= cap, cand, thr))
        return tuple(out)

    all_thr = lax.fori_loop(0, 31, search, tuple(jnp.zeros((n_exp, 1), jnp.int32) for _ in segs))
    for (off, n, cap, slot_off), bits, thr in zip(segs, all_bits, all_thr):
        a = aff_ref[0, :, off:off + n]
        gt = jnp.where(bits > thr, 1.0, 0.0)
        eq = jnp.where(bits == thr, 1.0, 0.0)
        need = cap - jnp.sum(gt, axis=1, keepdims=True)
        sel = gt + eq * jnp.where(_prefix_lanes(eq, upper) < need, 1.0, 0.0)
        slot = _prefix_lanes(sel, upper)
        pos = jnp.where(sel > 0.0, slot + slot_off, -1.0)
        pos_ref[0, :, off:off + n] = pos.astype(jnp.int32)
        gate_ref[0, :, off:off + n] = sel * a
        for blk in range(n // LANES):
            cols = slice(blk * LANES, (blk + 1) * LANES)
            rows = slice(off + blk * LANES, off + (blk + 1) * LANES)
            posn_ref[0, rows, :] = _transpose_exact(pos[:, cols], eye).astype(jnp.int32)
        assert cap <= 256 and (off + n) // LANES <= LANES
        tok = lax.broadcasted_iota(jnp.int32, (n, LANES), 0)
        blk_start = (lax.broadcasted_iota(jnp.int32, (n, LANES), 1) - off // LANES) * LANES
        lo_acc = lo_acc + _dot(slot.astype(BF16), jnp.where(tok == blk_start, 1.0, 0.0).astype(BF16))
        hi_acc = hi_acc + _dot((slot + sel).astype(BF16),
                               jnp.where(tok == blk_start + (LANES - 1), 1.0, 0.0).astype(BF16))
    lo_ref[0] = lo_acc.astype(jnp.int32)
    hi_ref[0] = hi_acc.astype(jnp.int32)


def _route_call(aff_t, segs):
    b, e, t = aff_t.shape
    return pl.pallas_call(
        functools.partial(_route_kernel, segs=tuple(segs)),
        grid=(b,),
        in_specs=[pl.BlockSpec((1, e, t), lambda bi: (bi, 0, 0))],
        out_specs=[pl.BlockSpec((1, e, t), lambda bi: (bi, 0, 0)),
                   pl.BlockSpec((1, e, t), lambda bi: (bi, 0, 0)),
                   pl.BlockSpec((1, t, e), lambda bi: (bi, 0, 0)),
                   pl.BlockSpec((1, e, LANES), lambda bi: (bi, 0, 0)),
                   pl.BlockSpec((1, e, LANES), lambda bi: (bi, 0, 0))],
        out_shape=[jax.ShapeDtypeStruct((b, e, t), jnp.int32),
                   jax.ShapeDtypeStruct((b, e, t), F32),
                   jax.ShapeDtypeStruct((b, t, e), jnp.int32),
                   jax.ShapeDtypeStruct((b, e, LANES), jnp.int32),
                   jax.ShapeDtypeStruct((b, e, LANES), jnp.int32)],
        compiler_params=_cparams("parallel"),
        name="route",
    )(aff_t)


def _gather_kernel(lo_ref, hi_ref, pos_ref, gate_ref, h_ref, o_ref, g_ref,
                   *, n_ctx_rows, n_lat_rows, cap_lat, sblk, win, margin):
    sb = pl.program_id(1)
    n_exp = pos_ref.shape[1]
    n_lat_blocks = cap_lat // sblk
    tok_per_blk = n_lat_rows // n_lat_blocks

    def emit(slot0, tok0, n_tok):
        rows = pl.ds(tok0, n_tok)
        slots = lax.broadcasted_iota(jnp.int32, (sblk, 1), 0) + slot0
        hits = [pos_ref[0, e:e + 1, rows] == slots for e in range(n_exp)]
        onehot = jnp.concatenate([jnp.where(h, 1.0, 0.0).astype(BF16) for h in hits], axis=0)
        res = _dot(onehot, h_ref[0, rows, :])
        for e in range(n_exp):
            o_ref[e, 0] = res[e * sblk:(e + 1) * sblk].astype(BF16)
            gsel = jnp.sum(jnp.where(hits[e], gate_ref[0, e:e + 1, rows], 0.0), axis=1, keepdims=True)
            g_ref[e, 0] = jnp.broadcast_to(gsel, (sblk, LANES))

    slot0 = sb * sblk
    w0 = jnp.clip(sb * tok_per_blk - margin, 0, n_lat_rows - win)
    tok0 = pl.multiple_of(n_ctx_rows + w0, LANES)
    first_blk = (n_ctx_rows + w0) // LANES
    fits = None
    for e in range(n_exp):
        ok = jnp.logical_and(lo_ref[0, e, first_blk] <= slot0,
                             hi_ref[0, e, first_blk + win // LANES - 1] >= slot0 + sblk)
        fits = ok if fits is None else jnp.logical_and(fits, ok)
    is_lat = sb < n_lat_blocks

    @pl.when(jnp.logical_and(is_lat, fits))
    def _():
        emit(slot0, tok0, win)

    @pl.when(jnp.logical_and(is_lat, jnp.logical_not(fits)))
    def _():
        emit(slot0, n_ctx_rows, n_lat_rows)

    if n_ctx_rows:
        @pl.when(jnp.logical_not(is_lat))
        def _():
            emit(cap_lat, 0, n_ctx_rows)


def _slot_window(tt, cap_lat, n_exp):
    return min(cap_lat, max(16, 2 * tt * EC_FACTOR // n_exp))


def _gather_call(lo, hi, pos, gate, h, cap_lat, cap_ctx, n_ctx_rows):
    b, e, t = pos.shape
    d = h.shape[2]
    n_slots = cap_lat + cap_ctx
    n_lat_rows = t - n_ctx_rows
    sblk = cap_ctx if cap_ctx else 32
    assert cap_lat % sblk == 0 and sblk % 16 == 0
    n_lat_blocks = cap_lat // sblk
    tok_per_blk = n_lat_rows // n_lat_blocks
    margin = tok_per_blk
    win = min(n_lat_rows, 3 * tok_per_blk)
    assert tok_per_blk % LANES == 0 and win % LANES == 0
    smem = lambda: pl.BlockSpec((1, e, LANES), lambda bi, si: (bi, 0, 0), memory_space=pltpu.SMEM)
    return pl.pallas_call(
        functools.partial(_gather_kernel, n_ctx_rows=n_ctx_rows, n_lat_rows=n_lat_rows, cap_lat=cap_lat,
                          sblk=sblk, win=win, margin=margin),
        grid=(b, n_slots // sblk),
        in_specs=[smem(), smem(),
                  pl.BlockSpec((1, e, t), lambda bi, si: (bi, 0, 0)),
                  pl.BlockSpec((1, e, t), lambda bi, si: (bi, 0, 0)),
                  pl.BlockSpec((1, t, d), lambda bi, si: (bi, 0, 0))],
        out_specs=[pl.BlockSpec((e, 1, sblk, d), lambda bi, si: (0, bi, si, 0)),
                   pl.BlockSpec((e, 1, sblk, LANES), lambda bi, si: (0, bi, si, 0))],
        out_shape=[jax.ShapeDtypeStruct((e, b, n_slots, d), BF16),
                   jax.ShapeDtypeStruct((e, b, n_slots, LANES), F32)],
        compiler_params=_cparams("parallel", "arbitrary"),
        name="gather",
    )(lo, hi, pos, gate, h)


def _ffn_up_kernel(x_ref, wg_ref, wu_ref, o_ref, wgb_ref, wub_ref):
    @pl.when(pl.program_id(2) == 0)
    def _():
        wgb_ref[...] = wg_ref[0, 0].astype(BF16)
        wub_ref[...] = wu_ref[0, 0].astype(BF16)

    x = x_ref[0]
    g = _dot(x, wgb_ref[...])
    u = _dot(x, wub_ref[...])
    o_ref[0] = (_silu(g) * u).astype(BF16)


def _ffn_down_kernel(a_ref, w_ref, gate_ref, o_ref, wb_ref):
    @pl.when(pl.program_id(2) == 0)
    def _():
        wb_ref[...] = w_ref[0, 0].astype(BF16)

    gate = jnp.concatenate([gate_ref[0]] * (o_ref.shape[2] // LANES), axis=1)
    o_ref[0] = (_dot(a_ref[0], wb_ref[...]) * gate).astype(BF16)


def _ffn_call(xs, gates, w_gu, w_dn, layer):
    e, m, d = xs.shape
    f = w_dn.shape[2]
    tm = _pick(m, (1152, 1024, 768, 512, 256, 128, 64, 32, 16, 8))
    tf = _pick(f, (512, 256, 128))
    nf = f // tf
    act = pl.pallas_call(
        _ffn_up_kernel,
        grid=(e, nf, m // tm),
        in_specs=[pl.BlockSpec((1, tm, d), lambda ei, j, i: (ei, i, 0)),
                  pl.BlockSpec((1, 1, d, tf), lambda ei, j, i: (layer, ei, 0, j)),
                  pl.BlockSpec((1, 1, d, tf), lambda ei, j, i: (layer, ei, 0, j + nf))],
        out_specs=pl.BlockSpec((1, tm, tf), lambda ei, j, i: (ei, i, j)),
        out_shape=jax.ShapeDtypeStruct((e, m, f), BF16),
        scratch_shapes=[pltpu.VMEM((d, tf), BF16), pltpu.VMEM((d, tf), BF16)],
        compiler_params=_cparams("parallel", "arbitrary", "arbitrary"),
        name="ffn_up",
    )(xs, w_gu, w_gu)
    tn = _pick(d, (1024, 512, 256, 128))
    return pl.pallas_call(
        _ffn_down_kernel,
        grid=(e, d // tn, m // tm),
        in_specs=[pl.BlockSpec((1, tm, f), lambda ei, j, i: (ei, i, 0)),
                  pl.BlockSpec((1, 1, f, tn), lambda ei, j, i: (layer, ei, 0, j)),
                  pl.BlockSpec((1, tm, LANES), lambda ei, j, i: (ei, i, 0))],
        out_specs=pl.BlockSpec((1, tm, tn), lambda ei, j, i: (ei, i, j)),
        out_shape=jax.ShapeDtypeStruct((e, m, d), BF16),
        scratch_shapes=[pltpu.VMEM((f, tn), BF16)],
        compiler_params=_cparams("parallel", "arbitrary", "arbitrary"),
        name="ffn_down",
    )(act, w_dn, gates)


def _combine_kernel(lo_ref, hi_ref, posn_ref, y_ref, x_ref, mod_ref, gf_ref, o_ref,
                    *, cap_lat, cap_ctx, n_ctx_tiles, blocks_per_tile, window, final_norm):
    n_exp = posn_ref.shape[2]
    i = pl.program_id(1)
    lane = lax.broadcasted_iota(jnp.int32, (1, n_exp), 1)
    posn = posn_ref[0].astype(F32)

    def pcol(e):
        return jnp.sum(jnp.where(lane == e, posn, 0.0), axis=1, keepdims=True)

    def scatter(slot0, cap):
        slots = lax.broadcasted_iota(jnp.int32, (1, cap), 1).astype(F32) + slot0
        hots = [jnp.where(pcol(e) == slots, 1.0, 0.0).astype(BF16) for e in range(n_exp)]
        rows = [y_ref[e, 0, slot0:slot0 + cap, :] for e in range(n_exp)]
        return _dot(jnp.concatenate(hots, axis=1), jnp.concatenate(rows, axis=0))

    def scatter_window(starts):
        w_tot = n_exp * window
        grp = lax.broadcasted_iota(jnp.int32, (n_exp, w_tot), 1) // window
        spread = jnp.where(grp == lax.broadcasted_iota(jnp.int32, (n_exp, w_tot), 0), 1.0, 0.0).astype(BF16)
        bpos = _dot(posn.astype(BF16), spread)
        wlane = lax.broadcasted_iota(jnp.int32, (1, w_tot), 1)
        target = (wlane % window).astype(F32)
        for e, st in enumerate(starts):
            target = jnp.where(wlane // window == e, target + st.astype(F32), target)
        onehot = jnp.where(bpos == target, 1.0, 0.0).astype(BF16)
        rows = [y_ref[e, 0, pl.ds(pl.multiple_of(st, 16), window), :] for e, st in enumerate(starts)]
        return _dot(onehot, jnp.concatenate(rows, axis=0))

    def finish(total, g2):
        out = x_ref[0] + g2 * total
        if final_norm:
            out = _rms(out) * gf_ref[...]
        o_ref[0] = out

    blk0 = i * blocks_per_tile
    starts = []
    fits = None
    for e in range(n_exp):
        lo = lo_ref[0, e, blk0]
        hi = hi_ref[0, e, blk0 + blocks_per_tile - 1]
        st = jnp.minimum(lax.shift_left(lax.shift_right_logical(lo, 4), 4), cap_lat - window)
        ok = hi - st <= window
        fits = ok if fits is None else jnp.logical_and(fits, ok)
        starts.append(st)
    is_lat = i >= n_ctx_tiles

    @pl.when(jnp.logical_and(is_lat, fits))
    def _():
        finish(scatter_window(starts), mod_ref[0, 1, 5:6, :])

    @pl.when(jnp.logical_and(is_lat, jnp.logical_not(fits)))
    def _():
        finish(scatter(0, cap_lat), mod_ref[0, 1, 5:6, :])

    if n_ctx_tiles:
        @pl.when(jnp.logical_not(is_lat))
        def _():
            finish(scatter(cap_lat, cap_ctx), mod_ref[0, 0, 5:6, :])


def _combine_call(lo, hi, posn, y, xa, mod, g_final, cap_lat, cap_ctx, n_ctx_rows, final_norm):
    b, t, d = xa.shape
    e = posn.shape[2]
    n_slots = y.shape[2]
    tt = ROW_TILE
    assert n_ctx_rows % tt == 0 and t % tt == 0
    window = _slot_window(tt, cap_lat, e)
    smem = lambda: pl.BlockSpec((1, e, LANES), lambda bi, i: (bi, 0, 0), memory_space=pltpu.SMEM)
    return pl.pallas_call(
        functools.partial(_combine_kernel, cap_lat=cap_lat, cap_ctx=cap_ctx, n_ctx_tiles=n_ctx_rows // tt,
                          blocks_per_tile=tt // LANES, window=window, final_norm=final_norm),
        grid=(b, t // tt),
        in_specs=[smem(), smem(),
                  pl.BlockSpec((1, tt, e), lambda bi, i: (bi, i, 0)),
                  pl.BlockSpec((e, 1, n_slots, d), lambda bi, i: (0, bi, 0, 0)),
                  pl.BlockSpec((1, tt, d), lambda bi, i: (bi, i, 0)),
                  pl.BlockSpec((1, 2, 6, d), lambda bi, i: (bi, 0, 0, 0)),
                  pl.BlockSpec((1, d), lambda bi, i: (0, 0))],
        out_specs=pl.BlockSpec((1, tt, d), lambda bi, i: (bi, i, 0)),
        out_shape=jax.ShapeDtypeStruct((b, t, d), F32),
        compiler_params=_cparams("parallel", "arbitrary"),
        name="combine",
    )(lo, hi, posn, y, xa, mod, g_final.reshape(1, d))


def _rope_tables(n_tokens, n_ctx_rows, head_dim):
    rows = n_tokens // GRID_W
    r = jnp.repeat(jnp.arange(rows, dtype=F32), GRID_W)
    col = jnp.tile(jnp.arange(GRID_W, dtype=F32), rows)
    axis_dim = head_dim // 2
    inv = ROPE_THETA ** (-jnp.arange(0, axis_dim, 2, dtype=F32) / axis_dim)
    ar, ac = r[:, None] * inv, col[:, None] * inv
    cos_p = jnp.concatenate([jnp.cos(ar), jnp.cos(ar), jnp.cos(ac), jnp.cos(ac)], axis=-1)
    sin_p = jnp.concatenate([-jnp.sin(ar), jnp.sin(ar), -jnp.sin(ac), jnp.sin(ac)], axis=-1)
    reps = LANES // head_dim
    cos_p, sin_p = jnp.tile(cos_p, (1, reps)), jnp.tile(sin_p, (1, reps))
    cos_t = jnp.concatenate([jnp.ones((n_ctx_rows, LANES), F32), cos_p], axis=0)
    sin_t = jnp.concatenate([jnp.zeros((n_ctx_rows, LANES), F32), sin_p], axis=0)
    return cos_t, sin_t


def _moe(h2, aff_t, x_res, mod, g_final, w_gu, w_dn, layer, segs, cap_lat, cap_ctx, n_ctx_rows, final_norm):
    b, t, d = h2.shape
    e = aff_t.shape[1]
    n_slots = cap_lat + cap_ctx
    pos, gate, posn, lo, hi = _route_call(aff_t, segs)
    xs, gs = _gather_call(lo, hi, pos, gate, h2, cap_lat, cap_ctx, n_ctx_rows)
    y = _ffn_call(xs.reshape(e, b * n_slots, d), gs.reshape(e, b * n_slots, LANES), w_gu, w_dn, layer)
    return _combine_call(lo, hi, posn, y.reshape(e, b, n_slots, d), x_res, mod, g_final, cap_lat, cap_ctx,
                         n_ctx_rows, final_norm)


def kernel(x, c, ctx, c_ctx, w_ada, b_ada, g_norm_mix, g_norm_ffn, w_in_even, gla_gate_w, gla_gate_b, gla_norm_g, swa_sink, w_out_even, w_qkv_odd, diff_lambda, diff_subln_g, w_out_odd, w_router, w_gate_up, w_down, g_final):
    b, s_len, d = x.shape
    lc = ctx.shape[1]
    depth = w_ada.shape[0]
    assert depth == 2 and lc % ROW_TILE == 0 and s_len % ROW_TILE == 0 and s_len % GRID_W == 0
    t = lc + s_len
    ncb = lc // ROW_TILE
    gla_dk, gla_dv = d // 4 // GLA_HEADS, d // 2 // GLA_HEADS
    swa_hd = d // 2 // SWA_HEADS
    diff_heads = d // (2 * DIFF_HD)
    n_exp = w_router.shape[2]
    assert gla_dk == LANES and swa_hd == LANES and 2 * DIFF_HD == LANES

    n_rows = -(-(b + 1) // 8) * 8
    cc = jnp.concatenate([c, c_ctx[None, :], jnp.zeros((n_rows - b - 1, d), F32)], axis=0)
    m_all = _ada_call(cc, w_ada, b_ada).reshape(depth, n_rows, 6, d)

    def mod_for(layer):
        lat = m_all[layer, :b]
        ctxm = jnp.broadcast_to(m_all[layer, b][None], (b, 6, d))
        return jnp.stack([ctxm, lat], axis=1)

    xa = (ctx, x)

    mod0 = mod_for(0)
    w_in = w_in_even[0]
    sizes = (GLA_HEADS * gla_dk, GLA_HEADS * gla_dk, GLA_HEADS * gla_dv, GLA_HEADS * gla_dv,
             GLA_GATE_RANK, GLA_GATE_RANK, SWA_HEADS * swa_hd, SWA_KV_HEADS * swa_hd, SWA_KV_HEADS * swa_hd)
    offs = np.concatenate([[0], np.cumsum(sizes)])
    seg = lambda i: w_in[:, offs[i]:offs[i + 1]]
    w_main = jnp.concatenate([seg(0), seg(1), seg(2), seg(3), seg(6), seg(7), seg(8)], axis=1).astype(BF16)
    w_rank = jnp.concatenate([seg(4), seg(5), jnp.zeros((d, LANES - 2 * GLA_GATE_RANK), F32)],
                             axis=1).astype(BF16)
    col_aq, col_ak = 0, sizes[0]
    col_av = col_ak + sizes[1]
    col_ag = col_av + sizes[2]
    col_bq = col_ag + sizes[3]
    col_bk = col_bq + sizes[6]
    col_bv = col_bk + sizes[7]
    n_main = col_bv + sizes[8]
    swa_q_scale = float(swa_hd) ** -0.5 * float(np.log2(np.e))
    rope0 = [(col_bq <= ch * COL_CHUNK < col_bv, swa_q_scale if col_bq <= ch * COL_CHUNK < col_bk else 1.0)
             for ch in range(n_main // COL_CHUNK)]
    cos_b, sin_b = _rope_tables(s_len, lc, swa_hd)
    p0, r0 = _proj_call(xa, mod0, g_norm_mix[0], w_main, cos_b, sin_b, rope0, swa_hd // 4, ncb, wr=w_rank)

    gw = gla_gate_w[0]
    gw_pad = jnp.zeros((2, LANES, GLA_HEADS * gla_dk), F32)
    gw_pad = gw_pad.at[0, 0:GLA_GATE_RANK].set(gw[0]).at[1, GLA_GATE_RANK:2 * GLA_GATE_RANK].set(gw[1])
    a_out = _gla_call(p0, r0, gw_pad.astype(BF16), gla_gate_b[0].reshape(2, 1, -1), gla_norm_g[0],
                      _gla_constants(), lc, col_aq, col_ak, col_av, col_ag, gla_dk, gla_dv)
    sink_col = jnp.broadcast_to(
        jnp.repeat(swa_sink[0].reshape(SWA_KV_HEADS, SWA_GROUP), SWA_BLOCK, axis=1)[:, :, None],
        (SWA_KV_HEADS, SWA_GROUP * SWA_BLOCK, LANES))
    b_out = _swa_call(p0, sink_col, lc, col_bq, col_bk, col_bv, swa_hd)

    w_o = w_out_even[0].astype(BF16)
    n_a = GLA_HEADS * gla_dv
    xa1, h2, aff_t = _outproj_call([a_out, b_out], [w_o[:n_a], w_o[n_a:]], xa, mod0, g_norm_ffn[0],
                                   w_router[0], 0, t, ncb)
    cap_lat = s_len * EC_FACTOR // n_exp
    cap_ctx = lc * EC_FACTOR // n_exp
    segs0 = [(lc, s_len, cap_lat, 0), (0, lc, cap_ctx, cap_lat)]
    xa2 = _moe(h2, aff_t, xa1, mod0, g_final, w_gate_up, w_down, 0, segs0, cap_lat, cap_ctx, lc, False)

    mod1 = mod_for(1)
    lambda_init = 0.8 - 0.6 * float(np.exp(-0.3 * 1))
    w_qkv = w_qkv_odd[0].astype(BF16)
    n_qkv = w_qkv.shape[1]
    q_scale = DIFF_HD ** -0.5 * float(np.log2(np.e))
    rope1 = [(ch * COL_CHUNK < 2 * d, q_scale if ch * COL_CHUNK < d else 1.0)
             for ch in range(n_qkv // COL_CHUNK)]
    cos_c, sin_c = _rope_tables(s_len, lc, DIFF_HD)
    (p1,) = _proj_call(xa2, mod1, g_norm_mix[1], w_qkv, cos_c, sin_c, rope1, DIFF_HD // 4, ncb)
    y1 = _diff_call(p1, diff_lambda[0], diff_subln_g[0], lc, diff_heads, lambda_init)
    x3, h2b, aff_tb = _outproj_call([y1], [w_out_odd[0].astype(BF16)], xa2, mod1, g_norm_ffn[1],
                                    w_router[1], ncb, s_len, ncb)
    segs1 = [(0, s_len, cap_lat, 0)]
    return _moe(h2b, aff_tb, x3, mod1, g_final, w_gate_up, w_down, 1, segs1, cap_lat, 0, 0, True)
```

```python
import functools

import numpy as np
import jax
import jax.numpy as jnp
from jax import lax
from jax.experimental import pallas as pl
from jax.experimental.pallas import tpu as pltpu

F32 = jnp.float32
BF16 = jnp.bfloat16

GRID_W = 64
ROPE_THETA = 10000.0
NORM_EPS = 1e-6
GLA_HEADS = 4
GLA_GATE_RANK = 16
GLA_GATE_NORM = 16.0
SWA_HEADS = 8
SWA_KV_HEADS = 2
SWA_GROUP = SWA_HEADS // SWA_KV_HEADS
SWA_WINDOW = 128
SWA_BLOCK = 128
DIFF_HD = 64
N_EXPERTS = 16
EC_FACTOR = 2

LANES = 128
ROW_TILE = 256
COL_CHUNK = 256
GLA_CHUNK = 256
GLA_LEVELS = 8
GLA_SAFE_RANGE = 60.0
DIFF_QBLOCK = 256
DIFF_SAFE_BOUND = 60.0
DIFF_BOUND_SLACK = 1.0 + 2.0 ** -6
VMEM_LIMIT = 56 * 1024 * 1024
NEG_BIG = -1e30


def _cparams(*sem):
    return pltpu.CompilerParams(dimension_semantics=sem, vmem_limit_bytes=VMEM_LIMIT)


def _pick(n, cands):
    for c in cands:
        if n % c == 0:
            return c
    raise ValueError(f"no tile for {n} in {cands}")


def _nt(a, b):
    return lax.dot_general(a, b, (((1,), (1,)), ((), ())), preferred_element_type=F32)


def _tn(a, b):
    return lax.dot_general(a, b, (((0,), (0,)), ((), ())), preferred_element_type=F32)


def _dot(a, b):
    return jnp.dot(a, b, preferred_element_type=F32)


def _split3(a):
    hi = a.astype(BF16)
    r1 = a - hi.astype(F32)
    mid = r1.astype(BF16)
    lo = (r1 - mid.astype(F32)).astype(BF16)
    return hi, mid, lo


def _split2(a):
    hi = a.astype(BF16)
    lo = (a - hi.astype(F32)).astype(BF16)
    return hi, lo


def _silu(x):
    return x * (1.0 / (1.0 + jnp.exp(-x)))


def _rms(x, eps=NORM_EPS):
    return x * lax.rsqrt(jnp.mean(x * x, axis=-1, keepdims=True) + eps)


def _row_specs(stream, n_ctx_blocks, off=0, stride=1):
    if isinstance(stream, tuple):
        assert stride == 1
        ctx, x = stream
        d = x.shape[2]
        return ([pl.BlockSpec((1, ROW_TILE, d), lambda bi, i: (bi, jnp.minimum(i + off, n_ctx_blocks - 1), 0)),
                 pl.BlockSpec((1, ROW_TILE, d), lambda bi, i: (bi, jnp.maximum(i + off - n_ctx_blocks, 0), 0))],
                [ctx, x])
    d = stream.shape[2]
    return [pl.BlockSpec((1, ROW_TILE, d), lambda bi, i: (bi, i * stride + off, 0))], [stream]


def _load_rows(row_refs, n_ctx_blocks, off=0):
    if len(row_refs) == 2:
        return jnp.where(pl.program_id(1) + off < n_ctx_blocks, row_refs[0][0], row_refs[1][0])
    return row_refs[0][0]


def _ada_kernel(c_ref, w_ref, b_ref, o_ref):
    s = _silu(c_ref[...])
    s_hi, s_lo = _split2(s)
    w_hi, w_lo = _split2(w_ref[0])
    acc = _dot(s_hi, w_hi) + (_dot(s_lo, w_hi) + _dot(s_hi, w_lo))
    o_ref[0] = acc + b_ref[0]


def _ada_call(cc, w_ada, b_ada):
    depth, d, n = w_ada.shape
    r = cc.shape[0]
    tn = _pick(n, (768, 512, 256, 128))
    return pl.pallas_call(
        _ada_kernel,
        grid=(depth, n // tn),
        in_specs=[pl.BlockSpec((r, d), lambda l, j: (0, 0)),
                  pl.BlockSpec((1, d, tn), lambda l, j: (l, 0, j)),
                  pl.BlockSpec((1, 1, tn), lambda l, j: (l, 0, j))],
        out_specs=pl.BlockSpec((1, r, tn), lambda l, j: (l, 0, j)),
        out_shape=jax.ShapeDtypeStruct((depth, r, n), F32),
        compiler_params=_cparams("parallel", "parallel"),
        name="adaln",
    )(cc, w_ada, b_ada.reshape(depth, 1, n))


def _proj_kernel(*refs, n_row_refs, n_ctx_blocks, chunk_ops, quarter, has_r):
    row_refs = refs[:n_row_refs]
    mod_ref, g_ref, w_ref, cos_ref, sin_ref = refs[n_row_refs:n_row_refs + 5]
    rest = refs[n_row_refs + 5:]
    if has_r:
        wr_ref, o_ref, r_ref = rest
    else:
        (o_ref,) = rest
    x = _load_rows(row_refs, n_ctx_blocks)
    y = _rms(x) * g_ref[...]
    h = (y * (1.0 + mod_ref[0, 0, 1:2, :]) + mod_ref[0, 0, 0:1, :]).astype(BF16)
    reps = COL_CHUNK // LANES
    cos_t = jnp.concatenate([cos_ref[...]] * reps, axis=1)
    sin_t = jnp.concatenate([sin_ref[...]] * reps, axis=1)
    lane = lax.broadcasted_iota(jnp.int32, (1, COL_CHUNK), 1)
    first = (lane % (2 * quarter)) < quarter
    for c, (rope, scale) in enumerate(chunk_ops):
        acc = _dot(h, w_ref[:, c * COL_CHUNK:(c + 1) * COL_CHUNK])
        if rope:
            partner = jnp.where(first, pltpu.roll(acc, COL_CHUNK - quarter, 1), pltpu.roll(acc, quarter, 1))
            acc = acc * cos_t + partner * sin_t
        if scale != 1.0:
            acc = acc * scale
        o_ref[0, :, c * COL_CHUNK:(c + 1) * COL_CHUNK] = acc.astype(BF16)
    if has_r:
        r_ref[0] = _dot(h, wr_ref[...]).astype(BF16)


def _proj_call(stream, mod, g, w, cos_t, sin_t, chunk_ops, quarter, n_ctx_blocks, wr=None):
    d, n = w.shape
    t = cos_t.shape[0]
    b = mod.shape[0]
    assert n == len(chunk_ops) * COL_CHUNK and t % ROW_TILE == 0
    has_r = wr is not None
    row_specs, row_args = _row_specs(stream, n_ctx_blocks)
    in_specs = row_specs + [
        pl.BlockSpec((1, 1, 6, d), lambda bi, i: (bi, jnp.where(i < n_ctx_blocks, 0, 1), 0, 0)),
        pl.BlockSpec((1, d), lambda bi, i: (0, 0)),
        pl.BlockSpec((d, n), lambda bi, i: (0, 0), pipeline_mode=pl.Buffered(1)),
        pl.BlockSpec((ROW_TILE, LANES), lambda bi, i: (i, 0)),
        pl.BlockSpec((ROW_TILE, LANES), lambda bi, i: (i, 0))]
    args = row_args + [mod, g.reshape(1, d), w, cos_t, sin_t]
    out_specs = [pl.BlockSpec((1, ROW_TILE, n), lambda bi, i: (bi, i, 0))]
    out_shape = [jax.ShapeDtypeStruct((b, t, n), BF16)]
    if has_r:
        in_specs.append(pl.BlockSpec((d, LANES), lambda bi, i: (0, 0)))
        args.append(wr)
        out_specs.append(pl.BlockSpec((1, ROW_TILE, LANES), lambda bi, i: (bi, i, 0)))
        out_shape.append(jax.ShapeDtypeStruct((b, t, LANES), BF16))
    return pl.pallas_call(
        functools.partial(_proj_kernel, n_row_refs=len(row_args), n_ctx_blocks=n_ctx_blocks,
                          chunk_ops=tuple(chunk_ops), quarter=quarter, has_r=has_r),
        grid=(b, t // ROW_TILE),
        in_specs=in_specs, out_specs=out_specs, out_shape=out_shape,
        compiler_params=_cparams("parallel", "parallel"),
        name="norm_proj",
    )(*args)


def _gla_constants():
    c = GLA_CHUNK
    idx = np.arange(c)
    cm = np.zeros((2, GLA_LEVELS + 2, c, c), np.float32)
    lmask = np.zeros((2, GLA_LEVELS + 2, c, c), np.float32)
    cm[0, 0] = (idx[None, :] <= idx[:, None])
    cm[1, 0] = (idx[None, :] >= idx[:, None])
    for l in range(GLA_LEVELS):
        s = c >> (l + 1)
        blk = idx // (2 * s)
        second = (idx % (2 * s)) >= s
        same = blk[:, None] == blk[None, :]
        sep_f = blk * 2 * s + s - 1
        cm[0, 1 + l] = (idx[None, :] <= sep_f[:, None])
        lmask[0, l] = same & second[:, None] & (~second)[None, :]
        sep_b = blk * 2 * s + s
        cm[1, 1 + l] = (idx[None, :] >= sep_b[:, None])
        lmask[1, l] = same & (~second)[:, None] & second[None, :]
    cm[:, GLA_LEVELS + 1] = 1.0
    lmask[:, GLA_LEVELS] = np.eye(c)
    lmask[:, GLA_LEVELS + 1] = cm[:, 0]
    return jnp.asarray(cm.reshape(2, (GLA_LEVELS + 2) * c, c), BF16), jnp.asarray(lmask, F32)


def _gla_kernel(q_ref, k_ref, v_ref, ag_ref, r_ref, gw_ref, gb_ref, ng_ref, cm_ref, lm_ref,
                o_ref, la_ref, of_ref, ob_ref, qe_ref, u_ref, ss_ref, dec_ref, st_ref,
                *, n_ctx_chunks, n_chunks, q_scale):
    c = GLA_CHUNK
    dv = st_ref.shape[1]
    r = r_ref[0]
    for d in range(2):
        z = _dot(r, gw_ref[d]) + gb_ref[d]
        la_ref[d] = (jnp.minimum(z, 0.0) - jnp.log(1.0 + jnp.exp(-jnp.abs(z)))) * (1.0 / GLA_GATE_NORM)
    st_ref[...] = jnp.zeros_like(st_ref)
    out_refs = (of_ref, ob_ref)

    def chunk_rows(ci):
        return pl.ds(pl.multiple_of(ci * c, c), c)

    def state_rows(ci):
        return pl.ds(pl.multiple_of(ci * dv, dv), dv)

    min_tot = None
    for d in range(2):
        m = jnp.min(jnp.sum(la_ref[d].reshape(n_chunks, c, la_ref.shape[2]), axis=1))
        min_tot = m if min_tot is None else jnp.minimum(min_tot, m)
    is_safe = min_tot >= -GLA_SAFE_RANGE

    group = 3 if n_chunks % 3 == 0 else 2

    def prep_fast(cp, carry):
        jobs = [(group * cp + u, d) for u in range(group) for d in range(2)]
        rows = [chunk_rows(ci) for ci, _ in jobs]
        qs = {u: q_ref[0, chunk_rows(group * cp + u), :].astype(F32) * q_scale for u in range(group)}
        ks = {u: k_ref[0, chunk_rows(group * cp + u), :].astype(F32) for u in range(group)}
        bcs = {}
        for j, (ci, d) in enumerate(jobs):
            g_hi, g_lo = _split2(la_ref[d, rows[j], :])
            cm = cm_ref[d, 0:c, :]
            bcs[j] = _dot(cm, g_hi) + _dot(cm, g_lo)
        qes, kins, decs = {}, {}, {}
        for j, (ci, d) in enumerate(jobs):
            bc = bcs[j]
            qes[j] = (qs[j // 2] * jnp.exp(bc)).astype(BF16)
            kins[j] = ks[j // 2] * jnp.exp(-bc)
            decs[j] = jnp.exp(bc[c - 1:c] if d == 0 else bc[0:1])
        atts = {j: _nt(qes[j], kins[j].astype(BF16)) for j in range(len(jobs))}
        for j, (ci, d) in enumerate(jobs):
            att = (lm_ref[d, GLA_LEVELS + 1] * atts[j]).astype(BF16)
            out_refs[d][rows[j], :] = _dot(att, v_ref[0, rows[j], :])
            qe_ref[d, rows[j], :] = qes[j]
            u_ref[d, state_rows(ci), :] = _tn(v_ref[0, rows[j], :], (kins[j] * decs[j]).astype(BF16))
            dec_ref[d, pl.ds(pl.multiple_of(ci * 8, 8), 8), :] = jnp.broadcast_to(decs[j], (8, decs[j].shape[1]))
        return carry

    def prep(ci, carry):
        rows = chunk_rows(ci)
        q = q_ref[0, rows, :].astype(F32) * q_scale
        k = k_ref[0, rows, :].astype(F32)
        v = v_ref[0, rows, :]
        qk = _nt(q.astype(BF16), k.astype(BF16))
        for d in range(2):
            g_hi, g_lo = _split2(la_ref[d, rows, :])
            cm = cm_ref[d]
            cum = _dot(cm, g_hi) + _dot(cm, g_lo)
            bc = cum[0:c]
            tot = cum[(GLA_LEVELS + 1) * c:(GLA_LEVELS + 2) * c]
            att = lm_ref[d, GLA_LEVELS] * qk
            for l in range(GLA_LEVELS):
                ref_l = cum[(1 + l) * c:(2 + l) * c]
                eq = jnp.exp(jnp.minimum(bc - ref_l, 0.0))
                ek = jnp.exp(jnp.minimum(ref_l - bc, 0.0))
                att = att + lm_ref[d, l] * _nt((q * eq).astype(BF16), (k * ek).astype(BF16))
            out_refs[d][rows, :] = _dot(att.astype(BF16), v)
            qe_ref[d, rows, :] = (q * jnp.exp(bc)).astype(BF16)
            u_ref[d, state_rows(ci), :] = _tn(v, (k * jnp.exp(tot - bc)).astype(BF16))
            dec_ref[d, pl.ds(pl.multiple_of(ci * 8, 8), 8), :] = jnp.exp(tot[0:8])
        return carry

    @pl.when(is_safe)
    def _():
        lax.fori_loop(0, n_chunks // group, prep_fast, 0)

    @pl.when(jnp.logical_not(is_safe))
    def _():
        lax.fori_loop(0, n_chunks, prep, 0)

    def scan(t, carry):
        cb = jnp.where(t < n_ctx_chunks, n_ctx_chunks - 1 - t, n_chunks - 1 - (t - n_ctx_chunks))
        for d, ci in ((0, t), (1, cb)):
            st = st_ref[d]
            ss_ref[d, state_rows(ci), :] = st.astype(BF16)
            dec = dec_ref[d, pl.ds(pl.multiple_of(ci * 8, 8), 1), :]
            st_ref[d] = st * dec + u_ref[d, state_rows(ci), :]
        return carry

    lax.fori_loop(0, n_chunks, scan, 0)

    def inter(ci, carry):
        rows = chunk_rows(ci)
        for d in range(2):
            out_refs[d][rows, :] += _nt(qe_ref[d, rows, :], ss_ref[d, state_rows(ci), :])
        return carry

    lax.fori_loop(0, n_chunks, inter, 0, unroll=2 * group)
    o = of_ref[...] + ob_ref[...]
    ag = ag_ref[0].astype(F32)
    o_ref[0] = (_rms(o) * ng_ref[...] * _silu(ag)).astype(BF16)


def _gla_call(p0, r, gw_pad, gb, norm_g, consts, n_ctx_rows, col_q, col_k, col_v, col_g, dk, dv):
    b, t, _ = p0.shape
    n_chunks = t // GLA_CHUNK
    cm, lm = consts
    kern = functools.partial(_gla_kernel, n_ctx_chunks=n_ctx_rows // GLA_CHUNK, n_chunks=n_chunks,
                             q_scale=float(dk) ** -0.5)
    full = lambda shape: pl.BlockSpec(shape, lambda bi, h: (0,) * len(shape))
    return pl.pallas_call(
        kern,
        grid=(b, GLA_HEADS),
        in_specs=[pl.BlockSpec((1, t, dk), lambda bi, h: (bi, 0, col_q // dk + h)),
                  pl.BlockSpec((1, t, dk), lambda bi, h: (bi, 0, col_k // dk + h)),
                  pl.BlockSpec((1, t, dv), lambda bi, h: (bi, 0, col_v // dv + h)),
                  pl.BlockSpec((1, t, dv), lambda bi, h: (bi, 0, col_g // dv + h)),
                  pl.BlockSpec((1, t, LANES), lambda bi, h: (bi, 0, 0)),
                  pl.BlockSpec((2, LANES, dk), lambda bi, h: (0, 0, h)),
                  pl.BlockSpec((2, 1, dk), lambda bi, h: (0, 0, h)),
                  full((1, dv)), full(cm.shape), full(lm.shape)],
        out_specs=pl.BlockSpec((1, t, dv), lambda bi, h: (bi, 0, h)),
        out_shape=jax.ShapeDtypeStruct((b, t, GLA_HEADS * dv), BF16),
        scratch_shapes=[pltpu.VMEM((2, t, dk), F32), pltpu.VMEM((t, dv), F32), pltpu.VMEM((t, dv), F32),
                        pltpu.VMEM((2, t, dk), BF16), pltpu.VMEM((2, n_chunks * dv, dk), F32),
                        pltpu.VMEM((2, n_chunks * dv, dk), BF16),
                        pltpu.VMEM((2, n_chunks * 8, dk), F32), pltpu.VMEM((2, dv, dk), F32)],
        compiler_params=_cparams("parallel", "parallel"),
        name="gla",
    )(p0, p0, p0, p0, r, gw_pad, gb, norm_g.reshape(1, dv), cm, lm)


def _swa_kernel(q_ref, k_ref, v_ref, sink_ref, o_ref, s_ref, *, n_ctx_rows, n_blocks):
    blk = SWA_BLOCK
    win = 3 * blk
    ncb = n_ctx_rows // blk
    nlb = n_blocks - ncb
    sink = sink_ref[0, :, 0:1] * float(np.log2(np.e))
    rel0 = (lax.broadcasted_iota(jnp.int32, (blk, win), 1) - lax.broadcasted_iota(jnp.int32, (blk, win), 0))
    no_bias = jnp.zeros((blk, n_ctx_rows), F32)

    def key_rows(j):
        kb0 = jnp.clip(j - ncb - 1, 0, nlb - 3)
        return kb0, pl.ds(pl.multiple_of(n_ctx_rows + kb0 * blk, blk), win)

    def scores(j, slot):
        kb0, rows = key_rows(j)
        q = q_ref[0, pl.ds(pl.multiple_of(j * blk, blk), blk), :]
        q4 = jnp.concatenate([q[:, g * blk:(g + 1) * blk] for g in range(SWA_GROUP)], axis=0)
        keys = jnp.concatenate([k_ref[0, rows, :], k_ref[0, 0:n_ctx_rows, :]], axis=0)
        rel = rel0 + jnp.where(j < ncb, 4 * win, (kb0 - (j - ncb)) * blk)
        bias = jnp.concatenate([jnp.where(jnp.abs(rel) <= SWA_WINDOW, 0.0, NEG_BIG), no_bias], axis=1)
        s_ref[slot] = _nt(q4, keys) + jnp.concatenate([bias] * SWA_GROUP, axis=0)

    def finish(j, slot):
        _, rows = key_rows(j)
        vals = jnp.concatenate([v_ref[0, rows, :], v_ref[0, 0:n_ctx_rows, :]], axis=0)
        s = s_ref[slot]
        m = jnp.maximum(jnp.max(s, axis=1, keepdims=True), sink)
        e = jnp.exp2(s - m)
        den = jnp.sum(e, axis=1, keepdims=True) + jnp.exp2(sink - m)
        o = _dot(e.astype(BF16), vals) / den
        o_ref[0, pl.ds(pl.multiple_of(j * blk, blk), blk), :] = jnp.concatenate(
            [o[g * blk:(g + 1) * blk] for g in range(SWA_GROUP)], axis=1).astype(BF16)

    scores(0, 0)

    def body(jj, carry):
        j0 = 2 * jj
        scores(j0 + 1, 1)
        finish(j0, 0)
        scores(j0 + 2, 0)
        finish(j0 + 1, 1)
        return carry

    lax.fori_loop(0, n_blocks // 2 - 1, body, 0)
    scores(n_blocks - 1, 1)
    finish(n_blocks - 2, 0)
    finish(n_blocks - 1, 1)


def _swa_call(p0, sink_col, n_ctx_rows, col_q, col_k, col_v, hd):
    b, t, _ = p0.shape
    blk = SWA_BLOCK
    nb = t // blk
    gq = SWA_GROUP * hd
    assert nb % 2 == 0 and nb - n_ctx_rows // blk >= 3 and SWA_WINDOW <= blk
    return pl.pallas_call(
        functools.partial(_swa_kernel, n_ctx_rows=n_ctx_rows, n_blocks=nb),
        grid=(b, SWA_KV_HEADS),
        in_specs=[pl.BlockSpec((1, t, gq), lambda bi, h: (bi, 0, col_q // gq + h)),
                  pl.BlockSpec((1, t, hd), lambda bi, h: (bi, 0, col_k // hd + h)),
                  pl.BlockSpec((1, t, hd), lambda bi, h: (bi, 0, col_v // hd + h)),
                  pl.BlockSpec((1, SWA_GROUP * blk, LANES), lambda bi, h: (h, 0, 0))],
        out_specs=pl.BlockSpec((1, t, gq), lambda bi, h: (bi, 0, h)),
        out_shape=jax.ShapeDtypeStruct((b, t, SWA_HEADS * hd), BF16),
        scratch_shapes=[pltpu.VMEM((2, SWA_GROUP * blk, 3 * blk + n_ctx_rows), F32)],
        compiler_params=_cparams("parallel", "parallel"),
        name="swa",
    )(p0, p0, p0, sink_col)


def _diff_kernel(lam_ref, q_ref, k_ref, v_ref, g_ref, o_ref, s_ref, kx_ref, vx_ref, qn_ref,
                 *, row_off, n_qblocks, lambda_init):
    tq = DIFF_QBLOCK
    hd2 = 2 * DIFF_HD
    n_keys = k_ref.shape[1]
    lam = lam_ref[...]
    s01 = jnp.sum(lam[0:1] * lam[1:2], axis=1, keepdims=True)
    s23 = jnp.sum(lam[2:3] * lam[3:4], axis=1, keepdims=True)
    lam_full = jnp.exp(s01) - jnp.exp(s23) + lambda_init
    lane = lax.broadcasted_iota(jnp.int32, (1, hd2), 1)
    first = lane < DIFF_HD
    col0 = lane == 0
    zero = jnp.zeros((), BF16)

    d_idx = lax.broadcasted_iota(jnp.int32, (hd2, 2 * hd2), 0)
    j_idx = lax.broadcasted_iota(jnp.int32, (hd2, 2 * hd2), 1)
    map_sum = jnp.where((d_idx < DIFF_HD) == (j_idx < hd2), 1.0, 0.0).astype(BF16)

    def sq_norms(x):
        xf = x.astype(F32)
        return _dot((xf * xf).astype(BF16), map_sum)

    kn = sq_norms(k_ref[0])
    kmax0 = jnp.sqrt(jnp.max(kn[:, 0:hd2], axis=0, keepdims=True)) * DIFF_BOUND_SLACK
    kmax1 = jnp.sqrt(jnp.max(kn[:, hd2:2 * hd2], axis=0, keepdims=True)) * DIFF_BOUND_SLACK
    qn_ref[...] = sq_norms(q_ref[0, row_off:, :])
    qmax0 = jnp.sqrt(jnp.max(qn_ref[:, 0:hd2], axis=0, keepdims=True))
    qmax1 = jnp.sqrt(jnp.max(qn_ref[:, hd2:2 * hd2], axis=0, keepdims=True))
    is_safe = jnp.max(jnp.maximum(qmax0 * kmax0, qmax1 * kmax1)) <= DIFF_SAFE_BOUND

    def q_block(j):
        q = q_ref[0, pl.ds(pl.multiple_of(row_off + j * tq, tq), tq), :]
        return q, jnp.where(first, q, zero), jnp.where(first, zero, q)

    def write_out(j, o):
        y = _rms(o) * g_ref[...] * (1.0 - lambda_init)
        o_ref[0, pl.ds(pl.multiple_of(j * tq, tq), tq), :] = y.astype(BF16)

    def scores_fast(j, slot):
        _, qa, qb = q_block(j)
        qn = jnp.sqrt(qn_ref[pl.ds(pl.multiple_of(j * tq, tq), tq), :])
        xa = jnp.where(col0, -(qn[:, 0:hd2] * kmax0), 0.0).astype(BF16)
        xb = jnp.where(col0, -(qn[:, hd2:2 * hd2] * kmax1), 0.0).astype(BF16)
        qq = jnp.concatenate([jnp.concatenate([qa, xa], axis=1), jnp.concatenate([qb, xb], axis=1)], axis=0)
        s_ref[slot] = _nt(qq, kx_ref[...])

    def finish_fast(j, slot):
        e = jnp.exp2(s_ref[slot]).astype(BF16)
        ov = _dot(e, vx_ref[...])
        p0 = ov[0:tq, 0:hd2] / ov[0:tq, hd2:2 * hd2]
        p1 = ov[tq:2 * tq, 0:hd2] / ov[tq:2 * tq, hd2:2 * hd2]
        write_out(j, p0 - lam_full * p1)

    def scores_max(j, slot):
        _, qa, qb = q_block(j)
        s_ref[slot] = _nt(jnp.concatenate([qa, qb], axis=0), k_ref[0])

    def finish_max(j, slot):
        s = s_ref[slot]
        e = jnp.exp2(s - jnp.max(s, axis=1, keepdims=True))
        z = jnp.sum(e, axis=1, keepdims=True)
        coef = lam_full * z[0:tq] / z[tq:2 * tq]
        a = e[0:tq] - coef * e[tq:2 * tq]
        write_out(j, _dot(a.astype(BF16), v_ref[0]) * (1.0 / z[0:tq]))

    def run(scores, finish, unroll=1):
        scores(0, 0)

        def body(jj, carry):
            j0 = 2 * jj
            scores(j0 + 1, 1)
            finish(j0, 0)
            scores(j0 + 2, 0)
            finish(j0 + 1, 1)
            return carry

        lax.fori_loop(0, n_qblocks // 2 - 1, body, 0, unroll=unroll)
        scores(n_qblocks - 1, 1)
        finish(n_qblocks - 2, 0)
        finish(n_qblocks - 1, 1)

    @pl.when(is_safe)
    def _():
        kx_ref[:, 0:hd2] = k_ref[0]
        kx_ref[:, hd2:2 * hd2] = jnp.broadcast_to(jnp.where(col0, 1.0, 0.0).astype(BF16), (n_keys, hd2))
        vx_ref[:, 0:hd2] = v_ref[0]
        vx_ref[:, hd2:2 * hd2] = jnp.ones((n_keys, hd2), BF16)
        run(scores_fast, finish_fast, unroll=True)

    @pl.when(jnp.logical_not(is_safe))
    def _():
        run(scores_max, finish_max)


def _diff_call(p1, lam, subln_g, n_ctx_rows, n_heads, lambda_init):
    b, t, _ = p1.shape
    s_len = t - n_ctx_rows
    hd2 = 2 * DIFF_HD
    n_qblocks = s_len // DIFF_QBLOCK
    assert n_qblocks >= 2 and n_qblocks % 2 == 0
    return pl.pallas_call(
        functools.partial(_diff_kernel, row_off=n_ctx_rows, n_qblocks=n_qblocks, lambda_init=lambda_init),
        grid=(b, n_heads),
        in_specs=[pl.BlockSpec((4, DIFF_HD), lambda bi, h: (0, 0)),
                  pl.BlockSpec((1, t, hd2), lambda bi, h: (bi, 0, h)),
                  pl.BlockSpec((1, t, hd2), lambda bi, h: (bi, 0, n_heads + h)),
                  pl.BlockSpec((1, t, hd2), lambda bi, h: (bi, 0, 2 * n_heads + h)),
                  pl.BlockSpec((1, hd2), lambda bi, h: (0, 0))],
        out_specs=pl.BlockSpec((1, s_len, hd2), lambda bi, h: (bi, 0, h)),
        out_shape=jax.ShapeDtypeStruct((b, s_len, n_heads * hd2), BF16),
        scratch_shapes=[pltpu.VMEM((2, 2 * DIFF_QBLOCK, t), F32), pltpu.VMEM((t, 2 * hd2), BF16),
                        pltpu.VMEM((t, 2 * hd2), BF16), pltpu.VMEM((s_len, 2 * hd2), F32)],
        compiler_params=_cparams("parallel", "parallel"),
        name="diff_attn",
    )(lam, p1, p1, p1, subln_g.reshape(1, hd2))


def _outproj_kernel(*refs, n_y, n_sub, n_row_refs, n_ctx_blocks, row_block_off):
    y_refs = refs[:n_y]
    w_refs = refs[n_y:2 * n_y]
    row_refs = refs[2 * n_y:2 * n_y + n_sub * n_row_refs]
    mod_ref, g_ref, wr_ref, xn_ref, h_ref, aff_ref = refs[2 * n_y + n_sub * n_row_refs:]
    n_exp = aff_ref.shape[1]
    subs = [slice(u * ROW_TILE, (u + 1) * ROW_TILE) for u in range(n_sub)]
    accs = []
    for rs in subs:
        acc = _dot(y_refs[0][0, rs, :], w_refs[0][...])
        for yr, wr in zip(y_refs[1:], w_refs[1:]):
            acc = acc + _dot(yr[0, rs, :], wr[...])
        accs.append(acc)
    for u, rs in enumerate(subs):
        x_rows = _load_rows(row_refs[u * n_row_refs:(u + 1) * n_row_refs], n_ctx_blocks, row_block_off)
        xn = x_rows + mod_ref[0, 0, 2:3, :] * accs[u]
        xn_ref[0, rs, :] = xn
        h = _rms(xn) * g_ref[...] * (1.0 + mod_ref[0, 0, 4:5, :]) + mod_ref[0, 0, 3:4, :]
        h_bf = h.astype(BF16)
        h_ref[0, rs, :] = h_bf
        logits = _dot(h_bf, wr_ref[...])
        logits = jnp.where(lax.broadcasted_iota(jnp.int32, (1, LANES), 1) < n_exp, logits, NEG_BIG)
        ex = jnp.exp(logits - jnp.max(logits, axis=1, keepdims=True))
        aff = ex / jnp.sum(ex, axis=1, keepdims=True)
        aff_ref[0, :, rs] = jnp.transpose(aff)[0:n_exp]


def _outproj_call(ys, ws, stream, mod, g, w_router, row_block_off, n_rows, n_ctx_blocks):
    b, d = mod.shape[0], mod.shape[3]
    n_y = len(ys)
    e = w_router.shape[1]
    w_router_pad = jnp.concatenate([w_router, jnp.zeros((d, LANES - e), F32)], axis=1).astype(BF16)
    all_latent = row_block_off >= n_ctx_blocks and not isinstance(stream, tuple)
    n_sub = 2 if all_latent and n_rows % (2 * ROW_TILE) == 0 else 1
    tile = n_sub * ROW_TILE
    row_specs, row_args = [], []
    for u in range(n_sub):
        specs_u, args_u = _row_specs(stream, n_ctx_blocks, row_block_off + u, n_sub)
        row_specs += specs_u
        row_args += args_u
    in_specs = ([pl.BlockSpec((1, tile, y.shape[2]), lambda bi, i: (bi, i, 0)) for y in ys]
                + [pl.BlockSpec(w.shape, lambda bi, i: (0, 0), pipeline_mode=pl.Buffered(1)) for w in ws]
                + row_specs
                + [pl.BlockSpec((1, 1, 6, d),
                                lambda bi, i: (bi, jnp.where(i * n_sub + row_block_off < n_ctx_blocks, 0, 1), 0, 0)),
                   pl.BlockSpec((1, d), lambda bi, i: (0, 0)),
                   pl.BlockSpec((d, LANES), lambda bi, i: (0, 0))])
    return pl.pallas_call(
        functools.partial(_outproj_kernel, n_y=n_y, n_sub=n_sub, n_row_refs=len(row_args) // n_sub,
                          n_ctx_blocks=n_ctx_blocks, row_block_off=row_block_off),
        grid=(b, n_rows // tile),
        in_specs=in_specs,
        out_specs=[pl.BlockSpec((1, tile, d), lambda bi, i: (bi, i, 0)),
                   pl.BlockSpec((1, tile, d), lambda bi, i: (bi, i, 0)),
                   pl.BlockSpec((1, e, tile), lambda bi, i: (bi, 0, i))],
        out_shape=[jax.ShapeDtypeStruct((b, n_rows, d), F32),
                   jax.ShapeDtypeStruct((b, n_rows, d), BF16),
                   jax.ShapeDtypeStruct((b, e, n_rows), F32)],
        compiler_params=_cparams("parallel", "parallel"),
        name="out_proj",
    )(*ys, *ws, *row_args, mod, g.reshape(1, d), w_router_pad)


def _prefix_lanes(m, upper):
    e, n = m.shape
    carry = jnp.zeros((e, 1), F32)
    outs = []
    for blk in range(n // LANES):
        mb = m[:, blk * LANES:(blk + 1) * LANES]
        inc = _dot(mb.astype(BF16), upper)
        outs.append(inc - mb + carry)
        carry = carry + jnp.sum(mb, axis=1, keepdims=True)
    return jnp.concatenate(outs, axis=1)


def _transpose_exact(x, eye):
    hi, mid, lo = _split3(x)
    return _nt(eye, hi) + (_nt(eye, mid) + _nt(eye, lo))


def _route_kernel(aff_ref, pos_ref, gate_ref, posn_ref, lo_ref, hi_ref, *, segs):
    n_exp = aff_ref.shape[1]
    lo_acc = jnp.zeros((n_exp, LANES), F32)
    hi_acc = jnp.zeros((n_exp, LANES), F32)
    li = lax.broadcasted_iota(jnp.int32, (LANES, LANES), 0)
    lj = lax.broadcasted_iota(jnp.int32, (LANES, LANES), 1)
    upper = jnp.where(li <= lj, 1.0, 0.0).astype(BF16)
    eye = jnp.where(li == lj, 1.0, 0.0).astype(BF16)
    all_bits = [pltpu.bitcast(aff_ref[0, :, off:off + n], jnp.int32) for (off, n, _, _) in segs]

    def search(it, thrs):
        out = []
        for bits, (_, _, cap, _), thr in zip(all_bits, segs, thrs):
            cand = thr | jnp.left_shift(jnp.int32(1), 30 - it)
            cnt = jnp.sum(jnp.where(bits >= cand, 1.0, 0.0), axis=1, keepdims=True)
            out.append(jnp.where(cnt >= cap, cand, thr))
        return tuple(out)

    all_thr = lax.fori_loop(0, 31, search, tuple(jnp.zeros((n_exp, 1), jnp.int32) for _ in segs))
    for (off, n, cap, slot_off), bits, thr in zip(segs, all_bits, all_thr):
        a = aff_ref[0, :, off:off + n]
        gt = jnp.where(bits > thr, 1.0, 0.0)
        eq = jnp.where(bits == thr, 1.0, 0.0)
        need = cap - jnp.sum(gt, axis=1, keepdims=True)
        sel = gt + eq * jnp.where(_prefix_lanes(eq, upper) < need, 1.0, 0.0)
        slot = _prefix_lanes(sel, upper)
        pos = jnp.where(sel > 0.0, slot + slot_off, -1.0)
        pos_ref[0, :, off:off + n] = pos.astype(jnp.int32)
        gate_ref[0, :, off:off + n] = sel * a
        for blk in range(n // LANES):
            cols = slice(blk * LANES, (blk + 1) * LANES)
            rows = slice(off + blk * LANES, off + (blk + 1) * LANES)
            posn_ref[0, rows, :] = _transpose_exact(pos[:, cols], eye).astype(jnp.int32)
        assert cap <= 256 and (off + n) // LANES <= LANES
        tok = lax.broadcasted_iota(jnp.int32, (n, LANES), 0)
        blk_start = (lax.broadcasted_iota(jnp.int32, (n, LANES), 1) - off // LANES) * LANES
        lo_acc = lo_acc + _dot(slot.astype(BF16), jnp.where(tok == blk_start, 1.0, 0.0).astype(BF16))
        hi_acc = hi_acc + _dot((slot + sel).astype(BF16),
                               jnp.where(tok == blk_start + (LANES - 1), 1.0, 0.0).astype(BF16))
    lo_ref[0] = lo_acc.astype(jnp.int32)
    hi_ref[0] = hi_acc.astype(jnp.int32)


def _route_call(aff_t, segs):
    b, e, t = aff_t.shape
    return pl.pallas_call(
        functools.partial(_route_kernel, segs=tuple(segs)),
        grid=(b,),
        in_specs=[pl.BlockSpec((1, e, t), lambda bi: (bi, 0, 0))],
        out_specs=[pl.BlockSpec((1, e, t), lambda bi: (bi, 0, 0)),
                   pl.BlockSpec((1, e, t), lambda bi: (bi, 0, 0)),
                   pl.BlockSpec((1, t, e), lambda bi: (bi, 0, 0)),
                   pl.BlockSpec((1, e, LANES), lambda bi: (bi, 0, 0)),
                   pl.BlockSpec((1, e, LANES), lambda bi: (bi, 0, 0))],
        out_shape=[jax.ShapeDtypeStruct((b, e, t), jnp.int32),
                   jax.ShapeDtypeStruct((b, e, t), F32),
                   jax.ShapeDtypeStruct((b, t, e), jnp.int32),
                   jax.ShapeDtypeStruct((b, e, LANES), jnp.int32),
                   jax.ShapeDtypeStruct((b, e, LANES), jnp.int32)],
        compiler_params=_cparams("parallel"),
        name="route",
    )(aff_t)


def _gather_kernel(lo_ref, hi_ref, pos_ref, gate_ref, h_ref, o_ref, g_ref,
                   *, n_ctx_rows, n_lat_rows, cap_lat, sblk, win, margin):
    sb = pl.program_id(1)
    n_exp = pos_ref.shape[1]
    n_lat_blocks = cap_lat // sblk
    tok_per_blk = n_lat_rows // n_lat_blocks

    def emit(slot0, tok0, n_tok):
        rows = pl.ds(tok0, n_tok)
        slots = lax.broadcasted_iota(jnp.int32, (sblk, 1), 0) + slot0
        hits = [pos_ref[0, e:e + 1, rows] == slots for e in range(n_exp)]
        onehot = jnp.concatenate([jnp.where(h, 1.0, 0.0).astype(BF16) for h in hits], axis=0)
        res = _dot(onehot, h_ref[0, rows, :])
        for e in range(n_exp):
            o_ref[e, 0] = res[e * sblk:(e + 1) * sblk].astype(BF16)
            gsel = jnp.sum(jnp.where(hits[e], gate_ref[0, e:e + 1, rows], 0.0), axis=1, keepdims=True)
            g_ref[e, 0] = jnp.broadcast_to(gsel, (sblk, LANES))

    slot0 = sb * sblk
    w0 = jnp.clip(sb * tok_per_blk - margin, 0, n_lat_rows - win)
    tok0 = pl.multiple_of(n_ctx_rows + w0, LANES)
    first_blk = (n_ctx_rows + w0) // LANES
    fits = None
    for e in range(n_exp):
        ok = jnp.logical_and(lo_ref[0, e, first_blk] <= slot0,
                             hi_ref[0, e, first_blk + win // LANES - 1] >= slot0 + sblk)
        fits = ok if fits is None else jnp.logical_and(fits, ok)
    is_lat = sb < n_lat_blocks

    @pl.when(jnp.logical_and(is_lat, fits))
    def _():
        emit(slot0, tok0, win)

    @pl.when(jnp.logical_and(is_lat, jnp.logical_not(fits)))
    def _():
        emit(slot0, n_ctx_rows, n_lat_rows)

    if n_ctx_rows:
        @pl.when(jnp.logical_not(is_lat))
        def _():
            emit(cap_lat, 0, n_ctx_rows)


def _slot_window(tt, cap_lat, n_exp):
    return min(cap_lat, max(16, 2 * tt * EC_FACTOR // n_exp))


def _gather_call(lo, hi, pos, gate, h, cap_lat, cap_ctx, n_ctx_rows):
    b, e, t = pos.shape
    d = h.shape[2]
    n_slots = cap_lat + cap_ctx
    n_lat_rows = t - n_ctx_rows
    sblk = cap_ctx if cap_ctx else 32
    assert cap_lat % sblk == 0 and sblk % 16 == 0
    n_lat_blocks = cap_lat // sblk
    tok_per_blk = n_lat_rows // n_lat_blocks
    margin = tok_per_blk
    win = min(n_lat_rows, 3 * tok_per_blk)
    assert tok_per_blk % LANES == 0 and win % LANES == 0
    smem = lambda: pl.BlockSpec((1, e, LANES), lambda bi, si: (bi, 0, 0), memory_space=pltpu.SMEM)
    return pl.pallas_call(
        functools.partial(_gather_kernel, n_ctx_rows=n_ctx_rows, n_lat_rows=n_lat_rows, cap_lat=cap_lat,
                          sblk=sblk, win=win, margin=margin),
        grid=(b, n_slots // sblk),
        in_specs=[smem(), smem(),
                  pl.BlockSpec((1, e, t), lambda bi, si: (bi, 0, 0)),
                  pl.BlockSpec((1, e, t), lambda bi, si: (bi, 0, 0)),
                  pl.BlockSpec((1, t, d), lambda bi, si: (bi, 0, 0))],
        out_specs=[pl.BlockSpec((e, 1, sblk, d), lambda bi, si: (0, bi, si, 0)),
                   pl.BlockSpec((e, 1, sblk, LANES), lambda bi, si: (0, bi, si, 0))],
        out_shape=[jax.ShapeDtypeStruct((e, b, n_slots, d), BF16),
                   jax.ShapeDtypeStruct((e, b, n_slots, LANES), F32)],
        compiler_params=_cparams("parallel", "arbitrary"),
        name="gather",
    )(lo, hi, pos, gate, h)


def _ffn_up_kernel(x_ref, wg_ref, wu_ref, o_ref, wgb_ref, wub_ref):
    @pl.when(pl.program_id(2) == 0)
    def _():
        wgb_ref[...] = wg_ref[0, 0].astype(BF16)
        wub_ref[...] = wu_ref[0, 0].astype(BF16)

    x = x_ref[0]
    g = _dot(x, wgb_ref[...])
    u = _dot(x, wub_ref[...])
    o_ref[0] = (_silu(g) * u).astype(BF16)


def _ffn_down_kernel(a_ref, w_ref, gate_ref, o_ref, wb_ref):
    @pl.when(pl.program_id(2) == 0)
    def _():
        wb_ref[...] = w_ref[0, 0].astype(BF16)

    gate = jnp.concatenate([gate_ref[0]] * (o_ref.shape[2] // LANES), axis=1)
    o_ref[0] = (_dot(a_ref[0], wb_ref[...]) * gate).astype(BF16)


def _ffn_call(xs, gates, w_gu, w_dn, layer):
    e, m, d = xs.shape
    f = w_dn.shape[2]
    tm = _pick(m, (1152, 1024, 768, 512, 256, 128, 64, 32, 16, 8))
    tf = _pick(f, (512, 256, 128))
    nf = f // tf
    act = pl.pallas_call(
        _ffn_up_kernel,
        grid=(e, nf, m // tm),
        in_specs=[pl.BlockSpec((1, tm, d), lambda ei, j, i: (ei, i, 0)),
                  pl.BlockSpec((1, 1, d, tf), lambda ei, j, i: (layer, ei, 0, j)),
                  pl.BlockSpec((1, 1, d, tf), lambda ei, j, i: (layer, ei, 0, j + nf))],
        out_specs=pl.BlockSpec((1, tm, tf), lambda ei, j, i: (ei, i, j)),
        out_shape=jax.ShapeDtypeStruct((e, m, f), BF16),
        scratch_shapes=[pltpu.VMEM((d, tf), BF16), pltpu.VMEM((d, tf), BF16)],
        compiler_params=_cparams("parallel", "arbitrary", "arbitrary"),
        name="ffn_up",
    )(xs, w_gu, w_gu)
    tn = _pick(d, (1024, 512, 256, 128))
    return pl.pallas_call(
        _ffn_down_kernel,
        grid=(e, d // tn, m // tm),
        in_specs=[pl.BlockSpec((1, tm, f), lambda ei, j, i: (ei, i, 0)),
                  pl.BlockSpec((1, 1, f, tn), lambda ei, j, i: (layer, ei, 0, j)),
                  pl.BlockSpec((1, tm, LANES), lambda ei, j, i: (ei, i, 0))],
        out_specs=pl.BlockSpec((1, tm, tn), lambda ei, j, i: (ei, i, j)),
        out_shape=jax.ShapeDtypeStruct((e, m, d), BF16),
        scratch_shapes=[pltpu.VMEM((f, tn), BF16)],
        compiler_params=_cparams("parallel", "arbitrary", "arbitrary"),
        name="ffn_down",
    )(act, w_dn, gates)


def _combine_kernel(lo_ref, hi_ref, posn_ref, y_ref, x_ref, mod_ref, gf_ref, o_ref,
                    *, cap_lat, cap_ctx, n_ctx_tiles, blocks_per_tile, window, final_norm):
    n_exp = posn_ref.shape[2]
    i = pl.program_id(1)
    lane = lax.broadcasted_iota(jnp.int32, (1, n_exp), 1)
    posn = posn_ref[0].astype(F32)

    def pcol(e):
        return jnp.sum(jnp.where(lane == e, posn, 0.0), axis=1, keepdims=True)

    def scatter(slot0, cap):
        slots = lax.broadcasted_iota(jnp.int32, (1, cap), 1).astype(F32) + slot0
        hots = [jnp.where(pcol(e) == slots, 1.0, 0.0).astype(BF16) for e in range(n_exp)]
        rows = [y_ref[e, 0, slot0:slot0 + cap, :] for e in range(n_exp)]
        return _dot(jnp.concatenate(hots, axis=1), jnp.concatenate(rows, axis=0))

    def scatter_window(starts):
        w_tot = n_exp * window
        grp = lax.broadcasted_iota(jnp.int32, (n_exp, w_tot), 1) // window
        spread = jnp.where(grp == lax.broadcasted_iota(jnp.int32, (n_exp, w_tot), 0), 1.0, 0.0).astype(BF16)
        bpos = _dot(posn.astype(BF16), spread)
        wlane = lax.broadcasted_iota(jnp.int32, (1, w_tot), 1)
        target = (wlane % window).astype(F32)
        for e, st in enumerate(starts):
            target = jnp.where(wlane // window == e, target + st.astype(F32), target)
        onehot = jnp.where(bpos == target, 1.0, 0.0).astype(BF16)
        rows = [y_ref[e, 0, pl.ds(pl.multiple_of(st, 16), window), :] for e, st in enumerate(starts)]
        return _dot(onehot, jnp.concatenate(rows, axis=0))

    def finish(total, g2):
        out = x_ref[0] + g2 * total
        if final_norm:
            out = _rms(out) * gf_ref[...]
        o_ref[0] = out

    blk0 = i * blocks_per_tile
    starts = []
    fits = None
    for e in range(n_exp):
        lo = lo_ref[0, e, blk0]
        hi = hi_ref[0, e, blk0 + blocks_per_tile - 1]
        st = jnp.minimum(lax.shift_left(lax.shift_right_logical(lo, 4), 4), cap_lat - window)
        ok = hi - st <= window
        fits = ok if fits is None else jnp.logical_and(fits, ok)
        starts.append(st)
    is_lat = i >= n_ctx_tiles

    @pl.when(jnp.logical_and(is_lat, fits))
    def _():
        finish(scatter_window(starts), mod_ref[0, 1, 5:6, :])

    @pl.when(jnp.logical_and(is_lat, jnp.logical_not(fits)))
    def _():
        finish(scatter(0, cap_lat), mod_ref[0, 1, 5:6, :])

    if n_ctx_tiles:
        @pl.when(jnp.logical_not(is_lat))
        def _():
            finish(scatter(cap_lat, cap_ctx), mod_ref[0, 0, 5:6, :])


def _combine_call(lo, hi, posn, y, xa, mod, g_final, cap_lat, cap_ctx, n_ctx_rows, final_norm):
    b, t, d = xa.shape
    e = posn.shape[2]
    n_slots = y.shape[2]
    tt = ROW_TILE
    assert n_ctx_rows % tt == 0 and t % tt == 0
    window = _slot_window(tt, cap_lat, e)
    smem = lambda: pl.BlockSpec((1, e, LANES), lambda bi, i: (bi, 0, 0), memory_space=pltpu.SMEM)
    return pl.pallas_call(
        functools.partial(_combine_kernel, cap_lat=cap_lat, cap_ctx=cap_ctx, n_ctx_tiles=n_ctx_rows // tt,
                          blocks_per_tile=tt // LANES, window=window, final_norm=final_norm),
        grid=(b, t // tt),
        in_specs=[smem(), smem(),
                  pl.BlockSpec((1, tt, e), lambda bi, i: (bi, i, 0)),
                  pl.BlockSpec((e, 1, n_slots, d), lambda bi, i: (0, bi, 0, 0)),
                  pl.BlockSpec((1, tt, d), lambda bi, i: (bi, i, 0)),
                  pl.BlockSpec((1, 2, 6, d), lambda bi, i: (bi, 0, 0, 0)),
                  pl.BlockSpec((1, d), lambda bi, i: (0, 0))],
        out_specs=pl.BlockSpec((1, tt, d), lambda bi, i: (bi, i, 0)),
        out_shape=jax.ShapeDtypeStruct((b, t, d), F32),
        compiler_params=_cparams("parallel", "arbitrary"),
        name="combine",
    )(lo, hi, posn, y, xa, mod, g_final.reshape(1, d))


def _rope_tables(n_tokens, n_ctx_rows, head_dim):
    rows = n_tokens // GRID_W
    r = jnp.repeat(jnp.arange(rows, dtype=F32), GRID_W)
    col = jnp.tile(jnp.arange(GRID_W, dtype=F32), rows)
    axis_dim = head_dim // 2
    inv = ROPE_THETA ** (-jnp.arange(0, axis_dim, 2, dtype=F32) / axis_dim)
    ar, ac = r[:, None] * inv, col[:, None] * inv
    cos_p = jnp.concatenate([jnp.cos(ar), jnp.cos(ar), jnp.cos(ac), jnp.cos(ac)], axis=-1)
    sin_p = jnp.concatenate([-jnp.sin(ar), jnp.sin(ar), -jnp.sin(ac), jnp.sin(ac)], axis=-1)
    reps = LANES // head_dim
    cos_p, sin_p = jnp.tile(cos_p, (1, reps)), jnp.tile(sin_p, (1, reps))
    cos_t = jnp.concatenate([jnp.ones((n_ctx_rows, LANES), F32), cos_p], axis=0)
    sin_t = jnp.concatenate([jnp.zeros((n_ctx_rows, LANES), F32), sin_p], axis=0)
    return cos_t, sin_t


def _moe(h2, aff_t, x_res, mod, g_final, w_gu, w_dn, layer, segs, cap_lat, cap_ctx, n_ctx_rows, final_norm):
    b, t, d = h2.shape
    e = aff_t.shape[1]
    n_slots = cap_lat + cap_ctx
    pos, gate, posn, lo, hi = _route_call(aff_t, segs)
    xs, gs = _gather_call(lo, hi, pos, gate, h2, cap_lat, cap_ctx, n_ctx_rows)
    y = _ffn_call(xs.reshape(e, b * n_slots, d), gs.reshape(e, b * n_slots, LANES), w_gu, w_dn, layer)
    return _combine_call(lo, hi, posn, y.reshape(e, b, n_slots, d), x_res, mod, g_final, cap_lat, cap_ctx,
                         n_ctx_rows, final_norm)


def kernel(x, c, ctx, c_ctx, w_ada, b_ada, g_norm_mix, g_norm_ffn, w_in_even, gla_gate_w, gla_gate_b, gla_norm_g, swa_sink, w_out_even, w_qkv_odd, diff_lambda, diff_subln_g, w_out_odd, w_router, w_gate_up, w_down, g_final):
    b, s_len, d = x.shape
    lc = ctx.shape[1]
    depth = w_ada.shape[0]
    assert depth == 2 and lc % ROW_TILE == 0 and s_len % ROW_TILE == 0 and s_len % GRID_W == 0
    t = lc + s_len
    ncb = lc // ROW_TILE
    gla_dk, gla_dv = d // 4 // GLA_HEADS, d // 2 // GLA_HEADS
    swa_hd = d // 2 // SWA_HEADS
    diff_heads = d // (2 * DIFF_HD)
    n_exp = w_router.shape[2]
    assert gla_dk == LANES and swa_hd == LANES and 2 * DIFF_HD == LANES

    n_rows = -(-(b + 1) // 8) * 8
    cc = jnp.concatenate([c, c_ctx[None, :], jnp.zeros((n_rows - b - 1, d), F32)], axis=0)
    m_all = _ada_call(cc, w_ada, b_ada).reshape(depth, n_rows, 6, d)

    def mod_for(layer):
        lat = m_all[layer, :b]
        ctxm = jnp.broadcast_to(m_all[layer, b][None], (b, 6, d))
        return jnp.stack([ctxm, lat], axis=1)

    xa = (ctx, x)

    mod0 = mod_for(0)
    w_in = w_in_even[0]
    sizes = (GLA_HEADS * gla_dk, GLA_HEADS * gla_dk, GLA_HEADS * gla_dv, GLA_HEADS * gla_dv,
             GLA_GATE_RANK, GLA_GATE_RANK, SWA_HEADS * swa_hd, SWA_KV_HEADS * swa_hd, SWA_KV_HEADS * swa_hd)
    offs = np.concatenate([[0], np.cumsum(sizes)])
    seg = lambda i: w_in[:, offs[i]:offs[i + 1]]
    w_main = jnp.concatenate([seg(0), seg(1), seg(2), seg(3), seg(6), seg(7), seg(8)], axis=1).astype(BF16)
    w_rank = jnp.concatenate([seg(4), seg(5), jnp.zeros((d, LANES - 2 * GLA_GATE_RANK), F32)],
                             axis=1).astype(BF16)
    col_aq, col_ak = 0, sizes[0]
    col_av = col_ak + sizes[1]
    col_ag = col_av + sizes[2]
    col_bq = col_ag + sizes[3]
    col_bk = col_bq + sizes[6]
    col_bv = col_bk + sizes[7]
    n_main = col_bv + sizes[8]
    swa_q_scale = float(swa_hd) ** -0.5 * float(np.log2(np.e))
    rope0 = [(col_bq <= ch * COL_CHUNK < col_bv, swa_q_scale if col_bq <= ch * COL_CHUNK < col_bk else 1.0)
             for ch in range(n_main // COL_CHUNK)]
    cos_b, sin_b = _rope_tables(s_len, lc, swa_hd)
    p0, r0 = _proj_call(xa, mod0, g_norm_mix[0], w_main, cos_b, sin_b, rope0, swa_hd // 4, ncb, wr=w_rank)

    gw = gla_gate_w[0]
    gw_pad = jnp.zeros((2, LANES, GLA_HEADS * gla_dk), F32)
    gw_pad = gw_pad.at[0, 0:GLA_GATE_RANK].set(gw[0]).at[1, GLA_GATE_RANK:2 * GLA_GATE_RANK].set(gw[1])
    a_out = _gla_call(p0, r0, gw_pad.astype(BF16), gla_gate_b[0].reshape(2, 1, -1), gla_norm_g[0],
                      _gla_constants(), lc, col_aq, col_ak, col_av, col_ag, gla_dk, gla_dv)
    sink_col = jnp.broadcast_to(
        jnp.repeat(swa_sink[0].reshape(SWA_KV_HEADS, SWA_GROUP), SWA_BLOCK, axis=1)[:, :, None],
        (SWA_KV_HEADS, SWA_GROUP * SWA_BLOCK, LANES))
    b_out = _swa_call(p0, sink_col, lc, col_bq, col_bk, col_bv, swa_hd)

    w_o = w_out_even[0].astype(BF16)
    n_a = GLA_HEADS * gla_dv
    xa1, h2, aff_t = _outproj_call([a_out, b_out], [w_o[:n_a], w_o[n_a:]], xa, mod0, g_norm_ffn[0],
                                   w_router[0], 0, t, ncb)
    cap_lat = s_len * EC_FACTOR // n_exp
    cap_ctx = lc * EC_FACTOR // n_exp
    segs0 = [(lc, s_len, cap_lat, 0), (0, lc, cap_ctx, cap_lat)]
    xa2 = _moe(h2, aff_t, xa1, mod0, g_final, w_gate_up, w_down, 0, segs0, cap_lat, cap_ctx, lc, False)

    mod1 = mod_for(1)
    lambda_init = 0.8 - 0.6 * float(np.exp(-0.3 * 1))
    w_qkv = w_qkv_odd[0].astype(BF16)
    n_qkv = w_qkv.shape[1]
    q_scale = DIFF_HD ** -0.5 * float(np.log2(np.e))
    rope1 = [(ch * COL_CHUNK < 2 * d, q_scale if ch * COL_CHUNK < d else 1.0)
             for ch in range(n_qkv // COL_CHUNK)]
    cos_c, sin_c = _rope_tables(s_len, lc, DIFF_HD)
    (p1,) = _proj_call(xa2, mod1, g_norm_mix[1], w_qkv, cos_c, sin_c, rope1, DIFF_HD // 4, ncb)
    y1 = _diff_call(p1, diff_lambda[0], diff_subln_g[0], lc, diff_heads, lambda_init)
    x3, h2b, aff_tb = _outproj_call([y1], [w_out_odd[0].astype(BF16)], xa2, mod1, g_norm_ffn[1],
                                    w_router[1], ncb, s_len, ncb)
    segs1 = [(0, s_len, cap_lat, 0)]
    return _moe(h2b, aff_tb, x3, mod1, g_final, w_gate_up, w_down, 1, segs1, cap_lat, 0, 0, True)
```

```python
import functools

import numpy as np
import jax
import jax.numpy as jnp
from jax import lax
from jax.experimental import pallas as pl
from jax.experimental.pallas import tpu as pltpu

F32 = jnp.float32
BF16 = jnp.bfloat16

GRID_W = 64
ROPE_THETA = 10000.0
NORM_EPS = 1e-6
GLA_HEADS = 4
GLA_GATE_RANK = 16
GLA_GATE_NORM = 16.0
SWA_HEADS = 8
SWA_KV_HEADS = 2
SWA_GROUP = SWA_HEADS // SWA_KV_HEADS
SWA_WINDOW = 128
SWA_BLOCK = 128
DIFF_HD = 64
N_EXPERTS = 16
EC_FACTOR = 2

LANES = 128
ROW_TILE = 256
COL_CHUNK = 256
GLA_CHUNK = 256
GLA_LEVELS = 8
GLA_SAFE_RANGE = 60.0
DIFF_QBLOCK = 256
DIFF_SAFE_BOUND = 60.0
DIFF_BOUND_SLACK = 1.0 + 2.0 ** -6
VMEM_LIMIT = 56 * 1024 * 1024
NEG_BIG = -1e30


def _cparams(*sem):
    return pltpu.CompilerParams(dimension_semantics=sem, vmem_limit_bytes=VMEM_LIMIT)


def _pick(n, cands):
    for c in cands:
        if n % c == 0:
            return c
    raise ValueError(f"no tile for {n} in {cands}")


def _nt(a, b):
    return lax.dot_general(a, b, (((1,), (1,)), ((), ())), preferred_element_type=F32)


def _tn(a, b):
    return lax.dot_general(a, b, (((0,), (0,)), ((), ())), preferred_element_type=F32)


def _dot(a, b):
    return jnp.dot(a, b, preferred_element_type=F32)


def _split3(a):
    hi = a.astype(BF16)
    r1 = a - hi.astype(F32)
    mid = r1.astype(BF16)
    lo = (r1 - mid.astype(F32)).astype(BF16)
    return hi, mid, lo


def _split2(a):
    hi = a.astype(BF16)
    lo = (a - hi.astype(F32)).astype(BF16)
    return hi, lo


def _silu(x):
    return x * (1.0 / (1.0 + jnp.exp(-x)))


def _rms(x, eps=NORM_EPS):
    return x * lax.rsqrt(jnp.mean(x * x, axis=-1, keepdims=True) + eps)


def _row_specs(stream, n_ctx_blocks, off=0, stride=1):
    if isinstance(stream, tuple):
        assert stride == 1
        ctx, x = stream
        d = x.shape[2]
        return ([pl.BlockSpec((1, ROW_TILE, d), lambda bi, i: (bi, jnp.minimum(i + off, n_ctx_blocks - 1), 0)),
                 pl.BlockSpec((1, ROW_TILE, d), lambda bi, i: (bi, jnp.maximum(i + off - n_ctx_blocks, 0), 0))],
                [ctx, x])
    d = stream.shape[2]
    return [pl.BlockSpec((1, ROW_TILE, d), lambda bi, i: (bi, i * stride + off, 0))], [stream]


def _load_rows(row_refs, n_ctx_blocks, off=0):
    if len(row_refs) == 2:
        return jnp.where(pl.program_id(1) + off < n_ctx_blocks, row_refs[0][0], row_refs[1][0])
    return row_refs[0][0]


def _ada_kernel(c_ref, w_ref, b_ref, o_ref):
    s = _silu(c_ref[...])
    s_hi, s_lo = _split2(s)
    w_hi, w_lo = _split2(w_ref[0])
    acc = _dot(s_hi, w_hi) + (_dot(s_lo, w_hi) + _dot(s_hi, w_lo))
    o_ref[0] = acc + b_ref[0]


def _ada_call(cc, w_ada, b_ada):
    depth, d, n = w_ada.shape
    r = cc.shape[0]
    tn = _pick(n, (768, 512, 256, 128))
    return pl.pallas_call(
        _ada_kernel,
        grid=(depth, n // tn),
        in_specs=[pl.BlockSpec((r, d), lambda l, j: (0, 0)),
                  pl.BlockSpec((1, d, tn), lambda l, j: (l, 0, j)),
                  pl.BlockSpec((1, 1, tn), lambda l, j: (l, 0, j))],
        out_specs=pl.BlockSpec((1, r, tn), lambda l, j: (l, 0, j)),
        out_shape=jax.ShapeDtypeStruct((depth, r, n), F32),
        compiler_params=_cparams("parallel", "parallel"),
        name="adaln",
    )(cc, w_ada, b_ada.reshape(depth, 1, n))


def _proj_kernel(*refs, n_row_refs, n_ctx_blocks, chunk_ops, quarter, has_r):
    row_refs = refs[:n_row_refs]
    mod_ref, g_ref, w_ref, cos_ref, sin_ref = refs[n_row_refs:n_row_refs + 5]
    rest = refs[n_row_refs + 5:]
    if has_r:
        wr_ref, o_ref, r_ref = rest
    else:
        (o_ref,) = rest
    x = _load_rows(row_refs, n_ctx_blocks)
    y = _rms(x) * g_ref[...]
    h = (y * (1.0 + mod_ref[0, 0, 1:2, :]) + mod_ref[0, 0, 0:1, :]).astype(BF16)
    reps = COL_CHUNK // LANES
    cos_t = jnp.concatenate([cos_ref[...]] * reps, axis=1)
    sin_t = jnp.concatenate([sin_ref[...]] * reps, axis=1)
    lane = lax.broadcasted_iota(jnp.int32, (1, COL_CHUNK), 1)
    first = (lane % (2 * quarter)) < quarter
    for c, (rope, scale) in enumerate(chunk_ops):
        acc = _dot(h, w_ref[:, c * COL_CHUNK:(c + 1) * COL_CHUNK])
        if rope:
            partner = jnp.where(first, pltpu.roll(acc, COL_CHUNK - quarter, 1), pltpu.roll(acc, quarter, 1))
            acc = acc * cos_t + partner * sin_t
        if scale != 1.0:
            acc = acc * scale
        o_ref[0, :, c * COL_CHUNK:(c + 1) * COL_CHUNK] = acc.astype(BF16)
    if has_r:
        r_ref[0] = _dot(h, wr_ref[...]).astype(BF16)


def _proj_call(stream, mod, g, w, cos_t, sin_t, chunk_ops, quarter, n_ctx_blocks, wr=None):
    d, n = w.shape
    t = cos_t.shape[0]
    b = mod.shape[0]
    assert n == len(chunk_ops) * COL_CHUNK and t % ROW_TILE == 0
    has_r = wr is not None
    row_specs, row_args = _row_specs(stream, n_ctx_blocks)
    in_specs = row_specs + [
        pl.BlockSpec((1, 1, 6, d), lambda bi, i: (bi, jnp.where(i < n_ctx_blocks, 0, 1), 0, 0)),
        pl.BlockSpec((1, d), lambda bi, i: (0, 0)),
        pl.BlockSpec((d, n), lambda bi, i: (0, 0), pipeline_mode=pl.Buffered(1)),
        pl.BlockSpec((ROW_TILE, LANES), lambda bi, i: (i, 0)),
        pl.BlockSpec((ROW_TILE, LANES), lambda bi, i: (i, 0))]
    args = row_args + [mod, g.reshape(1, d), w, cos_t, sin_t]
    out_specs = [pl.BlockSpec((1, ROW_TILE, n), lambda bi, i: (bi, i, 0))]
    out_shape = [jax.ShapeDtypeStruct((b, t, n), BF16)]
    if has_r:
        in_specs.append(pl.BlockSpec((d, LANES), lambda bi, i: (0, 0)))
        args.append(wr)
        out_specs.append(pl.BlockSpec((1, ROW_TILE, LANES), lambda bi, i: (bi, i, 0)))
        out_shape.append(jax.ShapeDtypeStruct((b, t, LANES), BF16))
    return pl.pallas_call(
        functools.partial(_proj_kernel, n_row_refs=len(row_args), n_ctx_blocks=n_ctx_blocks,
                          chunk_ops=tuple(chunk_ops), quarter=quarter, has_r=has_r),
        grid=(b, t // ROW_TILE),
        in_specs=in_specs, out_specs=out_specs, out_shape=out_shape,
        compiler_params=_cparams("parallel", "parallel"),
        name="norm_proj",
    )(*args)


def _gla_constants():
    c = GLA_CHUNK
    idx = np.arange(c)
    cm = np.zeros((2, GLA_LEVELS + 2, c, c), np.float32)
    lmask = np.zeros((2, GLA_LEVELS + 2, c, c), np.float32)
    cm[0, 0] = (idx[None, :] <= idx[:, None])
    cm[1, 0] = (idx[None, :] >= idx[:, None])
    for l in range(GLA_LEVELS):
        s = c >> (l + 1)
        blk = idx // (2 * s)
        second = (idx % (2 * s)) >= s
        same = blk[:, None] == blk[None, :]
        sep_f = blk * 2 * s + s - 1
        cm[0, 1 + l] = (idx[None, :] <= sep_f[:, None])
        lmask[0, l] = same & second[:, None] & (~second)[None, :]
        sep_b = blk * 2 * s + s
        cm[1, 1 + l] = (idx[None, :] >= sep_b[:, None])
        lmask[1, l] = same & (~second)[:, None] & second[None, :]
    cm[:, GLA_LEVELS + 1] = 1.0
    lmask[:, GLA_LEVELS] = np.eye(c)
    lmask[:, GLA_LEVELS + 1] = cm[:, 0]
    return jnp.asarray(cm.reshape(2, (GLA_LEVELS + 2) * c, c), BF16), jnp.asarray(lmask, F32)


def _gla_kernel(q_ref, k_ref, v_ref, ag_ref, r_ref, gw_ref, gb_ref, ng_ref, cm_ref, lm_ref,
                o_ref, la_ref, of_ref, ob_ref, qe_ref, u_ref, ss_ref, dec_ref, st_ref,
                *, n_ctx_chunks, n_chunks, q_scale):
    c = GLA_CHUNK
    dv = st_ref.shape[1]
    r = r_ref[0]
    for d in range(2):
        z = _dot(r, gw_ref[d]) + gb_ref[d]
        la_ref[d] = (jnp.minimum(z, 0.0) - jnp.log(1.0 + jnp.exp(-jnp.abs(z)))) * (1.0 / GLA_GATE_NORM)
    st_ref[...] = jnp.zeros_like(st_ref)
    out_refs = (of_ref, ob_ref)

    def chunk_rows(ci):
        return pl.ds(pl.multiple_of(ci * c, c), c)

    def state_rows(ci):
        return pl.ds(pl.multiple_of(ci * dv, dv), dv)

    min_tot = None
    for d in range(2):
        m = jnp.min(jnp.sum(la_ref[d].reshape(n_chunks, c, la_ref.shape[2]), axis=1))
        min_tot = m if min_tot is None else jnp.minimum(min_tot, m)
    is_safe = min_tot >= -GLA_SAFE_RANGE

    group = 3 if n_chunks % 3 == 0 else 2

    def prep_fast(cp, carry):
        jobs = [(group * cp + u, d) for u in range(group) for d in range(2)]
        rows = [chunk_rows(ci) for ci, _ in jobs]
        qs = {u: q_ref[0, chunk_rows(group * cp + u), :].astype(F32) * q_scale for u in range(group)}
        ks = {u: k_ref[0, chunk_rows(group * cp + u), :].astype(F32) for u in range(group)}
        bcs = {}
        for j, (ci, d) in enumerate(jobs):
            g_hi, g_lo = _split2(la_ref[d, rows[j], :])
            cm = cm_ref[d, 0:c, :]
            bcs[j] = _dot(cm, g_hi) + _dot(cm, g_lo)
        qes, kins, decs = {}, {}, {}
        for j, (ci, d) in enumerate(jobs):
            bc = bcs[j]
            qes[j] = (qs[j // 2] * jnp.exp(bc)).astype(BF16)
            kins[j] = ks[j // 2] * jnp.exp(-bc)
            decs[j] = jnp.exp(bc[c - 1:c] if d == 0 else bc[0:1])
        atts = {j: _nt(qes[j], kins[j].astype(BF16)) for j in range(len(jobs))}
        for j, (ci, d) in enumerate(jobs):
            att = (lm_ref[d, GLA_LEVELS + 1] * atts[j]).astype(BF16)
            out_refs[d][rows[j], :] = _dot(att, v_ref[0, rows[j], :])
            qe_ref[d, rows[j], :] = qes[j]
            u_ref[d, state_rows(ci), :] = _tn(v_ref[0, rows[j], :], (kins[j] * decs[j]).astype(BF16))
            dec_ref[d, pl.ds(pl.multiple_of(ci * 8, 8), 8), :] = jnp.broadcast_to(decs[j], (8, decs[j].shape[1]))
        return carry

    def prep(ci, carry):
        rows = chunk_rows(ci)
        q = q_ref[0, rows, :].astype(F32) * q_scale
        k = k_ref[0, rows, :].astype(F32)
        v = v_ref[0, rows, :]
        qk = _nt(q.astype(BF16), k.astype(BF16))
        for d in range(2):
            g_hi, g_lo = _split2(la_ref[d, rows, :])
            cm = cm_ref[d]
            cum = _dot(cm, g_hi) + _dot(cm, g_lo)
            bc = cum[0:c]
            tot = cum[(GLA_LEVELS + 1) * c:(GLA_LEVELS + 2) * c]
            att = lm_ref[d, GLA_LEVELS] * qk
            for l in range(GLA_LEVELS):
                ref_l = cum[(1 + l) * c:(2 + l) * c]
                eq = jnp.exp(jnp.minimum(bc - ref_l, 0.0))
                ek = jnp.exp(jnp.minimum(ref_l - bc, 0.0))
                att = att + lm_ref[d, l] * _nt((q * eq).astype(BF16), (k * ek).astype(BF16))
            out_refs[d][rows, :] = _dot(att.astype(BF16), v)
            qe_ref[d, rows, :] = (q * jnp.exp(bc)).astype(BF16)
            u_ref[d, state_rows(ci), :] = _tn(v, (k * jnp.exp(tot - bc)).astype(BF16))
            dec_ref[d, pl.ds(pl.multiple_of(ci * 8, 8), 8), :] = jnp.exp(tot[0:8])
        return carry

    @pl.when(is_safe)
    def _():
        lax.fori_loop(0, n_chunks // group, prep_fast, 0, unroll=True)

    @pl.when(jnp.logical_not(is_safe))
    def _():
        lax.fori_loop(0, n_chunks, prep, 0)

    def scan(t, carry):
        cb = jnp.where(t < n_ctx_chunks, n_ctx_chunks - 1 - t, n_chunks - 1 - (t - n_ctx_chunks))
        for d, ci in ((0, t), (1, cb)):
            st = st_ref[d]
            ss_ref[d, state_rows(ci), :] = st.astype(BF16)
            dec = dec_ref[d, pl.ds(pl.multiple_of(ci * 8, 8), 1), :]
            st_ref[d] = st * dec + u_ref[d, state_rows(ci), :]
        return carry

    lax.fori_loop(0, n_chunks, scan, 0)

    def inter(ci, carry):
        rows = chunk_rows(ci)
        for d in range(2):
            out_refs[d][rows, :] += _nt(qe_ref[d, rows, :], ss_ref[d, state_rows(ci), :])
        return carry

    lax.fori_loop(0, n_chunks, inter, 0, unroll=2 * group)
    o = of_ref[...] + ob_ref[...]
    ag = ag_ref[0].astype(F32)
    o_ref[0] = (_rms(o) * ng_ref[...] * _silu(ag)).astype(BF16)


def _gla_call(p0, r, gw_pad, gb, norm_g, consts, n_ctx_rows, col_q, col_k, col_v, col_g, dk, dv):
    b, t, _ = p0.shape
    n_chunks = t // GLA_CHUNK
    cm, lm = consts
    kern = functools.partial(_gla_kernel, n_ctx_chunks=n_ctx_rows // GLA_CHUNK, n_chunks=n_chunks,
                             q_scale=float(dk) ** -0.5)
    full = lambda shape: pl.BlockSpec(shape, lambda bi, h: (0,) * len(shape))
    return pl.pallas_call(
        kern,
        grid=(b, GLA_HEADS),
        in_specs=[pl.BlockSpec((1, t, dk), lambda bi, h: (bi, 0, col_q // dk + h)),
                  pl.BlockSpec((1, t, dk), lambda bi, h: (bi, 0, col_k // dk + h)),
                  pl.BlockSpec((1, t, dv), lambda bi, h: (bi, 0, col_v // dv + h)),
                  pl.BlockSpec((1, t, dv), lambda bi, h: (bi, 0, col_g // dv + h)),
                  pl.BlockSpec((1, t, LANES), lambda bi, h: (bi, 0, 0)),
                  pl.BlockSpec((2, LANES, dk), lambda bi, h: (0, 0, h)),
                  pl.BlockSpec((2, 1, dk), lambda bi, h: (0, 0, h)),
                  full((1, dv)), full(cm.shape), full(lm.shape)],
        out_specs=pl.BlockSpec((1, t, dv), lambda bi, h: (bi, 0, h)),
        out_shape=jax.ShapeDtypeStruct((b, t, GLA_HEADS * dv), BF16),
        scratch_shapes=[pltpu.VMEM((2, t, dk), F32), pltpu.VMEM((t, dv), F32), pltpu.VMEM((t, dv), F32),
                        pltpu.VMEM((2, t, dk), BF16), pltpu.VMEM((2, n_chunks * dv, dk), F32),
                        pltpu.VMEM((2, n_chunks * dv, dk), BF16),
                        pltpu.VMEM((2, n_chunks * 8, dk), F32), pltpu.VMEM((2, dv, dk), F32)],
        compiler_params=_cparams("parallel", "parallel"),
        name="gla",
    )(p0, p0, p0, p0, r, gw_pad, gb, norm_g.reshape(1, dv), cm, lm)


def _swa_kernel(q_ref, k_ref, v_ref, sink_ref, o_ref, s_ref, *, n_ctx_rows, n_blocks):
    blk = SWA_BLOCK
    win = 3 * blk
    ncb = n_ctx_rows // blk
    nlb = n_blocks - ncb
    sink = sink_ref[0, :, 0:1] * float(np.log2(np.e))
    rel0 = (lax.broadcasted_iota(jnp.int32, (blk, win), 1) - lax.broadcasted_iota(jnp.int32, (blk, win), 0))
    no_bias = jnp.zeros((blk, n_ctx_rows), F32)

    def key_rows(j):
        kb0 = jnp.clip(j - ncb - 1, 0, nlb - 3)
        return kb0, pl.ds(pl.multiple_of(n_ctx_rows + kb0 * blk, blk), win)

    def scores(j, slot):
        kb0, rows = key_rows(j)
        q = q_ref[0, pl.ds(pl.multiple_of(j * blk, blk), blk), :]
        q4 = jnp.concatenate([q[:, g * blk:(g + 1) * blk] for g in range(SWA_GROUP)], axis=0)
        keys = jnp.concatenate([k_ref[0, rows, :], k_ref[0, 0:n_ctx_rows, :]], axis=0)
        rel = rel0 + jnp.where(j < ncb, 4 * win, (kb0 - (j - ncb)) * blk)
        bias = jnp.concatenate([jnp.where(jnp.abs(rel) <= SWA_WINDOW, 0.0, NEG_BIG), no_bias], axis=1)
        s_ref[slot] = _nt(q4, keys) + jnp.concatenate([bias] * SWA_GROUP, axis=0)

    def finish(j, slot):
        _, rows = key_rows(j)
        vals = jnp.concatenate([v_ref[0, rows, :], v_ref[0, 0:n_ctx_rows, :]], axis=0)
        s = s_ref[slot]
        m = jnp.maximum(jnp.max(s, axis=1, keepdims=True), sink)
        e = jnp.exp2(s - m)
        den = jnp.sum(e, axis=1, keepdims=True) + jnp.exp2(sink - m)
        o = _dot(e.astype(BF16), vals) / den
        o_ref[0, pl.ds(pl.multiple_of(j * blk, blk), blk), :] = jnp.concatenate(
            [o[g * blk:(g + 1) * blk] for g in range(SWA_GROUP)], axis=1).astype(BF16)

    scores(0, 0)

    def body(jj, carry):
        j0 = 2 * jj
        scores(j0 + 1, 1)
        finish(j0, 0)
        scores(j0 + 2, 0)
        finish(j0 + 1, 1)
        return carry

    lax.fori_loop(0, n_blocks // 2 - 1, body, 0, unroll=True)
    scores(n_blocks - 1, 1)
    finish(n_blocks - 2, 0)
    finish(n_blocks - 1, 1)


def _swa_call(p0, sink_col, n_ctx_rows, col_q, col_k, col_v, hd):
    b, t, _ = p0.shape
    blk = SWA_BLOCK
    nb = t // blk
    gq = SWA_GROUP * hd
    assert nb % 2 == 0 and nb - n_ctx_rows // blk >= 3 and SWA_WINDOW <= blk
    return pl.pallas_call(
        functools.partial(_swa_kernel, n_ctx_rows=n_ctx_rows, n_blocks=nb),
        grid=(b, SWA_KV_HEADS),
        in_specs=[pl.BlockSpec((1, t, gq), lambda bi, h: (bi, 0, col_q // gq + h)),
                  pl.BlockSpec((1, t, hd), lambda bi, h: (bi, 0, col_k // hd + h)),
                  pl.BlockSpec((1, t, hd), lambda bi, h: (bi, 0, col_v // hd + h)),
                  pl.BlockSpec((1, SWA_GROUP * blk, LANES), lambda bi, h: (h, 0, 0))],
        out_specs=pl.BlockSpec((1, t, gq), lambda bi, h: (bi, 0, h)),
        out_shape=jax.ShapeDtypeStruct((b, t, SWA_HEADS * hd), BF16),
        scratch_shapes=[pltpu.VMEM((2, SWA_GROUP * blk, 3 * blk + n_ctx_rows), F32)],
        compiler_params=_cparams("parallel", "parallel"),
        name="swa",
    )(p0, p0, p0, sink_col)


def _diff_kernel(lam_ref, q_ref, k_ref, v_ref, g_ref, o_ref, s_ref, kx_ref, vx_ref, qn_ref,
                 *, row_off, n_qblocks, lambda_init):
    tq = DIFF_QBLOCK
    hd2 = 2 * DIFF_HD
    n_keys = k_ref.shape[1]
    lam = lam_ref[...]
    s01 = jnp.sum(lam[0:1] * lam[1:2], axis=1, keepdims=True)
    s23 = jnp.sum(lam[2:3] * lam[3:4], axis=1, keepdims=True)
    lam_full = jnp.exp(s01) - jnp.exp(s23) + lambda_init
    lane = lax.broadcasted_iota(jnp.int32, (1, hd2), 1)
    first = lane < DIFF_HD
    col0 = lane == 0
    zero = jnp.zeros((), BF16)

    d_idx = lax.broadcasted_iota(jnp.int32, (hd2, 2 * hd2), 0)
    j_idx = lax.broadcasted_iota(jnp.int32, (hd2, 2 * hd2), 1)
    map_sum = jnp.where((d_idx < DIFF_HD) == (j_idx < hd2), 1.0, 0.0).astype(BF16)

    def sq_norms(x):
        xf = x.astype(F32)
        return _dot((xf * xf).astype(BF16), map_sum)

    kn = sq_norms(k_ref[0])
    kmax0 = jnp.sqrt(jnp.max(kn[:, 0:hd2], axis=0, keepdims=True)) * DIFF_BOUND_SLACK
    kmax1 = jnp.sqrt(jnp.max(kn[:, hd2:2 * hd2], axis=0, keepdims=True)) * DIFF_BOUND_SLACK
    qn_ref[...] = sq_norms(q_ref[0, row_off:, :])
    qmax0 = jnp.sqrt(jnp.max(qn_ref[:, 0:hd2], axis=0, keepdims=True))
    qmax1 = jnp.sqrt(jnp.max(qn_ref[:, hd2:2 * hd2], axis=0, keepdims=True))
    is_safe = jnp.max(jnp.maximum(qmax0 * kmax0, qmax1 * kmax1)) <= DIFF_SAFE_BOUND

    def q_block(j):
        q = q_ref[0, pl.ds(pl.multiple_of(row_off + j * tq, tq), tq), :]
        return q, jnp.where(first, q, zero), jnp.where(first, zero, q)

    def write_out(j, o):
        y = _rms(o) * g_ref[...] * (1.0 - lambda_init)
        o_ref[0, pl.ds(pl.multiple_of(j * tq, tq), tq), :] = y.astype(BF16)

    def scores_fast(j, slot):
        _, qa, qb = q_block(j)
        qn = jnp.sqrt(qn_ref[pl.ds(pl.multiple_of(j * tq, tq), tq), :])
        xa = jnp.where(col0, -(qn[:, 0:hd2] * kmax0), 0.0).astype(BF16)
        xb = jnp.where(col0, -(qn[:, hd2:2 * hd2] * kmax1), 0.0).astype(BF16)
        qq = jnp.concatenate([jnp.concatenate([qa, xa], axis=1), jnp.concatenate([qb, xb], axis=1)], axis=0)
        s_ref[slot] = _nt(qq, kx_ref[...])

    def finish_fast(j, slot):
        e = jnp.exp2(s_ref[slot]).astype(BF16)
        ov = _dot(e, vx_ref[...])
        p0 = ov[0:tq, 0:hd2] / ov[0:tq, hd2:2 * hd2]
        p1 = ov[tq:2 * tq, 0:hd2] / ov[tq:2 * tq, hd2:2 * hd2]
        write_out(j, p0 - lam_full * p1)

    def scores_max(j, slot):
        _, qa, qb = q_block(j)
        s_ref[slot] = _nt(jnp.concatenate([qa, qb], axis=0), k_ref[0])

    def finish_max(j, slot):
        s = s_ref[slot]
        e = jnp.exp2(s - jnp.max(s, axis=1, keepdims=True))
        z = jnp.sum(e, axis=1, keepdims=True)
        coef = lam_full * z[0:tq] / z[tq:2 * tq]
        a = e[0:tq] - coef * e[tq:2 * tq]
        write_out(j, _dot(a.astype(BF16), v_ref[0]) * (1.0 / z[0:tq]))

    def run(scores, finish, unroll=1):
        scores(0, 0)

        def body(jj, carry):
            j0 = 2 * jj
            scores(j0 + 1, 1)
            finish(j0, 0)
            scores(j0 + 2, 0)
            finish(j0 + 1, 1)
            return carry

        lax.fori_loop(0, n_qblocks // 2 - 1, body, 0, unroll=unroll)
        scores(n_qblocks - 1, 1)
        finish(n_qblocks - 2, 0)
        finish(n_qblocks - 1, 1)

    @pl.when(is_safe)
    def _():
        kx_ref[:, 0:hd2] = k_ref[0]
        kx_ref[:, hd2:2 * hd2] = jnp.broadcast_to(jnp.where(col0, 1.0, 0.0).astype(BF16), (n_keys, hd2))
        vx_ref[:, 0:hd2] = v_ref[0]
        vx_ref[:, hd2:2 * hd2] = jnp.ones((n_keys, hd2), BF16)
        run(scores_fast, finish_fast, unroll=True)

    @pl.when(jnp.logical_not(is_safe))
    def _():
        run(scores_max, finish_max)


def _diff_call(p1, lam, subln_g, n_ctx_rows, n_heads, lambda_init):
    b, t, _ = p1.shape
    s_len = t - n_ctx_rows
    hd2 = 2 * DIFF_HD
    n_qblocks = s_len // DIFF_QBLOCK
    assert n_qblocks >= 2 and n_qblocks % 2 == 0
    return pl.pallas_call(
        functools.partial(_diff_kernel, row_off=n_ctx_rows, n_qblocks=n_qblocks, lambda_init=lambda_init),
        grid=(b, n_heads),
        in_specs=[pl.BlockSpec((4, DIFF_HD), lambda bi, h: (0, 0)),
                  pl.BlockSpec((1, t, hd2), lambda bi, h: (bi, 0, h)),
                  pl.BlockSpec((1, t, hd2), lambda bi, h: (bi, 0, n_heads + h)),
                  pl.BlockSpec((1, t, hd2), lambda bi, h: (bi, 0, 2 * n_heads + h)),
                  pl.BlockSpec((1, hd2), lambda bi, h: (0, 0))],
        out_specs=pl.BlockSpec((1, s_len, hd2), lambda bi, h: (bi, 0, h)),
        out_shape=jax.ShapeDtypeStruct((b, s_len, n_heads * hd2), BF16),
        scratch_shapes=[pltpu.VMEM((2, 2 * DIFF_QBLOCK, t), F32), pltpu.VMEM((t, 2 * hd2), BF16),
                        pltpu.VMEM((t, 2 * hd2), BF16), pltpu.VMEM((s_len, 2 * hd2), F32)],
        compiler_params=_cparams("parallel", "parallel"),
        name="diff_attn",
    )(lam, p1, p1, p1, subln_g.reshape(1, hd2))


def _outproj_kernel(*refs, n_y, n_sub, n_row_refs, n_ctx_blocks, row_block_off):
    y_refs = refs[:n_y]
    w_refs = refs[n_y:2 * n_y]
    row_refs = refs[2 * n_y:2 * n_y + n_sub * n_row_refs]
    mod_ref, g_ref, wr_ref, xn_ref, h_ref, aff_ref = refs[2 * n_y + n_sub * n_row_refs:]
    n_exp = aff_ref.shape[1]
    subs = [slice(u * ROW_TILE, (u + 1) * ROW_TILE) for u in range(n_sub)]
    accs = []
    for rs in subs:
        acc = _dot(y_refs[0][0, rs, :], w_refs[0][...])
        for yr, wr in zip(y_refs[1:], w_refs[1:]):
            acc = acc + _dot(yr[0, rs, :], wr[...])
        accs.append(acc)
    for u, rs in enumerate(subs):
        x_rows = _load_rows(row_refs[u * n_row_refs:(u + 1) * n_row_refs], n_ctx_blocks, row_block_off)
        xn = x_rows + mod_ref[0, 0, 2:3, :] * accs[u]
        xn_ref[0, rs, :] = xn
        h = _rms(xn) * g_ref[...] * (1.0 + mod_ref[0, 0, 4:5, :]) + mod_ref[0, 0, 3:4, :]
        h_bf = h.astype(BF16)
        h_ref[0, rs, :] = h_bf
        logits = _dot(h_bf, wr_ref[...])
        logits = jnp.where(lax.broadcasted_iota(jnp.int32, (1, LANES), 1) < n_exp, logits, NEG_BIG)
        ex = jnp.exp(logits - jnp.max(logits, axis=1, keepdims=True))
        aff = ex / jnp.sum(ex, axis=1, keepdims=True)
        aff_ref[0, :, rs] = jnp.transpose(aff)[0:n_exp]


def _outproj_call(ys, ws, stream, mod, g, w_router, row_block_off, n_rows, n_ctx_blocks):
    b, d = mod.shape[0], mod.shape[3]
    n_y = len(ys)
    e = w_router.shape[1]
    w_router_pad = jnp.concatenate([w_router, jnp.zeros((d, LANES - e), F32)], axis=1).astype(BF16)
    all_latent = row_block_off >= n_ctx_blocks and not isinstance(stream, tuple)
    n_sub = 2 if all_latent and n_rows % (2 * ROW_TILE) == 0 else 1
    tile = n_sub * ROW_TILE
    row_specs, row_args = [], []
    for u in range(n_sub):
        specs_u, args_u = _row_specs(stream, n_ctx_blocks, row_block_off + u, n_sub)
        row_specs += specs_u
        row_args += args_u
    in_specs = ([pl.BlockSpec((1, tile, y.shape[2]), lambda bi, i: (bi, i, 0)) for y in ys]
                + [pl.BlockSpec(w.shape, lambda bi, i: (0, 0), pipeline_mode=pl.Buffered(1)) for w in ws]
                + row_specs
                + [pl.BlockSpec((1, 1, 6, d),
                                lambda bi, i: (bi, jnp.where(i * n_sub + row_block_off < n_ctx_blocks, 0, 1), 0, 0)),
                   pl.BlockSpec((1, d), lambda bi, i: (0, 0)),
                   pl.BlockSpec((d, LANES), lambda bi, i: (0, 0))])
    return pl.pallas_call(
        functools.partial(_outproj_kernel, n_y=n_y, n_sub=n_sub, n_row_refs=len(row_args) // n_sub,
                          n_ctx_blocks=n_ctx_blocks, row_block_off=row_block_off),
        grid=(b, n_rows // tile),
        in_specs=in_specs,
        out_specs=[pl.BlockSpec((1, tile, d), lambda bi, i: (bi, i, 0)),
                   pl.BlockSpec((1, tile, d), lambda bi, i: (bi, i, 0)),
                   pl.BlockSpec((1, e, tile), lambda bi, i: (bi, 0, i))],
        out_shape=[jax.ShapeDtypeStruct((b, n_rows, d), F32),
                   jax.ShapeDtypeStruct((b, n_rows, d), BF16),
                   jax.ShapeDtypeStruct((b, e, n_rows), F32)],
        compiler_params=_cparams("parallel", "parallel"),
        name="out_proj",
    )(*ys, *ws, *row_args, mod, g.reshape(1, d), w_router_pad)


def _prefix_lanes(m, upper):
    e, n = m.shape
    carry = jnp.zeros((e, 1), F32)
    outs = []
    for blk in range(n // LANES):
        mb = m[:, blk * LANES:(blk + 1) * LANES]
        inc = _dot(mb.astype(BF16), upper)
        outs.append(inc - mb + carry)
        carry = carry + jnp.sum(mb, axis=1, keepdims=True)
    return jnp.concatenate(outs, axis=1)


def _transpose_exact(x, eye):
    hi, mid, lo = _split3(x)
    return _nt(eye, hi) + (_nt(eye, mid) + _nt(eye, lo))


def _route_kernel(aff_ref, pos_ref, gate_ref, posn_ref, lo_ref, hi_ref, *, segs):
    n_exp = aff_ref.shape[1]
    lo_acc = jnp.zeros((n_exp, LANES), F32)
    hi_acc = jnp.zeros((n_exp, LANES), F32)
    li = lax.broadcasted_iota(jnp.int32, (LANES, LANES), 0)
    lj = lax.broadcasted_iota(jnp.int32, (LANES, LANES), 1)
    upper = jnp.where(li <= lj, 1.0, 0.0).astype(BF16)
    eye = jnp.where(li == lj, 1.0, 0.0).astype(BF16)
    all_bits = [pltpu.bitcast(aff_ref[0, :, off:off + n], jnp.int32) for (off, n, _, _) in segs]

    def search(it, thrs):
        out = []
        for bits, (_, _, cap, _), thr in zip(all_bits, segs, thrs):
            cand = thr | jnp.left_shift(jnp.int32(1), 30 - it)
            cnt = jnp.sum(jnp.where(bits >= cand, 1.0, 0.0), axis=1, keepdims=True)
            out.append(jnp.where(cnt >= cap, cand, thr))
        return tuple(out)

    all_thr = lax.fori_loop(0, 31, search, tuple(jnp.zeros((n_exp, 1), jnp.int32) for _ in segs))
    for (off, n, cap, slot_off), bits, thr in zip(segs, all_bits, all_thr):
        a = aff_ref[0, :, off:off + n]
        gt = jnp.where(bits > thr, 1.0, 0.0)
        eq = jnp.where(bits == thr, 1.0, 0.0)
        need = cap - jnp.sum(gt, axis=1, keepdims=True)
        sel = gt + eq * jnp.where(_prefix_lanes(eq, upper) < need, 1.0, 0.0)
        slot = _prefix_lanes(sel, upper)
        pos = jnp.where(sel > 0.0, slot + slot_off, -1.0)
        pos_ref[0, :, off:off + n] = pos.astype(jnp.int32)
        gate_ref[0, :, off:off + n] = sel * a
        for blk in range(n // LANES):
            cols = slice(blk * LANES, (blk + 1) * LANES)
            rows = slice(off + blk * LANES, off + (blk + 1) * LANES)
            posn_ref[0, rows, :] = _transpose_exact(pos[:, cols], eye).astype(jnp.int32)
        assert cap <= 256 and (off + n) // LANES <= LANES
        tok = lax.broadcasted_iota(jnp.int32, (n, LANES), 0)
        blk_start = (lax.broadcasted_iota(jnp.int32, (n, LANES), 1) - off // LANES) * LANES
        lo_acc = lo_acc + _dot(slot.astype(BF16), jnp.where(tok == blk_start, 1.0, 0.0).astype(BF16))
        hi_acc = hi_acc + _dot((slot + sel).astype(BF16),
                               jnp.where(tok == blk_start + (LANES - 1), 1.0, 0.0).astype(BF16))
    lo_ref[0] = lo_acc.astype(jnp.int32)
    hi_ref[0] = hi_acc.astype(jnp.int32)


def _route_call(aff_t, segs):
    b, e, t = aff_t.shape
    return pl.pallas_call(
        functools.partial(_route_kernel, segs=tuple(segs)),
        grid=(b,),
        in_specs=[pl.BlockSpec((1, e, t), lambda bi: (bi, 0, 0))],
        out_specs=[pl.BlockSpec((1, e, t), lambda bi: (bi, 0, 0)),
                   pl.BlockSpec((1, e, t), lambda bi: (bi, 0, 0)),
                   pl.BlockSpec((1, t, e), lambda bi: (bi, 0, 0)),
                   pl.BlockSpec((1, e, LANES), lambda bi: (bi, 0, 0)),
                   pl.BlockSpec((1, e, LANES), lambda bi: (bi, 0, 0))],
        out_shape=[jax.ShapeDtypeStruct((b, e, t), jnp.int32),
                   jax.ShapeDtypeStruct((b, e, t), F32),
                   jax.ShapeDtypeStruct((b, t, e), jnp.int32),
                   jax.ShapeDtypeStruct((b, e, LANES), jnp.int32),
                   jax.ShapeDtypeStruct((b, e, LANES), jnp.int32)],
        compiler_params=_cparams("parallel"),
        name="route",
    )(aff_t)


def _gather_kernel(lo_ref, hi_ref, pos_ref, gate_ref, h_ref, o_ref, g_ref,
                   *, n_ctx_rows, n_lat_rows, cap_lat, sblk, win, margin):
    sb = pl.program_id(1)
    n_exp = pos_ref.shape[1]
    n_lat_blocks = cap_lat // sblk
    tok_per_blk = n_lat_rows // n_lat_blocks

    def emit(slot0, tok0, n_tok):
        rows = pl.ds(tok0, n_tok)
        slots = lax.broadcasted_iota(jnp.int32, (sblk, 1), 0) + slot0
        hits = [pos_ref[0, e:e + 1, rows] == slots for e in range(n_exp)]
        onehot = jnp.concatenate([jnp.where(h, 1.0, 0.0).astype(BF16) for h in hits], axis=0)
        res = _dot(onehot, h_ref[0, rows, :])
        for e in range(n_exp):
            o_ref[e, 0] = res[e * sblk:(e + 1) * sblk].astype(BF16)
            gsel = jnp.sum(jnp.where(hits[e], gate_ref[0, e:e + 1, rows], 0.0), axis=1, keepdims=True)
            g_ref[e, 0] = jnp.broadcast_to(gsel, (sblk, LANES))

    slot0 = sb * sblk
    w0 = jnp.clip(sb * tok_per_blk - margin, 0, n_lat_rows - win)
    tok0 = pl.multiple_of(n_ctx_rows + w0, LANES)
    first_blk = (n_ctx_rows + w0) // LANES
    fits = None
    for e in range(n_exp):
        ok = jnp.logical_and(lo_ref[0, e, first_blk] <= slot0,
                             hi_ref[0, e, first_blk + win // LANES - 1] >= slot0 + sblk)
        fits = ok if fits is None else jnp.logical_and(fits, ok)
    is_lat = sb < n_lat_blocks

    @pl.when(jnp.logical_and(is_lat, fits))
    def _():
        emit(slot0, tok0, win)

    @pl.when(jnp.logical_and(is_lat, jnp.logical_not(fits)))
    def _():
        emit(slot0, n_ctx_rows, n_lat_rows)

    if n_ctx_rows:
        @pl.when(jnp.logical_not(is_lat))
        def _():
            emit(cap_lat, 0, n_ctx_rows)


def _slot_window(tt, cap_lat, n_exp):
    return min(cap_lat, max(16, 2 * tt * EC_FACTOR // n_exp))


def _gather_call(lo, hi, pos, gate, h, cap_lat, cap_ctx, n_ctx_rows):
    b, e, t = pos.shape
    d = h.shape[2]
    n_slots = cap_lat + cap_ctx
    n_lat_rows = t - n_ctx_rows
    sblk = cap_ctx if cap_ctx else 32
    assert cap_lat % sblk == 0 and sblk % 16 == 0
    n_lat_blocks = cap_lat // sblk
    tok_per_blk = n_lat_rows // n_lat_blocks
    margin = tok_per_blk
    win = min(n_lat_rows, 3 * tok_per_blk)
    assert tok_per_blk % LANES == 0 and win % LANES == 0
    smem = lambda: pl.BlockSpec((1, e, LANES), lambda bi, si: (bi, 0, 0), memory_space=pltpu.SMEM)
    return pl.pallas_call(
        functools.partial(_gather_kernel, n_ctx_rows=n_ctx_rows, n_lat_rows=n_lat_rows, cap_lat=cap_lat,
                          sblk=sblk, win=win, margin=margin),
        grid=(b, n_slots // sblk),
        in_specs=[smem(), smem(),
                  pl.BlockSpec((1, e, t), lambda bi, si: (bi, 0, 0)),
                  pl.BlockSpec((1, e, t), lambda bi, si: (bi, 0, 0)),
                  pl.BlockSpec((1, t, d), lambda bi, si: (bi, 0, 0))],
        out_specs=[pl.BlockSpec((e, 1, sblk, d), lambda bi, si: (0, bi, si, 0)),
                   pl.BlockSpec((e, 1, sblk, LANES), lambda bi, si: (0, bi, si, 0))],
        out_shape=[jax.ShapeDtypeStruct((e, b, n_slots, d), BF16),
                   jax.ShapeDtypeStruct((e, b, n_slots, LANES), F32)],
        compiler_params=_cparams("parallel", "arbitrary"),
        name="gather",
    )(lo, hi, pos, gate, h)


def _ffn_up_kernel(x_ref, wg_ref, wu_ref, o_ref, wgb_ref, wub_ref):
    @pl.when(pl.program_id(2) == 0)
    def _():
        wgb_ref[...] = wg_ref[0, 0].astype(BF16)
        wub_ref[...] = wu_ref[0, 0].astype(BF16)

    x = x_ref[0]
    g = _dot(x, wgb_ref[...])
    u = _dot(x, wub_ref[...])
    o_ref[0] = (_silu(g) * u).astype(BF16)


def _ffn_down_kernel(a_ref, w_ref, gate_ref, o_ref, wb_ref):
    @pl.when(pl.program_id(2) == 0)
    def _():
        wb_ref[...] = w_ref[0, 0].astype(BF16)

    gate = jnp.concatenate([gate_ref[0]] * (o_ref.shape[2] // LANES), axis=1)
    o_ref[0] = (_dot(a_ref[0], wb_ref[...]) * gate).astype(BF16)


def _ffn_call(xs, gates, w_gu, w_dn, layer):
    e, m, d = xs.shape
    f = w_dn.shape[2]
    tm = _pick(m, (1152, 1024, 768, 512, 256, 128, 64, 32, 16, 8))
    tf = _pick(f, (512, 256, 128))
    nf = f // tf
    act = pl.pallas_call(
        _ffn_up_kernel,
        grid=(e, nf, m // tm),
        in_specs=[pl.BlockSpec((1, tm, d), lambda ei, j, i: (ei, i, 0)),
                  pl.BlockSpec((1, 1, d, tf), lambda ei, j, i: (layer, ei, 0, j)),
                  pl.BlockSpec((1, 1, d, tf), lambda ei, j, i: (layer, ei, 0, j + nf))],
        out_specs=pl.BlockSpec((1, tm, tf), lambda ei, j, i: (ei, i, j)),
        out_shape=jax.ShapeDtypeStruct((e, m, f), BF16),
        scratch_shapes=[pltpu.VMEM((d, tf), BF16), pltpu.VMEM((d, tf), BF16)],
        compiler_params=_cparams("parallel", "arbitrary", "arbitrary"),
        name="ffn_up",
    )(xs, w_gu, w_gu)
    tn = _pick(d, (1024, 512, 256, 128))
    return pl.pallas_call(
        _ffn_down_kernel,
        grid=(e, d // tn, m // tm),
        in_specs=[pl.BlockSpec((1, tm, f), lambda ei, j, i: (ei, i, 0)),
                  pl.BlockSpec((1, 1, f, tn), lambda ei, j, i: (layer, ei, 0, j)),
                  pl.BlockSpec((1, tm, LANES), lambda ei, j, i: (ei, i, 0))],
        out_specs=pl.BlockSpec((1, tm, tn), lambda ei, j, i: (ei, i, j)),
        out_shape=jax.ShapeDtypeStruct((e, m, d), BF16),
        scratch_shapes=[pltpu.VMEM((f, tn), BF16)],
        compiler_params=_cparams("parallel", "arbitrary", "arbitrary"),
        name="ffn_down",
    )(act, w_dn, gates)


def _combine_kernel(lo_ref, hi_ref, posn_ref, y_ref, x_ref, mod_ref, gf_ref, o_ref,
                    *, cap_lat, cap_ctx, n_ctx_tiles, blocks_per_tile, window, final_norm):
    n_exp = posn_ref.shape[2]
    i = pl.program_id(1)
    lane = lax.broadcasted_iota(jnp.int32, (1, n_exp), 1)
    posn = posn_ref[0].astype(F32)

    def pcol(e):
        return jnp.sum(jnp.where(lane == e, posn, 0.0), axis=1, keepdims=True)

    def scatter(slot0, cap):
        slots = lax.broadcasted_iota(jnp.int32, (1, cap), 1).astype(F32) + slot0
        hots = [jnp.where(pcol(e) == slots, 1.0, 0.0).astype(BF16) for e in range(n_exp)]
        rows = [y_ref[e, 0, slot0:slot0 + cap, :] for e in range(n_exp)]
        return _dot(jnp.concatenate(hots, axis=1), jnp.concatenate(rows, axis=0))

    def scatter_window(starts):
        w_tot = n_exp * window
        grp = lax.broadcasted_iota(jnp.int32, (n_exp, w_tot), 1) // window
        spread = jnp.where(grp == lax.broadcasted_iota(jnp.int32, (n_exp, w_tot), 0), 1.0, 0.0).astype(BF16)
        bpos = _dot(posn.astype(BF16), spread)
        wlane = lax.broadcasted_iota(jnp.int32, (1, w_tot), 1)
        target = (wlane % window).astype(F32)
        for e, st in enumerate(starts):
            target = jnp.where(wlane // window == e, target + st.astype(F32), target)
        onehot = jnp.where(bpos == target, 1.0, 0.0).astype(BF16)
        rows = [y_ref[e, 0, pl.ds(pl.multiple_of(st, 16), window), :] for e, st in enumerate(starts)]
        return _dot(onehot, jnp.concatenate(rows, axis=0))

    def finish(total, g2):
        out = x_ref[0] + g2 * total
        if final_norm:
            out = _rms(out) * gf_ref[...]
        o_ref[0] = out

    blk0 = i * blocks_per_tile
    starts = []
    fits = None
    for e in range(n_exp):
        lo = lo_ref[0, e, blk0]
        hi = hi_ref[0, e, blk0 + blocks_per_tile - 1]
        st = jnp.minimum(lax.shift_left(lax.shift_right_logical(lo, 4), 4), cap_lat - window)
        ok = hi - st <= window
        fits = ok if fits is None else jnp.logical_and(fits, ok)
        starts.append(st)
    is_lat = i >= n_ctx_tiles

    @pl.when(jnp.logical_and(is_lat, fits))
    def _():
        finish(scatter_window(starts), mod_ref[0, 1, 5:6, :])

    @pl.when(jnp.logical_and(is_lat, jnp.logical_not(fits)))
    def _():
        finish(scatter(0, cap_lat), mod_ref[0, 1, 5:6, :])

    if n_ctx_tiles:
        @pl.when(jnp.logical_not(is_lat))
        def _():
            finish(scatter(cap_lat, cap_ctx), mod_ref[0, 0, 5:6, :])


def _combine_call(lo, hi, posn, y, xa, mod, g_final, cap_lat, cap_ctx, n_ctx_rows, final_norm):
    b, t, d = xa.shape
    e = posn.shape[2]
    n_slots = y.shape[2]
    tt = ROW_TILE
    assert n_ctx_rows % tt == 0 and t % tt == 0
    window = _slot_window(tt, cap_lat, e)
    smem = lambda: pl.BlockSpec((1, e, LANES), lambda bi, i: (bi, 0, 0), memory_space=pltpu.SMEM)
    return pl.pallas_call(
        functools.partial(_combine_kernel, cap_lat=cap_lat, cap_ctx=cap_ctx, n_ctx_tiles=n_ctx_rows // tt,
                          blocks_per_tile=tt // LANES, window=window, final_norm=final_norm),
        grid=(b, t // tt),
        in_specs=[smem(), smem(),
                  pl.BlockSpec((1, tt, e), lambda bi, i: (bi, i, 0)),
                  pl.BlockSpec((e, 1, n_slots, d), lambda bi, i: (0, bi, 0, 0)),
                  pl.BlockSpec((1, tt, d), lambda bi, i: (bi, i, 0)),
                  pl.BlockSpec((1, 2, 6, d), lambda bi, i: (bi, 0, 0, 0)),
                  pl.BlockSpec((1, d), lambda bi, i: (0, 0))],
        out_specs=pl.BlockSpec((1, tt, d), lambda bi, i: (bi, i, 0)),
        out_shape=jax.ShapeDtypeStruct((b, t, d), F32),
        compiler_params=_cparams("parallel", "arbitrary"),
        name="combine",
    )(lo, hi, posn, y, xa, mod, g_final.reshape(1, d))


def _rope_tables(n_tokens, n_ctx_rows, head_dim):
    rows = n_tokens // GRID_W
    r = jnp.repeat(jnp.arange(rows, dtype=F32), GRID_W)
    col = jnp.tile(jnp.arange(GRID_W, dtype=F32), rows)
    axis_dim = head_dim // 2
    inv = ROPE_THETA ** (-jnp.arange(0, axis_dim, 2, dtype=F32) / axis_dim)
    ar, ac = r[:, None] * inv, col[:, None] * inv
    cos_p = jnp.concatenate([jnp.cos(ar), jnp.cos(ar), jnp.cos(ac), jnp.cos(ac)], axis=-1)
    sin_p = jnp.concatenate([-jnp.sin(ar), jnp.sin(ar), -jnp.sin(ac), jnp.sin(ac)], axis=-1)
    reps = LANES // head_dim
    cos_p, sin_p = jnp.tile(cos_p, (1, reps)), jnp.tile(sin_p, (1, reps))
    cos_t = jnp.concatenate([jnp.ones((n_ctx_rows, LANES), F32), cos_p], axis=0)
    sin_t = jnp.concatenate([jnp.zeros((n_ctx_rows, LANES), F32), sin_p], axis=0)
    return cos_t, sin_t


def _moe(h2, aff_t, x_res, mod, g_final, w_gu, w_dn, layer, segs, cap_lat, cap_ctx, n_ctx_rows, final_norm):
    b, t, d = h2.shape
    e = aff_t.shape[1]
    n_slots = cap_lat + cap_ctx
    pos, gate, posn, lo, hi = _route_call(aff_t, segs)
    xs, gs = _gather_call(lo, hi, pos, gate, h2, cap_lat, cap_ctx, n_ctx_rows)
    y = _ffn_call(xs.reshape(e, b * n_slots, d), gs.reshape(e, b * n_slots, LANES), w_gu, w_dn, layer)
    return _combine_call(lo, hi, posn, y.reshape(e, b, n_slots, d), x_res, mod, g_final, cap_lat, cap_ctx,
                         n_ctx_rows, final_norm)


def kernel(x, c, ctx, c_ctx, w_ada, b_ada, g_norm_mix, g_norm_ffn, w_in_even, gla_gate_w, gla_gate_b, gla_norm_g, swa_sink, w_out_even, w_qkv_odd, diff_lambda, diff_subln_g, w_out_odd, w_router, w_gate_up, w_down, g_final):
    b, s_len, d = x.shape
    lc = ctx.shape[1]
    depth = w_ada.shape[0]
    assert depth == 2 and lc % ROW_TILE == 0 and s_len % ROW_TILE == 0 and s_len % GRID_W == 0
    t = lc + s_len
    ncb = lc // ROW_TILE
    gla_dk, gla_dv = d // 4 // GLA_HEADS, d // 2 // GLA_HEADS
    swa_hd = d // 2 // SWA_HEADS
    diff_heads = d // (2 * DIFF_HD)
    n_exp = w_router.shape[2]
    assert gla_dk == LANES and swa_hd == LANES and 2 * DIFF_HD == LANES

    n_rows = -(-(b + 1) // 8) * 8
    cc = jnp.concatenate([c, c_ctx[None, :], jnp.zeros((n_rows - b - 1, d), F32)], axis=0)
    m_all = _ada_call(cc, w_ada, b_ada).reshape(depth, n_rows, 6, d)

    def mod_for(layer):
        lat = m_all[layer, :b]
        ctxm = jnp.broadcast_to(m_all[layer, b][None], (b, 6, d))
        return jnp.stack([ctxm, lat], axis=1)

    xa = (ctx, x)

    mod0 = mod_for(0)
    w_in = w_in_even[0]
    sizes = (GLA_HEADS * gla_dk, GLA_HEADS * gla_dk, GLA_HEADS * gla_dv, GLA_HEADS * gla_dv,
             GLA_GATE_RANK, GLA_GATE_RANK, SWA_HEADS * swa_hd, SWA_KV_HEADS * swa_hd, SWA_KV_HEADS * swa_hd)
    offs = np.concatenate([[0], np.cumsum(sizes)])
    seg = lambda i: w_in[:, offs[i]:offs[i + 1]]
    w_main = jnp.concatenate([seg(0), seg(1), seg(2), seg(3), seg(6), seg(7), seg(8)], axis=1).astype(BF16)
    w_rank = jnp.concatenate([seg(4), seg(5), jnp.zeros((d, LANES - 2 * GLA_GATE_RANK), F32)],
                             axis=1).astype(BF16)
    col_aq, col_ak = 0, sizes[0]
    col_av = col_ak + sizes[1]
    col_ag = col_av + sizes[2]
    col_bq = col_ag + sizes[3]
    col_bk = col_bq + sizes[6]
    col_bv = col_bk + sizes[7]
    n_main = col_bv + sizes[8]
    swa_q_scale = float(swa_hd) ** -0.5 * float(np.log2(np.e))
    rope0 = [(col_bq <= ch * COL_CHUNK < col_bv, swa_q_scale if col_bq <= ch * COL_CHUNK < col_bk else 1.0)
             for ch in range(n_main // COL_CHUNK)]
    cos_b, sin_b = _rope_tables(s_len, lc, swa_hd)
    p0, r0 = _proj_call(xa, mod0, g_norm_mix[0], w_main, cos_b, sin_b, rope0, swa_hd // 4, ncb, wr=w_rank)

    gw = gla_gate_w[0]
    gw_pad = jnp.zeros((2, LANES, GLA_HEADS * gla_dk), F32)
    gw_pad = gw_pad.at[0, 0:GLA_GATE_RANK].set(gw[0]).at[1, GLA_GATE_RANK:2 * GLA_GATE_RANK].set(gw[1])
    a_out = _gla_call(p0, r0, gw_pad.astype(BF16), gla_gate_b[0].reshape(2, 1, -1), gla_norm_g[0],
                      _gla_constants(), lc, col_aq, col_ak, col_av, col_ag, gla_dk, gla_dv)
    sink_col = jnp.broadcast_to(
        jnp.repeat(swa_sink[0].reshape(SWA_KV_HEADS, SWA_GROUP), SWA_BLOCK, axis=1)[:, :, None],
        (SWA_KV_HEADS, SWA_GROUP * SWA_BLOCK, LANES))
    b_out = _swa_call(p0, sink_col, lc, col_bq, col_bk, col_bv, swa_hd)

    w_o = w_out_even[0].astype(BF16)
    n_a = GLA_HEADS * gla_dv
    xa1, h2, aff_t = _outproj_call([a_out, b_out], [w_o[:n_a], w_o[n_a:]], xa, mod0, g_norm_ffn[0],
                                   w_router[0], 0, t, ncb)
    cap_lat = s_len * EC_FACTOR // n_exp
    cap_ctx = lc * EC_FACTOR // n_exp
    segs0 = [(lc, s_len, cap_lat, 0), (0, lc, cap_ctx, cap_lat)]
    xa2 = _moe(h2, aff_t, xa1, mod0, g_final, w_gate_up, w_down, 0, segs0, cap_lat, cap_ctx, lc, False)

    mod1 = mod_for(1)
    lambda_init = 0.8 - 0.6 * float(np.exp(-0.3 * 1))
    w_qkv = w_qkv_odd[0].astype(BF16)
    n_qkv = w_qkv.shape[1]
    q_scale = DIFF_HD ** -0.5 * float(np.log2(np.e))
    rope1 = [(ch * COL_CHUNK < 2 * d, q_scale if ch * COL_CHUNK < d else 1.0)
             for ch in range(n_qkv // COL_CHUNK)]
    cos_c, sin_c = _rope_tables(s_len, lc, DIFF_HD)
    (p1,) = _proj_call(xa2, mod1, g_norm_mix[1], w_qkv, cos_c, sin_c, rope1, DIFF_HD // 4, ncb)
    y1 = _diff_call(p1, diff_lambda[0], diff_subln_g[0], lc, diff_heads, lambda_init)
    x3, h2b, aff_tb = _outproj_call([y1], [w_out_odd[0].astype(BF16)], xa2, mod1, g_norm_ffn[1],
                                    w_router[1], ncb, s_len, ncb)
    segs1 = [(0, s_len, cap_lat, 0)]
    return _moe(h2b, aff_tb, x3, mod1, g_final, w_gate_up, w_down, 1, segs1, cap_lat, 0, 0, True)
```
